```python
import jax, jax.numpy as jnp
from jax import lax
import numpy as np

D_MODEL = 1024
BATCH = 8
SEQ = 4096
DEPTH = 2

GRID_W = 64
CTX_LEN = 256
D_MIX = D_MODEL
HEAD_DIM = 64
ATT_WIDTH = D_MIX // 2
RWKV_WIDTH = D_MIX - ATT_WIDTH
ATT_HEADS = ATT_WIDTH // HEAD_DIM
ATT_KV_HEADS = 2
ATT_GROUP = ATT_HEADS // ATT_KV_HEADS
ATT_KV_WIDTH = ATT_KV_HEADS * HEAD_DIM
WINDOW = 128
QBLK = 128
ROPE_BASE = 10000.0
ROPE_FREQS = HEAD_DIM // 4
RWKV_HEADS = RWKV_WIDTH // HEAD_DIM
LORA_DECAY = 64
LORA_ICLR = 64
LORA_VRES = 32
LORA_GATE = 128
N_DIRS = 2
ATT_COLS = ATT_WIDTH + 2 * ATT_KV_WIDTH
RWKV_COLS = 3 * RWKV_WIDTH + N_DIRS * (LORA_DECAY + LORA_ICLR) + LORA_GATE
IN_COLS = ATT_COLS + RWKV_COLS
ATT_SPLITS = (ATT_WIDTH, ATT_WIDTH + ATT_KV_WIDTH, ATT_COLS)
RWKV_SPLITS = (RWKV_WIDTH, 2 * RWKV_WIDTH, 3 * RWKV_WIDTH,
               3 * RWKV_WIDTH + N_DIRS * LORA_DECAY,
               3 * RWKV_WIDTH + N_DIRS * (LORA_DECAY + LORA_ICLR))
N_EXPERTS = 16
N_GROUPS = 4
EXPERTS_PER_GROUP = N_EXPERTS // N_GROUPS
TOP_K = 2
GROUP_SCORE_K = 2
D_EXPERT = D_MODEL
MOE_BLK = 256
NORM_EPS = 1e-6
GN_EPS = 64e-5
NEG_INF = -1e30
ATT_SCALE = HEAD_DIM ** -0.5

kernel_name = 'hybrid_swa_rwkv7_moe_prefix_dit'


def rmsnorm(x, g):
    xf = x.astype(jnp.float32)
    y = xf * lax.rsqrt(jnp.mean(xf * xf, axis=-1, keepdims=True) + NORM_EPS)
    return (y * g).astype(x.dtype)


def modulate(h, shift, scale):
    return h * (1.0 + scale) + shift


def axial_rope_tables(rows_count):
    row = jnp.repeat(jnp.arange(rows_count, dtype=jnp.float32), GRID_W)
    col = jnp.tile(jnp.arange(GRID_W, dtype=jnp.float32), rows_count)
    inv_freq = ROPE_BASE ** (-jnp.arange(ROPE_FREQS, dtype=jnp.float32) / ROPE_FREQS)
    ang_r = row[:, None] * inv_freq[None, :]
    ang_c = col[:, None] * inv_freq[None, :]
    ang = jnp.concatenate([ang_r, ang_r, ang_c, ang_c], axis=-1)
    return jnp.cos(ang), jnp.sin(ang)


def rotate_half_axial(t):
    t = t.reshape(t.shape[:-1] + (2, 2, ROPE_FREQS))
    t = jnp.stack([-t[..., 1, :], t[..., 0, :]], axis=-2)
    return t.reshape(t.shape[:-3] + (HEAD_DIM,))


def apply_rope(t, cos, sin):
    return t * cos[:, None, :] + rotate_half_axial(t) * sin[:, None, :]


def windowed_attention(q, k, v, kc, vc, sink):
    B, L = q.shape[0], q.shape[1]
    C = kc.shape[1]
    nb = L // QBLK

    def key_windows(t):
        tp = jnp.pad(t, ((0, 0), (QBLK, QBLK), (0, 0), (0, 0)))
        tp = tp.reshape(B, nb + 2, QBLK, ATT_KV_HEADS, HEAD_DIM)
        tw = jnp.concatenate([tp[:, :-2], tp[:, 1:-1], tp[:, 2:]], axis=2)
        return jnp.moveaxis(tw, 1, 0)

    kw, vw = key_windows(k), key_windows(v)
    qb = jnp.moveaxis(q.reshape(B, nb, QBLK, ATT_KV_HEADS, ATT_GROUP, HEAD_DIM), 1, 0)
    qi = jnp.arange(QBLK)[:, None]
    kj = jnp.arange(3 * QBLK)[None, :] - QBLK
    in_window = jnp.abs(kj - qi) <= WINDOW
    s_sink = jnp.broadcast_to(sink.astype(jnp.float32)[None, :, :, None, None],
                              (B, ATT_KV_HEADS, ATT_GROUP, QBLK, 1))
    W = 3 * QBLK

    def one_block(args):
        n, qn, kn, vn = args
        kpos = n * QBLK + kj
        valid = in_window & (kpos >= 0) & (kpos < L)
        s_loc = jnp.einsum('bqhgd,bkhd->bhgqk', qn, kn).astype(jnp.float32) * ATT_SCALE
        s_loc = jnp.where(valid, s_loc, NEG_INF)
        s_ctx = jnp.einsum('bqhgd,bchd->bhgqc', qn, kc).astype(jnp.float32) * ATT_SCALE
        p = jax.nn.softmax(jnp.concatenate([s_loc, s_ctx, s_sink], axis=-1), axis=-1)
        o = jnp.einsum('bhgqk,bkhd->bqhgd', p[..., :W].astype(vn.dtype), vn)
        o = o + jnp.einsum('bhgqc,bchd->bqhgd', p[..., W:W + C].astype(vc.dtype), vc)
        return o

    out = lax.map(one_block, (jnp.arange(nb), qb, kw, vw))
    return jnp.moveaxis(out, 0, 1).reshape(B, L, ATT_WIDTH)


def context_attention(qc, kc, vc, sink):
    B, C = qc.shape[0], qc.shape[1]
    s = jnp.einsum('bqhgd,bkhd->bhgqk', qc, kc).astype(jnp.float32) * ATT_SCALE
    s_sink = jnp.broadcast_to(sink.astype(jnp.float32)[None, :, :, None, None], s.shape[:-1] + (1,))
    p = jax.nn.softmax(jnp.concatenate([s, s_sink], axis=-1), axis=-1)[..., :C]
    o = jnp.einsum('bhgqk,bkhd->bqhgd', p.astype(vc.dtype), vc)
    return o.reshape(B, C, ATT_WIDTH)


def centred_shift(u, mu_prev, mu_next):
    prev = jnp.pad(u, ((0, 0), (1, 0), (0, 0)))[:, :-1]
    nxt = jnp.pad(u, ((0, 0), (0, 1), (0, 0)))[:, 1:]
    return u + mu_prev * (prev - u) + mu_next * (nxt - u)


def rwkv_features(u, v_first, P):
    u = u.astype(jnp.float32)
    B, T = u.shape[0], u.shape[1]
    r, k, v, wd, ad, gd = jnp.split(u, RWKV_SPLITS, axis=-1)
    if v_first is None:
        v_first = v
    else:
        v = v + (v_first - v) * jax.nn.sigmoid(P['v0'] + (v @ P['v1']) @ P['v2'])
    wd = wd.reshape(B, T, N_DIRS, LORA_DECAY)
    ad = ad.reshape(B, T, N_DIRS, LORA_ICLR)
    decay_in = P['w0'] + jnp.einsum('btdr,drc->btdc', jnp.tanh(wd), P['w2'])
    decay = jnp.exp(-jnp.exp(-jax.nn.softplus(-decay_in) - 0.5))
    a = jax.nn.sigmoid(P['a0'] + jnp.einsum('btdr,drc->btdc', ad, P['a2']))
    heads = lambda t: t.reshape(t.shape[:-1] + (RWKV_HEADS, HEAD_DIM))
    kk = heads(k * P['k_k'])
    kk = kk / jnp.maximum(jnp.sqrt(jnp.sum(kk * kk, axis=-1, keepdims=True)), 1e-12)
    k_dir = k[:, :, None, :] * (1.0 + (a - 1.0) * P['k_a'])
    g = jax.nn.sigmoid(gd) @ P['g2']
    f = {'r': heads(r), 'k': heads(k_dir), 'v': heads(v), 'decay': heads(decay),
         'a': heads(a), 'kk': kk, 'g': heads(g)}
    return f, v_first


def rwkv_scan(S0, r, decay, k, v, kk, a, reverse):
    tm = lambda t: jnp.moveaxis(t, 1, 0)
    emit = r is not None
    xs = (tm(decay), tm(k), tm(v), tm(kk), tm(kk * a))
    if emit:
        xs = xs + (tm(r),)

    def step(S, inp):
        w_t, k_t, v_t, kk_t, b_t = inp[:5]
        s_kk = jnp.einsum('bhvk,bhk->bhv', S, kk_t)
        S = (S * w_t[:, :, None, :] - s_kk[..., None] * b_t[:, :, None, :]
             + v_t[..., None] * k_t[:, :, None, :])
        y = jnp.einsum('bhvk,bhk->bhv', S, inp[5]) if emit else None
        return S, y

    S, ys = lax.scan(step, S0, xs, reverse=reverse)
    return S, (jnp.moveaxis(ys, 0, 1) if emit else None)


def rwkv_readout(y, f, P):
    B, T = y.shape[0], y.shape[1]
    mu = jnp.mean(y, axis=-1, keepdims=True)
    var = jnp.mean(jnp.square(y - mu), axis=-1, keepdims=True)
    gn = (y - mu) * lax.rsqrt(var + GN_EPS) * P['ln_w'] + P['ln_b']
    bonus = jnp.sum(f['r'][:, :, None] * f['k'] * P['r_k'], axis=(2, 4))[..., None] * f['v']
    return ((gn + bonus) * f['g']).reshape(B, T, RWKV_WIDTH)


def rwkv_mixer(fl, fc, P, emit_ctx):
    B = fl['v'].shape[0]
    S_zero = jnp.zeros((B, RWKV_HEADS, HEAD_DIM, HEAD_DIM), jnp.float32)
    ys_l, ys_c = [], []
    for d in range(N_DIRS):
        rev = d == 1
        S_ctx, yc = rwkv_scan(S_zero, fc['r'] if emit_ctx else None, fc['decay'][:, :, d],
                              fc['k'][:, :, d], fc['v'], fc['kk'], fc['a'][:, :, d], rev)
        _, yl = rwkv_scan(S_ctx, fl['r'], fl['decay'][:, :, d], fl['k'][:, :, d],
                          fl['v'], fl['kk'], fl['a'][:, :, d], rev)
        ys_l.append(yl)
        ys_c.append(yc)
    out_l = rwkv_readout(ys_l[0] + ys_l[1], fl, P)
    out_c = rwkv_readout(ys_c[0] + ys_c[1], fc, P) if emit_ctx else None
    return out_l, out_c


def route(h, router_w, router_b):
    T = h.shape[0]
    scores = jax.nn.sigmoid((h @ router_w).astype(jnp.float32))
    grouped = (scores + router_b).reshape(T, N_GROUPS, EXPERTS_PER_GROUP)
    group_score = jnp.sum(lax.top_k(grouped, GROUP_SCORE_K)[0], axis=-1)
    g_idx = jnp.argmax(group_score, axis=-1)
    in_group = jnp.take_along_axis(grouped, g_idx[:, None, None], axis=1)[:, 0]
    _, local = lax.top_k(in_group, TOP_K)
    e_idx = g_idx[:, None] * EXPERTS_PER_GROUP + local
    gates = jnp.take_along_axis(scores, e_idx, axis=1)
    gates = gates / jnp.sum(gates, axis=-1, keepdims=True)
    return e_idx, gates


def moe_ffn(h, router_w, router_b, w_gate, w_up, w_down):
    T, D = h.shape
    e_idx, gates = route(h, router_w, router_b)
    flat_e = e_idx.reshape(-1)
    n_rows = TOP_K * T
    nb = -(-n_rows // MOE_BLK) + N_EXPERTS
    counts = jnp.bincount(flat_e, length=N_EXPERTS)
    padded = (counts + MOE_BLK - 1) // MOE_BLK * MOE_BLK
    pad_ends = jnp.cumsum(padded)
    pad_starts = pad_ends - padded
    starts = jnp.cumsum(counts) - counts
    order = jnp.argsort(flat_e)
    sorted_e = flat_e[order]
    dest = pad_starts[sorted_e] + jnp.arange(n_rows) - starts[sorted_e]
    row_token = jnp.full((nb * MOE_BLK,), T, jnp.int32).at[dest].set((order // TOP_K).astype(jnp.int32))
    row_gate = jnp.zeros((nb * MOE_BLK,), jnp.float32).at[dest].set(gates.reshape(-1)[order])
    blk_expert = jnp.minimum(jnp.searchsorted(pad_ends, jnp.arange(nb) * MOE_BLK, side='right'),
                             N_EXPERTS - 1)
    h_pad = jnp.concatenate([h, jnp.zeros((1, D), h.dtype)], axis=0)
    xb = h_pad[row_token].reshape(nb, MOE_BLK, D)

    def expert_block(args):
        xe, e = args
        return (jax.nn.silu(xe @ w_gate[e]) * (xe @ w_up[e])) @ w_down[e]

    yb = lax.map(expert_block, (xb, blk_expert)).reshape(nb * MOE_BLK, D)
    out = jnp.zeros((T + 1, D), yb.dtype).at[row_token].add(yb * row_gate[:, None].astype(yb.dtype))
    return out[:T]


def hybrid_layer(xl, xc, c, c_ctx, vfl, vfc, P, router_w, router_b, cos, sin, is_last):
    B, L = xl.shape[0], xl.shape[1]
    C = xc.shape[1]
    mod_l = (jax.nn.silu(c) @ P['w_mod'] + P['b_mod'])[:, None, :]
    mod_c = (jax.nn.silu(c_ctx) @ P['w_mod'] + P['b_mod'])[None, None, :]
    sh_a, sc_a, ga, sh_f, sc_f, gf = jnp.split(mod_l, 6, axis=-1)
    csh_a, csc_a, cga, csh_f, csc_f, cgf = jnp.split(mod_c, 6, axis=-1)

    pl = modulate(rmsnorm(xl, P['norm_mix']), sh_a, sc_a) @ P['w_in']
    pc = modulate(rmsnorm(xc, P['norm_mix']), csh_a, csc_a) @ P['w_in']
    ql, kl, vl, rwl = jnp.split(pl, ATT_SPLITS, axis=-1)
    qc, kc, vc, rwc = jnp.split(pc, ATT_SPLITS, axis=-1)

    ql = apply_rope(ql.reshape(B, L, ATT_HEADS, HEAD_DIM), cos, sin)
    ql = ql.reshape(B, L, ATT_KV_HEADS, ATT_GROUP, HEAD_DIM)
    kl = apply_rope(kl.reshape(B, L, ATT_KV_HEADS, HEAD_DIM), cos, sin)
    vl = vl.reshape(B, L, ATT_KV_HEADS, HEAD_DIM)
    kc = kc.reshape(B, C, ATT_KV_HEADS, HEAD_DIM)
    vc = vc.reshape(B, C, ATT_KV_HEADS, HEAD_DIM)
    att_l = windowed_attention(ql, kl, vl, kc, vc, P['sink'])

    fc, vfc = rwkv_features(centred_shift(rwc, P['mu_prev'], P['mu_next']), vfc, P)
    fl, vfl = rwkv_features(centred_shift(rwl, P['mu_prev'], P['mu_next']), vfl, P)
    rwkv_l, rwkv_c = rwkv_mixer(fl, fc, P, not is_last)

    mix_l = jnp.concatenate([att_l, rwkv_l.astype(att_l.dtype)], axis=-1) @ P['w_out']
    xl = xl + ga * mix_l
    if not is_last:
        att_c = context_attention(qc.reshape(B, C, ATT_KV_HEADS, ATT_GROUP, HEAD_DIM), kc, vc, P['sink'])
        mix_c = jnp.concatenate([att_c, rwkv_c.astype(att_c.dtype)], axis=-1) @ P['w_out']
        xc = xc + cga * mix_c

    hfl = modulate(rmsnorm(xl, P['norm_ffn']), sh_f, sc_f).reshape(B * L, D_MODEL)
    if is_last:
        y = moe_ffn(hfl, router_w, router_b, P['e_gate'], P['e_up'], P['e_down'])
        xl = xl + gf * y.reshape(B, L, D_MODEL)
    else:
        hfc = modulate(rmsnorm(xc, P['norm_ffn']), csh_f, csc_f).reshape(B * C, D_MODEL)
        y = moe_ffn(jnp.concatenate([hfl, hfc], axis=0), router_w, router_b,
                    P['e_gate'], P['e_up'], P['e_down'])
        xl = xl + gf * y[:B * L].reshape(B, L, D_MODEL)
        xc = xc + cgf * y[B * L:].reshape(B, C, D_MODEL)
    return xl, xc, vfl, vfc


def setup_inputs(seed: int = 0) -> dict:
    key = jax.random.key(seed)
    ks = jax.random.split(key, 32)
    D = D_MODEL
    RW = RWKV_WIDTH
    E = N_EXPERTS
    nrm = lambda k, shape, s: jax.random.normal(k, shape, jnp.float32) * s
    uni = lambda k, shape, lo, hi: jax.random.uniform(k, shape, jnp.float32, lo, hi)
    return {
        'x': nrm(ks[0], (BATCH, SEQ, D), 1.0),
        'c': nrm(ks[1], (BATCH, D), 1.0),
        'ctx': nrm(ks[2], (BATCH, CTX_LEN, D), 1.0),
        'c_ctx': nrm(ks[3], (D,), 1.0),
        'w_mod': nrm(ks[4], (DEPTH, D, 6 * D), 0.5 * D ** -0.5),
        'b_mod': nrm(ks[5], (DEPTH, 6 * D), 0.02),
        'norm_mix_g': 1.0 + nrm(ks[6], (DEPTH, D), 0.02),
        'norm_ffn_g': 1.0 + nrm(ks[7], (DEPTH, D), 0.02),
        'w_in': nrm(ks[8], (DEPTH, D, IN_COLS), D ** -0.5),
        'w_out': nrm(ks[9], (DEPTH, D_MIX, D), D_MIX ** -0.5),
        'att_sink': nrm(ks[10], (DEPTH, ATT_HEADS), 0.5),
        'shift_mu_prev': uni(ks[11], (DEPTH, RWKV_COLS), 0.0, 0.5),
        'shift_mu_next': uni(ks[12], (DEPTH, RWKV_COLS), 0.0, 0.5),
        'decay_w0': uni(ks[13], (DEPTH, N_DIRS, RW), -2.0, 1.0),
        'decay_w2': nrm(ks[14], (DEPTH, N_DIRS, LORA_DECAY, RW), 0.5 * LORA_DECAY ** -0.5),
        'iclr_a0': nrm(ks[15], (DEPTH, N_DIRS, RW), 0.1),
        'iclr_a2': nrm(ks[16], (DEPTH, N_DIRS, LORA_ICLR, RW), 0.5 * LORA_ICLR ** -0.5),
        'vres_v0': nrm(ks[17], (DEPTH - 1, RW), 0.1),
        'vres_v1': nrm(ks[18], (DEPTH - 1, RW, LORA_VRES), RW ** -0.5),
        'vres_v2': nrm(ks[19], (DEPTH - 1, LORA_VRES, RW), 0.5 * LORA_VRES ** -0.5),
        'gate_g2': nrm(ks[20], (DEPTH, LORA_GATE, RW), LORA_GATE ** -0.5),
        'k_k': 0.85 + nrm(ks[21], (DEPTH, RW), 0.02),
        'k_a': 1.0 + nrm(ks[22], (DEPTH, RW), 0.02),
        'r_k': nrm(ks[23], (DEPTH, RWKV_HEADS, HEAD_DIM), 0.1),
        'ln_x_w': 1.0 + nrm(ks[24], (DEPTH, RW), 0.02),
        'ln_x_b': nrm(ks[25], (DEPTH, RW), 0.02),
        'router_w': nrm(ks[26], (D, E), D ** -0.5),
        'router_b': nrm(ks[27], (E,), 0.01),
        'expert_w_gate': nrm(ks[28], (DEPTH, E, D, D_EXPERT), D ** -0.5),
        'expert_w_up': nrm(ks[29], (DEPTH, E, D, D_EXPERT), D ** -0.5),
        'expert_w_down': nrm(ks[30], (DEPTH, E, D_EXPERT, D), D_EXPERT ** -0.5),
        'final_norm_g': 1.0 + nrm(ks[31], (D,), 0.02),
    }


def reference(x, c, ctx, c_ctx, w_mod, b_mod, norm_mix_g, norm_ffn_g, w_in, w_out, att_sink,
              shift_mu_prev, shift_mu_next, decay_w0, decay_w2, iclr_a0, iclr_a2,
              vres_v0, vres_v1, vres_v2, gate_g2, k_k, k_a, r_k, ln_x_w, ln_x_b,
              router_w, router_b, expert_w_gate, expert_w_up, expert_w_down, final_norm_g):
    L = x.shape[1]
    ROWS = L // GRID_W
    cos, sin = axial_rope_tables(ROWS)
    xl, xc = x, ctx
    vfl, vfc = None, None
    for l in range(DEPTH):
        P = {'w_mod': w_mod[l], 'b_mod': b_mod[l], 'norm_mix': norm_mix_g[l], 'norm_ffn': norm_ffn_g[l],
             'w_in': w_in[l], 'w_out': w_out[l],
             'sink': att_sink[l].reshape(ATT_KV_HEADS, ATT_GROUP),
             'mu_prev': shift_mu_prev[l], 'mu_next': shift_mu_next[l],
             'w0': decay_w0[l], 'w2': decay_w2[l], 'a0': iclr_a0[l], 'a2': iclr_a2[l],
             'g2': gate_g2[l], 'k_k': k_k[l], 'k_a': k_a[l], 'r_k': r_k[l],
             'ln_w': ln_x_w[l].reshape(RWKV_HEADS, HEAD_DIM),
             'ln_b': ln_x_b[l].reshape(RWKV_HEADS, HEAD_DIM),
             'e_gate': expert_w_gate[l], 'e_up': expert_w_up[l], 'e_down': expert_w_down[l]}
        if l > 0:
            P['v0'] = vres_v0[l - 1]
            P['v1'] = vres_v1[l - 1]
            P['v2'] = vres_v2[l - 1]
        xl, xc, vfl, vfc = hybrid_layer(xl, xc, c, c_ctx, vfl, vfc, P, router_w, router_b,
                                        cos, sin, l == DEPTH - 1)
    return rmsnorm(xl, final_norm_g)
```

```python
import functools
import math

import jax
import jax.numpy as jnp
from jax import lax
from jax.experimental import pallas as pl
from jax.experimental.pallas import tpu as pltpu

f32 = jnp.float32
bf16 = jnp.bfloat16
i32 = jnp.int32

D = 1024
SEQ = 4096
CTX = 256
TB = CTX + SEQ
GRID_W = 64
HEAD_DIM = 64
ATT_WIDTH = 512
ATT_HEADS = 8
KV_HEADS = 2
ATT_GROUP = ATT_HEADS // KV_HEADS
KV_WIDTH = KV_HEADS * HEAD_DIM
RW = 512
RWKV_HEADS = 8
LORA_DECAY = 64
LORA_ICLR = 64
LORA_VRES = 32
LORA_GATE = 128
RWKV_COLS = 3 * RW + 2 * (LORA_DECAY + LORA_ICLR) + LORA_GATE
ATT_COLS = ATT_WIDTH + 2 * KV_WIDTH
IN_COLS = ATT_COLS + RWKV_COLS
N_EXPERTS = 16
N_GROUPS = 4
EXPERTS_PER_GROUP = 4
TOP_K = 2
MOE_BLK = 256
NORM_EPS = 1e-6
GN_EPS = 64e-5
NEG_INF = -1e30
ATT_SCALE = HEAD_DIM ** -0.5
ROPE_BASE = 10000.0
ROPE_FREQS = HEAD_DIM // 4

LANES = 128
TM = 256
QB = 128
CH = 64
HG = 4
GW = HG * HEAD_DIM
VMEM_LIMIT = 48 * 1024 * 1024


def _cparams(sem):
    return pltpu.CompilerParams(dimension_semantics=sem, vmem_limit_bytes=VMEM_LIMIT)


def _sigmoid(x):
    return 1.0 / (1.0 + jnp.exp(-x))


def _div_pow2(x, n):
    assert n & (n - 1) == 0
    return lax.shift_right_logical(x, n.bit_length() - 1)


def _mod_pow2(x, n):
    assert n & (n - 1) == 0
    return lax.bitwise_and(x, n - 1)


def _dot(a, b):
    return jnp.dot(a, b, preferred_element_type=f32)


def _dot_nt(a, b):
    return lax.dot_general(a, b, (((1,), (1,)), ((), ())), preferred_element_type=f32)


def _dot_tn(a, b):
    return lax.dot_general(a, b, (((0,), (0,)), ((), ())), preferred_element_type=f32)


def _split2(x):
    hi = x.astype(bf16)
    lo = (x - hi.astype(f32)).astype(bf16)
    return hi, lo


def _split3(x):
    hi = x.astype(bf16)
    r1 = x - hi.astype(f32)
    mid = r1.astype(bf16)
    lo = (r1 - mid.astype(f32)).astype(bf16)
    return hi, mid, lo


def _dot_exact_rhs(x, m):
    hi, mid, lo = _split3(x)
    return _dot(hi, m) + _dot(mid, m) + _dot(lo, m)


def _dot_exact_lhs(m, x):
    hi, mid, lo = _split3(x)
    return _dot(m, hi) + _dot(m, mid) + _dot(m, lo)


def _rmsnorm(x, g):
    ms = jnp.mean(x * x, axis=-1, keepdims=True)
    return x * lax.rsqrt(ms + NORM_EPS) * g


def _head_ones():
    r = _div_pow2(lax.broadcasted_iota(i32, (RW, RW), 0), HEAD_DIM)
    c = _div_pow2(lax.broadcasted_iota(i32, (RW, RW), 1), HEAD_DIM)
    return (r == c).astype(bf16)


def _mod_kernel(c_ref, w_ref, b_ref, o_ref):
    c = c_ref[...]
    s = (c * _sigmoid(c)).astype(bf16)
    o_ref[0] = _dot(s, w_ref[0].astype(bf16)) + b_ref[0]


def _mod_call(cond, w_mod, b_mod):
    nb = cond.shape[0]
    depth = w_mod.shape[0]
    tn = 1024
    return pl.pallas_call(
        _mod_kernel,
        grid=(depth, 6 * D // tn),
        in_specs=[
            pl.BlockSpec((nb, D), lambda l, j: (0, 0)),
            pl.BlockSpec((1, D, tn), lambda l, j: (l, 0, j)),
            pl.BlockSpec((1, 1, tn), lambda l, j: (l, 0, j)),
        ],
        out_specs=pl.BlockSpec((1, nb, tn), lambda l, j: (l, 0, j)),
        out_shape=jax.ShapeDtypeStruct((depth, nb, 6 * D), f32),
        compiler_params=_cparams(("arbitrary", "arbitrary")),
        name="mod",
    )(cond, w_mod, b_mod.reshape(depth, 1, 6 * D))


def _mod_index(nbatch, ctx_tiles):
    return lambda b, i: (jnp.where(i < ctx_tiles, nbatch, b), 0, 0)


def _in_proj_kernel(x_ref, mod_ref, g_ref, w_ref, cos_ref, sin_ref, q_ref, k_ref, v_ref, rw_ref):
    x = x_ref[0]
    tm = x.shape[0]
    h = _rmsnorm(x, g_ref[...])
    sh = mod_ref[0, :, 0:D]
    sc = mod_ref[0, :, D:2 * D]
    h = (h * (1.0 + sc) + sh).astype(bf16)
    p = _dot(h, w_ref[...])
    cos = cos_ref[...]
    sin = sin_ref[...]
    lane = lax.broadcasted_iota(i32, (tm, LANES), 1)
    first_half = _mod_pow2(lane, 2 * ROPE_FREQS) < ROPE_FREQS

    def rope(t):
        rot = jnp.where(first_half, -pltpu.roll(t, LANES - ROPE_FREQS, 1), pltpu.roll(t, ROPE_FREQS, 1))
        return t * cos + rot * sin

    for j in range(ATT_WIDTH // LANES):
        t = rope(p[:, j * LANES:(j + 1) * LANES]).astype(bf16)
        q_ref[0, 2 * j] = t[:, :HEAD_DIM]
        q_ref[0, 2 * j + 1] = t[:, HEAD_DIM:]
    kt = rope(p[:, ATT_WIDTH:ATT_WIDTH + KV_WIDTH]).astype(bf16)
    vt = p[:, ATT_WIDTH + KV_WIDTH:ATT_COLS].astype(bf16)
    for hh in range(KV_HEADS):
        k_ref[0, hh] = kt[:, hh * HEAD_DIM:(hh + 1) * HEAD_DIM]
        v_ref[0, hh] = vt[:, hh * HEAD_DIM:(hh + 1) * HEAD_DIM]
    rw_ref[0] = p[:, ATT_COLS:]


def _in_proj_call(x, mod, g, w_in, cos, sin):
    nbatch, tb, _ = x.shape
    nt = tb // TM
    return pl.pallas_call(
        _in_proj_kernel,
        grid=(nbatch, nt),
        in_specs=[
            pl.BlockSpec((1, TM, D), lambda b, i: (b, i, 0)),
            pl.BlockSpec((1, 1, 6 * D), _mod_index(nbatch, CTX // TM)),
            pl.BlockSpec((1, D), lambda b, i: (0, 0)),
            pl.BlockSpec((D, IN_COLS), lambda b, i: (0, 0)),
            pl.BlockSpec((TM, LANES), lambda b, i: (i, 0)),
            pl.BlockSpec((TM, LANES), lambda b, i: (i, 0)),
        ],
        out_specs=[
            pl.BlockSpec((1, ATT_HEADS, TM, HEAD_DIM), lambda b, i: (b, 0, i, 0)),
            pl.BlockSpec((1, KV_HEADS, TM, HEAD_DIM), lambda b, i: (b, 0, i, 0)),
            pl.BlockSpec((1, KV_HEADS, TM, HEAD_DIM), lambda b, i: (b, 0, i, 0)),
            pl.BlockSpec((1, TM, RWKV_COLS), lambda b, i: (b, i, 0)),
        ],
        out_shape=[
            jax.ShapeDtypeStruct((nbatch, ATT_HEADS, tb, HEAD_DIM), bf16),
            jax.ShapeDtypeStruct((nbatch, KV_HEADS, tb, HEAD_DIM), bf16),
            jax.ShapeDtypeStruct((nbatch, KV_HEADS, tb, HEAD_DIM), bf16),
            jax.ShapeDtypeStruct((nbatch, tb, RWKV_COLS), f32),
        ],
        compiler_params=_cparams(("parallel", "arbitrary")),
        name="in_proj",
    )(x, mod, g, w_in, cos, sin)


def _attn_kernel(sink_ref, q_ref, kp_ref, kc_ref, kn_ref, vp_ref, vc_ref, vn_ref, kx_ref, vx_ref, o_ref,
                 *, nblk, ctx_blks):
    i = pl.program_id(1)
    is_lat = i >= ctx_blks
    prev_ok = jnp.logical_and(is_lat, i - 1 >= ctx_blks)
    next_ok = jnp.logical_and(is_lat, i + 1 <= nblk - 1)
    rows = ATT_GROUP * QB
    qi = _mod_pow2(lax.broadcasted_iota(i32, (rows, QB), 0), QB)
    kj = lax.broadcasted_iota(i32, (rows, QB), 1)
    mask_p = jnp.logical_and(kj >= qi, prev_ok)
    mask_n = jnp.logical_and(kj <= qi, next_ok)
    row_head = _div_pow2(lax.broadcasted_iota(i32, (rows, 1), 0), QB)
    outs = []
    for h in range(KV_HEADS):
        qh = q_ref[0, ATT_GROUP * h:ATT_GROUP * (h + 1)].reshape(rows, HEAD_DIM)
        s_p = jnp.where(mask_p, _dot_nt(qh, kp_ref[0, h]) * ATT_SCALE, NEG_INF)
        s_c = jnp.where(is_lat, _dot_nt(qh, kc_ref[0, h]) * ATT_SCALE, NEG_INF)
        s_n = jnp.where(mask_n, _dot_nt(qh, kn_ref[0, h]) * ATT_SCALE, NEG_INF)
        s_x = _dot_nt(qh, kx_ref[0, h]) * ATT_SCALE
        sink = jnp.zeros((rows, 1), f32)
        for g in range(ATT_GROUP):
            sink = jnp.where(row_head == g, sink_ref[ATT_GROUP * h + g], sink)
        m = jnp.maximum(jnp.max(s_p, axis=-1, keepdims=True), jnp.max(s_c, axis=-1, keepdims=True))
        m = jnp.maximum(m, jnp.max(s_n, axis=-1, keepdims=True))
        m = jnp.maximum(m, jnp.max(s_x, axis=-1, keepdims=True))
        m = jnp.maximum(m, sink)
        e_p = jnp.exp(s_p - m)
        e_c = jnp.exp(s_c - m)
        e_n = jnp.exp(s_n - m)
        e_x = jnp.exp(s_x - m)
        den = (jnp.sum(e_p, axis=-1, keepdims=True) + jnp.sum(e_c, axis=-1, keepdims=True)
               + jnp.sum(e_n, axis=-1, keepdims=True) + jnp.sum(e_x, axis=-1, keepdims=True)
               + jnp.exp(sink - m))
        o = (_dot(e_p.astype(bf16), vp_ref[0, h]) + _dot(e_c.astype(bf16), vc_ref[0, h])
             + _dot(e_n.astype(bf16), vn_ref[0, h]) + _dot(e_x.astype(bf16), vx_ref[0, h]))
        o = o / den
        outs.extend(o[g * QB:(g + 1) * QB] for g in range(ATT_GROUP))
    o_ref[0] = jnp.concatenate(outs, axis=1).astype(bf16)


def _attn_call(sink, q, k, v):
    nbatch, _, tb, _ = q.shape
    nblk = tb // QB
    ctx_blks = CTX // QB
    kv_blk = (1, KV_HEADS, QB, HEAD_DIM)
    prev_map = lambda b, i: (b, 0, jnp.maximum(i - 1, 0), 0)
    cur_map = lambda b, i: (b, 0, i, 0)
    next_map = lambda b, i: (b, 0, jnp.minimum(i + 1, nblk - 1), 0)
    ctx_spec = pl.BlockSpec((1, KV_HEADS, CTX, HEAD_DIM), lambda b, i: (b, 0, 0, 0))
    return pl.pallas_call(
        functools.partial(_attn_kernel, nblk=nblk, ctx_blks=ctx_blks),
        grid=(nbatch, nblk),
        in_specs=[
            pl.BlockSpec(memory_space=pltpu.SMEM),
            pl.BlockSpec((1, ATT_HEADS, QB, HEAD_DIM), cur_map),
            pl.BlockSpec(kv_blk, prev_map), pl.BlockSpec(kv_blk, cur_map), pl.BlockSpec(kv_blk, next_map),
            pl.BlockSpec(kv_blk, prev_map), pl.BlockSpec(kv_blk, cur_map), pl.BlockSpec(kv_blk, next_map),
            ctx_spec, ctx_spec,
        ],
        out_specs=pl.BlockSpec((1, QB, ATT_WIDTH), lambda b, i: (b, i, 0)),
        out_shape=jax.ShapeDtypeStruct((nbatch, tb, ATT_WIDTH), bf16),
        compiler_params=_cparams(("parallel", "arbitrary")),
        name="attention",
    )(sink, q, k, k, k, v, v, v, k, v)


def _feat_kernel(*refs, nt, ctx_tiles, has_vres):
    if has_vres:
        (rw_ref, hp_ref, hn_ref, vf_ref, mu_ref, vec_ref, w2_ref, a2_ref, g2_ref, v1_ref, v2_ref,
         r_ref, v_ref, kk_ref, bv_ref, g_ref, kd_ref, lw_ref, bd_ref) = refs
    else:
        (rw_ref, hp_ref, hn_ref, mu_ref, vec_ref, w2_ref, a2_ref, g2_ref,
         r_ref, v_ref, kk_ref, bv_ref, g_ref, kd_ref, lw_ref, bd_ref) = refs
    i = pl.program_id(1)
    u0 = rw_ref[0]
    tm = u0.shape[0]
    prev_zero = jnp.logical_or(i == 0, i == ctx_tiles)
    next_zero = jnp.logical_or(i == ctx_tiles - 1, i == nt - 1)
    halo_p = jnp.where(prev_zero, 0.0, hp_ref[0, 7:8, :])
    halo_n = jnp.where(next_zero, 0.0, hn_ref[0, 0:1, :])
    row = lax.broadcasted_iota(i32, (tm, 1), 0)
    prev = jnp.where(row == 0, halo_p, pltpu.roll(u0, 1, 0))
    nxt = jnp.where(row == tm - 1, halo_n, pltpu.roll(u0, tm - 1, 0))
    u = u0 + mu_ref[0:1, :] * (prev - u0) + mu_ref[1:2, :] * (nxt - u0)

    r = u[:, 0:RW]
    k = u[:, RW:2 * RW]
    v = u[:, 2 * RW:3 * RW]
    wd = u[:, 3 * RW:3 * RW + 2 * LORA_DECAY]
    ad = u[:, 3 * RW + 2 * LORA_DECAY:3 * RW + 2 * (LORA_DECAY + LORA_ICLR)]
    gd = u[:, 3 * RW + 2 * (LORA_DECAY + LORA_ICLR):]
    k_k = vec_ref[0:1, :]
    k_a = vec_ref[1:2, :]
    r_k = vec_ref[2:3, :]
    ones = _head_ones()

    if has_vres:
        lo = _dot(v.astype(bf16), v1_ref[...])
        gate = _sigmoid(vec_ref[3:4, :] + _dot(lo.astype(bf16), v2_ref[...]))
        v = v + (vf_ref[0] - v) * gate
    decay_in = _dot(jnp.tanh(wd).astype(bf16), w2_ref[...])
    a_in = _dot(ad.astype(bf16), a2_ref[...])
    kk = k * k_k
    n2 = _dot_exact_rhs(kk * kk, ones)
    kk = kk / jnp.maximum(jnp.sqrt(n2), 1e-12)
    g = _dot(_sigmoid(gd).astype(bf16), g2_ref[...])
    ksum = jnp.zeros_like(k)
    for d in range(2):
        w0 = vec_ref[4 + d:5 + d, :]
        a0 = vec_ref[6 + d:7 + d, :]
        lw = -_sigmoid(w0 + decay_in[:, d * RW:(d + 1) * RW]) * math.exp(-0.5)
        a = _sigmoid(a0 + a_in[:, d * RW:(d + 1) * RW])
        kd = k * (1.0 + (a - 1.0) * k_a)
        ksum = ksum + kd
        kd_ref[d, 0] = kd
        lw_ref[d, 0] = lw
        bd_ref[d, 0] = kk * a
    bonus = _dot_exact_rhs(r * ksum * r_k, ones)
    r_ref[0] = r
    v_ref[0] = v
    kk_ref[0] = kk
    bv_ref[0] = bonus * v
    g_ref[0] = g


def _feat_call(rw, v_first, mu, vec, w2bd, a2bd, g2, v1, v2):
    nbatch, tb, _ = rw.shape
    nt = tb // TM
    has_vres = v_first is not None
    sub = TM // 8
    tile = lambda w: pl.BlockSpec((1, TM, w), lambda b, i: (b, i, 0))
    full = lambda a: pl.BlockSpec(a.shape, lambda b, i: (0,) * a.ndim)
    in_specs = [
        tile(RWKV_COLS),
        pl.BlockSpec((1, 8, RWKV_COLS), lambda b, i: (b, jnp.maximum(i * sub - 1, 0), 0)),
        pl.BlockSpec((1, 8, RWKV_COLS), lambda b, i: (b, jnp.minimum((i + 1) * sub, tb // 8 - 1), 0)),
    ]
    args = [rw, rw, rw]
    if has_vres:
        in_specs.append(tile(RW))
        args.append(v_first)
    consts = [mu, vec, w2bd, a2bd, g2] + ([v1, v2] if has_vres else [])
    in_specs += [full(a) for a in consts]
    args += consts
    dir_spec = pl.BlockSpec((2, 1, TM, RW), lambda b, i: (0, b, i, 0))
    tok = jax.ShapeDtypeStruct((nbatch, tb, RW), f32)
    dtok = jax.ShapeDtypeStruct((2, nbatch, tb, RW), f32)
    return pl.pallas_call(
        functools.partial(_feat_kernel, nt=nt, ctx_tiles=CTX // TM, has_vres=has_vres),
        grid=(nbatch, nt),
        in_specs=in_specs,
        out_specs=[tile(RW)] * 5 + [dir_spec] * 3,
        out_shape=[tok] * 5 + [dtok] * 3,
        compiler_params=_cparams(("parallel", "arbitrary")),
        name="rwkv_features",
    )(*args)


def _block_diag_rows(x, width):
    cb = _div_pow2(lax.broadcasted_iota(i32, x.shape, 1), width)
    return jnp.concatenate([jnp.where(cb == h, x, jnp.zeros_like(x)) for h in range(HG)], axis=0)


def _scan_chunk(s_prev, r, v, kk, k, lw, b, reverse):
    row = lax.broadcasted_iota(i32, (CH, CH), 0)
    col = lax.broadcasted_iota(i32, (CH, CH), 1)
    incl = (col >= row) if reverse else (col <= row)
    strict = (col > row) if reverse else (col < row)
    gam = _dot_exact_lhs(incl.astype(bf16), lw)
    gtot = jnp.sum(lw, axis=0, keepdims=True)
    e_neg = jnp.exp(-gam)
    a_t = (-kk * jnp.exp(gam - lw)).astype(bf16)
    r_t = (r * jnp.exp(gam)).astype(bf16)
    b_t = b * e_neg
    k_t = k * e_neg
    ar = jnp.concatenate([a_t, r_t], axis=0)
    bk = jnp.concatenate([_block_diag_rows(b_t.astype(bf16), HEAD_DIM),
                          _block_diag_rows(k_t.astype(bf16), HEAD_DIM)], axis=0)
    gram = _dot_nt(ar, bk)
    wide = HG * CH
    row_t = lax.broadcasted_iota(i32, (CH, wide), 0)
    col_t = _mod_pow2(lax.broadcasted_iota(i32, (CH, wide), 1), CH)
    incl_t = (col_t >= row_t) if reverse else (col_t <= row_t)
    strict_t = (col_t > row_t) if reverse else (col_t < row_t)
    l_ab = jnp.where(strict_t, gram[:CH, :wide], 0.0)
    l_ak = jnp.where(strict_t, gram[:CH, wide:], 0.0)
    q_b = jnp.where(incl_t, gram[CH:, :wide], 0.0)
    q_k = jnp.where(incl_t, gram[CH:, wide:], 0.0)
    p = l_ab
    t = (row_t == col_t).astype(f32) + p
    n = 2
    while n < CH:
        pb = p.astype(bf16)
        p = _dot(pb, _block_diag_rows(pb, CH))
        t = t + _dot(t.astype(bf16), _block_diag_rows(p.astype(bf16), CH))
        n *= 2
    ars = _dot_nt(ar, s_prev.astype(bf16))
    v_bd = _block_diag_rows(v.astype(bf16), HEAD_DIM)
    rhs = ars[:CH] + _dot(l_ak.astype(bf16), v_bd)
    u = _dot(t.astype(bf16), _block_diag_rows(rhs.astype(bf16), HEAD_DIM))
    u_bd = _block_diag_rows(u.astype(bf16), HEAD_DIM)
    y = ars[CH:] + _dot(jnp.concatenate([q_k, q_b], axis=1).astype(bf16),
                        jnp.concatenate([v_bd, u_bd], axis=0))
    eg = jnp.exp(gtot)
    vu = jnp.concatenate([v, u], axis=0).astype(bf16)
    kb = jnp.concatenate([k_t * eg, b_t * eg], axis=0).astype(bf16)
    s_add = _dot_tn(vu, kb)
    rb = _div_pow2(lax.broadcasted_iota(i32, (GW, GW), 0), HEAD_DIM)
    cb = _div_pow2(lax.broadcasted_iota(i32, (GW, GW), 1), HEAD_DIM)
    s_new = s_prev * eg + jnp.where(rb == cb, s_add, 0.0)
    return s_new, y


def _scan_kernel(rf_ref, vf_ref, kkf_ref, kf_ref, lwf_ref, bf_ref,
                 rb_ref, vb_ref, kkb_ref, kb_ref, lwb_ref, bb_ref,
                 yf_ref, yb_ref, s_ref):
    @pl.when(pl.program_id(1) == 0)
    def _():
        s_ref[...] = jnp.zeros_like(s_ref)

    dirs = ((rf_ref, vf_ref, kkf_ref, kf_ref, lwf_ref, bf_ref, yf_ref),
            (rb_ref, vb_ref, kkb_ref, kb_ref, lwb_ref, bb_ref, yb_ref))
    for d, (r_ref, v_ref, kk_ref, k_ref, lw_ref, b_ref, y_ref) in enumerate(dirs):
        for g in range(RW // GW):
            sl = slice(g * GW, (g + 1) * GW)
            s_new, y = _scan_chunk(s_ref[d, g], r_ref[0, :, sl], v_ref[0, :, sl], kk_ref[0, :, sl],
                                   k_ref[0, 0, :, sl], lw_ref[0, 0, :, sl], b_ref[0, 0, :, sl], d == 1)
            s_ref[d, g] = s_new
            y_ref[0, :, sl] = y


def _scan_call(r, v, kk, kd, lw, bd):
    nbatch, tb, _ = r.shape
    nc = tb // CH
    cc = CTX // CH
    rev = lambda j: jnp.where(j < cc, cc - 1 - j, nc - 1 + cc - j)
    tok_f = pl.BlockSpec((1, CH, RW), lambda b, j: (b, j, 0))
    tok_b = pl.BlockSpec((1, CH, RW), lambda b, j: (b, rev(j), 0))
    dir_f = pl.BlockSpec((1, 1, CH, RW), lambda b, j: (0, b, j, 0))
    dir_b = pl.BlockSpec((1, 1, CH, RW), lambda b, j: (1, b, rev(j), 0))
    out = jax.ShapeDtypeStruct((nbatch, tb, RW), f32)
    return pl.pallas_call(
        _scan_kernel,
        grid=(nbatch, nc),
        in_specs=[tok_f, tok_f, tok_f, dir_f, dir_f, dir_f, tok_b, tok_b, tok_b, dir_b, dir_b, dir_b],
        out_specs=[tok_f, tok_b],
        out_shape=[out, out],
        scratch_shapes=[pltpu.VMEM((2, RW // GW, GW, GW), f32)],
        compiler_params=_cparams(("parallel", "arbitrary")),
        name="rwkv_scan",
    )(r, v, kk, kd, lw, bd, r, v, kk, kd, lw, bd)


def _mix_kernel(x_ref, att_ref, yf_ref, yb_ref, bv_ref, g_ref, mod_ref, ln_ref, wo_ref, gf_ref, wr_ref,
                xo_ref, hf_ref, sc_ref):
    ones = _head_ones()
    y = yf_ref[0] + yb_ref[0]
    inv = 1.0 / HEAD_DIM
    mu = _dot_exact_rhs(y, ones) * inv
    dlt = y - mu
    var = _dot_exact_rhs(dlt * dlt, ones) * inv
    gn = dlt * lax.rsqrt(var + GN_EPS) * ln_ref[0:1, :] + ln_ref[1:2, :]
    rwk = ((gn + bv_ref[0]) * g_ref[0]).astype(bf16)
    mix = _dot(att_ref[0], wo_ref[0:ATT_WIDTH, :]) + _dot(rwk, wo_ref[ATT_WIDTH:, :])
    gate_a = mod_ref[0, :, 2 * D:3 * D]
    xn = x_ref[0] + gate_a * mix
    xo_ref[0] = xn
    sh = mod_ref[0, :, 3 * D:4 * D]
    sc = mod_ref[0, :, 4 * D:5 * D]
    hf = _rmsnorm(xn, gf_ref[...]) * (1.0 + sc) + sh
    hf_ref[0] = hf.astype(bf16)
    h_hi, h_lo = _split2(hf)
    logits = _dot(h_hi, wr_ref[0]) + _dot(h_lo, wr_ref[0]) + _dot(h_hi, wr_ref[1])
    sc_ref[0] = _sigmoid(logits)


def _mix_call(x, att, yf, yb, bv, g, mod, ln, w_out, g_ffn, w_router):
    nbatch, tb, _ = x.shape
    nt = tb // TM
    tile = lambda w: pl.BlockSpec((1, TM, w), lambda b, i: (b, i, 0))
    full = lambda a: pl.BlockSpec(a.shape, lambda b, i: (0,) * a.ndim)
    return pl.pallas_call(
        _mix_kernel,
        grid=(nbatch, nt),
        in_specs=[tile(D), tile(ATT_WIDTH), tile(RW), tile(RW), tile(RW), tile(RW),
                  pl.BlockSpec((1, 1, 6 * D), _mod_index(nbatch, CTX // TM)),
                  full(ln), full(w_out), full(g_ffn), full(w_router)],
        out_specs=[tile(D), tile(D), tile(LANES)],
        out_shape=[jax.ShapeDtypeStruct((nbatch, tb, D), f32),
                   jax.ShapeDtypeStruct((nbatch, tb, D), bf16),
                   jax.ShapeDtypeStruct((nbatch, tb, LANES), f32)],
        compiler_params=_cparams(("parallel", "arbitrary")),
        name="mix_out",
    )(x, att, yf, yb, bv, g, mod, ln, w_out, g_ffn, w_router)


def _ffn_kernel(be_ref, nu_ref, x_ref, wg_ref, wu_ref, wd_ref, y_ref):
    i = pl.program_id(0)

    @pl.when(i < nu_ref[0])
    def _():
        x = x_ref[...]
        gt = _dot(x, wg_ref[0])
        up = _dot(x, wu_ref[0])
        hid = (gt * _sigmoid(gt) * up).astype(bf16)
        y_ref[...] = _dot(hid, wd_ref[0])

    @pl.when(i >= nu_ref[0])
    def _():
        y_ref[...] = jnp.zeros_like(y_ref)


def _ffn_call(blk_expert, n_used, x_sorted, wg, wu, wd):
    nrows = x_sorted.shape[0]
    nblk = nrows // MOE_BLK
    wspec = pl.BlockSpec((1, D, D), lambda i, be, nu: (be[i], 0, 0))
    return pl.pallas_call(
        _ffn_kernel,
        grid_spec=pltpu.PrefetchScalarGridSpec(
            num_scalar_prefetch=2,
            grid=(nblk,),
            in_specs=[pl.BlockSpec((MOE_BLK, D), lambda i, be, nu: (i, 0)), wspec, wspec, wspec],
            out_specs=pl.BlockSpec((MOE_BLK, D), lambda i, be, nu: (i, 0)),
        ),
        out_shape=jax.ShapeDtypeStruct((nrows, D), f32),
        compiler_params=_cparams(("arbitrary",)),
        name="moe_ffn",
    )(blk_expert, n_used, x_sorted, wg, wu, wd)


def _combine_kernel(x_ref, y_ref, mod_ref, o_ref):
    o_ref[0] = x_ref[0] + mod_ref[0, :, 5 * D:6 * D] * y_ref[0]


def _combine_final_kernel(x_ref, y_ref, mod_ref, g_ref, o_ref):
    xn = x_ref[0] + mod_ref[0, :, 5 * D:6 * D] * y_ref[0]
    o_ref[0] = _rmsnorm(xn, g_ref[...])


def _combine_call(x, y, mod, final_g):
    nbatch, tb, _ = x.shape
    ctx_tiles = CTX // TM
    if final_g is None:
        nt = tb // TM
        tile = pl.BlockSpec((1, TM, D), lambda b, i: (b, i, 0))
        return pl.pallas_call(
            _combine_kernel,
            grid=(nbatch, nt),
            in_specs=[tile, tile, pl.BlockSpec((1, 1, 6 * D), _mod_index(nbatch, ctx_tiles))],
            out_specs=tile,
            out_shape=jax.ShapeDtypeStruct((nbatch, tb, D), f32),
            compiler_params=_cparams(("parallel", "arbitrary")),
            name="ffn_residual",
        )(x, y, mod)
    nt = (tb - CTX) // TM
    tile_in = pl.BlockSpec((1, TM, D), lambda b, i: (b, i + ctx_tiles, 0))
    return pl.pallas_call(
        _combine_final_kernel,
        grid=(nbatch, nt),
        in_specs=[tile_in, tile_in, pl.BlockSpec((1, 1, 6 * D), lambda b, i: (b, 0, 0)),
                  pl.BlockSpec((1, D), lambda b, i: (0, 0))],
        out_specs=pl.BlockSpec((1, TM, D), lambda b, i: (b, i, 0)),
        out_shape=jax.ShapeDtypeStruct((nbatch, tb - CTX, D), f32),
        compiler_params=_cparams(("parallel", "arbitrary")),
        name="ffn_residual_final",
    )(x, y, mod, final_g)


def _route(scores, router_b):
    t = scores.shape[0]
    grouped = (scores + router_b).reshape(t, N_GROUPS, EXPERTS_PER_GROUP)
    group_score = jnp.sum(lax.top_k(grouped, 2)[0], axis=-1)
    g_idx = jnp.argmax(group_score, axis=-1)
    in_group = jnp.take_along_axis(grouped, g_idx[:, None, None], axis=1)[:, 0]
    _, local = lax.top_k(in_group, TOP_K)
    e_idx = g_idx[:, None] * EXPERTS_PER_GROUP + local
    gates = jnp.take_along_axis(scores, e_idx, axis=1)
    gates = gates / jnp.sum(gates, axis=-1, keepdims=True)
    return e_idx.astype(i32), gates


def _dispatch_plan(e_idx):
    t = e_idx.shape[0]
    n_rows = TOP_K * t
    nblk = -(-n_rows // MOE_BLK) + N_EXPERTS
    flat_e = e_idx.reshape(-1)
    onehot = (flat_e[:, None] == jnp.arange(N_EXPERTS, dtype=i32)[None, :]).astype(i32)
    rank = jnp.cumsum(onehot, axis=0) - onehot
    counts = jnp.sum(onehot, axis=0)
    padded = (counts + MOE_BLK - 1) // MOE_BLK * MOE_BLK
    pad_ends = jnp.cumsum(padded)
    pad_starts = pad_ends - padded
    dest = pad_starts[flat_e] + jnp.take_along_axis(rank, flat_e[:, None], axis=1)[:, 0]
    blk_expert = jnp.minimum(
        jnp.searchsorted(pad_ends, jnp.arange(nblk, dtype=i32) * MOE_BLK, side='right'),
        N_EXPERTS - 1).astype(i32)
    n_used = (pad_ends[-1] // MOE_BLK).astype(i32).reshape(1)
    return dest.astype(i32), blk_expert, n_used, nblk


def _rope_tables(tb):
    rows = SEQ // GRID_W
    row = jnp.repeat(jnp.arange(rows, dtype=f32), GRID_W)
    col = jnp.tile(jnp.arange(GRID_W, dtype=f32), rows)
    inv_freq = ROPE_BASE ** (-jnp.arange(ROPE_FREQS, dtype=f32) / ROPE_FREQS)
    ang_r = row[:, None] * inv_freq[None, :]
    ang_c = col[:, None] * inv_freq[None, :]
    ang = jnp.concatenate([ang_r, ang_r, ang_c, ang_c], axis=-1)
    cos = jnp.concatenate([jnp.ones((CTX, HEAD_DIM), f32), jnp.cos(ang)], axis=0)
    sin = jnp.concatenate([jnp.zeros((CTX, HEAD_DIM), f32), jnp.sin(ang)], axis=0)
    return jnp.tile(cos, (1, LANES // HEAD_DIM)), jnp.tile(sin, (1, LANES // HEAD_DIM))


def _block_diag2(w):
    z = jnp.zeros_like(w[0])
    return jnp.concatenate([jnp.concatenate([w[0], z], axis=1), jnp.concatenate([z, w[1]], axis=1)], axis=0)


def kernel(x, c, ctx, c_ctx, w_mod, b_mod, norm_mix_g, norm_ffn_g, w_in, w_out, att_sink, shift_mu_prev, shift_mu_next, decay_w0, decay_w2, iclr_a0, iclr_a2, vres_v0, vres_v1, vres_v2, gate_g2, k_k, k_a, r_k, ln_x_w, ln_x_b, router_w, router_b, expert_w_gate, expert_w_up, expert_w_down, final_norm_g):
    nbatch = x.shape[0]
    depth = w_mod.shape[0]
    tb = ctx.shape[1] + x.shape[1]
    xa = jnp.concatenate([ctx, x], axis=1)
    nb_pad = -(-(nbatch + 1) // 8) * 8
    cond = jnp.zeros((nb_pad, D), f32).at[:nbatch].set(c).at[nbatch].set(c_ctx)
    mod_all = _mod_call(cond, w_mod, b_mod).reshape(depth, nb_pad, 1, 6 * D)
    cos, sin = _rope_tables(tb)
    wr_hi = jnp.zeros((D, LANES), f32).at[:, :N_EXPERTS].set(router_w).astype(bf16)
    wr_lo = (jnp.zeros((D, LANES), f32).at[:, :N_EXPERTS].set(router_w) - wr_hi.astype(f32)).astype(bf16)
    w_router = jnp.stack([wr_hi, wr_lo])
    v_first = None
    for l in range(depth):
        mod = mod_all[l]
        q, k, v, rw = _in_proj_call(xa, mod, norm_mix_g[l].reshape(1, D), w_in[l].astype(bf16), cos, sin)
        att = _attn_call(att_sink[l], q, k, v)
        mu = jnp.stack([shift_mu_prev[l], shift_mu_next[l]])
        v0 = vres_v0[l - 1] if l > 0 else jnp.zeros((RW,), f32)
        vec = jnp.stack([k_k[l], k_a[l], r_k[l].reshape(RW), v0,
                         decay_w0[l, 0], decay_w0[l, 1], iclr_a0[l, 0], iclr_a0[l, 1]])
        if l > 0:
            v1 = jnp.zeros((RW, LANES), f32).at[:, :LORA_VRES].set(vres_v1[l - 1]).astype(bf16)
            v2 = jnp.zeros((LANES, RW), f32).at[:LORA_VRES].set(vres_v2[l - 1]).astype(bf16)
        else:
            v1 = v2 = None
        r_, v_, kk, bv, g, kd, lw, bd = _feat_call(
            rw, v_first, mu, vec, _block_diag2(decay_w2[l]).astype(bf16), _block_diag2(iclr_a2[l]).astype(bf16),
            gate_g2[l].astype(bf16), v1, v2)
        if l == 0:
            v_first = v_
        yf, yb = _scan_call(r_, v_, kk, kd, lw, bd)
        ln = jnp.stack([ln_x_w[l], ln_x_b[l]])
        xa, hf, scores = _mix_call(xa, att, yf, yb, bv, g, mod, ln, w_out[l].astype(bf16),
                                   norm_ffn_g[l].reshape(1, D), w_router)
        ntok = nbatch * tb
        e_idx, gates = _route(scores.reshape(ntok, LANES)[:, :N_EXPERTS], router_b)
        dest, blk_expert, n_used, nblk = _dispatch_plan(e_idx)
        row_token = jnp.full((nblk * MOE_BLK,), ntok, i32).at[dest].set(jnp.arange(TOP_K * ntok, dtype=i32) // TOP_K)
        hf_pad = jnp.concatenate([hf.reshape(ntok, D), jnp.zeros((1, D), bf16)], axis=0)
        x_sorted = hf_pad[row_token]
        y_sorted = _ffn_call(blk_expert, n_used, x_sorted, expert_w_gate[l].astype(bf16),
                             expert_w_up[l].astype(bf16), expert_w_down[l].astype(bf16))
        y_tok = y_sorted[dest].reshape(ntok, TOP_K, D)
        moe = jnp.sum(y_tok * gates[:, :, None], axis=1).reshape(nbatch, tb, D)
        xa = _combine_call(xa, moe, mod, final_norm_g.reshape(1, D) if l == depth - 1 else None)
    return xa
```

```python
import functools
import math

import jax
import jax.numpy as jnp
from jax import lax
from jax.experimental import pallas as pl
from jax.experimental.pallas import tpu as pltpu

f32 = jnp.float32
bf16 = jnp.bfloat16
i32 = jnp.int32

D = 1024
SEQ = 4096
CTX = 256
TB = CTX + SEQ
GRID_W = 64
HEAD_DIM = 64
ATT_WIDTH = 512
ATT_HEADS = 8
KV_HEADS = 2
ATT_GROUP = ATT_HEADS // KV_HEADS
KV_WIDTH = KV_HEADS * HEAD_DIM
RW = 512
RWKV_HEADS = 8
LORA_DECAY = 64
LORA_ICLR = 64
LORA_VRES = 32
LORA_GATE = 128
RWKV_COLS = 3 * RW + 2 * (LORA_DECAY + LORA_ICLR) + LORA_GATE
ATT_COLS = ATT_WIDTH + 2 * KV_WIDTH
IN_COLS = ATT_COLS + RWKV_COLS
N_EXPERTS = 16
N_GROUPS = 4
EXPERTS_PER_GROUP = 4
TOP_K = 2
MOE_BLK = 256
NORM_EPS = 1e-6
GN_EPS = 64e-5
NEG_INF = -1e30
ATT_SCALE = HEAD_DIM ** -0.5
ROPE_BASE = 10000.0
ROPE_FREQS = HEAD_DIM // 4

LANES = 128
TM = 256
QB = 128
CH = 64
HG = 4
GW = HG * HEAD_DIM
VMEM_LIMIT = 48 * 1024 * 1024


def _cparams(sem):
    return pltpu.CompilerParams(dimension_semantics=sem, vmem_limit_bytes=VMEM_LIMIT)


def _sigmoid(x):
    return 1.0 / (1.0 + jnp.exp(-x))


def _div_pow2(x, n):
    assert n & (n - 1) == 0
    return lax.shift_right_logical(x, n.bit_length() - 1)


def _mod_pow2(x, n):
    assert n & (n - 1) == 0
    return lax.bitwise_and(x, n - 1)


def _dot(a, b):
    return jnp.dot(a, b, preferred_element_type=f32)


def _dot_nt(a, b):
    return lax.dot_general(a, b, (((1,), (1,)), ((), ())), preferred_element_type=f32)


def _dot_tn(a, b):
    return lax.dot_general(a, b, (((0,), (0,)), ((), ())), preferred_element_type=f32)


def _split2(x):
    hi = x.astype(bf16)
    lo = (x - hi.astype(f32)).astype(bf16)
    return hi, lo


def _split3(x):
    hi = x.astype(bf16)
    r1 = x - hi.astype(f32)
    mid = r1.astype(bf16)
    lo = (r1 - mid.astype(f32)).astype(bf16)
    return hi, mid, lo


def _dot_exact_rhs(x, m):
    hi, mid, lo = _split3(x)
    return _dot(hi, m) + _dot(mid, m) + _dot(lo, m)


def _dot_exact_lhs(m, x):
    hi, mid, lo = _split3(x)
    return _dot(m, hi) + _dot(m, mid) + _dot(m, lo)


def _rmsnorm(x, g):
    ms = jnp.mean(x * x, axis=-1, keepdims=True)
    return x * lax.rsqrt(ms + NORM_EPS) * g


def _head_ones():
    r = _div_pow2(lax.broadcasted_iota(i32, (RW, RW), 0), HEAD_DIM)
    c = _div_pow2(lax.broadcasted_iota(i32, (RW, RW), 1), HEAD_DIM)
    return (r == c).astype(bf16)


def _mod_kernel(c_ref, w_ref, b_ref, o_ref):
    c = c_ref[...]
    s = (c * _sigmoid(c)).astype(bf16)
    o_ref[0] = _dot(s, w_ref[0].astype(bf16)) + b_ref[0]


def _mod_call(cond, w_mod, b_mod):
    nb = cond.shape[0]
    depth = w_mod.shape[0]
    tn = 1024
    return pl.pallas_call(
        _mod_kernel,
        grid=(depth, 6 * D // tn),
        in_specs=[
            pl.BlockSpec((nb, D), lambda l, j: (0, 0)),
            pl.BlockSpec((1, D, tn), lambda l, j: (l, 0, j)),
            pl.BlockSpec((1, 1, tn), lambda l, j: (l, 0, j)),
        ],
        out_specs=pl.BlockSpec((1, nb, tn), lambda l, j: (l, 0, j)),
        out_shape=jax.ShapeDtypeStruct((depth, nb, 6 * D), f32),
        compiler_params=_cparams(("arbitrary", "arbitrary")),
        name="mod",
    )(cond, w_mod, b_mod.reshape(depth, 1, 6 * D))


def _mod_index(nbatch, ctx_tiles):
    return lambda b, i: (jnp.where(i < ctx_tiles, nbatch, b), 0, 0)


def _in_proj_kernel(x_ref, mod_ref, g_ref, w_ref, cos_ref, sin_ref, q_ref, k_ref, v_ref, rw_ref):
    x = x_ref[0]
    tm = x.shape[0]
    h = _rmsnorm(x, g_ref[...])
    sh = mod_ref[0, :, 0:D]
    sc = mod_ref[0, :, D:2 * D]
    h = (h * (1.0 + sc) + sh).astype(bf16)
    p = _dot(h, w_ref[...])
    cos = cos_ref[...]
    sin = sin_ref[...]
    lane = lax.broadcasted_iota(i32, (tm, LANES), 1)
    first_half = _mod_pow2(lane, 2 * ROPE_FREQS) < ROPE_FREQS

    def rope(t):
        rot = jnp.where(first_half, -pltpu.roll(t, LANES - ROPE_FREQS, 1), pltpu.roll(t, ROPE_FREQS, 1))
        return t * cos + rot * sin

    for j in range(ATT_WIDTH // LANES):
        t = rope(p[:, j * LANES:(j + 1) * LANES]).astype(bf16)
        q_ref[0, 2 * j] = t[:, :HEAD_DIM]
        q_ref[0, 2 * j + 1] = t[:, HEAD_DIM:]
    kt = rope(p[:, ATT_WIDTH:ATT_WIDTH + KV_WIDTH]).astype(bf16)
    vt = p[:, ATT_WIDTH + KV_WIDTH:ATT_COLS].astype(bf16)
    for hh in range(KV_HEADS):
        k_ref[0, hh] = kt[:, hh * HEAD_DIM:(hh + 1) * HEAD_DIM]
        v_ref[0, hh] = vt[:, hh * HEAD_DIM:(hh + 1) * HEAD_DIM]
    rw_ref[0] = p[:, ATT_COLS:]


def _in_proj_call(x, mod, g, w_in, cos, sin):
    nbatch, tb, _ = x.shape
    nt = tb // TM
    return pl.pallas_call(
        _in_proj_kernel,
        grid=(nbatch, nt),
        in_specs=[
            pl.BlockSpec((1, TM, D), lambda b, i: (b, i, 0)),
            pl.BlockSpec((1, 1, 6 * D), _mod_index(nbatch, CTX // TM)),
            pl.BlockSpec((1, D), lambda b, i: (0, 0)),
            pl.BlockSpec((D, IN_COLS), lambda b, i: (0, 0)),
            pl.BlockSpec((TM, LANES), lambda b, i: (i, 0)),
            pl.BlockSpec((TM, LANES), lambda b, i: (i, 0)),
        ],
        out_specs=[
            pl.BlockSpec((1, ATT_HEADS, TM, HEAD_DIM), lambda b, i: (b, 0, i, 0)),
            pl.BlockSpec((1, KV_HEADS, TM, HEAD_DIM), lambda b, i: (b, 0, i, 0)),
            pl.BlockSpec((1, KV_HEADS, TM, HEAD_DIM), lambda b, i: (b, 0, i, 0)),
            pl.BlockSpec((1, TM, RWKV_COLS), lambda b, i: (b, i, 0)),
        ],
        out_shape=[
            jax.ShapeDtypeStruct((nbatch, ATT_HEADS, tb, HEAD_DIM), bf16),
            jax.ShapeDtypeStruct((nbatch, KV_HEADS, tb, HEAD_DIM), bf16),
            jax.ShapeDtypeStruct((nbatch, KV_HEADS, tb, HEAD_DIM), bf16),
            jax.ShapeDtypeStruct((nbatch, tb, RWKV_COLS), f32),
        ],
        compiler_params=_cparams(("parallel", "arbitrary")),
        name="in_proj",
    )(x, mod, g, w_in, cos, sin)


def _attn_kernel(sink_ref, q_ref, kp_ref, kc_ref, kn_ref, vp_ref, vc_ref, vn_ref, kx_ref, vx_ref, o_ref,
                 *, nblk, ctx_blks):
    i = pl.program_id(1)
    is_lat = i >= ctx_blks
    prev_ok = jnp.logical_and(is_lat, i - 1 >= ctx_blks)
    next_ok = jnp.logical_and(is_lat, i + 1 <= nblk - 1)
    rows = ATT_GROUP * QB
    qi = _mod_pow2(lax.broadcasted_iota(i32, (rows, QB), 0), QB)
    kj = lax.broadcasted_iota(i32, (rows, QB), 1)
    mask_p = jnp.logical_and(kj >= qi, prev_ok)
    mask_n = jnp.logical_and(kj <= qi, next_ok)
    row_head = _div_pow2(lax.broadcasted_iota(i32, (rows, 1), 0), QB)
    outs = []
    for h in range(KV_HEADS):
        qh = q_ref[0, ATT_GROUP * h:ATT_GROUP * (h + 1)].reshape(rows, HEAD_DIM)
        s_p = jnp.where(mask_p, _dot_nt(qh, kp_ref[0, h]) * ATT_SCALE, NEG_INF)
        s_c = jnp.where(is_lat, _dot_nt(qh, kc_ref[0, h]) * ATT_SCALE, NEG_INF)
        s_n = jnp.where(mask_n, _dot_nt(qh, kn_ref[0, h]) * ATT_SCALE, NEG_INF)
        s_x = _dot_nt(qh, kx_ref[0, h]) * ATT_SCALE
        sink = jnp.zeros((rows, 1), f32)
        for g in range(ATT_GROUP):
            sink = jnp.where(row_head == g, sink_ref[ATT_GROUP * h + g], sink)
        m = jnp.maximum(jnp.max(s_p, axis=-1, keepdims=True), jnp.max(s_c, axis=-1, keepdims=True))
        m = jnp.maximum(m, jnp.max(s_n, axis=-1, keepdims=True))
        m = jnp.maximum(m, jnp.max(s_x, axis=-1, keepdims=True))
        m = jnp.maximum(m, sink)
        e_p = jnp.exp(s_p - m)
        e_c = jnp.exp(s_c - m)
        e_n = jnp.exp(s_n - m)
        e_x = jnp.exp(s_x - m)
        den = (jnp.sum(e_p, axis=-1, keepdims=True) + jnp.sum(e_c, axis=-1, keepdims=True)
               + jnp.sum(e_n, axis=-1, keepdims=True) + jnp.sum(e_x, axis=-1, keepdims=True)
               + jnp.exp(sink - m))
        o = (_dot(e_p.astype(bf16), vp_ref[0, h]) + _dot(e_c.astype(bf16), vc_ref[0, h])
             + _dot(e_n.astype(bf16), vn_ref[0, h]) + _dot(e_x.astype(bf16), vx_ref[0, h]))
        o = o / den
        outs.extend(o[g * QB:(g + 1) * QB] for g in range(ATT_GROUP))
    o_ref[0] = jnp.concatenate(outs, axis=1).astype(bf16)


def _attn_call(sink, q, k, v):
    nbatch, _, tb, _ = q.shape
    nblk = tb // QB
    ctx_blks = CTX // QB
    kv_blk = (1, KV_HEADS, QB, HEAD_DIM)
    prev_map = lambda b, i: (b, 0, jnp.maximum(i - 1, 0), 0)
    cur_map = lambda b, i: (b, 0, i, 0)
    next_map = lambda b, i: (b, 0, jnp.minimum(i + 1, nblk - 1), 0)
    ctx_spec = pl.BlockSpec((1, KV_HEADS, CTX, HEAD_DIM), lambda b, i: (b, 0, 0, 0))
    return pl.pallas_call(
        functools.partial(_attn_kernel, nblk=nblk, ctx_blks=ctx_blks),
        grid=(nbatch, nblk),
        in_specs=[
            pl.BlockSpec(memory_space=pltpu.SMEM),
            pl.BlockSpec((1, ATT_HEADS, QB, HEAD_DIM), cur_map),
            pl.BlockSpec(kv_blk, prev_map), pl.BlockSpec(kv_blk, cur_map), pl.BlockSpec(kv_blk, next_map),
            pl.BlockSpec(kv_blk, prev_map), pl.BlockSpec(kv_blk, cur_map), pl.BlockSpec(kv_blk, next_map),
            ctx_spec, ctx_spec,
        ],
        out_specs=pl.BlockSpec((1, QB, ATT_WIDTH), lambda b, i: (b, i, 0)),
        out_shape=jax.ShapeDtypeStruct((nbatch, tb, ATT_WIDTH), bf16),
        compiler_params=_cparams(("parallel", "arbitrary")),
        name="attention",
    )(sink, q, k, k, k, v, v, v, k, v)


def _feat_kernel(*refs, nt, ctx_tiles, has_vres):
    if has_vres:
        (rw_ref, hp_ref, hn_ref, vf_ref, mu_ref, vec_ref, w2_ref, a2_ref, g2_ref, v1_ref, v2_ref,
         r_ref, v_ref, kk_ref, bv_ref, g_ref, kd_ref, lw_ref, bd_ref) = refs
    else:
        (rw_ref, hp_ref, hn_ref, mu_ref, vec_ref, w2_ref, a2_ref, g2_ref,
         r_ref, v_ref, kk_ref, bv_ref, g_ref, kd_ref, lw_ref, bd_ref) = refs
    i = pl.program_id(1)
    u0 = rw_ref[0]
    tm = u0.shape[0]
    prev_zero = jnp.logical_or(i == 0, i == ctx_tiles)
    next_zero = jnp.logical_or(i == ctx_tiles - 1, i == nt - 1)
    halo_p = jnp.where(prev_zero, 0.0, hp_ref[0, 7:8, :])
    halo_n = jnp.where(next_zero, 0.0, hn_ref[0, 0:1, :])
    row = lax.broadcasted_iota(i32, (tm, 1), 0)
    prev = jnp.where(row == 0, halo_p, pltpu.roll(u0, 1, 0))
    nxt = jnp.where(row == tm - 1, halo_n, pltpu.roll(u0, tm - 1, 0))
    u = u0 + mu_ref[0:1, :] * (prev - u0) + mu_ref[1:2, :] * (nxt - u0)

    r = u[:, 0:RW]
    k = u[:, RW:2 * RW]
    v = u[:, 2 * RW:3 * RW]
    wd = u[:, 3 * RW:3 * RW + 2 * LORA_DECAY]
    ad = u[:, 3 * RW + 2 * LORA_DECAY:3 * RW + 2 * (LORA_DECAY + LORA_ICLR)]
    gd = u[:, 3 * RW + 2 * (LORA_DECAY + LORA_ICLR):]
    k_k = vec_ref[0:1, :]
    k_a = vec_ref[1:2, :]
    r_k = vec_ref[2:3, :]
    ones = _head_ones()

    if has_vres:
        lo = _dot(v.astype(bf16), v1_ref[...])
        gate = _sigmoid(vec_ref[3:4, :] + _dot(lo.astype(bf16), v2_ref[...]))
        v = v + (vf_ref[0] - v) * gate
    decay_in = _dot(jnp.tanh(wd).astype(bf16), w2_ref[...])
    a_in = _dot(ad.astype(bf16), a2_ref[...])
    kk = k * k_k
    n2 = _dot_exact_rhs(kk * kk, ones)
    kk = kk / jnp.maximum(jnp.sqrt(n2), 1e-12)
    g = _dot(_sigmoid(gd).astype(bf16), g2_ref[...])
    ksum = jnp.zeros_like(k)
    for d in range(2):
        w0 = vec_ref[4 + d:5 + d, :]
        a0 = vec_ref[6 + d:7 + d, :]
        lw = -_sigmoid(w0 + decay_in[:, d * RW:(d + 1) * RW]) * math.exp(-0.5)
        a = _sigmoid(a0 + a_in[:, d * RW:(d + 1) * RW])
        kd = k * (1.0 + (a - 1.0) * k_a)
        ksum = ksum + kd
        kd_ref[d, 0] = kd
        lw_ref[d, 0] = lw
        bd_ref[d, 0] = kk * a
    bonus = _dot_exact_rhs(r * ksum * r_k, ones)
    r_ref[0] = r
    v_ref[0] = v
    kk_ref[0] = kk
    bv_ref[0] = bonus * v
    g_ref[0] = g


def _feat_call(rw, v_first, mu, vec, w2bd, a2bd, g2, v1, v2):
    nbatch, tb, _ = rw.shape
    nt = tb // TM
    has_vres = v_first is not None
    sub = TM // 8
    tile = lambda w: pl.BlockSpec((1, TM, w), lambda b, i: (b, i, 0))
    full = lambda a: pl.BlockSpec(a.shape, lambda b, i: (0,) * a.ndim)
    in_specs = [
        tile(RWKV_COLS),
        pl.BlockSpec((1, 8, RWKV_COLS), lambda b, i: (b, jnp.maximum(i * sub - 1, 0), 0)),
        pl.BlockSpec((1, 8, RWKV_COLS), lambda b, i: (b, jnp.minimum((i + 1) * sub, tb // 8 - 1), 0)),
    ]
    args = [rw, rw, rw]
    if has_vres:
        in_specs.append(tile(RW))
        args.append(v_first)
    consts = [mu, vec, w2bd, a2bd, g2] + ([v1, v2] if has_vres else [])
    in_specs += [full(a) for a in consts]
    args += consts
    dir_spec = pl.BlockSpec((2, 1, TM, RW), lambda b, i: (0, b, i, 0))
    tok = jax.ShapeDtypeStruct((nbatch, tb, RW), f32)
    dtok = jax.ShapeDtypeStruct((2, nbatch, tb, RW), f32)
    return pl.pallas_call(
        functools.partial(_feat_kernel, nt=nt, ctx_tiles=CTX // TM, has_vres=has_vres),
        grid=(nbatch, nt),
        in_specs=in_specs,
        out_specs=[tile(RW)] * 5 + [dir_spec] * 3,
        out_shape=[tok] * 5 + [dtok] * 3,
        compiler_params=_cparams(("parallel", "arbitrary")),
        name="rwkv_features",
    )(*args)


def _block_diag_rows(x, width):
    cb = _div_pow2(lax.broadcasted_iota(i32, x.shape, 1), width)
    return jnp.concatenate([jnp.where(cb == h, x, jnp.zeros_like(x)) for h in range(HG)], axis=0)


def _scan_chunk(s_prev, r, v, kk, k, lw, b, reverse):
    row = lax.broadcasted_iota(i32, (CH, CH), 0)
    col = lax.broadcasted_iota(i32, (CH, CH), 1)
    incl = (col >= row) if reverse else (col <= row)
    strict = (col > row) if reverse else (col < row)
    gam = _dot_exact_lhs(incl.astype(bf16), lw)
    gtot = jnp.sum(lw, axis=0, keepdims=True)
    e_neg = jnp.exp(-gam)
    a_t = (-kk * jnp.exp(gam - lw)).astype(bf16)
    r_t = (r * jnp.exp(gam)).astype(bf16)
    b_t = b * e_neg
    k_t = k * e_neg
    ar = jnp.concatenate([a_t, r_t], axis=0)
    bk = jnp.concatenate([_block_diag_rows(b_t.astype(bf16), HEAD_DIM),
                          _block_diag_rows(k_t.astype(bf16), HEAD_DIM)], axis=0)
    gram = _dot_nt(ar, bk)
    wide = HG * CH
    row_t = lax.broadcasted_iota(i32, (CH, wide), 0)
    col_t = _mod_pow2(lax.broadcasted_iota(i32, (CH, wide), 1), CH)
    incl_t = (col_t >= row_t) if reverse else (col_t <= row_t)
    strict_t = (col_t > row_t) if reverse else (col_t < row_t)
    l_ab = jnp.where(strict_t, gram[:CH, :wide], 0.0)
    l_ak = jnp.where(strict_t, gram[:CH, wide:], 0.0)
    q_b = jnp.where(incl_t, gram[CH:, :wide], 0.0)
    q_k = jnp.where(incl_t, gram[CH:, wide:], 0.0)
    p = l_ab
    t = (row_t == col_t).astype(f32) + p
    n = 2
    while n < CH:
        pb = p.astype(bf16)
        p = _dot(pb, _block_diag_rows(pb, CH))
        t = t + _dot(t.astype(bf16), _block_diag_rows(p.astype(bf16), CH))
        n *= 2
    ars = _dot_nt(ar, s_prev.astype(bf16))
    v_bd = _block_diag_rows(v.astype(bf16), HEAD_DIM)
    rhs = ars[:CH] + _dot(l_ak.astype(bf16), v_bd)
    u = _dot(t.astype(bf16), _block_diag_rows(rhs.astype(bf16), HEAD_DIM))
    u_bd = _block_diag_rows(u.astype(bf16), HEAD_DIM)
    y = ars[CH:] + _dot(jnp.concatenate([q_k, q_b], axis=1).astype(bf16),
                        jnp.concatenate([v_bd, u_bd], axis=0))
    eg = jnp.exp(gtot)
    vu = jnp.concatenate([v, u], axis=0).astype(bf16)
    kb = jnp.concatenate([k_t * eg, b_t * eg], axis=0).astype(bf16)
    s_add = _dot_tn(vu, kb)
    rb = _div_pow2(lax.broadcasted_iota(i32, (GW, GW), 0), HEAD_DIM)
    cb = _div_pow2(lax.broadcasted_iota(i32, (GW, GW), 1), HEAD_DIM)
    s_new = s_prev * eg + jnp.where(rb == cb, s_add, 0.0)
    return s_new, y


def _scan_kernel(rf_ref, vf_ref, kkf_ref, kf_ref, lwf_ref, bf_ref,
                 rb_ref, vb_ref, kkb_ref, kb_ref, lwb_ref, bb_ref,
                 yf_ref, yb_ref, s_ref):
    @pl.when(pl.program_id(1) == 0)
    def _():
        s_ref[...] = jnp.zeros_like(s_ref)

    dirs = ((rf_ref, vf_ref, kkf_ref, kf_ref, lwf_ref, bf_ref, yf_ref),
            (rb_ref, vb_ref, kkb_ref, kb_ref, lwb_ref, bb_ref, yb_ref))
    for d, (r_ref, v_ref, kk_ref, k_ref, lw_ref, b_ref, y_ref) in enumerate(dirs):
        for g in range(RW // GW):
            sl = slice(g * GW, (g + 1) * GW)
            s_new, y = _scan_chunk(s_ref[d, g], r_ref[0, :, sl], v_ref[0, :, sl], kk_ref[0, :, sl],
                                   k_ref[0, 0, :, sl], lw_ref[0, 0, :, sl], b_ref[0, 0, :, sl], d == 1)
            s_ref[d, g] = s_new
            y_ref[0, :, sl] = y


def _scan_call(r, v, kk, kd, lw, bd):
    nbatch, tb, _ = r.shape
    nc = tb // CH
    cc = CTX // CH
    rev = lambda j: jnp.where(j < cc, cc - 1 - j, nc - 1 + cc - j)
    tok_f = pl.BlockSpec((1, CH, RW), lambda b, j: (b, j, 0))
    tok_b = pl.BlockSpec((1, CH, RW), lambda b, j: (b, rev(j), 0))
    dir_f = pl.BlockSpec((1, 1, CH, RW), lambda b, j: (0, b, j, 0))
    dir_b = pl.BlockSpec((1, 1, CH, RW), lambda b, j: (1, b, rev(j), 0))
    out = jax.ShapeDtypeStruct((nbatch, tb, RW), f32)
    return pl.pallas_call(
        _scan_kernel,
        grid=(nbatch, nc),
        in_specs=[tok_f, tok_f, tok_f, dir_f, dir_f, dir_f, tok_b, tok_b, tok_b, dir_b, dir_b, dir_b],
        out_specs=[tok_f, tok_b],
        out_shape=[out, out],
        scratch_shapes=[pltpu.VMEM((2, RW // GW, GW, GW), f32)],
        compiler_params=_cparams(("parallel", "arbitrary")),
        name="rwkv_scan",
    )(r, v, kk, kd, lw, bd, r, v, kk, kd, lw, bd)


def _mix_kernel(x_ref, att_ref, yf_ref, yb_ref, bv_ref, g_ref, mod_ref, ln_ref, wo_ref, gf_ref, wr_ref, rb_ref,
                xo_ref, hf_ref, e_ref, gt_ref):
    ones = _head_ones()
    y = yf_ref[0] + yb_ref[0]
    inv = 1.0 / HEAD_DIM
    mu = _dot_exact_rhs(y, ones) * inv
    dlt = y - mu
    var = _dot_exact_rhs(dlt * dlt, ones) * inv
    gn = dlt * lax.rsqrt(var + GN_EPS) * ln_ref[0:1, :] + ln_ref[1:2, :]
    rwk = ((gn + bv_ref[0]) * g_ref[0]).astype(bf16)
    mix = _dot(att_ref[0], wo_ref[0:ATT_WIDTH, :]) + _dot(rwk, wo_ref[ATT_WIDTH:, :])
    gate_a = mod_ref[0, :, 2 * D:3 * D]
    xn = x_ref[0] + gate_a * mix
    xo_ref[0] = xn
    sh = mod_ref[0, :, 3 * D:4 * D]
    sc = mod_ref[0, :, 4 * D:5 * D]
    hf = _rmsnorm(xn, gf_ref[...]) * (1.0 + sc) + sh
    hf_ref[0] = hf.astype(bf16)
    h_hi, h_lo = _split2(hf)
    logits = _dot_nt(wr_ref[0], h_hi) + _dot_nt(wr_ref[0], h_lo) + _dot_nt(wr_ref[1], h_hi)
    e0, e1, g0, g1 = _route_rows(_sigmoid(logits), rb_ref[...])
    e_ref[0] = jnp.concatenate([e0, e1], axis=0)
    gt_ref[0] = jnp.concatenate([g0, g1], axis=0)


def _route_rows(scores, bias):
    biased = scores + bias
    row = lambda a, e: a[e:e + 1, :]
    best = None
    for gi in range(N_GROUPS):
        m = [row(biased, gi * EXPERTS_PER_GROUP + j) for j in range(EXPERTS_PER_GROUP)]
        gs = None
        for a in range(EXPERTS_PER_GROUP):
            for b in range(a + 1, EXPERTS_PER_GROUP):
                pair = m[a] + m[b]
                gs = pair if gs is None else jnp.maximum(gs, pair)
        if best is None:
            best, g_idx = gs, jnp.zeros(gs.shape, i32)
        else:
            better = gs > best
            g_idx = jnp.where(better, gi, g_idx)
            best = jnp.where(better, gs, best)

    def pick(a, j):
        out = row(a, j)
        for gi in range(1, N_GROUPS):
            out = jnp.where(g_idx == gi, row(a, gi * EXPERTS_PER_GROUP + j), out)
        return out

    vb = [pick(biased, j) for j in range(EXPERTS_PER_GROUP)]
    vs = [pick(scores, j) for j in range(EXPERTS_PER_GROUP)]

    def argmax_first(vals):
        bv, bi = vals[0], jnp.zeros(vals[0].shape, i32)
        for j in range(1, len(vals)):
            better = vals[j] > bv
            bi = jnp.where(better, j, bi)
            bv = jnp.where(better, vals[j], bv)
        return bi

    i1 = argmax_first(vb)
    i2 = argmax_first([jnp.where(i1 == j, -jnp.inf, vb[j]) for j in range(EXPERTS_PER_GROUP)])
    sel = lambda idx: sum(jnp.where(idx == j, vs[j], 0.0) for j in range(EXPERTS_PER_GROUP))
    s1, s2 = sel(i1), sel(i2)
    tot = s1 + s2
    base = g_idx * EXPERTS_PER_GROUP
    return base + i1, base + i2, s1 / tot, s2 / tot


def _mix_call(x, att, yf, yb, bv, g, mod, ln, w_out, g_ffn, w_router, b_router):
    nbatch, tb, _ = x.shape
    nt = tb // TM
    tile = lambda w: pl.BlockSpec((1, TM, w), lambda b, i: (b, i, 0))
    full = lambda a: pl.BlockSpec(a.shape, lambda b, i: (0,) * a.ndim)
    route = pl.BlockSpec((1, TOP_K, TM), lambda b, i: (b, 0, i))
    return pl.pallas_call(
        _mix_kernel,
        grid=(nbatch, nt),
        in_specs=[tile(D), tile(ATT_WIDTH), tile(RW), tile(RW), tile(RW), tile(RW),
                  pl.BlockSpec((1, 1, 6 * D), _mod_index(nbatch, CTX // TM)),
                  full(ln), full(w_out), full(g_ffn), full(w_router), full(b_router)],
        out_specs=[tile(D), tile(D), route, route],
        out_shape=[jax.ShapeDtypeStruct((nbatch, tb, D), f32),
                   jax.ShapeDtypeStruct((nbatch, tb, D), bf16),
                   jax.ShapeDtypeStruct((nbatch, TOP_K, tb), i32),
                   jax.ShapeDtypeStruct((nbatch, TOP_K, tb), f32)],
        compiler_params=_cparams(("parallel", "arbitrary")),
        name="mix_out",
    )(x, att, yf, yb, bv, g, mod, ln, w_out, g_ffn, w_router, b_router)


def _ffn_kernel(be_ref, nu_ref, x_ref, wg_ref, wu_ref, wd_ref, y_ref):
    i = pl.program_id(0)

    @pl.when(i < nu_ref[0])
    def _():
        x = x_ref[...]
        gt = _dot(x, wg_ref[0])
        up = _dot(x, wu_ref[0])
        hid = (gt * _sigmoid(gt) * up).astype(bf16)
        y_ref[...] = _dot(hid, wd_ref[0])

    @pl.when(i >= nu_ref[0])
    def _():
        y_ref[...] = jnp.zeros_like(y_ref)


def _ffn_call(blk_expert, n_used, x_sorted, wg, wu, wd):
    nrows = x_sorted.shape[0]
    nblk = nrows // MOE_BLK
    wspec = pl.BlockSpec((1, D, D), lambda i, be, nu: (be[i], 0, 0))
    return pl.pallas_call(
        _ffn_kernel,
        grid_spec=pltpu.PrefetchScalarGridSpec(
            num_scalar_prefetch=2,
            grid=(nblk,),
            in_specs=[pl.BlockSpec((MOE_BLK, D), lambda i, be, nu: (i, 0)), wspec, wspec, wspec],
            out_specs=pl.BlockSpec((MOE_BLK, D), lambda i, be, nu: (i, 0)),
        ),
        out_shape=jax.ShapeDtypeStruct((nrows, D), f32),
        compiler_params=_cparams(("arbitrary",)),
        name="moe_ffn",
    )(blk_expert, n_used, x_sorted, wg, wu, wd)


def _combine_kernel(x_ref, y_ref, mod_ref, o_ref):
    o_ref[0] = x_ref[0] + mod_ref[0, :, 5 * D:6 * D] * y_ref[0]


def _combine_final_kernel(x_ref, y_ref, mod_ref, g_ref, o_ref):
    xn = x_ref[0] + mod_ref[0, :, 5 * D:6 * D] * y_ref[0]
    o_ref[0] = _rmsnorm(xn, g_ref[...])


def _combine_call(x, y, mod, final_g):
    nbatch, tb, _ = x.shape
    ctx_tiles = CTX // TM
    if final_g is None:
        nt = tb // TM
        tile = pl.BlockSpec((1, TM, D), lambda b, i: (b, i, 0))
        return pl.pallas_call(
            _combine_kernel,
            grid=(nbatch, nt),
            in_specs=[tile, tile, pl.BlockSpec((1, 1, 6 * D), _mod_index(nbatch, ctx_tiles))],
            out_specs=tile,
            out_shape=jax.ShapeDtypeStruct((nbatch, tb, D), f32),
            compiler_params=_cparams(("parallel", "arbitrary")),
            name="ffn_residual",
        )(x, y, mod)
    nt = (tb - CTX) // TM
    tile_in = pl.BlockSpec((1, TM, D), lambda b, i: (b, i + ctx_tiles, 0))
    return pl.pallas_call(
        _combine_final_kernel,
        grid=(nbatch, nt),
        in_specs=[tile_in, tile_in, pl.BlockSpec((1, 1, 6 * D), lambda b, i: (b, 0, 0)),
                  pl.BlockSpec((1, D), lambda b, i: (0, 0))],
        out_specs=pl.BlockSpec((1, TM, D), lambda b, i: (b, i, 0)),
        out_shape=jax.ShapeDtypeStruct((nbatch, tb - CTX, D), f32),
        compiler_params=_cparams(("parallel", "arbitrary")),
        name="ffn_residual_final",
    )(x, y, mod, final_g)


def _dispatch_plan(e_idx):
    t = e_idx.shape[0]
    n_rows = TOP_K * t
    nblk = -(-n_rows // MOE_BLK) + N_EXPERTS
    flat_e = e_idx.reshape(-1)
    onehot = (flat_e[:, None] == jnp.arange(N_EXPERTS, dtype=i32)[None, :]).astype(i32)
    rank = jnp.cumsum(onehot, axis=0) - onehot
    counts = jnp.sum(onehot, axis=0)
    padded = (counts + MOE_BLK - 1) // MOE_BLK * MOE_BLK
    pad_ends = jnp.cumsum(padded)
    pad_starts = pad_ends - padded
    dest = pad_starts[flat_e] + jnp.take_along_axis(rank, flat_e[:, None], axis=1)[:, 0]
    blk_expert = jnp.minimum(
        jnp.searchsorted(pad_ends, jnp.arange(nblk, dtype=i32) * MOE_BLK, side='right'),
        N_EXPERTS - 1).astype(i32)
    n_used = (pad_ends[-1] // MOE_BLK).astype(i32).reshape(1)
    return dest.astype(i32), blk_expert, n_used, nblk


def _rope_tables(tb):
    rows = SEQ // GRID_W
    row = jnp.repeat(jnp.arange(rows, dtype=f32), GRID_W)
    col = jnp.tile(jnp.arange(GRID_W, dtype=f32), rows)
    inv_freq = ROPE_BASE ** (-jnp.arange(ROPE_FREQS, dtype=f32) / ROPE_FREQS)
    ang_r = row[:, None] * inv_freq[None, :]
    ang_c = col[:, None] * inv_freq[None, :]
    ang = jnp.concatenate([ang_r, ang_r, ang_c, ang_c], axis=-1)
    cos = jnp.concatenate([jnp.ones((CTX, HEAD_DIM), f32), jnp.cos(ang)], axis=0)
    sin = jnp.concatenate([jnp.zeros((CTX, HEAD_DIM), f32), jnp.sin(ang)], axis=0)
    return jnp.tile(cos, (1, LANES // HEAD_DIM)), jnp.tile(sin, (1, LANES // HEAD_DIM))


def _block_diag2(w):
    z = jnp.zeros_like(w[0])
    return jnp.concatenate([jnp.concatenate([w[0], z], axis=1), jnp.concatenate([z, w[1]], axis=1)], axis=0)


def kernel(x, c, ctx, c_ctx, w_mod, b_mod, norm_mix_g, norm_ffn_g, w_in, w_out, att_sink, shift_mu_prev, shift_mu_next, decay_w0, decay_w2, iclr_a0, iclr_a2, vres_v0, vres_v1, vres_v2, gate_g2, k_k, k_a, r_k, ln_x_w, ln_x_b, router_w, router_b, expert_w_gate, expert_w_up, expert_w_down, final_norm_g):
    nbatch = x.shape[0]
    depth = w_mod.shape[0]
    tb = ctx.shape[1] + x.shape[1]
    xa = jnp.concatenate([ctx, x], axis=1)
    nb_pad = -(-(nbatch + 1) // 8) * 8
    cond = jnp.zeros((nb_pad, D), f32).at[:nbatch].set(c).at[nbatch].set(c_ctx)
    mod_all = _mod_call(cond, w_mod, b_mod).reshape(depth, nb_pad, 1, 6 * D)
    cos, sin = _rope_tables(tb)
    wr_hi = router_w.T.astype(bf16)
    wr_lo = (router_w.T - wr_hi.astype(f32)).astype(bf16)
    w_router = jnp.stack([wr_hi, wr_lo])
    b_router = router_b.reshape(N_EXPERTS, 1)
    v_first = None
    for l in range(depth):
        mod = mod_all[l]
        q, k, v, rw = _in_proj_call(xa, mod, norm_mix_g[l].reshape(1, D), w_in[l].astype(bf16), cos, sin)
        att = _attn_call(att_sink[l], q, k, v)
        mu = jnp.stack([shift_mu_prev[l], shift_mu_next[l]])
        v0 = vres_v0[l - 1] if l > 0 else jnp.zeros((RW,), f32)
        vec = jnp.stack([k_k[l], k_a[l], r_k[l].reshape(RW), v0,
                         decay_w0[l, 0], decay_w0[l, 1], iclr_a0[l, 0], iclr_a0[l, 1]])
        if l > 0:
            v1 = jnp.zeros((RW, LANES), f32).at[:, :LORA_VRES].set(vres_v1[l - 1]).astype(bf16)
            v2 = jnp.zeros((LANES, RW), f32).at[:LORA_VRES].set(vres_v2[l - 1]).astype(bf16)
        else:
            v1 = v2 = None
        r_, v_, kk, bv, g, kd, lw, bd = _feat_call(
            rw, v_first, mu, vec, _block_diag2(decay_w2[l]).astype(bf16), _block_diag2(iclr_a2[l]).astype(bf16),
            gate_g2[l].astype(bf16), v1, v2)
        if l == 0:
            v_first = v_
        yf, yb = _scan_call(r_, v_, kk, kd, lw, bd)
        ln = jnp.stack([ln_x_w[l], ln_x_b[l]])
        xa, hf, e_t, g_t = _mix_call(xa, att, yf, yb, bv, g, mod, ln, w_out[l].astype(bf16),
                                     norm_ffn_g[l].reshape(1, D), w_router, b_router)
        ntok = nbatch * tb
        e_idx = jnp.swapaxes(e_t, 1, 2).reshape(ntok, TOP_K)
        gates = jnp.swapaxes(g_t, 1, 2).reshape(ntok, TOP_K)
        dest, blk_expert, n_used, nblk = _dispatch_plan(e_idx)
        row_token = jnp.full((nblk * MOE_BLK,), ntok, i32).at[dest].set(jnp.arange(TOP_K * ntok, dtype=i32) // TOP_K)
        hf_pad = jnp.concatenate([hf.reshape(ntok, D), jnp.zeros((1, D), bf16)], axis=0)
        x_sorted = hf_pad[row_token]
        y_sorted = _ffn_call(blk_expert, n_used, x_sorted, expert_w_gate[l].astype(bf16),
                             expert_w_up[l].astype(bf16), expert_w_down[l].astype(bf16))
        y_tok = y_sorted[dest].reshape(ntok, TOP_K, D)
        moe = jnp.sum(y_tok * gates[:, :, None], axis=1).reshape(nbatch, tb, D)
        xa = _combine_call(xa, moe, mod, final_norm_g.reshape(1, D) if l == depth - 1 else None)
    return xa
```

```python
import functools
import math

import jax
import jax.numpy as jnp
from jax import lax
from jax.experimental import pallas as pl
from jax.experimental.pallas import tpu as pltpu

f32 = jnp.float32
bf16 = jnp.bfloat16
i32 = jnp.int32

D = 1024
SEQ = 4096
CTX = 256
TB = CTX + SEQ
GRID_W = 64
HEAD_DIM = 64
ATT_WIDTH = 512
ATT_HEADS = 8
KV_HEADS = 2
ATT_GROUP = ATT_HEADS // KV_HEADS
KV_WIDTH = KV_HEADS * HEAD_DIM
RW = 512
RWKV_HEADS = 8
LORA_DECAY = 64
LORA_ICLR = 64
LORA_VRES = 32
LORA_GATE = 128
RWKV_COLS = 3 * RW + 2 * (LORA_DECAY + LORA_ICLR) + LORA_GATE
ATT_COLS = ATT_WIDTH + 2 * KV_WIDTH
IN_COLS = ATT_COLS + RWKV_COLS
N_EXPERTS = 16
N_GROUPS = 4
EXPERTS_PER_GROUP = 4
TOP_K = 2
MOE_BLK = 256
NORM_EPS = 1e-6
GN_EPS = 64e-5
NEG_INF = -1e30
ATT_SCALE = HEAD_DIM ** -0.5
ROPE_BASE = 10000.0
ROPE_FREQS = HEAD_DIM // 4

LANES = 128
TM = 256
QB = 128
CH = 64
HG = 4
GW = HG * HEAD_DIM
VMEM_LIMIT = 48 * 1024 * 1024


def _cparams(sem):
    return pltpu.CompilerParams(dimension_semantics=sem, vmem_limit_bytes=VMEM_LIMIT)


def _sigmoid(x):
    return 1.0 / (1.0 + jnp.exp(-x))


def _div_pow2(x, n):
    assert n & (n - 1) == 0
    return lax.shift_right_logical(x, n.bit_length() - 1)


def _mod_pow2(x, n):
    assert n & (n - 1) == 0
    return lax.bitwise_and(x, n - 1)


def _dot(a, b):
    return jnp.dot(a, b, preferred_element_type=f32)


def _dot_nt(a, b):
    return lax.dot_general(a, b, (((1,), (1,)), ((), ())), preferred_element_type=f32)


def _dot_tn(a, b):
    return lax.dot_general(a, b, (((0,), (0,)), ((), ())), preferred_element_type=f32)


def _split2(x):
    hi = x.astype(bf16)
    lo = (x - hi.astype(f32)).astype(bf16)
    return hi, lo


def _split3(x):
    hi = x.astype(bf16)
    r1 = x - hi.astype(f32)
    mid = r1.astype(bf16)
    lo = (r1 - mid.astype(f32)).astype(bf16)
    return hi, mid, lo


def _dot_exact_rhs(x, m):
    hi, mid, lo = _split3(x)
    return _dot(hi, m) + _dot(mid, m) + _dot(lo, m)


def _dot_exact_lhs(m, x):
    hi, mid, lo = _split3(x)
    return _dot(m, hi) + _dot(m, mid) + _dot(m, lo)


def _rmsnorm(x, g):
    ms = jnp.mean(x * x, axis=-1, keepdims=True)
    return x * lax.rsqrt(ms + NORM_EPS) * g


def _head_ones():
    r = _div_pow2(lax.broadcasted_iota(i32, (RW, RW), 0), HEAD_DIM)
    c = _div_pow2(lax.broadcasted_iota(i32, (RW, RW), 1), HEAD_DIM)
    return (r == c).astype(bf16)


def _mod_kernel(c_ref, w_ref, b_ref, o_ref):
    c = c_ref[...]
    s = (c * _sigmoid(c)).astype(bf16)
    o_ref[0] = _dot(s, w_ref[0].astype(bf16)) + b_ref[0]


def _mod_call(cond, w_mod, b_mod):
    nb = cond.shape[0]
    depth = w_mod.shape[0]
    tn = 1024
    return pl.pallas_call(
        _mod_kernel,
        grid=(depth, 6 * D // tn),
        in_specs=[
            pl.BlockSpec((nb, D), lambda l, j: (0, 0)),
            pl.BlockSpec((1, D, tn), lambda l, j: (l, 0, j)),
            pl.BlockSpec((1, 1, tn), lambda l, j: (l, 0, j)),
        ],
        out_specs=pl.BlockSpec((1, nb, tn), lambda l, j: (l, 0, j)),
        out_shape=jax.ShapeDtypeStruct((depth, nb, 6 * D), f32),
        compiler_params=_cparams(("arbitrary", "arbitrary")),
        name="mod",
    )(cond, w_mod, b_mod.reshape(depth, 1, 6 * D))


def _mod_index(nbatch, ctx_tiles):
    return lambda b, i: (jnp.where(i < ctx_tiles, nbatch, b), 0, 0)


def _in_proj_kernel(x_ref, mod_ref, g_ref, w_ref, cos_ref, sin_ref, q_ref, k_ref, v_ref, rw_ref):
    x = x_ref[0]
    tm = x.shape[0]
    h = _rmsnorm(x, g_ref[...])
    sh = mod_ref[0, :, 0:D]
    sc = mod_ref[0, :, D:2 * D]
    h = (h * (1.0 + sc) + sh).astype(bf16)
    p = _dot(h, w_ref[...])
    cos = cos_ref[...]
    sin = sin_ref[...]
    lane = lax.broadcasted_iota(i32, (tm, LANES), 1)
    first_half = _mod_pow2(lane, 2 * ROPE_FREQS) < ROPE_FREQS

    def rope(t):
        rot = jnp.where(first_half, -pltpu.roll(t, LANES - ROPE_FREQS, 1), pltpu.roll(t, ROPE_FREQS, 1))
        return t * cos + rot * sin

    for j in range(ATT_WIDTH // LANES):
        t = rope(p[:, j * LANES:(j + 1) * LANES]).astype(bf16)
        q_ref[0, 2 * j] = t[:, :HEAD_DIM]
        q_ref[0, 2 * j + 1] = t[:, HEAD_DIM:]
    kt = rope(p[:, ATT_WIDTH:ATT_WIDTH + KV_WIDTH]).astype(bf16)
    vt = p[:, ATT_WIDTH + KV_WIDTH:ATT_COLS].astype(bf16)
    for hh in range(KV_HEADS):
        k_ref[0, hh] = kt[:, hh * HEAD_DIM:(hh + 1) * HEAD_DIM]
        v_ref[0, hh] = vt[:, hh * HEAD_DIM:(hh + 1) * HEAD_DIM]
    rw_ref[0] = p[:, ATT_COLS:]


def _in_proj_call(x, mod, g, w_in, cos, sin):
    nbatch, tb, _ = x.shape
    nt = tb // TM
    return pl.pallas_call(
        _in_proj_kernel,
        grid=(nbatch, nt),
        in_specs=[
            pl.BlockSpec((1, TM, D), lambda b, i: (b, i, 0)),
            pl.BlockSpec((1, 1, 6 * D), _mod_index(nbatch, CTX // TM)),
            pl.BlockSpec((1, D), lambda b, i: (0, 0)),
            pl.BlockSpec((D, IN_COLS), lambda b, i: (0, 0)),
            pl.BlockSpec((TM, LANES), lambda b, i: (i, 0)),
            pl.BlockSpec((TM, LANES), lambda b, i: (i, 0)),
        ],
        out_specs=[
            pl.BlockSpec((1, ATT_HEADS, TM, HEAD_DIM), lambda b, i: (b, 0, i, 0)),
            pl.BlockSpec((1, KV_HEADS, TM, HEAD_DIM), lambda b, i: (b, 0, i, 0)),
            pl.BlockSpec((1, KV_HEADS, TM, HEAD_DIM), lambda b, i: (b, 0, i, 0)),
            pl.BlockSpec((1, TM, RWKV_COLS), lambda b, i: (b, i, 0)),
        ],
        out_shape=[
            jax.ShapeDtypeStruct((nbatch, ATT_HEADS, tb, HEAD_DIM), bf16),
            jax.ShapeDtypeStruct((nbatch, KV_HEADS, tb, HEAD_DIM), bf16),
            jax.ShapeDtypeStruct((nbatch, KV_HEADS, tb, HEAD_DIM), bf16),
            jax.ShapeDtypeStruct((nbatch, tb, RWKV_COLS), f32),
        ],
        compiler_params=_cparams(("parallel", "arbitrary")),
        name="in_proj",
    )(x, mod, g, w_in, cos, sin)


def _attn_kernel(sink_ref, q_ref, kp_ref, kc_ref, kn_ref, vp_ref, vc_ref, vn_ref, kx_ref, vx_ref, o_ref,
                 *, nblk, ctx_blks):
    i = pl.program_id(1)
    is_lat = i >= ctx_blks
    prev_ok = jnp.logical_and(is_lat, i - 1 >= ctx_blks)
    next_ok = jnp.logical_and(is_lat, i + 1 <= nblk - 1)
    rows = ATT_GROUP * QB
    qi = _mod_pow2(lax.broadcasted_iota(i32, (rows, QB), 0), QB)
    kj = lax.broadcasted_iota(i32, (rows, QB), 1)
    mask_p = jnp.logical_and(kj >= qi, prev_ok)
    mask_n = jnp.logical_and(kj <= qi, next_ok)
    row_head = _div_pow2(lax.broadcasted_iota(i32, (rows, 1), 0), QB)
    outs = []
    for h in range(KV_HEADS):
        qh = q_ref[0, ATT_GROUP * h:ATT_GROUP * (h + 1)].reshape(rows, HEAD_DIM)
        s_p = jnp.where(mask_p, _dot_nt(qh, kp_ref[0, h]) * ATT_SCALE, NEG_INF)
        s_c = jnp.where(is_lat, _dot_nt(qh, kc_ref[0, h]) * ATT_SCALE, NEG_INF)
        s_n = jnp.where(mask_n, _dot_nt(qh, kn_ref[0, h]) * ATT_SCALE, NEG_INF)
        s_x = _dot_nt(qh, kx_ref[0, h]) * ATT_SCALE
        sink = jnp.zeros((rows, 1), f32)
        for g in range(ATT_GROUP):
            sink = jnp.where(row_head == g, sink_ref[ATT_GROUP * h + g], sink)
        m = jnp.maximum(jnp.max(s_p, axis=-1, keepdims=True), jnp.max(s_c, axis=-1, keepdims=True))
        m = jnp.maximum(m, jnp.max(s_n, axis=-1, keepdims=True))
        m = jnp.maximum(m, jnp.max(s_x, axis=-1, keepdims=True))
        m = jnp.maximum(m, sink)
        e_p = jnp.exp(s_p - m)
        e_c = jnp.exp(s_c - m)
        e_n = jnp.exp(s_n - m)
        e_x = jnp.exp(s_x - m)
        den = (jnp.sum(e_p, axis=-1, keepdims=True) + jnp.sum(e_c, axis=-1, keepdims=True)
               + jnp.sum(e_n, axis=-1, keepdims=True) + jnp.sum(e_x, axis=-1, keepdims=True)
               + jnp.exp(sink - m))
        o = (_dot(e_p.astype(bf16), vp_ref[0, h]) + _dot(e_c.astype(bf16), vc_ref[0, h])
             + _dot(e_n.astype(bf16), vn_ref[0, h]) + _dot(e_x.astype(bf16), vx_ref[0, h]))
        o = o / den
        outs.extend(o[g * QB:(g + 1) * QB] for g in range(ATT_GROUP))
    o_ref[0] = jnp.concatenate(outs, axis=1).astype(bf16)


def _attn_call(sink, q, k, v):
    nbatch, _, tb, _ = q.shape
    nblk = tb // QB
    ctx_blks = CTX // QB
    kv_blk = (1, KV_HEADS, QB, HEAD_DIM)
    prev_map = lambda b, i: (b, 0, jnp.maximum(i - 1, 0), 0)
    cur_map = lambda b, i: (b, 0, i, 0)
    next_map = lambda b, i: (b, 0, jnp.minimum(i + 1, nblk - 1), 0)
    ctx_spec = pl.BlockSpec((1, KV_HEADS, CTX, HEAD_DIM), lambda b, i: (b, 0, 0, 0))
    return pl.pallas_call(
        functools.partial(_attn_kernel, nblk=nblk, ctx_blks=ctx_blks),
        grid=(nbatch, nblk),
        in_specs=[
            pl.BlockSpec(memory_space=pltpu.SMEM),
            pl.BlockSpec((1, ATT_HEADS, QB, HEAD_DIM), cur_map),
            pl.BlockSpec(kv_blk, prev_map), pl.BlockSpec(kv_blk, cur_map), pl.BlockSpec(kv_blk, next_map),
            pl.BlockSpec(kv_blk, prev_map), pl.BlockSpec(kv_blk, cur_map), pl.BlockSpec(kv_blk, next_map),
            ctx_spec, ctx_spec,
        ],
        out_specs=pl.BlockSpec((1, QB, ATT_WIDTH), lambda b, i: (b, i, 0)),
        out_shape=jax.ShapeDtypeStruct((nbatch, tb, ATT_WIDTH), bf16),
        compiler_params=_cparams(("parallel", "arbitrary")),
        name="attention",
    )(sink, q, k, k, k, v, v, v, k, v)


def _feat_kernel(*refs, nt, ctx_tiles, has_vres):
    if has_vres:
        (rw_ref, hp_ref, hn_ref, vf_ref, mu_ref, vec_ref, w2_ref, a2_ref, g2_ref, v1_ref, v2_ref,
         r_ref, v_ref, kk_ref, bv_ref, g_ref, kd_ref, lw_ref, bd_ref) = refs
    else:
        (rw_ref, hp_ref, hn_ref, mu_ref, vec_ref, w2_ref, a2_ref, g2_ref,
         r_ref, v_ref, kk_ref, bv_ref, g_ref, kd_ref, lw_ref, bd_ref) = refs
    i = pl.program_id(1)
    u0 = rw_ref[0]
    tm = u0.shape[0]
    prev_zero = jnp.logical_or(i == 0, i == ctx_tiles)
    next_zero = jnp.logical_or(i == ctx_tiles - 1, i == nt - 1)
    halo_p = jnp.where(prev_zero, 0.0, hp_ref[0, 7:8, :])
    halo_n = jnp.where(next_zero, 0.0, hn_ref[0, 0:1, :])
    row = lax.broadcasted_iota(i32, (tm, 1), 0)
    prev = jnp.where(row == 0, halo_p, pltpu.roll(u0, 1, 0))
    nxt = jnp.where(row == tm - 1, halo_n, pltpu.roll(u0, tm - 1, 0))
    u = u0 + mu_ref[0:1, :] * (prev - u0) + mu_ref[1:2, :] * (nxt - u0)

    r = u[:, 0:RW]
    k = u[:, RW:2 * RW]
    v = u[:, 2 * RW:3 * RW]
    wd = u[:, 3 * RW:3 * RW + 2 * LORA_DECAY]
    ad = u[:, 3 * RW + 2 * LORA_DECAY:3 * RW + 2 * (LORA_DECAY + LORA_ICLR)]
    gd = u[:, 3 * RW + 2 * (LORA_DECAY + LORA_ICLR):]
    k_k = vec_ref[0:1, :]
    k_a = vec_ref[1:2, :]
    r_k = vec_ref[2:3, :]
    ones = _head_ones()

    if has_vres:
        lo = _dot(v.astype(bf16), v1_ref[...])
        gate = _sigmoid(vec_ref[3:4, :] + _dot(lo.astype(bf16), v2_ref[...]))
        v = v + (vf_ref[0] - v) * gate
    decay_in = _dot(jnp.tanh(wd).astype(bf16), w2_ref[...])
    a_in = _dot(ad.astype(bf16), a2_ref[...])
    kk = k * k_k
    n2 = _dot_exact_rhs(kk * kk, ones)
    kk = kk / jnp.maximum(jnp.sqrt(n2), 1e-12)
    g = _dot(_sigmoid(gd).astype(bf16), g2_ref[...])
    ksum = jnp.zeros_like(k)
    for d in range(2):
        w0 = vec_ref[4 + d:5 + d, :]
        a0 = vec_ref[6 + d:7 + d, :]
        lw = -_sigmoid(w0 + decay_in[:, d * RW:(d + 1) * RW]) * math.exp(-0.5)
        a = _sigmoid(a0 + a_in[:, d * RW:(d + 1) * RW])
        kd = k * (1.0 + (a - 1.0) * k_a)
        ksum = ksum + kd
        kd_ref[d, 0] = kd
        lw_ref[d, 0] = lw
        bd_ref[d, 0] = kk * a
    bonus = _dot_exact_rhs(r * ksum * r_k, ones)
    r_ref[0] = r
    v_ref[0] = v
    kk_ref[0] = kk
    bv_ref[0] = bonus * v
    g_ref[0] = g


def _feat_call(rw, v_first, mu, vec, w2bd, a2bd, g2, v1, v2):
    nbatch, tb, _ = rw.shape
    nt = tb // TM
    has_vres = v_first is not None
    sub = TM // 8
    tile = lambda w: pl.BlockSpec((1, TM, w), lambda b, i: (b, i, 0))
    full = lambda a: pl.BlockSpec(a.shape, lambda b, i: (0,) * a.ndim)
    in_specs = [
        tile(RWKV_COLS),
        pl.BlockSpec((1, 8, RWKV_COLS), lambda b, i: (b, jnp.maximum(i * sub - 1, 0), 0)),
        pl.BlockSpec((1, 8, RWKV_COLS), lambda b, i: (b, jnp.minimum((i + 1) * sub, tb // 8 - 1), 0)),
    ]
    args = [rw, rw, rw]
    if has_vres:
        in_specs.append(tile(RW))
        args.append(v_first)
    consts = [mu, vec, w2bd, a2bd, g2] + ([v1, v2] if has_vres else [])
    in_specs += [full(a) for a in consts]
    args += consts
    dir_spec = pl.BlockSpec((2, 1, TM, RW), lambda b, i: (0, b, i, 0))
    tok = jax.ShapeDtypeStruct((nbatch, tb, RW), f32)
    dtok = jax.ShapeDtypeStruct((2, nbatch, tb, RW), f32)
    return pl.pallas_call(
        functools.partial(_feat_kernel, nt=nt, ctx_tiles=CTX // TM, has_vres=has_vres),
        grid=(nbatch, nt),
        in_specs=in_specs,
        out_specs=[tile(RW)] * 5 + [dir_spec] * 3,
        out_shape=[tok] * 5 + [dtok] * 3,
        compiler_params=_cparams(("parallel", "arbitrary")),
        name="rwkv_features",
    )(*args)


def _block_diag_rows(x, width):
    cb = _div_pow2(lax.broadcasted_iota(i32, x.shape, 1), width)
    return jnp.concatenate([jnp.where(cb == h, x, jnp.zeros_like(x)) for h in range(HG)], axis=0)


def _scan_chunks(probs):
    wide = HG * CH
    row_t = lax.broadcasted_iota(i32, (CH, wide), 0)
    col_t = _mod_pow2(lax.broadcasted_iota(i32, (CH, wide), 1), CH)
    row_g = lax.broadcasted_iota(i32, (CH, GW), 0)
    incl_t = {False: col_t <= row_t, True: col_t >= row_t}
    strict_t = {False: col_t < row_t, True: col_t > row_t}
    eye_t = (row_t == col_t).astype(f32)
    rb = _div_pow2(lax.broadcasted_iota(i32, (GW, GW), 0), HEAD_DIM)
    cb = _div_pow2(lax.broadcasted_iota(i32, (GW, GW), 1), HEAD_DIM)
    n = len(probs)
    rev = [p[7] for p in probs]
    def cumsum_rows(x, reverse):
        s = 1
        while s < CH:
            if reverse:
                x = x + jnp.where(row_g < CH - s, pltpu.roll(x, CH - s, 0), 0.0)
            else:
                x = x + jnp.where(row_g >= s, pltpu.roll(x, s, 0), 0.0)
            s *= 2
        return x

    gam = [cumsum_rows(probs[i][5], rev[i]) for i in range(n)]
    ar, bk, k_t, b_t = [], [], [], []
    for i, (s_prev, r, v, kk, k, lw, b, _) in enumerate(probs):
        e_neg = jnp.exp(-gam[i])
        a_s = (-kk * jnp.exp(gam[i] - lw)).astype(bf16)
        r_s = (r * jnp.exp(gam[i])).astype(bf16)
        b_t.append(b * e_neg)
        k_t.append(k * e_neg)
        ar.append(jnp.concatenate([a_s, r_s], axis=0))
        bk.append(jnp.concatenate([_block_diag_rows(b_t[i].astype(bf16), HEAD_DIM),
                                   _block_diag_rows(k_t[i].astype(bf16), HEAD_DIM)], axis=0))
    gram = [_dot_nt(ar[i], bk[i]) for i in range(n)]
    ars = [_dot_nt(ar[i], probs[i][0].astype(bf16)) for i in range(n)]
    v_bd = [_block_diag_rows(probs[i][2].astype(bf16), HEAD_DIM) for i in range(n)]
    p0 = [jnp.where(strict_t[rev[i]], gram[i][:CH, :wide], 0.0).astype(bf16) for i in range(n)]
    lq = [jnp.concatenate([jnp.where(strict_t[rev[i]], gram[i][:CH, wide:], 0.0),
                           jnp.where(incl_t[rev[i]], gram[i][CH:, wide:], 0.0)], axis=0).astype(bf16)
          for i in range(n)]
    lqv = [_dot(lq[i], v_bd[i]) for i in range(n)]
    rhs = [ars[i][:CH] + lqv[i][:CH] for i in range(n)]
    t = [eye_t + p0[i].astype(f32) for i in range(n)]
    p = [_dot(p0[i], _block_diag_rows(p0[i], CH)).astype(bf16) for i in range(n)]
    m = 4
    while m < CH:
        tp = [_dot(jnp.concatenate([t[i].astype(bf16), p[i]], axis=0), _block_diag_rows(p[i], CH))
              for i in range(n)]
        t = [t[i] + tp[i][:CH] for i in range(n)]
        p = [tp[i][CH:].astype(bf16) for i in range(n)]
        m *= 2
    t = [t[i] + _dot(t[i].astype(bf16), _block_diag_rows(p[i], CH)) for i in range(n)]
    u = [_dot(t[i].astype(bf16), _block_diag_rows(rhs[i].astype(bf16), HEAD_DIM)) for i in range(n)]
    out = []
    for i in range(n):
        s_prev, v, lw = probs[i][0], probs[i][2], probs[i][5]
        q_b = jnp.where(incl_t[rev[i]], gram[i][CH:, :wide], 0.0).astype(bf16)
        u_bd = _block_diag_rows(u[i].astype(bf16), HEAD_DIM)
        y = ars[i][CH:] + lqv[i][CH:] + _dot(q_b, u_bd)
        eg = jnp.exp(jnp.sum(lw, axis=0, keepdims=True))
        vu = jnp.concatenate([v, u[i]], axis=0).astype(bf16)
        kb = jnp.concatenate([k_t[i] * eg, b_t[i] * eg], axis=0).astype(bf16)
        s_add = _dot_tn(vu, kb)
        out.append((s_prev * eg + jnp.where(rb == cb, s_add, 0.0), y))
    return out


def _scan_kernel(rf_ref, vf_ref, kkf_ref, kf_ref, lwf_ref, bf_ref,
                 rb_ref, vb_ref, kkb_ref, kb_ref, lwb_ref, bb_ref,
                 yf_ref, yb_ref, s_ref):
    @pl.when(pl.program_id(1) == 0)
    def _():
        s_ref[...] = jnp.zeros_like(s_ref)

    dirs = ((rf_ref, vf_ref, kkf_ref, kf_ref, lwf_ref, bf_ref, yf_ref),
            (rb_ref, vb_ref, kkb_ref, kb_ref, lwb_ref, bb_ref, yb_ref))
    probs, dest = [], []
    for d, (r_ref, v_ref, kk_ref, k_ref, lw_ref, b_ref, y_ref) in enumerate(dirs):
        for g in range(RW // GW):
            sl = slice(g * GW, (g + 1) * GW)
            probs.append((s_ref[d, g], r_ref[0, :, sl], v_ref[0, :, sl], kk_ref[0, :, sl],
                          k_ref[0, 0, :, sl], lw_ref[0, 0, :, sl], b_ref[0, 0, :, sl], d == 1))
            dest.append((d, g, y_ref, sl))
    for (d, g, y_ref, sl), (s_new, y) in zip(dest, _scan_chunks(probs)):
        s_ref[d, g] = s_new
        y_ref[0, :, sl] = y


def _scan_call(r, v, kk, kd, lw, bd):
    nbatch, tb, _ = r.shape
    nc = tb // CH
    cc = CTX // CH
    rev = lambda j: jnp.where(j < cc, cc - 1 - j, nc - 1 + cc - j)
    tok_f = pl.BlockSpec((1, CH, RW), lambda b, j: (b, j, 0))
    tok_b = pl.BlockSpec((1, CH, RW), lambda b, j: (b, rev(j), 0))
    dir_f = pl.BlockSpec((1, 1, CH, RW), lambda b, j: (0, b, j, 0))
    dir_b = pl.BlockSpec((1, 1, CH, RW), lambda b, j: (1, b, rev(j), 0))
    out = jax.ShapeDtypeStruct((nbatch, tb, RW), f32)
    return pl.pallas_call(
        _scan_kernel,
        grid=(nbatch, nc),
        in_specs=[tok_f, tok_f, tok_f, dir_f, dir_f, dir_f, tok_b, tok_b, tok_b, dir_b, dir_b, dir_b],
        out_specs=[tok_f, tok_b],
        out_shape=[out, out],
        scratch_shapes=[pltpu.VMEM((2, RW // GW, GW, GW), f32)],
        compiler_params=_cparams(("parallel", "arbitrary")),
        name="rwkv_scan",
    )(r, v, kk, kd, lw, bd, r, v, kk, kd, lw, bd)


def _mix_kernel(x_ref, att_ref, yf_ref, yb_ref, bv_ref, g_ref, mod_ref, ln_ref, wo_ref, gf_ref, wr_ref, rb_ref,
                xo_ref, hf_ref, e_ref, gt_ref):
    ones = _head_ones()
    y = yf_ref[0] + yb_ref[0]
    inv = 1.0 / HEAD_DIM
    mu = _dot_exact_rhs(y, ones) * inv
    dlt = y - mu
    var = _dot_exact_rhs(dlt * dlt, ones) * inv
    gn = dlt * lax.rsqrt(var + GN_EPS) * ln_ref[0:1, :] + ln_ref[1:2, :]
    rwk = ((gn + bv_ref[0]) * g_ref[0]).astype(bf16)
    mix = _dot(att_ref[0], wo_ref[0:ATT_WIDTH, :]) + _dot(rwk, wo_ref[ATT_WIDTH:, :])
    gate_a = mod_ref[0, :, 2 * D:3 * D]
    xn = x_ref[0] + gate_a * mix
    xo_ref[0] = xn
    sh = mod_ref[0, :, 3 * D:4 * D]
    sc = mod_ref[0, :, 4 * D:5 * D]
    hf = _rmsnorm(xn, gf_ref[...]) * (1.0 + sc) + sh
    hf_ref[0] = hf.astype(bf16)
    h_hi, h_lo = _split2(hf)
    logits = _dot_nt(wr_ref[0], h_hi) + _dot_nt(wr_ref[0], h_lo) + _dot_nt(wr_ref[1], h_hi)
    e0, e1, g0, g1 = _route_rows(_sigmoid(logits), rb_ref[...])
    e_ref[0] = jnp.concatenate([e0, e1], axis=0)
    gt_ref[0] = jnp.concatenate([g0, g1], axis=0)


def _route_rows(scores, bias):
    biased = scores + bias
    row = lambda a, e: a[e:e + 1, :]
    best = None
    for gi in range(N_GROUPS):
        m = [row(biased, gi * EXPERTS_PER_GROUP + j) for j in range(EXPERTS_PER_GROUP)]
        gs = None
        for a in range(EXPERTS_PER_GROUP):
            for b in range(a + 1, EXPERTS_PER_GROUP):
                pair = m[a] + m[b]
                gs = pair if gs is None else jnp.maximum(gs, pair)
        if best is None:
            best, g_idx = gs, jnp.zeros(gs.shape, i32)
        else:
            better = gs > best
            g_idx = jnp.where(better, gi, g_idx)
            best = jnp.where(better, gs, best)

    def pick(a, j):
        out = row(a, j)
        for gi in range(1, N_GROUPS):
            out = jnp.where(g_idx == gi, row(a, gi * EXPERTS_PER_GROUP + j), out)
        return out

    vb = [pick(biased, j) for j in range(EXPERTS_PER_GROUP)]
    vs = [pick(scores, j) for j in range(EXPERTS_PER_GROUP)]

    def argmax_first(vals):
        bv, bi = vals[0], jnp.zeros(vals[0].shape, i32)
        for j in range(1, len(vals)):
            better = vals[j] > bv
            bi = jnp.where(better, j, bi)
            bv = jnp.where(better, vals[j], bv)
        return bi

    i1 = argmax_first(vb)
    i2 = argmax_first([jnp.where(i1 == j, -jnp.inf, vb[j]) for j in range(EXPERTS_PER_GROUP)])
    sel = lambda idx: sum(jnp.where(idx == j, vs[j], 0.0) for j in range(EXPERTS_PER_GROUP))
    s1, s2 = sel(i1), sel(i2)
    tot = s1 + s2
    base = g_idx * EXPERTS_PER_GROUP
    return base + i1, base + i2, s1 / tot, s2 / tot


def _mix_call(x, att, yf, yb, bv, g, mod, ln, w_out, g_ffn, w_router, b_router):
    nbatch, tb, _ = x.shape
    nt = tb // TM
    tile = lambda w: pl.BlockSpec((1, TM, w), lambda b, i: (b, i, 0))
    full = lambda a: pl.BlockSpec(a.shape, lambda b, i: (0,) * a.ndim)
    route = pl.BlockSpec((1, TOP_K, TM), lambda b, i: (b, 0, i))
    return pl.pallas_call(
        _mix_kernel,
        grid=(nbatch, nt),
        in_specs=[tile(D), tile(ATT_WIDTH), tile(RW), tile(RW), tile(RW), tile(RW),
                  pl.BlockSpec((1, 1, 6 * D), _mod_index(nbatch, CTX // TM)),
                  full(ln), full(w_out), full(g_ffn), full(w_router), full(b_router)],
        out_specs=[tile(D), tile(D), route, route],
        out_shape=[jax.ShapeDtypeStruct((nbatch, tb, D), f32),
                   jax.ShapeDtypeStruct((nbatch, tb, D), bf16),
                   jax.ShapeDtypeStruct((nbatch, TOP_K, tb), i32),
                   jax.ShapeDtypeStruct((nbatch, TOP_K, tb), f32)],
        compiler_params=_cparams(("parallel", "arbitrary")),
        name="mix_out",
    )(x, att, yf, yb, bv, g, mod, ln, w_out, g_ffn, w_router, b_router)


def _ffn_kernel(be_ref, nu_ref, x_ref, wg_ref, wu_ref, wd_ref, y_ref):
    i = pl.program_id(0)

    @pl.when(i < nu_ref[0])
    def _():
        x = x_ref[...]
        gt = _dot(x, wg_ref[0])
        up = _dot(x, wu_ref[0])
        hid = (gt * _sigmoid(gt) * up).astype(bf16)
        y_ref[...] = _dot(hid, wd_ref[0])

    @pl.when(i >= nu_ref[0])
    def _():
        y_ref[...] = jnp.zeros_like(y_ref)


def _ffn_call(blk_expert, n_used, x_sorted, wg, wu, wd):
    nrows = x_sorted.shape[0]
    nblk = nrows // MOE_BLK
    wspec = pl.BlockSpec((1, D, D), lambda i, be, nu: (be[i], 0, 0))
    return pl.pallas_call(
        _ffn_kernel,
        grid_spec=pltpu.PrefetchScalarGridSpec(
            num_scalar_prefetch=2,
            grid=(nblk,),
            in_specs=[pl.BlockSpec((MOE_BLK, D), lambda i, be, nu: (i, 0)), wspec, wspec, wspec],
            out_specs=pl.BlockSpec((MOE_BLK, D), lambda i, be, nu: (i, 0)),
        ),
        out_shape=jax.ShapeDtypeStruct((nrows, D), f32),
        compiler_params=_cparams(("arbitrary",)),
        name="moe_ffn",
    )(blk_expert, n_used, x_sorted, wg, wu, wd)


def _combine_kernel(x_ref, y_ref, mod_ref, o_ref):
    o_ref[0] = x_ref[0] + mod_ref[0, :, 5 * D:6 * D] * y_ref[0]


def _combine_final_kernel(x_ref, y_ref, mod_ref, g_ref, o_ref):
    xn = x_ref[0] + mod_ref[0, :, 5 * D:6 * D] * y_ref[0]
    o_ref[0] = _rmsnorm(xn, g_ref[...])


def _combine_call(x, y, mod, final_g):
    nbatch, tb, _ = x.shape
    ctx_tiles = CTX // TM
    if final_g is None:
        nt = tb // TM
        tile = pl.BlockSpec((1, TM, D), lambda b, i: (b, i, 0))
        return pl.pallas_call(
            _combine_kernel,
            grid=(nbatch, nt),
            in_specs=[tile, tile, pl.BlockSpec((1, 1, 6 * D), _mod_index(nbatch, ctx_tiles))],
            out_specs=tile,
            out_shape=jax.ShapeDtypeStruct((nbatch, tb, D), f32),
            compiler_params=_cparams(("parallel", "arbitrary")),
            name="ffn_residual",
        )(x, y, mod)
    nt = (tb - CTX) // TM
    tile_in = pl.BlockSpec((1, TM, D), lambda b, i: (b, i + ctx_tiles, 0))
    return pl.pallas_call(
        _combine_final_kernel,
        grid=(nbatch, nt),
        in_specs=[tile_in, tile_in, pl.BlockSpec((1, 1, 6 * D), lambda b, i: (b, 0, 0)),
                  pl.BlockSpec((1, D), lambda b, i: (0, 0))],
        out_specs=pl.BlockSpec((1, TM, D), lambda b, i: (b, i, 0)),
        out_shape=jax.ShapeDtypeStruct((nbatch, tb - CTX, D), f32),
        compiler_params=_cparams(("parallel", "arbitrary")),
        name="ffn_residual_final",
    )(x, y, mod, final_g)


def _dispatch_plan(e_idx):
    t = e_idx.shape[0]
    n_rows = TOP_K * t
    nblk = -(-n_rows // MOE_BLK) + N_EXPERTS
    flat_e = e_idx.reshape(-1)
    onehot = (flat_e[:, None] == jnp.arange(N_EXPERTS, dtype=i32)[None, :]).astype(i32)
    rank = jnp.cumsum(onehot, axis=0) - onehot
    counts = jnp.sum(onehot, axis=0)
    padded = (counts + MOE_BLK - 1) // MOE_BLK * MOE_BLK
    pad_ends = jnp.cumsum(padded)
    pad_starts = pad_ends - padded
    dest = pad_starts[flat_e] + jnp.take_along_axis(rank, flat_e[:, None], axis=1)[:, 0]
    blk_expert = jnp.minimum(
        jnp.searchsorted(pad_ends, jnp.arange(nblk, dtype=i32) * MOE_BLK, side='right'),
        N_EXPERTS - 1).astype(i32)
    n_used = (pad_ends[-1] // MOE_BLK).astype(i32).reshape(1)
    return dest.astype(i32), blk_expert, n_used, nblk


def _rope_tables(tb):
    rows = SEQ // GRID_W
    row = jnp.repeat(jnp.arange(rows, dtype=f32), GRID_W)
    col = jnp.tile(jnp.arange(GRID_W, dtype=f32), rows)
    inv_freq = ROPE_BASE ** (-jnp.arange(ROPE_FREQS, dtype=f32) / ROPE_FREQS)
    ang_r = row[:, None] * inv_freq[None, :]
    ang_c = col[:, None] * inv_freq[None, :]
    ang = jnp.concatenate([ang_r, ang_r, ang_c, ang_c], axis=-1)
    cos = jnp.concatenate([jnp.ones((CTX, HEAD_DIM), f32), jnp.cos(ang)], axis=0)
    sin = jnp.concatenate([jnp.zeros((CTX, HEAD_DIM), f32), jnp.sin(ang)], axis=0)
    return jnp.tile(cos, (1, LANES // HEAD_DIM)), jnp.tile(sin, (1, LANES // HEAD_DIM))


def _block_diag2(w):
    z = jnp.zeros_like(w[0])
    return jnp.concatenate([jnp.concatenate([w[0], z], axis=1), jnp.concatenate([z, w[1]], axis=1)], axis=0)


def kernel(x, c, ctx, c_ctx, w_mod, b_mod, norm_mix_g, norm_ffn_g, w_in, w_out, att_sink, shift_mu_prev, shift_mu_next, decay_w0, decay_w2, iclr_a0, iclr_a2, vres_v0, vres_v1, vres_v2, gate_g2, k_k, k_a, r_k, ln_x_w, ln_x_b, router_w, router_b, expert_w_gate, expert_w_up, expert_w_down, final_norm_g):
    nbatch = x.shape[0]
    depth = w_mod.shape[0]
    tb = ctx.shape[1] + x.shape[1]
    xa = jnp.concatenate([ctx, x], axis=1)
    nb_pad = -(-(nbatch + 1) // 8) * 8
    cond = jnp.zeros((nb_pad, D), f32).at[:nbatch].set(c).at[nbatch].set(c_ctx)
    mod_all = _mod_call(cond, w_mod, b_mod).reshape(depth, nb_pad, 1, 6 * D)
    cos, sin = _rope_tables(tb)
    wr_hi = router_w.T.astype(bf16)
    wr_lo = (router_w.T - wr_hi.astype(f32)).astype(bf16)
    w_router = jnp.stack([wr_hi, wr_lo])
    b_router = router_b.reshape(N_EXPERTS, 1)
    v_first = None
    for l in range(depth):
        mod = mod_all[l]
        q, k, v, rw = _in_proj_call(xa, mod, norm_mix_g[l].reshape(1, D), w_in[l].astype(bf16), cos, sin)
        att = _attn_call(att_sink[l], q, k, v)
        mu = jnp.stack([shift_mu_prev[l], shift_mu_next[l]])
        v0 = vres_v0[l - 1] if l > 0 else jnp.zeros((RW,), f32)
        vec = jnp.stack([k_k[l], k_a[l], r_k[l].reshape(RW), v0,
                         decay_w0[l, 0], decay_w0[l, 1], iclr_a0[l, 0], iclr_a0[l, 1]])
        if l > 0:
            v1 = jnp.zeros((RW, LANES), f32).at[:, :LORA_VRES].set(vres_v1[l - 1]).astype(bf16)
            v2 = jnp.zeros((LANES, RW), f32).at[:LORA_VRES].set(vres_v2[l - 1]).astype(bf16)
        else:
            v1 = v2 = None
        r_, v_, kk, bv, g, kd, lw, bd = _feat_call(
            rw, v_first, mu, vec, _block_diag2(decay_w2[l]).astype(bf16), _block_diag2(iclr_a2[l]).astype(bf16),
            gate_g2[l].astype(bf16), v1, v2)
        if l == 0:
            v_first = v_
        yf, yb = _scan_call(r_, v_, kk, kd, lw, bd)
        ln = jnp.stack([ln_x_w[l], ln_x_b[l]])
        xa, hf, e_t, g_t = _mix_call(xa, att, yf, yb, bv, g, mod, ln, w_out[l].astype(bf16),
                                     norm_ffn_g[l].reshape(1, D), w_router, b_router)
        ntok = nbatch * tb
        e_idx = jnp.swapaxes(e_t, 1, 2).reshape(ntok, TOP_K)
        gates = jnp.swapaxes(g_t, 1, 2).reshape(ntok, TOP_K)
        dest, blk_expert, n_used, nblk = _dispatch_plan(e_idx)
        row_token = jnp.full((nblk * MOE_BLK,), ntok, i32).at[dest].set(jnp.arange(TOP_K * ntok, dtype=i32) // TOP_K)
        hf_pad = jnp.concatenate([hf.reshape(ntok, D), jnp.zeros((1, D), bf16)], axis=0)
        x_sorted = hf_pad[row_token]
        y_sorted = _ffn_call(blk_expert, n_used, x_sorted, expert_w_gate[l].astype(bf16),
                             expert_w_up[l].astype(bf16), expert_w_down[l].astype(bf16))
        y_tok = y_sorted[dest].reshape(ntok, TOP_K, D)
        moe = jnp.sum(y_tok * gates[:, :, None], axis=1).reshape(nbatch, tb, D)
        xa = _combine_call(xa, moe, mod, final_norm_g.reshape(1, D) if l == depth - 1 else None)
    return xa
```

```python
import functools
import math

import jax
import jax.numpy as jnp
from jax import lax
from jax.experimental import pallas as pl
from jax.experimental.pallas import tpu as pltpu

f32 = jnp.float32
bf16 = jnp.bfloat16
i32 = jnp.int32
u32 = jnp.uint32

D = 1024
SEQ = 4096
CTX = 256
TB = CTX + SEQ
GRID_W = 64
HEAD_DIM = 64
ATT_WIDTH = 512
ATT_HEADS = 8
KV_HEADS = 2
ATT_GROUP = ATT_HEADS // KV_HEADS
KV_WIDTH = KV_HEADS * HEAD_DIM
RW = 512
RWKV_HEADS = 8
LORA_DECAY = 64
LORA_ICLR = 64
LORA_VRES = 32
LORA_GATE = 128
RWKV_COLS = 3 * RW + 2 * (LORA_DECAY + LORA_ICLR) + LORA_GATE
ATT_COLS = ATT_WIDTH + 2 * KV_WIDTH
IN_COLS = ATT_COLS + RWKV_COLS
N_EXPERTS = 16
N_GROUPS = 4
EXPERTS_PER_GROUP = 4
TOP_K = 2
MOE_BLK = 256
NORM_EPS = 1e-6
GN_EPS = 64e-5
NEG_INF = -1e30
ATT_SCALE = HEAD_DIM ** -0.5
ROPE_BASE = 10000.0
ROPE_FREQS = HEAD_DIM // 4

LANES = 128
TM = 256
QB = 128
CH = 64
HG = 4
GW = HG * HEAD_DIM
SEG_ALIGN = 8
LROWS = -(-(TOP_K * TM + N_EXPERTS * SEG_ALIGN) // LANES) * LANES
VMEM_LIMIT = 48 * 1024 * 1024


def _cparams(sem):
    return pltpu.CompilerParams(dimension_semantics=sem, vmem_limit_bytes=VMEM_LIMIT)


def _sigmoid(x):
    return 1.0 / (1.0 + jnp.exp(-x))


def _div_pow2(x, n):
    assert n & (n - 1) == 0
    return lax.shift_right_logical(x, n.bit_length() - 1)


def _mod_pow2(x, n):
    assert n & (n - 1) == 0
    return lax.bitwise_and(x, n - 1)


def _round_up_pow2(x, n):
    assert n & (n - 1) == 0
    return lax.bitwise_and(x + (n - 1), ~(n - 1))


def _dot(a, b):
    return jnp.dot(a, b, preferred_element_type=f32)


def _dot_nt(a, b):
    return lax.dot_general(a, b, (((1,), (1,)), ((), ())), preferred_element_type=f32)


def _dot_tn(a, b):
    return lax.dot_general(a, b, (((0,), (0,)), ((), ())), preferred_element_type=f32)


def _split2(x):
    hi = x.astype(bf16)
    lo = (x - hi.astype(f32)).astype(bf16)
    return hi, lo


def _split3(x):
    hi = x.astype(bf16)
    r1 = x - hi.astype(f32)
    mid = r1.astype(bf16)
    lo = (r1 - mid.astype(f32)).astype(bf16)
    return hi, mid, lo


def _dot_exact_rhs(x, m):
    hi, mid, lo = _split3(x)
    return _dot(hi, m) + _dot(mid, m) + _dot(lo, m)


def _dot_exact_lhs(m, x):
    hi, mid, lo = _split3(x)
    return _dot(m, hi) + _dot(m, mid) + _dot(m, lo)


def _rmsnorm(x, g):
    ms = jnp.mean(x * x, axis=-1, keepdims=True)
    return x * lax.rsqrt(ms + NORM_EPS) * g


def _head_ones():
    r = _div_pow2(lax.broadcasted_iota(i32, (RW, RW), 0), HEAD_DIM)
    c = _div_pow2(lax.broadcasted_iota(i32, (RW, RW), 1), HEAD_DIM)
    return (r == c).astype(bf16)


def _mod_kernel(c_ref, w_ref, b_ref, o_ref):
    c = c_ref[...]
    s = (c * _sigmoid(c)).astype(bf16)
    o_ref[0] = _dot(s, w_ref[0].astype(bf16)) + b_ref[0]


def _mod_call(cond, w_mod, b_mod):
    nb = cond.shape[0]
    depth = w_mod.shape[0]
    tn = 1024
    return pl.pallas_call(
        _mod_kernel,
        grid=(depth, 6 * D // tn),
        in_specs=[
            pl.BlockSpec((nb, D), lambda l, j: (0, 0)),
            pl.BlockSpec((1, D, tn), lambda l, j: (l, 0, j)),
            pl.BlockSpec((1, 1, tn), lambda l, j: (l, 0, j)),
        ],
        out_specs=pl.BlockSpec((1, nb, tn), lambda l, j: (l, 0, j)),
        out_shape=jax.ShapeDtypeStruct((depth, nb, 6 * D), f32),
        compiler_params=_cparams(("arbitrary", "arbitrary")),
        name="mod",
    )(cond, w_mod, b_mod.reshape(depth, 1, 6 * D))


def _mod_index(nbatch, ctx_tiles):
    return lambda b, i: (jnp.where(i < ctx_tiles, nbatch, b), 0, 0)


def _in_proj_kernel(x_ref, mod_ref, g_ref, w_ref, cos_ref, sin_ref, q_ref, k_ref, v_ref, rw_ref):
    x = x_ref[0]
    tm = x.shape[0]
    h = _rmsnorm(x, g_ref[...])
    sh = mod_ref[0, :, 0:D]
    sc = mod_ref[0, :, D:2 * D]
    h = (h * (1.0 + sc) + sh).astype(bf16)
    p = _dot(h, w_ref[...])
    cos = cos_ref[...]
    sin = sin_ref[...]
    lane = lax.broadcasted_iota(i32, (tm, LANES), 1)
    first_half = _mod_pow2(lane, 2 * ROPE_FREQS) < ROPE_FREQS

    def rope(t):
        rot = jnp.where(first_half, -pltpu.roll(t, LANES - ROPE_FREQS, 1), pltpu.roll(t, ROPE_FREQS, 1))
        return t * cos + rot * sin

    for j in range(ATT_WIDTH // LANES):
        t = rope(p[:, j * LANES:(j + 1) * LANES]).astype(bf16)
        q_ref[0, 2 * j] = t[:, :HEAD_DIM]
        q_ref[0, 2 * j + 1] = t[:, HEAD_DIM:]
    kt = rope(p[:, ATT_WIDTH:ATT_WIDTH + KV_WIDTH]).astype(bf16)
    vt = p[:, ATT_WIDTH + KV_WIDTH:ATT_COLS].astype(bf16)
    for hh in range(KV_HEADS):
        k_ref[0, hh] = kt[:, hh * HEAD_DIM:(hh + 1) * HEAD_DIM]
        v_ref[0, hh] = vt[:, hh * HEAD_DIM:(hh + 1) * HEAD_DIM]
    rw_ref[0] = p[:, ATT_COLS:]


def _in_proj_call(x, mod, g, w_in, cos, sin):
    nbatch, tb, _ = x.shape
    nt = tb // TM
    return pl.pallas_call(
        _in_proj_kernel,
        grid=(nbatch, nt),
        in_specs=[
            pl.BlockSpec((1, TM, D), lambda b, i: (b, i, 0)),
            pl.BlockSpec((1, 1, 6 * D), _mod_index(nbatch, CTX // TM)),
            pl.BlockSpec((1, D), lambda b, i: (0, 0)),
            pl.BlockSpec((D, IN_COLS), lambda b, i: (0, 0)),
            pl.BlockSpec((TM, LANES), lambda b, i: (i, 0)),
            pl.BlockSpec((TM, LANES), lambda b, i: (i, 0)),
        ],
        out_specs=[
            pl.BlockSpec((1, ATT_HEADS, TM, HEAD_DIM), lambda b, i: (b, 0, i, 0)),
            pl.BlockSpec((1, KV_HEADS, TM, HEAD_DIM), lambda b, i: (b, 0, i, 0)),
            pl.BlockSpec((1, KV_HEADS, TM, HEAD_DIM), lambda b, i: (b, 0, i, 0)),
            pl.BlockSpec((1, TM, RWKV_COLS), lambda b, i: (b, i, 0)),
        ],
        out_shape=[
            jax.ShapeDtypeStruct((nbatch, ATT_HEADS, tb, HEAD_DIM), bf16),
            jax.ShapeDtypeStruct((nbatch, KV_HEADS, tb, HEAD_DIM), bf16),
            jax.ShapeDtypeStruct((nbatch, KV_HEADS, tb, HEAD_DIM), bf16),
            jax.ShapeDtypeStruct((nbatch, tb, RWKV_COLS), f32),
        ],
        compiler_params=_cparams(("parallel", "arbitrary")),
        name="in_proj",
    )(x, mod, g, w_in, cos, sin)


def _attn_kernel(sink_ref, q_ref, kp_ref, kc_ref, kn_ref, vp_ref, vc_ref, vn_ref, kx_ref, vx_ref, o_ref,
                 *, nblk, ctx_blks):
    i = pl.program_id(1)
    is_lat = i >= ctx_blks
    prev_ok = jnp.logical_and(is_lat, i - 1 >= ctx_blks)
    next_ok = jnp.logical_and(is_lat, i + 1 <= nblk - 1)
    rows = ATT_GROUP * QB
    qi = _mod_pow2(lax.broadcasted_iota(i32, (rows, QB), 0), QB)
    kj = lax.broadcasted_iota(i32, (rows, QB), 1)
    mask_p = jnp.logical_and(kj >= qi, prev_ok)
    mask_n = jnp.logical_and(kj <= qi, next_ok)
    row_head = _div_pow2(lax.broadcasted_iota(i32, (rows, 1), 0), QB)
    outs = []
    for h in range(KV_HEADS):
        qh = q_ref[0, ATT_GROUP * h:ATT_GROUP * (h + 1)].reshape(rows, HEAD_DIM)
        s_p = jnp.where(mask_p, _dot_nt(qh, kp_ref[0, h]) * ATT_SCALE, NEG_INF)
        s_c = jnp.where(is_lat, _dot_nt(qh, kc_ref[0, h]) * ATT_SCALE, NEG_INF)
        s_n = jnp.where(mask_n, _dot_nt(qh, kn_ref[0, h]) * ATT_SCALE, NEG_INF)
        s_x = _dot_nt(qh, kx_ref[0, h]) * ATT_SCALE
        sink = jnp.zeros((rows, 1), f32)
        for g in range(ATT_GROUP):
            sink = jnp.where(row_head == g, sink_ref[ATT_GROUP * h + g], sink)
        m = jnp.maximum(jnp.max(s_p, axis=-1, keepdims=True), jnp.max(s_c, axis=-1, keepdims=True))
        m = jnp.maximum(m, jnp.max(s_n, axis=-1, keepdims=True))
        m = jnp.maximum(m, jnp.max(s_x, axis=-1, keepdims=True))
        m = jnp.maximum(m, sink)
        e_p = jnp.exp(s_p - m)
        e_c = jnp.exp(s_c - m)
        e_n = jnp.exp(s_n - m)
        e_x = jnp.exp(s_x - m)
        den = (jnp.sum(e_p, axis=-1, keepdims=True) + jnp.sum(e_c, axis=-1, keepdims=True)
               + jnp.sum(e_n, axis=-1, keepdims=True) + jnp.sum(e_x, axis=-1, keepdims=True)
               + jnp.exp(sink - m))
        o = (_dot(e_p.astype(bf16), vp_ref[0, h]) + _dot(e_c.astype(bf16), vc_ref[0, h])
             + _dot(e_n.astype(bf16), vn_ref[0, h]) + _dot(e_x.astype(bf16), vx_ref[0, h]))
        o = o / den
        outs.extend(o[g * QB:(g + 1) * QB] for g in range(ATT_GROUP))
    o_ref[0] = jnp.concatenate(outs, axis=1).astype(bf16)


def _attn_call(sink, q, k, v):
    nbatch, _, tb, _ = q.shape
    nblk = tb // QB
    ctx_blks = CTX // QB
    kv_blk = (1, KV_HEADS, QB, HEAD_DIM)
    prev_map = lambda b, i: (b, 0, jnp.maximum(i - 1, 0), 0)
    cur_map = lambda b, i: (b, 0, i, 0)
    next_map = lambda b, i: (b, 0, jnp.minimum(i + 1, nblk - 1), 0)
    ctx_spec = pl.BlockSpec((1, KV_HEADS, CTX, HEAD_DIM), lambda b, i: (b, 0, 0, 0))
    return pl.pallas_call(
        functools.partial(_attn_kernel, nblk=nblk, ctx_blks=ctx_blks),
        grid=(nbatch, nblk),
        in_specs=[
            pl.BlockSpec(memory_space=pltpu.SMEM),
            pl.BlockSpec((1, ATT_HEADS, QB, HEAD_DIM), cur_map),
            pl.BlockSpec(kv_blk, prev_map), pl.BlockSpec(kv_blk, cur_map), pl.BlockSpec(kv_blk, next_map),
            pl.BlockSpec(kv_blk, prev_map), pl.BlockSpec(kv_blk, cur_map), pl.BlockSpec(kv_blk, next_map),
            ctx_spec, ctx_spec,
        ],
        out_specs=pl.BlockSpec((1, QB, ATT_WIDTH), lambda b, i: (b, i, 0)),
        out_shape=jax.ShapeDtypeStruct((nbatch, tb, ATT_WIDTH), bf16),
        compiler_params=_cparams(("parallel", "arbitrary")),
        name="attention",
    )(sink, q, k, k, k, v, v, v, k, v)


def _feat_kernel(*refs, nt, ctx_tiles, has_vres):
    if has_vres:
        (rw_ref, hp_ref, hn_ref, vf_ref, mu_ref, vec_ref, w2_ref, a2_ref, g2_ref, v1_ref, v2_ref,
         r_ref, v_ref, kk_ref, bv_ref, g_ref, kd_ref, lw_ref, bd_ref) = refs
    else:
        (rw_ref, hp_ref, hn_ref, mu_ref, vec_ref, w2_ref, a2_ref, g2_ref,
         r_ref, v_ref, kk_ref, bv_ref, g_ref, kd_ref, lw_ref, bd_ref) = refs
    i = pl.program_id(1)
    u0 = rw_ref[0]
    tm = u0.shape[0]
    prev_zero = jnp.logical_or(i == 0, i == ctx_tiles)
    next_zero = jnp.logical_or(i == ctx_tiles - 1, i == nt - 1)
    halo_p = jnp.where(prev_zero, 0.0, hp_ref[0, 7:8, :])
    halo_n = jnp.where(next_zero, 0.0, hn_ref[0, 0:1, :])
    row = lax.broadcasted_iota(i32, (tm, 1), 0)
    prev = jnp.where(row == 0, halo_p, pltpu.roll(u0, 1, 0))
    nxt = jnp.where(row == tm - 1, halo_n, pltpu.roll(u0, tm - 1, 0))
    u = u0 + mu_ref[0:1, :] * (prev - u0) + mu_ref[1:2, :] * (nxt - u0)

    r = u[:, 0:RW]
    k = u[:, RW:2 * RW]
    v = u[:, 2 * RW:3 * RW]
    wd = u[:, 3 * RW:3 * RW + 2 * LORA_DECAY]
    ad = u[:, 3 * RW + 2 * LORA_DECAY:3 * RW + 2 * (LORA_DECAY + LORA_ICLR)]
    gd = u[:, 3 * RW + 2 * (LORA_DECAY + LORA_ICLR):]
    k_k = vec_ref[0:1, :]
    k_a = vec_ref[1:2, :]
    r_k = vec_ref[2:3, :]
    ones = _head_ones()

    if has_vres:
        lo = _dot(v.astype(bf16), v1_ref[...])
        gate = _sigmoid(vec_ref[3:4, :] + _dot(lo.astype(bf16), v2_ref[...]))
        v = v + (vf_ref[0] - v) * gate
    decay_in = _dot(jnp.tanh(wd).astype(bf16), w2_ref[...])
    a_in = _dot(ad.astype(bf16), a2_ref[...])
    kk = k * k_k
    n2 = _dot_exact_rhs(kk * kk, ones)
    kk = kk / jnp.maximum(jnp.sqrt(n2), 1e-12)
    g = _dot(_sigmoid(gd).astype(bf16), g2_ref[...])
    ksum = jnp.zeros_like(k)
    for d in range(2):
        w0 = vec_ref[4 + d:5 + d, :]
        a0 = vec_ref[6 + d:7 + d, :]
        lw = -_sigmoid(w0 + decay_in[:, d * RW:(d + 1) * RW]) * math.exp(-0.5)
        a = _sigmoid(a0 + a_in[:, d * RW:(d + 1) * RW])
        kd = k * (1.0 + (a - 1.0) * k_a)
        ksum = ksum + kd
        kd_ref[d, 0] = kd
        lw_ref[d, 0] = lw
        bd_ref[d, 0] = kk * a
    bonus = _dot_exact_rhs(r * ksum * r_k, ones)
    r_ref[0] = r
    v_ref[0] = v
    kk_ref[0] = kk
    bv_ref[0] = bonus * v
    g_ref[0] = g


def _feat_call(rw, v_first, mu, vec, w2bd, a2bd, g2, v1, v2):
    nbatch, tb, _ = rw.shape
    nt = tb // TM
    has_vres = v_first is not None
    sub = TM // 8
    tile = lambda w: pl.BlockSpec((1, TM, w), lambda b, i: (b, i, 0))
    full = lambda a: pl.BlockSpec(a.shape, lambda b, i: (0,) * a.ndim)
    in_specs = [
        tile(RWKV_COLS),
        pl.BlockSpec((1, 8, RWKV_COLS), lambda b, i: (b, jnp.maximum(i * sub - 1, 0), 0)),
        pl.BlockSpec((1, 8, RWKV_COLS), lambda b, i: (b, jnp.minimum((i + 1) * sub, tb // 8 - 1), 0)),
    ]
    args = [rw, rw, rw]
    if has_vres:
        in_specs.append(tile(RW))
        args.append(v_first)
    consts = [mu, vec, w2bd, a2bd, g2] + ([v1, v2] if has_vres else [])
    in_specs += [full(a) for a in consts]
    args += consts
    dir_spec = pl.BlockSpec((2, 1, TM, RW), lambda b, i: (0, b, i, 0))
    tok = jax.ShapeDtypeStruct((nbatch, tb, RW), f32)
    dtok = jax.ShapeDtypeStruct((2, nbatch, tb, RW), f32)
    return pl.pallas_call(
        functools.partial(_feat_kernel, nt=nt, ctx_tiles=CTX // TM, has_vres=has_vres),
        grid=(nbatch, nt),
        in_specs=in_specs,
        out_specs=[tile(RW)] * 5 + [dir_spec] * 3,
        out_shape=[tok] * 5 + [dtok] * 3,
        compiler_params=_cparams(("parallel", "arbitrary")),
        name="rwkv_features",
    )(*args)


def _block_diag_rows(x, width):
    cb = _div_pow2(lax.broadcasted_iota(i32, x.shape, 1), width)
    return jnp.concatenate([jnp.where(cb == h, x, jnp.zeros_like(x)) for h in range(HG)], axis=0)


def _scan_chunks(probs):
    wide = HG * CH
    row_t = lax.broadcasted_iota(i32, (CH, wide), 0)
    col_t = _mod_pow2(lax.broadcasted_iota(i32, (CH, wide), 1), CH)
    row_g = lax.broadcasted_iota(i32, (CH, GW), 0)
    incl_t = {False: col_t <= row_t, True: col_t >= row_t}
    strict_t = {False: col_t < row_t, True: col_t > row_t}
    eye_t = (row_t == col_t).astype(f32)
    rb = _div_pow2(lax.broadcasted_iota(i32, (GW, GW), 0), HEAD_DIM)
    cb = _div_pow2(lax.broadcasted_iota(i32, (GW, GW), 1), HEAD_DIM)
    n = len(probs)
    rev = [p[7] for p in probs]
    def cumsum_rows(x, reverse):
        s = 1
        while s < CH:
            if reverse:
                x = x + jnp.where(row_g < CH - s, pltpu.roll(x, CH - s, 0), 0.0)
            else:
                x = x + jnp.where(row_g >= s, pltpu.roll(x, s, 0), 0.0)
            s *= 2
        return x

    gam = [cumsum_rows(probs[i][5], rev[i]) for i in range(n)]
    ar, bk, k_t, b_t = [], [], [], []
    for i, (s_prev, r, v, kk, k, lw, b, _) in enumerate(probs):
        e_neg = jnp.exp(-gam[i])
        a_s = (-kk * jnp.exp(gam[i] - lw)).astype(bf16)
        r_s = (r * jnp.exp(gam[i])).astype(bf16)
        b_t.append(b * e_neg)
        k_t.append(k * e_neg)
        ar.append(jnp.concatenate([a_s, r_s], axis=0))
        bk.append(jnp.concatenate([_block_diag_rows(b_t[i].astype(bf16), HEAD_DIM),
                                   _block_diag_rows(k_t[i].astype(bf16), HEAD_DIM)], axis=0))
    gram = [_dot_nt(ar[i], bk[i]) for i in range(n)]
    ars = [_dot_nt(ar[i], probs[i][0].astype(bf16)) for i in range(n)]
    v_bd = [_block_diag_rows(probs[i][2].astype(bf16), HEAD_DIM) for i in range(n)]
    p0 = [jnp.where(strict_t[rev[i]], gram[i][:CH, :wide], 0.0).astype(bf16) for i in range(n)]
    lq = [jnp.concatenate([jnp.where(strict_t[rev[i]], gram[i][:CH, wide:], 0.0),
                           jnp.where(incl_t[rev[i]], gram[i][CH:, wide:], 0.0)], axis=0).astype(bf16)
          for i in range(n)]
    lqv = [_dot(lq[i], v_bd[i]) for i in range(n)]
    rhs = [ars[i][:CH] + lqv[i][:CH] for i in range(n)]
    t = [eye_t + p0[i].astype(f32) for i in range(n)]
    p = [_dot(p0[i], _block_diag_rows(p0[i], CH)).astype(bf16) for i in range(n)]
    m = 4
    while m < CH:
        tp = [_dot(jnp.concatenate([t[i].astype(bf16), p[i]], axis=0), _block_diag_rows(p[i], CH))
              for i in range(n)]
        t = [t[i] + tp[i][:CH] for i in range(n)]
        p = [tp[i][CH:].astype(bf16) for i in range(n)]
        m *= 2
    t = [t[i] + _dot(t[i].astype(bf16), _block_diag_rows(p[i], CH)) for i in range(n)]
    u = [_dot(t[i].astype(bf16), _block_diag_rows(rhs[i].astype(bf16), HEAD_DIM)) for i in range(n)]
    out = []
    for i in range(n):
        s_prev, v, lw = probs[i][0], probs[i][2], probs[i][5]
        q_b = jnp.where(incl_t[rev[i]], gram[i][CH:, :wide], 0.0).astype(bf16)
        u_bd = _block_diag_rows(u[i].astype(bf16), HEAD_DIM)
        y = ars[i][CH:] + lqv[i][CH:] + _dot(q_b, u_bd)
        eg = jnp.exp(jnp.sum(lw, axis=0, keepdims=True))
        vu = jnp.concatenate([v, u[i]], axis=0).astype(bf16)
        kb = jnp.concatenate([k_t[i] * eg, b_t[i] * eg], axis=0).astype(bf16)
        s_add = _dot_tn(vu, kb)
        out.append((s_prev * eg + jnp.where(rb == cb, s_add, 0.0), y))
    return out


def _scan_kernel(rf_ref, vf_ref, kkf_ref, kf_ref, lwf_ref, bf_ref,
                 rb_ref, vb_ref, kkb_ref, kb_ref, lwb_ref, bb_ref,
                 yf_ref, yb_ref, s_ref):
    @pl.when(pl.program_id(1) == 0)
    def _():
        s_ref[...] = jnp.zeros_like(s_ref)

    dirs = ((rf_ref, vf_ref, kkf_ref, kf_ref, lwf_ref, bf_ref, yf_ref),
            (rb_ref, vb_ref, kkb_ref, kb_ref, lwb_ref, bb_ref, yb_ref))
    probs, dest = [], []
    for d, (r_ref, v_ref, kk_ref, k_ref, lw_ref, b_ref, y_ref) in enumerate(dirs):
        for g in range(RW // GW):
            sl = slice(g * GW, (g + 1) * GW)
            probs.append((s_ref[d, g], r_ref[0, :, sl], v_ref[0, :, sl], kk_ref[0, :, sl],
                          k_ref[0, 0, :, sl], lw_ref[0, 0, :, sl], b_ref[0, 0, :, sl], d == 1))
            dest.append((d, g, y_ref, sl))
    for (d, g, y_ref, sl), (s_new, y) in zip(dest, _scan_chunks(probs)):
        s_ref[d, g] = s_new
        y_ref[0, :, sl] = y


def _scan_call(r, v, kk, kd, lw, bd):
    nbatch, tb, _ = r.shape
    nc = tb // CH
    cc = CTX // CH
    rev = lambda j: jnp.where(j < cc, cc - 1 - j, nc - 1 + cc - j)
    tok_f = pl.BlockSpec((1, CH, RW), lambda b, j: (b, j, 0))
    tok_b = pl.BlockSpec((1, CH, RW), lambda b, j: (b, rev(j), 0))
    dir_f = pl.BlockSpec((1, 1, CH, RW), lambda b, j: (0, b, j, 0))
    dir_b = pl.BlockSpec((1, 1, CH, RW), lambda b, j: (1, b, rev(j), 0))
    out = jax.ShapeDtypeStruct((nbatch, tb, RW), f32)
    return pl.pallas_call(
        _scan_kernel,
        grid=(nbatch, nc),
        in_specs=[tok_f, tok_f, tok_f, dir_f, dir_f, dir_f, tok_b, tok_b, tok_b, dir_b, dir_b, dir_b],
        out_specs=[tok_f, tok_b],
        out_shape=[out, out],
        scratch_shapes=[pltpu.VMEM((2, RW // GW, GW, GW), f32)],
        compiler_params=_cparams(("parallel", "arbitrary")),
        name="rwkv_scan",
    )(r, v, kk, kd, lw, bd, r, v, kk, kd, lw, bd)


def _pack_bf16_pairs(x):
    bits = pltpu.bitcast(x, u32)
    half = x.shape[1] // 2
    return bits[:, :half] | lax.shift_right_logical(bits[:, half:], jnp.uint32(16))


def _unpack_bf16_pairs(p):
    hi = pltpu.bitcast(p & jnp.uint32(0xFFFF0000), f32)
    lo = pltpu.bitcast(lax.shift_left(p, jnp.uint32(16)), f32)
    return jnp.concatenate([hi, lo], axis=1).astype(bf16)


def _mix_kernel(x_ref, att_ref, yf_ref, yb_ref, bv_ref, g_ref, mod_ref, ln_ref, wo_ref, gf_ref, wr_ref, rb_ref,
                xo_ref, xl_ref, lp_ref, gt_ref, cnt_ref):
    ones = _head_ones()
    y = yf_ref[0] + yb_ref[0]
    inv = 1.0 / HEAD_DIM
    mu = _dot_exact_rhs(y, ones) * inv
    dlt = y - mu
    var = _dot_exact_rhs(dlt * dlt, ones) * inv
    gn = dlt * lax.rsqrt(var + GN_EPS) * ln_ref[0:1, :] + ln_ref[1:2, :]
    rwk = ((gn + bv_ref[0]) * g_ref[0]).astype(bf16)
    mix = _dot(att_ref[0], wo_ref[0:ATT_WIDTH, :]) + _dot(rwk, wo_ref[ATT_WIDTH:, :])
    gate_a = mod_ref[0, :, 2 * D:3 * D]
    xn = x_ref[0] + gate_a * mix
    xo_ref[0] = xn
    sh = mod_ref[0, :, 3 * D:4 * D]
    sc = mod_ref[0, :, 4 * D:5 * D]
    hf = _rmsnorm(xn, gf_ref[...]) * (1.0 + sc) + sh
    h_hi, h_lo = _split2(hf)
    logits = _dot_nt(wr_ref[0], h_hi) + _dot_nt(wr_ref[0], h_lo) + _dot_nt(wr_ref[1], h_hi)
    _route_and_sort(hf, _sigmoid(logits), rb_ref[...], xl_ref, lp_ref, gt_ref, cnt_ref)


def _route_and_sort(hf, scores, bias, xl_ref, lp_ref, gt_ref, cnt_ref):
    tm = hf.shape[0]
    e0, e1, g0, g1 = _route_rows(scores, bias)
    gt_ref[0] = jnp.concatenate([g0, g1], axis=0)
    eio = lax.broadcasted_iota(i32, (N_EXPERTS, tm), 0)
    oh0 = (eio == e0).astype(f32)
    oh1 = (eio == e1).astype(f32)
    both = oh0 + oh1
    upper = (lax.broadcasted_iota(i32, (tm, tm), 0) < lax.broadcasted_iota(i32, (tm, tm), 1)).astype(bf16)
    before = _dot(both.astype(bf16), upper)
    cnt = jnp.sum(both, axis=1, keepdims=True)
    seg = _round_up_pow2(cnt.astype(i32), SEG_ALIGN).astype(f32)
    lower = (lax.broadcasted_iota(i32, (N_EXPERTS, N_EXPERTS), 0)
             > lax.broadcasted_iota(i32, (N_EXPERTS, N_EXPERTS), 1)).astype(bf16)
    seg_start = _dot(lower, jnp.broadcast_to(seg, (N_EXPERTS, tm)).astype(bf16))
    pos = seg_start + before
    lp0 = jnp.sum(oh0 * pos, axis=0, keepdims=True).astype(i32)
    lp1 = jnp.sum(oh1 * pos, axis=0, keepdims=True).astype(i32)
    lp_ref[0] = jnp.concatenate([lp0, lp1], axis=0)
    cnt_ref[0] = jnp.broadcast_to(cnt, (N_EXPERTS, LANES))
    jj = lax.broadcasted_iota(i32, (LROWS, tm), 0)
    sel_t = jnp.logical_or(jj == lp0, jj == lp1).astype(bf16)
    xl_ref[0] = _pack_bf16_pairs(_dot(sel_t, hf.astype(bf16)))


def _route_rows(scores, bias):
    biased = scores + bias
    row = lambda a, e: a[e:e + 1, :]
    best = None
    for gi in range(N_GROUPS):
        m = [row(biased, gi * EXPERTS_PER_GROUP + j) for j in range(EXPERTS_PER_GROUP)]
        gs = None
        for a in range(EXPERTS_PER_GROUP):
            for b in range(a + 1, EXPERTS_PER_GROUP):
                pair = m[a] + m[b]
                gs = pair if gs is None else jnp.maximum(gs, pair)
        if best is None:
            best, g_idx = gs, jnp.zeros(gs.shape, i32)
        else:
            better = gs > best
            g_idx = jnp.where(better, gi, g_idx)
            best = jnp.where(better, gs, best)

    def pick(a, j):
        out = row(a, j)
        for gi in range(1, N_GROUPS):
            out = jnp.where(g_idx == gi, row(a, gi * EXPERTS_PER_GROUP + j), out)
        return out

    vb = [pick(biased, j) for j in range(EXPERTS_PER_GROUP)]
    vs = [pick(scores, j) for j in range(EXPERTS_PER_GROUP)]

    def argmax_first(vals):
        bv, bi = vals[0], jnp.zeros(vals[0].shape, i32)
        for j in range(1, len(vals)):
            better = vals[j] > bv
            bi = jnp.where(better, j, bi)
            bv = jnp.where(better, vals[j], bv)
        return bi

    i1 = argmax_first(vb)
    i2 = argmax_first([jnp.where(i1 == j, -jnp.inf, vb[j]) for j in range(EXPERTS_PER_GROUP)])
    sel = lambda idx: sum(jnp.where(idx == j, vs[j], 0.0) for j in range(EXPERTS_PER_GROUP))
    s1, s2 = sel(i1), sel(i2)
    tot = s1 + s2
    base = g_idx * EXPERTS_PER_GROUP
    return base + i1, base + i2, s1 / tot, s2 / tot


def _mix_call(x, att, yf, yb, bv, g, mod, ln, w_out, g_ffn, w_router, b_router):
    nbatch, tb, _ = x.shape
    nt = tb // TM
    tile = lambda w: pl.BlockSpec((1, TM, w), lambda b, i: (b, i, 0))
    full = lambda a: pl.BlockSpec(a.shape, lambda b, i: (0,) * a.ndim)
    route = pl.BlockSpec((1, TOP_K, TM), lambda b, i: (b, 0, i))
    return pl.pallas_call(
        _mix_kernel,
        grid=(nbatch, nt),
        in_specs=[tile(D), tile(ATT_WIDTH), tile(RW), tile(RW), tile(RW), tile(RW),
                  pl.BlockSpec((1, 1, 6 * D), _mod_index(nbatch, CTX // TM)),
                  full(ln), full(w_out), full(g_ffn), full(w_router), full(b_router)],
        out_specs=[tile(D), pl.BlockSpec((1, LROWS, D // 2), lambda b, i: (b, i, 0)), route, route,
                   pl.BlockSpec((1, N_EXPERTS, LANES), lambda b, i: (b, i, 0))],
        out_shape=[jax.ShapeDtypeStruct((nbatch, tb, D), f32),
                   jax.ShapeDtypeStruct((nbatch, nt * LROWS, D // 2), u32),
                   jax.ShapeDtypeStruct((nbatch, TOP_K, tb), i32),
                   jax.ShapeDtypeStruct((nbatch, TOP_K, tb), f32),
                   jax.ShapeDtypeStruct((nbatch, nt * N_EXPERTS, LANES), f32)],
        compiler_params=_cparams(("parallel", "arbitrary")),
        name="mix_out",
    )(x, att, yf, yb, bv, g, mod, ln, w_out, g_ffn, w_router, b_router)


def _segment_copies(fn, tile, base_ref, seg_ref, ls_ref, src, dst, sem, src_is_global):
    for e in range(N_EXPERTS):
        idx = tile * N_EXPERTS + e
        seg = seg_ref[idx]
        g0 = base_ref[idx]
        l0 = ls_ref[idx]
        size = TM
        while size >= SEG_ALIGN:
            done = lax.bitwise_and(seg, ~(2 * size - 1))

            @pl.when(lax.bitwise_and(seg, size) != 0)
            def _():
                g_rows = pl.ds(pl.multiple_of(g0 + done, SEG_ALIGN), size)
                l_rows = pl.ds(pl.multiple_of(l0 + done, SEG_ALIGN), size)
                s_rows, d_rows = (g_rows, l_rows) if src_is_global else (l_rows, g_rows)
                fn(pltpu.make_async_copy(src.at[s_rows, :], dst.at[d_rows, :], sem))

            size //= 2


def _dispatch_kernel(base_ref, seg_ref, ls_ref, xl_ref, xs_in_ref, xs_ref, sem):
    del xs_in_ref
    tile = pl.program_id(0)
    local = xl_ref.at[pl.ds(pl.multiple_of(tile * LROWS, SEG_ALIGN), LROWS), :]
    args = (tile, base_ref, seg_ref, ls_ref, local, xs_ref, sem, False)
    _segment_copies(lambda cp: cp.start(), *args)
    _segment_copies(lambda cp: cp.wait(), *args)


def _dispatch_call(base, seg, lstart, x_local, nrows):
    ntiles = x_local.shape[0] // LROWS
    any_spec = pl.BlockSpec(memory_space=pl.ANY)
    return pl.pallas_call(
        _dispatch_kernel,
        grid_spec=pltpu.PrefetchScalarGridSpec(
            num_scalar_prefetch=3,
            grid=(ntiles,),
            in_specs=[any_spec, any_spec],
            out_specs=any_spec,
            scratch_shapes=[pltpu.SemaphoreType.DMA(())],
        ),
        out_shape=jax.ShapeDtypeStruct((nrows, D // 2), u32),
        input_output_aliases={4: 0},
        compiler_params=_cparams(("arbitrary",)),
        name="moe_dispatch",
    )(base, seg, lstart, x_local, jnp.zeros((nrows, D // 2), u32))


def _ffn_kernel(be_ref, nu_ref, x_ref, wg_ref, wu_ref, wd_ref, y_ref):
    i = pl.program_id(0)

    @pl.when(i < nu_ref[0])
    def _():
        x = _unpack_bf16_pairs(x_ref[...])
        gt = _dot(x, wg_ref[0])
        up = _dot(x, wu_ref[0])
        hid = (gt * _sigmoid(gt) * up).astype(bf16)
        y = _dot(hid, wd_ref[0])
        y_ref[...] = _pack_bf16_pairs(y.astype(bf16).astype(f32))

    @pl.when(i >= nu_ref[0])
    def _():
        y_ref[...] = jnp.zeros_like(y_ref)


def _ffn_call(blk_expert, n_used, x_sorted, wg, wu, wd):
    nrows = x_sorted.shape[0]
    nblk = nrows // MOE_BLK
    wspec = pl.BlockSpec((1, D, D), lambda i, be, nu: (be[i], 0, 0))
    return pl.pallas_call(
        _ffn_kernel,
        grid_spec=pltpu.PrefetchScalarGridSpec(
            num_scalar_prefetch=2,
            grid=(nblk,),
            in_specs=[pl.BlockSpec((MOE_BLK, D // 2), lambda i, be, nu: (i, 0)), wspec, wspec, wspec],
            out_specs=pl.BlockSpec((MOE_BLK, D // 2), lambda i, be, nu: (i, 0)),
        ),
        out_shape=jax.ShapeDtypeStruct((nrows, D // 2), u32),
        compiler_params=_cparams(("arbitrary",)),
        name="moe_ffn",
    )(blk_expert, n_used, x_sorted, wg, wu, wd)


def _combine_kernel(base_ref, seg_ref, ls_ref, x_ref, lp_ref, gt_ref, mod_ref, g_ref, y_ref, o_ref, ybuf, sem,
                    *, tiles_per_batch, first_tile, final):
    b = pl.program_id(0)
    i = pl.program_id(1)
    tile = b * tiles_per_batch + i + first_tile

    @pl.when(jnp.logical_and(b == 0, i == 0))
    def _():
        ybuf[...] = jnp.zeros_like(ybuf)

    args = (tile, base_ref, seg_ref, ls_ref, y_ref, ybuf, sem, True)
    _segment_copies(lambda cp: cp.start(), *args)
    _segment_copies(lambda cp: cp.wait(), *args)
    y_loc = _unpack_bf16_pairs(ybuf[...])
    tm = x_ref.shape[1]
    jj = lax.broadcasted_iota(i32, (LROWS, tm), 0)
    gmat = (jnp.where(jj == lp_ref[0, 0:1, :], gt_ref[0, 0:1, :], 0.0)
            + jnp.where(jj == lp_ref[0, 1:2, :], gt_ref[0, 1:2, :], 0.0))
    g_hi, g_lo = _split2(gmat)
    moe = _dot_tn(g_hi, y_loc) + _dot_tn(g_lo, y_loc)
    xn = x_ref[0] + mod_ref[0, :, 5 * D:6 * D] * moe
    o_ref[0] = _rmsnorm(xn, g_ref[...]) if final else xn


def _combine_call(base, seg, lstart, x, lpos, gates, mod, y_sorted, final_g):
    nbatch, tb, _ = x.shape
    ctx_tiles = CTX // TM
    tiles_per_batch = tb // TM
    final = final_g is not None
    first_tile = ctx_tiles if final else 0
    nt = tiles_per_batch - first_tile
    tile = pl.BlockSpec((1, TM, D), lambda b, i, *_: (b, i + first_tile, 0))
    route = pl.BlockSpec((1, TOP_K, TM), lambda b, i, *_: (b, 0, i + first_tile))
    mod_map = _mod_index(nbatch, ctx_tiles - first_tile)
    g_arr = final_g if final else jnp.ones((1, D), f32)
    return pl.pallas_call(
        functools.partial(_combine_kernel, tiles_per_batch=tiles_per_batch, first_tile=first_tile, final=final),
        grid_spec=pltpu.PrefetchScalarGridSpec(
            num_scalar_prefetch=3,
            grid=(nbatch, nt),
            in_specs=[tile, route, route,
                      pl.BlockSpec((1, 1, 6 * D), lambda b, i, *_: mod_map(b, i)),
                      pl.BlockSpec((1, D), lambda b, i, *_: (0, 0)),
                      pl.BlockSpec(memory_space=pl.ANY)],
            out_specs=pl.BlockSpec((1, TM, D), lambda b, i, *_: (b, i, 0)),
            scratch_shapes=[pltpu.VMEM((LROWS, D // 2), u32), pltpu.SemaphoreType.DMA(())],
        ),
        out_shape=jax.ShapeDtypeStruct((nbatch, nt * TM, D), f32),
        compiler_params=_cparams(("arbitrary", "arbitrary")),
        name="ffn_residual_final" if final else "ffn_residual",
    )(base, seg, lstart, x, lpos, gates, mod, g_arr, y_sorted)


def _moe_rows(ntok):
    ntiles = ntok // TM
    worst = TOP_K * ntok + ntiles * N_EXPERTS * (SEG_ALIGN - 1)
    return (-(-worst // MOE_BLK) + N_EXPERTS) * MOE_BLK


def _segment_plan(cnt, nblk):
    seg = (cnt + SEG_ALIGN - 1) // SEG_ALIGN * SEG_ALIGN
    lstart = jnp.cumsum(seg, axis=1) - seg
    rows = jnp.sum(seg, axis=0)
    padded = (rows + MOE_BLK - 1) // MOE_BLK * MOE_BLK
    pad_ends = jnp.cumsum(padded)
    base = (pad_ends - padded)[None, :] + jnp.cumsum(seg, axis=0) - seg
    blk_expert = jnp.minimum(
        jnp.searchsorted(pad_ends, jnp.arange(nblk, dtype=i32) * MOE_BLK, side='right'), N_EXPERTS - 1)
    n_used = (pad_ends[-1] // MOE_BLK).reshape(1)
    flat = lambda a: a.reshape(-1).astype(i32)
    return flat(base), flat(seg), flat(lstart), blk_expert.astype(i32), n_used.astype(i32)


def _rope_tables(tb):
    rows = SEQ // GRID_W
    row = jnp.repeat(jnp.arange(rows, dtype=f32), GRID_W)
    col = jnp.tile(jnp.arange(GRID_W, dtype=f32), rows)
    inv_freq = ROPE_BASE ** (-jnp.arange(ROPE_FREQS, dtype=f32) / ROPE_FREQS)
    ang_r = row[:, None] * inv_freq[None, :]
    ang_c = col[:, None] * inv_freq[None, :]
    ang = jnp.concatenate([ang_r, ang_r, ang_c, ang_c], axis=-1)
    cos = jnp.concatenate([jnp.ones((CTX, HEAD_DIM), f32), jnp.cos(ang)], axis=0)
    sin = jnp.concatenate([jnp.zeros((CTX, HEAD_DIM), f32), jnp.sin(ang)], axis=0)
    return jnp.tile(cos, (1, LANES // HEAD_DIM)), jnp.tile(sin, (1, LANES // HEAD_DIM))


def _block_diag2(w):
    z = jnp.zeros_like(w[0])
    return jnp.concatenate([jnp.concatenate([w[0], z], axis=1), jnp.concatenate([z, w[1]], axis=1)], axis=0)


def kernel(x, c, ctx, c_ctx, w_mod, b_mod, norm_mix_g, norm_ffn_g, w_in, w_out, att_sink, shift_mu_prev, shift_mu_next, decay_w0, decay_w2, iclr_a0, iclr_a2, vres_v0, vres_v1, vres_v2, gate_g2, k_k, k_a, r_k, ln_x_w, ln_x_b, router_w, router_b, expert_w_gate, expert_w_up, expert_w_down, final_norm_g):
    nbatch = x.shape[0]
    depth = w_mod.shape[0]
    tb = ctx.shape[1] + x.shape[1]
    xa = jnp.concatenate([ctx, x], axis=1)
    nb_pad = -(-(nbatch + 1) // 8) * 8
    cond = jnp.zeros((nb_pad, D), f32).at[:nbatch].set(c).at[nbatch].set(c_ctx)
    mod_all = _mod_call(cond, w_mod, b_mod).reshape(depth, nb_pad, 1, 6 * D)
    cos, sin = _rope_tables(tb)
    wr_hi = router_w.T.astype(bf16)
    wr_lo = (router_w.T - wr_hi.astype(f32)).astype(bf16)
    w_router = jnp.stack([wr_hi, wr_lo])
    b_router = router_b.reshape(N_EXPERTS, 1)
    v_first = None
    for l in range(depth):
        mod = mod_all[l]
        q, k, v, rw = _in_proj_call(xa, mod, norm_mix_g[l].reshape(1, D), w_in[l].astype(bf16), cos, sin)
        att = _attn_call(att_sink[l], q, k, v)
        mu = jnp.stack([shift_mu_prev[l], shift_mu_next[l]])
        v0 = vres_v0[l - 1] if l > 0 else jnp.zeros((RW,), f32)
        vec = jnp.stack([k_k[l], k_a[l], r_k[l].reshape(RW), v0,
                         decay_w0[l, 0], decay_w0[l, 1], iclr_a0[l, 0], iclr_a0[l, 1]])
        if l > 0:
            v1 = jnp.zeros((RW, LANES), f32).at[:, :LORA_VRES].set(vres_v1[l - 1]).astype(bf16)
            v2 = jnp.zeros((LANES, RW), f32).at[:LORA_VRES].set(vres_v2[l - 1]).astype(bf16)
        else:
            v1 = v2 = None
        r_, v_, kk, bv, g, kd, lw, bd = _feat_call(
            rw, v_first, mu, vec, _block_diag2(decay_w2[l]).astype(bf16), _block_diag2(iclr_a2[l]).astype(bf16),
            gate_g2[l].astype(bf16), v1, v2)
        if l == 0:
            v_first = v_
        yf, yb = _scan_call(r_, v_, kk, kd, lw, bd)
        ln = jnp.stack([ln_x_w[l], ln_x_b[l]])
        xa, x_local, lpos, gates, cnt = _mix_call(xa, att, yf, yb, bv, g, mod, ln, w_out[l].astype(bf16),
                                                  norm_ffn_g[l].reshape(1, D), w_router, b_router)
        ntok = nbatch * tb
        nrows = _moe_rows(ntok)
        cnt = cnt[:, :, 0].astype(i32).reshape(ntok // TM, N_EXPERTS)
        base, seg, lstart, blk_expert, n_used = _segment_plan(cnt, nrows // MOE_BLK)
        x_sorted = _dispatch_call(base, seg, lstart, x_local.reshape(-1, D // 2), nrows)
        y_sorted = _ffn_call(blk_expert, n_used, x_sorted, expert_w_gate[l].astype(bf16),
                             expert_w_up[l].astype(bf16), expert_w_down[l].astype(bf16))
        xa = _combine_call(base, seg, lstart, xa, lpos, gates, mod, y_sorted,
                           final_norm_g.reshape(1, D) if l == depth - 1 else None)
    return xa
```

```python
import functools
import math

import jax
import jax.numpy as jnp
from jax import lax
from jax.experimental import pallas as pl
from jax.experimental.pallas import tpu as pltpu

f32 = jnp.float32
bf16 = jnp.bfloat16
i32 = jnp.int32
u32 = jnp.uint32

D = 1024
SEQ = 4096
CTX = 256
TB = CTX + SEQ
GRID_W = 64
HEAD_DIM = 64
ATT_WIDTH = 512
ATT_HEADS = 8
KV_HEADS = 2
ATT_GROUP = ATT_HEADS // KV_HEADS
KV_WIDTH = KV_HEADS * HEAD_DIM
RW = 512
RWKV_HEADS = 8
LORA_DECAY = 64
LORA_ICLR = 64
LORA_VRES = 32
LORA_GATE = 128
RWKV_COLS = 3 * RW + 2 * (LORA_DECAY + LORA_ICLR) + LORA_GATE
ATT_COLS = ATT_WIDTH + 2 * KV_WIDTH
IN_COLS = ATT_COLS + RWKV_COLS
N_EXPERTS = 16
N_GROUPS = 4
EXPERTS_PER_GROUP = 4
TOP_K = 2
MOE_BLK = 256
NORM_EPS = 1e-6
GN_EPS = 64e-5
NEG_INF = -1e30
ATT_SCALE = HEAD_DIM ** -0.5
ROPE_BASE = 10000.0
ROPE_FREQS = HEAD_DIM // 4

LANES = 128
TM = 256
QB = 128
CH = 64
HG = 4
GW = HG * HEAD_DIM
SEG_ALIGN = 8
LROWS = -(-(TOP_K * TM + N_EXPERTS * SEG_ALIGN) // LANES) * LANES
VMEM_LIMIT = 48 * 1024 * 1024


def _cparams(sem):
    return pltpu.CompilerParams(dimension_semantics=sem, vmem_limit_bytes=VMEM_LIMIT)


def _sigmoid(x):
    return 1.0 / (1.0 + jnp.exp(-x))


def _div_pow2(x, n):
    assert n & (n - 1) == 0
    return lax.shift_right_logical(x, n.bit_length() - 1)


def _mod_pow2(x, n):
    assert n & (n - 1) == 0
    return lax.bitwise_and(x, n - 1)


def _round_up_pow2(x, n):
    assert n & (n - 1) == 0
    return lax.bitwise_and(x + (n - 1), ~(n - 1))


def _dot(a, b):
    return jnp.dot(a, b, preferred_element_type=f32)


def _dot_nt(a, b):
    return lax.dot_general(a, b, (((1,), (1,)), ((), ())), preferred_element_type=f32)


def _dot_tn(a, b):
    return lax.dot_general(a, b, (((0,), (0,)), ((), ())), preferred_element_type=f32)


def _split2(x):
    hi = x.astype(bf16)
    lo = (x - hi.astype(f32)).astype(bf16)
    return hi, lo


def _split3(x):
    hi = x.astype(bf16)
    r1 = x - hi.astype(f32)
    mid = r1.astype(bf16)
    lo = (r1 - mid.astype(f32)).astype(bf16)
    return hi, mid, lo


def _dot_exact_rhs(x, m):
    hi, mid, lo = _split3(x)
    return _dot(hi, m) + _dot(mid, m) + _dot(lo, m)


def _dot_exact_lhs(m, x):
    hi, mid, lo = _split3(x)
    return _dot(m, hi) + _dot(m, mid) + _dot(m, lo)


def _rmsnorm(x, g):
    ms = jnp.mean(x * x, axis=-1, keepdims=True)
    return x * lax.rsqrt(ms + NORM_EPS) * g


def _head_ones():
    r = _div_pow2(lax.broadcasted_iota(i32, (RW, RW), 0), HEAD_DIM)
    c = _div_pow2(lax.broadcasted_iota(i32, (RW, RW), 1), HEAD_DIM)
    return (r == c).astype(bf16)


def _mod_kernel(c_ref, w_ref, b_ref, o_ref):
    c = c_ref[...]
    s = (c * _sigmoid(c)).astype(bf16)
    o_ref[0] = _dot(s, w_ref[0].astype(bf16)) + b_ref[0]


def _mod_call(cond, w_mod, b_mod):
    nb = cond.shape[0]
    depth = w_mod.shape[0]
    tn = 1024
    return pl.pallas_call(
        _mod_kernel,
        grid=(depth, 6 * D // tn),
        in_specs=[
            pl.BlockSpec((nb, D), lambda l, j: (0, 0)),
            pl.BlockSpec((1, D, tn), lambda l, j: (l, 0, j)),
            pl.BlockSpec((1, 1, tn), lambda l, j: (l, 0, j)),
        ],
        out_specs=pl.BlockSpec((1, nb, tn), lambda l, j: (l, 0, j)),
        out_shape=jax.ShapeDtypeStruct((depth, nb, 6 * D), f32),
        compiler_params=_cparams(("arbitrary", "arbitrary")),
        name="mod",
    )(cond, w_mod, b_mod.reshape(depth, 1, 6 * D))


def _mod_index(nbatch, ctx_tiles):
    return lambda b, i: (jnp.where(i < ctx_tiles, nbatch, b), 0, 0)


def _in_proj_kernel(x_ref, mod_ref, g_ref, w_ref, cos_ref, sin_ref, q_ref, k_ref, v_ref, rw_ref):
    x = x_ref[0]
    tm = x.shape[0]
    h = _rmsnorm(x, g_ref[...])
    sh = mod_ref[0, :, 0:D]
    sc = mod_ref[0, :, D:2 * D]
    h = (h * (1.0 + sc) + sh).astype(bf16)
    p = _dot(h, w_ref[...])
    cos = cos_ref[...]
    sin = sin_ref[...]
    lane = lax.broadcasted_iota(i32, (tm, LANES), 1)
    first_half = _mod_pow2(lane, 2 * ROPE_FREQS) < ROPE_FREQS

    def rope(t):
        rot = jnp.where(first_half, -pltpu.roll(t, LANES - ROPE_FREQS, 1), pltpu.roll(t, ROPE_FREQS, 1))
        return t * cos + rot * sin

    for j in range(ATT_WIDTH // LANES):
        t = rope(p[:, j * LANES:(j + 1) * LANES]).astype(bf16)
        q_ref[0, 2 * j] = t[:, :HEAD_DIM]
        q_ref[0, 2 * j + 1] = t[:, HEAD_DIM:]
    kt = rope(p[:, ATT_WIDTH:ATT_WIDTH + KV_WIDTH]).astype(bf16)
    vt = p[:, ATT_WIDTH + KV_WIDTH:ATT_COLS].astype(bf16)
    for hh in range(KV_HEADS):
        k_ref[0, hh] = kt[:, hh * HEAD_DIM:(hh + 1) * HEAD_DIM]
        v_ref[0, hh] = vt[:, hh * HEAD_DIM:(hh + 1) * HEAD_DIM]
    rw_ref[0] = p[:, ATT_COLS:]


def _in_proj_call(x, mod, g, w_in, cos, sin):
    nbatch, tb, _ = x.shape
    nt = tb // TM
    return pl.pallas_call(
        _in_proj_kernel,
        grid=(nbatch, nt),
        in_specs=[
            pl.BlockSpec((1, TM, D), lambda b, i: (b, i, 0)),
            pl.BlockSpec((1, 1, 6 * D), _mod_index(nbatch, CTX // TM)),
            pl.BlockSpec((1, D), lambda b, i: (0, 0)),
            pl.BlockSpec((D, IN_COLS), lambda b, i: (0, 0)),
            pl.BlockSpec((TM, LANES), lambda b, i: (i, 0)),
            pl.BlockSpec((TM, LANES), lambda b, i: (i, 0)),
        ],
        out_specs=[
            pl.BlockSpec((1, ATT_HEADS, TM, HEAD_DIM), lambda b, i: (b, 0, i, 0)),
            pl.BlockSpec((1, KV_HEADS, TM, HEAD_DIM), lambda b, i: (b, 0, i, 0)),
            pl.BlockSpec((1, KV_HEADS, TM, HEAD_DIM), lambda b, i: (b, 0, i, 0)),
            pl.BlockSpec((1, TM, RWKV_COLS), lambda b, i: (b, i, 0)),
        ],
        out_shape=[
            jax.ShapeDtypeStruct((nbatch, ATT_HEADS, tb, HEAD_DIM), bf16),
            jax.ShapeDtypeStruct((nbatch, KV_HEADS, tb, HEAD_DIM), bf16),
            jax.ShapeDtypeStruct((nbatch, KV_HEADS, tb, HEAD_DIM), bf16),
            jax.ShapeDtypeStruct((nbatch, tb, RWKV_COLS), f32),
        ],
        compiler_params=_cparams(("parallel", "arbitrary")),
        name="in_proj",
    )(x, mod, g, w_in, cos, sin)


def _attn_kernel(sink_ref, q_ref, kp_ref, kc_ref, kn_ref, vp_ref, vc_ref, vn_ref, kx_ref, vx_ref, o_ref,
                 *, nblk, ctx_blks):
    i = pl.program_id(1)
    is_lat = i >= ctx_blks
    prev_ok = jnp.logical_and(is_lat, i - 1 >= ctx_blks)
    next_ok = jnp.logical_and(is_lat, i + 1 <= nblk - 1)
    rows = ATT_GROUP * QB
    qi = _mod_pow2(lax.broadcasted_iota(i32, (rows, QB), 0), QB)
    kj = lax.broadcasted_iota(i32, (rows, QB), 1)
    mask_p = jnp.logical_and(kj >= qi, prev_ok)
    mask_n = jnp.logical_and(kj <= qi, next_ok)
    row_head = _div_pow2(lax.broadcasted_iota(i32, (rows, 1), 0), QB)
    outs = []
    for h in range(KV_HEADS):
        qh = q_ref[0, ATT_GROUP * h:ATT_GROUP * (h + 1)].reshape(rows, HEAD_DIM)
        s_p = jnp.where(mask_p, _dot_nt(qh, kp_ref[0, h]) * ATT_SCALE, NEG_INF)
        s_c = jnp.where(is_lat, _dot_nt(qh, kc_ref[0, h]) * ATT_SCALE, NEG_INF)
        s_n = jnp.where(mask_n, _dot_nt(qh, kn_ref[0, h]) * ATT_SCALE, NEG_INF)
        s_x = _dot_nt(qh, kx_ref[0, h]) * ATT_SCALE
        sink = jnp.zeros((rows, 1), f32)
        for g in range(ATT_GROUP):
            sink = jnp.where(row_head == g, sink_ref[ATT_GROUP * h + g], sink)
        m = jnp.maximum(jnp.max(s_p, axis=-1, keepdims=True), jnp.max(s_c, axis=-1, keepdims=True))
        m = jnp.maximum(m, jnp.max(s_n, axis=-1, keepdims=True))
        m = jnp.maximum(m, jnp.max(s_x, axis=-1, keepdims=True))
        m = jnp.maximum(m, sink)
        e_p = jnp.exp(s_p - m)
        e_c = jnp.exp(s_c - m)
        e_n = jnp.exp(s_n - m)
        e_x = jnp.exp(s_x - m)
        den = (jnp.sum(e_p, axis=-1, keepdims=True) + jnp.sum(e_c, axis=-1, keepdims=True)
               + jnp.sum(e_n, axis=-1, keepdims=True) + jnp.sum(e_x, axis=-1, keepdims=True)
               + jnp.exp(sink - m))
        o = (_dot(e_p.astype(bf16), vp_ref[0, h]) + _dot(e_c.astype(bf16), vc_ref[0, h])
             + _dot(e_n.astype(bf16), vn_ref[0, h]) + _dot(e_x.astype(bf16), vx_ref[0, h]))
        o = o / den
        outs.extend(o[g * QB:(g + 1) * QB] for g in range(ATT_GROUP))
    o_ref[0] = jnp.concatenate(outs, axis=1).astype(bf16)


def _attn_call(sink, q, k, v):
    nbatch, _, tb, _ = q.shape
    nblk = tb // QB
    ctx_blks = CTX // QB
    kv_blk = (1, KV_HEADS, QB, HEAD_DIM)
    prev_map = lambda b, i: (b, 0, jnp.maximum(i - 1, 0), 0)
    cur_map = lambda b, i: (b, 0, i, 0)
    next_map = lambda b, i: (b, 0, jnp.minimum(i + 1, nblk - 1), 0)
    ctx_spec = pl.BlockSpec((1, KV_HEADS, CTX, HEAD_DIM), lambda b, i: (b, 0, 0, 0))
    return pl.pallas_call(
        functools.partial(_attn_kernel, nblk=nblk, ctx_blks=ctx_blks),
        grid=(nbatch, nblk),
        in_specs=[
            pl.BlockSpec(memory_space=pltpu.SMEM),
            pl.BlockSpec((1, ATT_HEADS, QB, HEAD_DIM), cur_map),
            pl.BlockSpec(kv_blk, prev_map), pl.BlockSpec(kv_blk, cur_map), pl.BlockSpec(kv_blk, next_map),
            pl.BlockSpec(kv_blk, prev_map), pl.BlockSpec(kv_blk, cur_map), pl.BlockSpec(kv_blk, next_map),
            ctx_spec, ctx_spec,
        ],
        out_specs=pl.BlockSpec((1, QB, ATT_WIDTH), lambda b, i: (b, i, 0)),
        out_shape=jax.ShapeDtypeStruct((nbatch, tb, ATT_WIDTH), bf16),
        compiler_params=_cparams(("parallel", "arbitrary")),
        name="attention",
    )(sink, q, k, k, k, v, v, v, k, v)


def _feat_kernel(*refs, nt, ctx_tiles, has_vres):
    if has_vres:
        (rw_ref, hp_ref, hn_ref, vf_ref, mu_ref, vec_ref, w2_ref, a2_ref, g2_ref, v1_ref, v2_ref,
         r_ref, v_ref, kk_ref, bv_ref, g_ref, kd_ref, lw_ref, bd_ref) = refs
    else:
        (rw_ref, hp_ref, hn_ref, mu_ref, vec_ref, w2_ref, a2_ref, g2_ref,
         r_ref, v_ref, kk_ref, bv_ref, g_ref, kd_ref, lw_ref, bd_ref) = refs
    i = pl.program_id(1)
    u0 = rw_ref[0]
    tm = u0.shape[0]
    prev_zero = jnp.logical_or(i == 0, i == ctx_tiles)
    next_zero = jnp.logical_or(i == ctx_tiles - 1, i == nt - 1)
    halo_p = jnp.where(prev_zero, 0.0, hp_ref[0, 7:8, :])
    halo_n = jnp.where(next_zero, 0.0, hn_ref[0, 0:1, :])
    row = lax.broadcasted_iota(i32, (tm, 1), 0)
    prev = jnp.where(row == 0, halo_p, pltpu.roll(u0, 1, 0))
    nxt = jnp.where(row == tm - 1, halo_n, pltpu.roll(u0, tm - 1, 0))
    u = u0 + mu_ref[0:1, :] * (prev - u0) + mu_ref[1:2, :] * (nxt - u0)

    r = u[:, 0:RW]
    k = u[:, RW:2 * RW]
    v = u[:, 2 * RW:3 * RW]
    wd = u[:, 3 * RW:3 * RW + 2 * LORA_DECAY]
    ad = u[:, 3 * RW + 2 * LORA_DECAY:3 * RW + 2 * (LORA_DECAY + LORA_ICLR)]
    gd = u[:, 3 * RW + 2 * (LORA_DECAY + LORA_ICLR):]
    k_k = vec_ref[0:1, :]
    k_a = vec_ref[1:2, :]
    r_k = vec_ref[2:3, :]
    ones = _head_ones()

    if has_vres:
        lo = _dot(v.astype(bf16), v1_ref[...])
        gate = _sigmoid(vec_ref[3:4, :] + _dot(lo.astype(bf16), v2_ref[...]))
        v = v + (vf_ref[0] - v) * gate
    decay_in = _dot(jnp.tanh(wd).astype(bf16), w2_ref[...])
    a_in = _dot(ad.astype(bf16), a2_ref[...])
    kk = k * k_k
    n2 = _dot_exact_rhs(kk * kk, ones)
    kk = kk / jnp.maximum(jnp.sqrt(n2), 1e-12)
    g = _dot(_sigmoid(gd).astype(bf16), g2_ref[...])
    ksum = jnp.zeros_like(k)
    for d in range(2):
        w0 = vec_ref[4 + d:5 + d, :]
        a0 = vec_ref[6 + d:7 + d, :]
        lw = -_sigmoid(w0 + decay_in[:, d * RW:(d + 1) * RW]) * math.exp(-0.5)
        a = _sigmoid(a0 + a_in[:, d * RW:(d + 1) * RW])
        kd = k * (1.0 + (a - 1.0) * k_a)
        ksum = ksum + kd
        kd_ref[d, 0] = kd
        lw_ref[d, 0] = lw
        bd_ref[d, 0] = kk * a
    bonus = _dot_exact_rhs(r * ksum * r_k, ones)
    r_ref[0] = r
    v_ref[0] = v
    kk_ref[0] = kk
    bv_ref[0] = bonus * v
    g_ref[0] = g


def _feat_call(rw, v_first, mu, vec, w2bd, a2bd, g2, v1, v2):
    nbatch, tb, _ = rw.shape
    nt = tb // TM
    has_vres = v_first is not None
    sub = TM // 8
    tile = lambda w: pl.BlockSpec((1, TM, w), lambda b, i: (b, i, 0))
    full = lambda a: pl.BlockSpec(a.shape, lambda b, i: (0,) * a.ndim)
    in_specs = [
        tile(RWKV_COLS),
        pl.BlockSpec((1, 8, RWKV_COLS), lambda b, i: (b, jnp.maximum(i * sub - 1, 0), 0)),
        pl.BlockSpec((1, 8, RWKV_COLS), lambda b, i: (b, jnp.minimum((i + 1) * sub, tb // 8 - 1), 0)),
    ]
    args = [rw, rw, rw]
    if has_vres:
        in_specs.append(tile(RW))
        args.append(v_first)
    consts = [mu, vec, w2bd, a2bd, g2] + ([v1, v2] if has_vres else [])
    in_specs += [full(a) for a in consts]
    args += consts
    dir_spec = pl.BlockSpec((2, 1, TM, RW), lambda b, i: (0, b, i, 0))
    tok = jax.ShapeDtypeStruct((nbatch, tb, RW), f32)
    dtok = jax.ShapeDtypeStruct((2, nbatch, tb, RW), f32)
    return pl.pallas_call(
        functools.partial(_feat_kernel, nt=nt, ctx_tiles=CTX // TM, has_vres=has_vres),
        grid=(nbatch, nt),
        in_specs=in_specs,
        out_specs=[tile(RW)] * 5 + [dir_spec] * 3,
        out_shape=[tok] * 5 + [dtok] * 3,
        compiler_params=_cparams(("parallel", "arbitrary")),
        name="rwkv_features",
    )(*args)


def _block_diag_rows(x, width):
    cb = _div_pow2(lax.broadcasted_iota(i32, x.shape, 1), width)
    return jnp.concatenate([jnp.where(cb == h, x, jnp.zeros_like(x)) for h in range(HG)], axis=0)


def _scan_chunks(probs):
    wide = HG * CH
    row_t = lax.broadcasted_iota(i32, (CH, wide), 0)
    col_t = _mod_pow2(lax.broadcasted_iota(i32, (CH, wide), 1), CH)
    row_g = lax.broadcasted_iota(i32, (CH, GW), 0)
    incl_t = {False: col_t <= row_t, True: col_t >= row_t}
    strict_t = {False: col_t < row_t, True: col_t > row_t}
    eye_t = (row_t == col_t).astype(f32)
    rb = _div_pow2(lax.broadcasted_iota(i32, (GW, GW), 0), HEAD_DIM)
    cb = _div_pow2(lax.broadcasted_iota(i32, (GW, GW), 1), HEAD_DIM)
    n = len(probs)
    rev = [p[7] for p in probs]
    def cumsum_rows(x, reverse):
        s = 1
        while s < CH:
            if reverse:
                x = x + jnp.where(row_g < CH - s, pltpu.roll(x, CH - s, 0), 0.0)
            else:
                x = x + jnp.where(row_g >= s, pltpu.roll(x, s, 0), 0.0)
            s *= 2
        return x

    gam = [cumsum_rows(probs[i][5], rev[i]) for i in range(n)]
    ar, bk, k_t, b_t = [], [], [], []
    for i, (s_prev, r, v, kk, k, lw, b, _) in enumerate(probs):
        e_neg = jnp.exp(-gam[i])
        a_s = (-kk * jnp.exp(gam[i] - lw)).astype(bf16)
        r_s = (r * jnp.exp(gam[i])).astype(bf16)
        b_t.append(b * e_neg)
        k_t.append(k * e_neg)
        ar.append(jnp.concatenate([a_s, r_s], axis=0))
        bk.append(jnp.concatenate([_block_diag_rows(b_t[i].astype(bf16), HEAD_DIM),
                                   _block_diag_rows(k_t[i].astype(bf16), HEAD_DIM)], axis=0))
    gram = [_dot_nt(ar[i], bk[i]) for i in range(n)]
    ars = [_dot_nt(ar[i], probs[i][0].astype(bf16)) for i in range(n)]
    v_bd = [_block_diag_rows(probs[i][2].astype(bf16), HEAD_DIM) for i in range(n)]
    p0 = [jnp.where(strict_t[rev[i]], gram[i][:CH, :wide], 0.0).astype(bf16) for i in range(n)]
    lq = [jnp.concatenate([jnp.where(strict_t[rev[i]], gram[i][:CH, wide:], 0.0),
                           jnp.where(incl_t[rev[i]], gram[i][CH:, wide:], 0.0)], axis=0).astype(bf16)
          for i in range(n)]
    lqv = [_dot(lq[i], v_bd[i]) for i in range(n)]
    rhs = [ars[i][:CH] + lqv[i][:CH] for i in range(n)]
    t = [eye_t + p0[i].astype(f32) for i in range(n)]
    p = [_dot(p0[i], _block_diag_rows(p0[i], CH)).astype(bf16) for i in range(n)]
    m = 4
    while m < CH:
        tp = [_dot(jnp.concatenate([t[i].astype(bf16), p[i]], axis=0), _block_diag_rows(p[i], CH))
              for i in range(n)]
        t = [t[i] + tp[i][:CH] for i in range(n)]
        p = [tp[i][CH:].astype(bf16) for i in range(n)]
        m *= 2
    t = [t[i] + _dot(t[i].astype(bf16), _block_diag_rows(p[i], CH)) for i in range(n)]
    u = [_dot(t[i].astype(bf16), _block_diag_rows(rhs[i].astype(bf16), HEAD_DIM)) for i in range(n)]
    out = []
    for i in range(n):
        s_prev, v, lw = probs[i][0], probs[i][2], probs[i][5]
        q_b = jnp.where(incl_t[rev[i]], gram[i][CH:, :wide], 0.0).astype(bf16)
        u_bd = _block_diag_rows(u[i].astype(bf16), HEAD_DIM)
        y = ars[i][CH:] + lqv[i][CH:] + _dot(q_b, u_bd)
        eg = jnp.exp(jnp.sum(lw, axis=0, keepdims=True))
        vu = jnp.concatenate([v, u[i]], axis=0).astype(bf16)
        kb = jnp.concatenate([k_t[i] * eg, b_t[i] * eg], axis=0).astype(bf16)
        s_add = _dot_tn(vu, kb)
        out.append((s_prev * eg + jnp.where(rb == cb, s_add, 0.0), y))
    return out


def _scan_kernel(rf_ref, vf_ref, kkf_ref, kf_ref, lwf_ref, bf_ref,
                 rb_ref, vb_ref, kkb_ref, kb_ref, lwb_ref, bb_ref,
                 yf_ref, yb_ref, s_ref):
    @pl.when(pl.program_id(1) == 0)
    def _():
        s_ref[...] = jnp.zeros_like(s_ref)

    dirs = ((rf_ref, vf_ref, kkf_ref, kf_ref, lwf_ref, bf_ref, yf_ref),
            (rb_ref, vb_ref, kkb_ref, kb_ref, lwb_ref, bb_ref, yb_ref))
    probs, dest = [], []
    for d, (r_ref, v_ref, kk_ref, k_ref, lw_ref, b_ref, y_ref) in enumerate(dirs):
        for g in range(RW // GW):
            sl = slice(g * GW, (g + 1) * GW)
            probs.append((s_ref[d, g], r_ref[0, :, sl], v_ref[0, :, sl], kk_ref[0, :, sl],
                          k_ref[0, 0, :, sl], lw_ref[0, 0, :, sl], b_ref[0, 0, :, sl], d == 1))
            dest.append((d, g, y_ref, sl))
    for (d, g, y_ref, sl), (s_new, y) in zip(dest, _scan_chunks(probs)):
        s_ref[d, g] = s_new
        y_ref[0, :, sl] = y


def _scan_call(r, v, kk, kd, lw, bd):
    nbatch, tb, _ = r.shape
    nc = tb // CH
    cc = CTX // CH
    rev = lambda j: jnp.where(j < cc, cc - 1 - j, nc - 1 + cc - j)
    tok_f = pl.BlockSpec((1, CH, RW), lambda b, j: (b, j, 0))
    tok_b = pl.BlockSpec((1, CH, RW), lambda b, j: (b, rev(j), 0))
    dir_f = pl.BlockSpec((1, 1, CH, RW), lambda b, j: (0, b, j, 0))
    dir_b = pl.BlockSpec((1, 1, CH, RW), lambda b, j: (1, b, rev(j), 0))
    out = jax.ShapeDtypeStruct((nbatch, tb, RW), f32)
    return pl.pallas_call(
        _scan_kernel,
        grid=(nbatch, nc),
        in_specs=[tok_f, tok_f, tok_f, dir_f, dir_f, dir_f, tok_b, tok_b, tok_b, dir_b, dir_b, dir_b],
        out_specs=[tok_f, tok_b],
        out_shape=[out, out],
        scratch_shapes=[pltpu.VMEM((2, RW // GW, GW, GW), f32)],
        compiler_params=_cparams(("parallel", "arbitrary")),
        name="rwkv_scan",
    )(r, v, kk, kd, lw, bd, r, v, kk, kd, lw, bd)


def _pack_bf16_pairs(x):
    bits = pltpu.bitcast(x, u32)
    half = x.shape[1] // 2
    return bits[:, :half] | lax.shift_right_logical(bits[:, half:], jnp.uint32(16))


def _unpack_bf16_pairs(p):
    hi = pltpu.bitcast(p & jnp.uint32(0xFFFF0000), f32)
    lo = pltpu.bitcast(lax.shift_left(p, jnp.uint32(16)), f32)
    return jnp.concatenate([hi, lo], axis=1).astype(bf16)


def _mix_kernel(x_ref, att_ref, yf_ref, yb_ref, bv_ref, g_ref, mod_ref, ln_ref, wo_ref, gf_ref, wr_ref, rb_ref,
                xo_ref, xl_ref, lp_ref, gt_ref, cnt_ref):
    ones = _head_ones()
    y = yf_ref[0] + yb_ref[0]
    inv = 1.0 / HEAD_DIM
    mu = _dot_exact_rhs(y, ones) * inv
    dlt = y - mu
    var = _dot_exact_rhs(dlt * dlt, ones) * inv
    gn = dlt * lax.rsqrt(var + GN_EPS) * ln_ref[0:1, :] + ln_ref[1:2, :]
    rwk = ((gn + bv_ref[0]) * g_ref[0]).astype(bf16)
    mix = _dot(att_ref[0], wo_ref[0:ATT_WIDTH, :]) + _dot(rwk, wo_ref[ATT_WIDTH:, :])
    gate_a = mod_ref[0, :, 2 * D:3 * D]
    xn = x_ref[0] + gate_a * mix
    xo_ref[0] = xn
    sh = mod_ref[0, :, 3 * D:4 * D]
    sc = mod_ref[0, :, 4 * D:5 * D]
    hf = _rmsnorm(xn, gf_ref[...]) * (1.0 + sc) + sh
    h_hi, h_lo = _split2(hf)
    logits = _dot_nt(wr_ref[0], h_hi) + _dot_nt(wr_ref[0], h_lo) + _dot_nt(wr_ref[1], h_hi)
    _route_and_sort(hf, _sigmoid(logits), rb_ref[...], xl_ref, lp_ref, gt_ref, cnt_ref)


def _route_and_sort(hf, scores, bias, xl_ref, lp_ref, gt_ref, cnt_ref):
    tm = hf.shape[0]
    e0, e1, g0, g1 = _route_rows(scores, bias)
    gt_ref[0] = jnp.concatenate([g0, g1], axis=0)
    eio = lax.broadcasted_iota(i32, (N_EXPERTS, tm), 0)
    oh0 = (eio == e0).astype(f32)
    oh1 = (eio == e1).astype(f32)
    both = oh0 + oh1
    upper = (lax.broadcasted_iota(i32, (tm, tm), 0) < lax.broadcasted_iota(i32, (tm, tm), 1)).astype(bf16)
    before = _dot(both.astype(bf16), upper)
    cnt = jnp.sum(both, axis=1, keepdims=True)
    seg = _round_up_pow2(cnt.astype(i32), SEG_ALIGN).astype(f32)
    lower = (lax.broadcasted_iota(i32, (N_EXPERTS, N_EXPERTS), 0)
             > lax.broadcasted_iota(i32, (N_EXPERTS, N_EXPERTS), 1)).astype(bf16)
    seg_start = _dot(lower, jnp.broadcast_to(seg, (N_EXPERTS, tm)).astype(bf16))
    pos = seg_start + before
    lp0 = jnp.sum(oh0 * pos, axis=0, keepdims=True).astype(i32)
    lp1 = jnp.sum(oh1 * pos, axis=0, keepdims=True).astype(i32)
    lp_ref[0] = jnp.concatenate([lp0, lp1], axis=0)
    cnt_ref[0] = jnp.broadcast_to(cnt, (N_EXPERTS, LANES))
    jj = lax.broadcasted_iota(i32, (LROWS, tm), 0)
    sel_t = jnp.logical_or(jj == lp0, jj == lp1).astype(bf16)
    xl_ref[0] = _pack_bf16_pairs(_dot(sel_t, hf.astype(bf16)))


def _route_rows(scores, bias):
    biased = scores + bias
    row = lambda a, e: a[e:e + 1, :]
    best = None
    for gi in range(N_GROUPS):
        m = [row(biased, gi * EXPERTS_PER_GROUP + j) for j in range(EXPERTS_PER_GROUP)]
        gs = None
        for a in range(EXPERTS_PER_GROUP):
            for b in range(a + 1, EXPERTS_PER_GROUP):
                pair = m[a] + m[b]
                gs = pair if gs is None else jnp.maximum(gs, pair)
        if best is None:
            best, g_idx = gs, jnp.zeros(gs.shape, i32)
        else:
            better = gs > best
            g_idx = jnp.where(better, gi, g_idx)
            best = jnp.where(better, gs, best)

    def pick(a, j):
        out = row(a, j)
        for gi in range(1, N_GROUPS):
            out = jnp.where(g_idx == gi, row(a, gi * EXPERTS_PER_GROUP + j), out)
        return out

    vb = [pick(biased, j) for j in range(EXPERTS_PER_GROUP)]
    vs = [pick(scores, j) for j in range(EXPERTS_PER_GROUP)]

    def argmax_first(vals):
        bv, bi = vals[0], jnp.zeros(vals[0].shape, i32)
        for j in range(1, len(vals)):
            better = vals[j] > bv
            bi = jnp.where(better, j, bi)
            bv = jnp.where(better, vals[j], bv)
        return bi

    i1 = argmax_first(vb)
    i2 = argmax_first([jnp.where(i1 == j, -jnp.inf, vb[j]) for j in range(EXPERTS_PER_GROUP)])
    sel = lambda idx: sum(jnp.where(idx == j, vs[j], 0.0) for j in range(EXPERTS_PER_GROUP))
    s1, s2 = sel(i1), sel(i2)
    tot = s1 + s2
    base = g_idx * EXPERTS_PER_GROUP
    return base + i1, base + i2, s1 / tot, s2 / tot


def _mix_call(x, att, yf, yb, bv, g, mod, ln, w_out, g_ffn, w_router, b_router):
    nbatch, tb, _ = x.shape
    nt = tb // TM
    tile = lambda w: pl.BlockSpec((1, TM, w), lambda b, i: (b, i, 0))
    full = lambda a: pl.BlockSpec(a.shape, lambda b, i: (0,) * a.ndim)
    route = pl.BlockSpec((1, TOP_K, TM), lambda b, i: (b, 0, i))
    return pl.pallas_call(
        _mix_kernel,
        grid=(nbatch, nt),
        in_specs=[tile(D), tile(ATT_WIDTH), tile(RW), tile(RW), tile(RW), tile(RW),
                  pl.BlockSpec((1, 1, 6 * D), _mod_index(nbatch, CTX // TM)),
                  full(ln), full(w_out), full(g_ffn), full(w_router), full(b_router)],
        out_specs=[tile(D), pl.BlockSpec((1, LROWS, D // 2), lambda b, i: (b, i, 0)), route, route,
                   pl.BlockSpec((1, N_EXPERTS, LANES), lambda b, i: (b, i, 0))],
        out_shape=[jax.ShapeDtypeStruct((nbatch, tb, D), f32),
                   jax.ShapeDtypeStruct((nbatch, nt * LROWS, D // 2), u32),
                   jax.ShapeDtypeStruct((nbatch, TOP_K, tb), i32),
                   jax.ShapeDtypeStruct((nbatch, TOP_K, tb), f32),
                   jax.ShapeDtypeStruct((nbatch, nt * N_EXPERTS, LANES), f32)],
        compiler_params=_cparams(("parallel", "arbitrary")),
        name="mix_out",
    )(x, att, yf, yb, bv, g, mod, ln, w_out, g_ffn, w_router, b_router)


def _segment_copies(fn, tile, base_ref, seg_ref, ls_ref, src, dst, sem, src_is_global):
    for e in range(N_EXPERTS):
        idx = tile * N_EXPERTS + e
        seg = seg_ref[idx]
        g0 = base_ref[idx]
        l0 = ls_ref[idx]
        size = TM
        while size >= SEG_ALIGN:
            done = lax.bitwise_and(seg, ~(2 * size - 1))

            @pl.when(lax.bitwise_and(seg, size) != 0)
            def _():
                g_rows = pl.ds(pl.multiple_of(g0 + done, SEG_ALIGN), size)
                l_rows = pl.ds(pl.multiple_of(l0 + done, SEG_ALIGN), size)
                s_rows, d_rows = (g_rows, l_rows) if src_is_global else (l_rows, g_rows)
                fn(pltpu.make_async_copy(src.at[s_rows, :], dst.at[d_rows, :], sem))

            size //= 2


def _dispatch_kernel(base_ref, seg_ref, ls_ref, xl_ref, xs_in_ref, xs_ref, sem):
    del xs_in_ref
    tile = pl.program_id(0)
    args = (tile, base_ref, seg_ref, ls_ref, xl_ref, xs_ref, sem, False)
    _segment_copies(lambda cp: cp.start(), *args)
    _segment_copies(lambda cp: cp.wait(), *args)


def _dispatch_call(base, seg, lstart, x_local, nrows):
    ntiles = x_local.shape[0] // LROWS
    any_spec = pl.BlockSpec(memory_space=pl.ANY)
    return pl.pallas_call(
        _dispatch_kernel,
        grid_spec=pltpu.PrefetchScalarGridSpec(
            num_scalar_prefetch=3,
            grid=(ntiles,),
            in_specs=[pl.BlockSpec((LROWS, D // 2), lambda t, *_: (t, 0)), any_spec],
            out_specs=any_spec,
            scratch_shapes=[pltpu.SemaphoreType.DMA(())],
        ),
        out_shape=jax.ShapeDtypeStruct((nrows, D // 2), u32),
        input_output_aliases={4: 0},
        compiler_params=_cparams(("arbitrary",)),
        name="moe_dispatch",
    )(base, seg, lstart, x_local, jnp.zeros((nrows, D // 2), u32))


def _ffn_kernel(be_ref, nu_ref, x_ref, wg_ref, wu_ref, wd_ref, y_ref):
    i = pl.program_id(0)

    @pl.when(i < nu_ref[0])
    def _():
        x = _unpack_bf16_pairs(x_ref[...])
        gt = _dot(x, wg_ref[0])
        up = _dot(x, wu_ref[0])
        hid = (gt * _sigmoid(gt) * up).astype(bf16)
        y = _dot(hid, wd_ref[0])
        y_ref[...] = _pack_bf16_pairs(y.astype(bf16).astype(f32))

    @pl.when(i >= nu_ref[0])
    def _():
        y_ref[...] = jnp.zeros_like(y_ref)


def _ffn_call(blk_expert, n_used, x_sorted, wg, wu, wd):
    nrows = x_sorted.shape[0]
    nblk = nrows // MOE_BLK
    wspec = pl.BlockSpec((1, D, D), lambda i, be, nu: (be[i], 0, 0))
    return pl.pallas_call(
        _ffn_kernel,
        grid_spec=pltpu.PrefetchScalarGridSpec(
            num_scalar_prefetch=2,
            grid=(nblk,),
            in_specs=[pl.BlockSpec((MOE_BLK, D // 2), lambda i, be, nu: (i, 0)), wspec, wspec, wspec],
            out_specs=pl.BlockSpec((MOE_BLK, D // 2), lambda i, be, nu: (i, 0)),
        ),
        out_shape=jax.ShapeDtypeStruct((nrows, D // 2), u32),
        compiler_params=_cparams(("arbitrary",)),
        name="moe_ffn",
    )(blk_expert, n_used, x_sorted, wg, wu, wd)


def _combine_kernel(base_ref, seg_ref, ls_ref, x_ref, lp_ref, gt_ref, mod_ref, g_ref, y_ref, o_ref, ybuf, sem,
                    *, tiles_per_batch, first_tile, final):
    b = pl.program_id(0)
    i = pl.program_id(1)
    tile = b * tiles_per_batch + i + first_tile

    @pl.when(jnp.logical_and(b == 0, i == 0))
    def _():
        ybuf[...] = jnp.zeros_like(ybuf)

    args = (tile, base_ref, seg_ref, ls_ref, y_ref, ybuf, sem, True)
    _segment_copies(lambda cp: cp.start(), *args)
    _segment_copies(lambda cp: cp.wait(), *args)
    y_loc = _unpack_bf16_pairs(ybuf[...])
    tm = x_ref.shape[1]
    jj = lax.broadcasted_iota(i32, (LROWS, tm), 0)
    gmat = (jnp.where(jj == lp_ref[0, 0:1, :], gt_ref[0, 0:1, :], 0.0)
            + jnp.where(jj == lp_ref[0, 1:2, :], gt_ref[0, 1:2, :], 0.0))
    g_hi, g_lo = _split2(gmat)
    moe = _dot_tn(g_hi, y_loc) + _dot_tn(g_lo, y_loc)
    xn = x_ref[0] + mod_ref[0, :, 5 * D:6 * D] * moe
    o_ref[0] = _rmsnorm(xn, g_ref[...]) if final else xn


def _combine_call(base, seg, lstart, x, lpos, gates, mod, y_sorted, final_g):
    nbatch, tb, _ = x.shape
    ctx_tiles = CTX // TM
    tiles_per_batch = tb // TM
    final = final_g is not None
    first_tile = ctx_tiles if final else 0
    nt = tiles_per_batch - first_tile
    tile = pl.BlockSpec((1, TM, D), lambda b, i, *_: (b, i + first_tile, 0))
    route = pl.BlockSpec((1, TOP_K, TM), lambda b, i, *_: (b, 0, i + first_tile))
    mod_map = _mod_index(nbatch, ctx_tiles - first_tile)
    g_arr = final_g if final else jnp.ones((1, D), f32)
    return pl.pallas_call(
        functools.partial(_combine_kernel, tiles_per_batch=tiles_per_batch, first_tile=first_tile, final=final),
        grid_spec=pltpu.PrefetchScalarGridSpec(
            num_scalar_prefetch=3,
            grid=(nbatch, nt),
            in_specs=[tile, route, route,
                      pl.BlockSpec((1, 1, 6 * D), lambda b, i, *_: mod_map(b, i)),
                      pl.BlockSpec((1, D), lambda b, i, *_: (0, 0)),
                      pl.BlockSpec(memory_space=pl.ANY)],
            out_specs=pl.BlockSpec((1, TM, D), lambda b, i, *_: (b, i, 0)),
            scratch_shapes=[pltpu.VMEM((LROWS, D // 2), u32), pltpu.SemaphoreType.DMA(())],
        ),
        out_shape=jax.ShapeDtypeStruct((nbatch, nt * TM, D), f32),
        compiler_params=_cparams(("arbitrary", "arbitrary")),
        name="ffn_residual_final" if final else "ffn_residual",
    )(base, seg, lstart, x, lpos, gates, mod, g_arr, y_sorted)


def _moe_rows(ntok):
    ntiles = ntok // TM
    worst = TOP_K * ntok + ntiles * N_EXPERTS * (SEG_ALIGN - 1)
    return (-(-worst // MOE_BLK) + N_EXPERTS) * MOE_BLK


def _segment_plan(cnt, nblk):
    seg = (cnt + SEG_ALIGN - 1) // SEG_ALIGN * SEG_ALIGN
    lstart = jnp.cumsum(seg, axis=1) - seg
    rows = jnp.sum(seg, axis=0)
    padded = (rows + MOE_BLK - 1) // MOE_BLK * MOE_BLK
    pad_ends = jnp.cumsum(padded)
    base = (pad_ends - padded)[None, :] + jnp.cumsum(seg, axis=0) - seg
    blk_expert = jnp.minimum(
        jnp.searchsorted(pad_ends, jnp.arange(nblk, dtype=i32) * MOE_BLK, side='right'), N_EXPERTS - 1)
    n_used = (pad_ends[-1] // MOE_BLK).reshape(1)
    flat = lambda a: a.reshape(-1).astype(i32)
    return flat(base), flat(seg), flat(lstart), blk_expert.astype(i32), n_used.astype(i32)


def _rope_tables(tb):
    rows = SEQ // GRID_W
    row = jnp.repeat(jnp.arange(rows, dtype=f32), GRID_W)
    col = jnp.tile(jnp.arange(GRID_W, dtype=f32), rows)
    inv_freq = ROPE_BASE ** (-jnp.arange(ROPE_FREQS, dtype=f32) / ROPE_FREQS)
    ang_r = row[:, None] * inv_freq[None, :]
    ang_c = col[:, None] * inv_freq[None, :]
    ang = jnp.concatenate([ang_r, ang_r, ang_c, ang_c], axis=-1)
    cos = jnp.concatenate([jnp.ones((CTX, HEAD_DIM), f32), jnp.cos(ang)], axis=0)
    sin = jnp.concatenate([jnp.zeros((CTX, HEAD_DIM), f32), jnp.sin(ang)], axis=0)
    return jnp.tile(cos, (1, LANES // HEAD_DIM)), jnp.tile(sin, (1, LANES // HEAD_DIM))


def _block_diag2(w):
    z = jnp.zeros_like(w[0])
    return jnp.concatenate([jnp.concatenate([w[0], z], axis=1), jnp.concatenate([z, w[1]], axis=1)], axis=0)


def kernel(x, c, ctx, c_ctx, w_mod, b_mod, norm_mix_g, norm_ffn_g, w_in, w_out, att_sink, shift_mu_prev, shift_mu_next, decay_w0, decay_w2, iclr_a0, iclr_a2, vres_v0, vres_v1, vres_v2, gate_g2, k_k, k_a, r_k, ln_x_w, ln_x_b, router_w, router_b, expert_w_gate, expert_w_up, expert_w_down, final_norm_g):
    nbatch = x.shape[0]
    depth = w_mod.shape[0]
    tb = ctx.shape[1] + x.shape[1]
    xa = jnp.concatenate([ctx, x], axis=1)
    nb_pad = -(-(nbatch + 1) // 8) * 8
    cond = jnp.zeros((nb_pad, D), f32).at[:nbatch].set(c).at[nbatch].set(c_ctx)
    mod_all = _mod_call(cond, w_mod, b_mod).reshape(depth, nb_pad, 1, 6 * D)
    cos, sin = _rope_tables(tb)
    wr_hi = router_w.T.astype(bf16)
    wr_lo = (router_w.T - wr_hi.astype(f32)).astype(bf16)
    w_router = jnp.stack([wr_hi, wr_lo])
    b_router = router_b.reshape(N_EXPERTS, 1)
    v_first = None
    for l in range(depth):
        mod = mod_all[l]
        q, k, v, rw = _in_proj_call(xa, mod, norm_mix_g[l].reshape(1, D), w_in[l].astype(bf16), cos, sin)
        att = _attn_call(att_sink[l], q, k, v)
        mu = jnp.stack([shift_mu_prev[l], shift_mu_next[l]])
        v0 = vres_v0[l - 1] if l > 0 else jnp.zeros((RW,), f32)
        vec = jnp.stack([k_k[l], k_a[l], r_k[l].reshape(RW), v0,
                         decay_w0[l, 0], decay_w0[l, 1], iclr_a0[l, 0], iclr_a0[l, 1]])
        if l > 0:
            v1 = jnp.zeros((RW, LANES), f32).at[:, :LORA_VRES].set(vres_v1[l - 1]).astype(bf16)
            v2 = jnp.zeros((LANES, RW), f32).at[:LORA_VRES].set(vres_v2[l - 1]).astype(bf16)
        else:
            v1 = v2 = None
        r_, v_, kk, bv, g, kd, lw, bd = _feat_call(
            rw, v_first, mu, vec, _block_diag2(decay_w2[l]).astype(bf16), _block_diag2(iclr_a2[l]).astype(bf16),
            gate_g2[l].astype(bf16), v1, v2)
        if l == 0:
            v_first = v_
        yf, yb = _scan_call(r_, v_, kk, kd, lw, bd)
        ln = jnp.stack([ln_x_w[l], ln_x_b[l]])
        xa, x_local, lpos, gates, cnt = _mix_call(xa, att, yf, yb, bv, g, mod, ln, w_out[l].astype(bf16),
                                                  norm_ffn_g[l].reshape(1, D), w_router, b_router)
        ntok = nbatch * tb
        nrows = _moe_rows(ntok)
        cnt = cnt[:, :, 0].astype(i32).reshape(ntok // TM, N_EXPERTS)
        base, seg, lstart, blk_expert, n_used = _segment_plan(cnt, nrows // MOE_BLK)
        x_sorted = _dispatch_call(base, seg, lstart, x_local.reshape(-1, D // 2), nrows)
        y_sorted = _ffn_call(blk_expert, n_used, x_sorted, expert_w_gate[l].astype(bf16),
                             expert_w_up[l].astype(bf16), expert_w_down[l].astype(bf16))
        xa = _combine_call(base, seg, lstart, xa, lpos, gates, mod, y_sorted,
                           final_norm_g.reshape(1, D) if l == depth - 1 else None)
    return xa
```

```python
import functools
import math

import jax
import jax.numpy as jnp
from jax import lax
from jax.experimental import pallas as pl
from jax.experimental.pallas import tpu as pltpu

f32 = jnp.float32
bf16 = jnp.bfloat16
i32 = jnp.int32
u32 = jnp.uint32

D = 1024
SEQ = 4096
CTX = 256
TB = CTX + SEQ
GRID_W = 64
HEAD_DIM = 64
ATT_WIDTH = 512
ATT_HEADS = 8
KV_HEADS = 2
ATT_GROUP = ATT_HEADS // KV_HEADS
KV_WIDTH = KV_HEADS * HEAD_DIM
RW = 512
RWKV_HEADS = 8
LORA_DECAY = 64
LORA_ICLR = 64
LORA_VRES = 32
LORA_GATE = 128
RWKV_COLS = 3 * RW + 2 * (LORA_DECAY + LORA_ICLR) + LORA_GATE
ATT_COLS = ATT_WIDTH + 2 * KV_WIDTH
IN_COLS = ATT_COLS + RWKV_COLS
N_EXPERTS = 16
N_GROUPS = 4
EXPERTS_PER_GROUP = 4
TOP_K = 2
MOE_BLK = 256
NORM_EPS = 1e-6
GN_EPS = 64e-5
NEG_INF = -1e30
ATT_SCALE = HEAD_DIM ** -0.5
ROPE_BASE = 10000.0
ROPE_FREQS = HEAD_DIM // 4

LANES = 128
TM = 256
QB = 128
CH = 64
HG = 4
GW = HG * HEAD_DIM
SEG_ALIGN = 8
LROWS = -(-(TOP_K * TM + N_EXPERTS * SEG_ALIGN) // LANES) * LANES
VMEM_LIMIT = 48 * 1024 * 1024


def _cparams(sem):
    return pltpu.CompilerParams(dimension_semantics=sem, vmem_limit_bytes=VMEM_LIMIT)


def _sigmoid(x):
    return 1.0 / (1.0 + jnp.exp(-x))


def _div_pow2(x, n):
    assert n & (n - 1) == 0
    return lax.shift_right_logical(x, n.bit_length() - 1)


def _mod_pow2(x, n):
    assert n & (n - 1) == 0
    return lax.bitwise_and(x, n - 1)


def _round_up_pow2(x, n):
    assert n & (n - 1) == 0
    return lax.bitwise_and(x + (n - 1), ~(n - 1))


def _dot(a, b):
    return jnp.dot(a, b, preferred_element_type=f32)


def _dot_nt(a, b):
    return lax.dot_general(a, b, (((1,), (1,)), ((), ())), preferred_element_type=f32)


def _dot_tn(a, b):
    return lax.dot_general(a, b, (((0,), (0,)), ((), ())), preferred_element_type=f32)


def _split2(x):
    hi = x.astype(bf16)
    lo = (x - hi.astype(f32)).astype(bf16)
    return hi, lo


def _dot_split_lhs(x, m):
    hi, lo = _split2(x)
    return _dot(hi, m) + _dot(lo, m)


def _rmsnorm(x, g):
    ms = jnp.mean(x * x, axis=-1, keepdims=True)
    return x * lax.rsqrt(ms + NORM_EPS) * g


def _head_ones():
    r = _div_pow2(lax.broadcasted_iota(i32, (RW, RW), 0), HEAD_DIM)
    c = _div_pow2(lax.broadcasted_iota(i32, (RW, RW), 1), HEAD_DIM)
    return (r == c).astype(bf16)


def _mod_kernel(c_ref, w_ref, b_ref, o_ref):
    c = c_ref[...]
    s = (c * _sigmoid(c)).astype(bf16)
    o_ref[0] = _dot(s, w_ref[0].astype(bf16)) + b_ref[0]


def _mod_call(cond, w_mod, b_mod):
    nb = cond.shape[0]
    depth = w_mod.shape[0]
    tn = 1024
    return pl.pallas_call(
        _mod_kernel,
        grid=(depth, 6 * D // tn),
        in_specs=[
            pl.BlockSpec((nb, D), lambda l, j: (0, 0)),
            pl.BlockSpec((1, D, tn), lambda l, j: (l, 0, j)),
            pl.BlockSpec((1, 1, tn), lambda l, j: (l, 0, j)),
        ],
        out_specs=pl.BlockSpec((1, nb, tn), lambda l, j: (l, 0, j)),
        out_shape=jax.ShapeDtypeStruct((depth, nb, 6 * D), f32),
        compiler_params=_cparams(("arbitrary", "arbitrary")),
        name="mod",
    )(cond, w_mod, b_mod.reshape(depth, 1, 6 * D))


def _mod_index(nbatch, ctx_tiles):
    return lambda b, i: (jnp.where(i < ctx_tiles, nbatch, b), 0, 0)


def _in_proj_kernel(x_ref, mod_ref, g_ref, w_ref, cos_ref, sin_ref, q_ref, k_ref, v_ref, rw_ref):
    x = x_ref[0]
    tm = x.shape[0]
    h = _rmsnorm(x, g_ref[...])
    sh = mod_ref[0, :, 0:D]
    sc = mod_ref[0, :, D:2 * D]
    h = (h * (1.0 + sc) + sh).astype(bf16)
    p = _dot(h, w_ref[...])
    cos = cos_ref[...]
    sin = sin_ref[...]
    lane = lax.broadcasted_iota(i32, (tm, LANES), 1)
    first_half = _mod_pow2(lane, 2 * ROPE_FREQS) < ROPE_FREQS

    def rope(t):
        rot = jnp.where(first_half, -pltpu.roll(t, LANES - ROPE_FREQS, 1), pltpu.roll(t, ROPE_FREQS, 1))
        return t * cos + rot * sin

    for j in range(ATT_WIDTH // LANES):
        t = rope(p[:, j * LANES:(j + 1) * LANES]).astype(bf16)
        q_ref[0, 2 * j] = t[:, :HEAD_DIM]
        q_ref[0, 2 * j + 1] = t[:, HEAD_DIM:]
    kt = rope(p[:, ATT_WIDTH:ATT_WIDTH + KV_WIDTH]).astype(bf16)
    vt = p[:, ATT_WIDTH + KV_WIDTH:ATT_COLS].astype(bf16)
    for hh in range(KV_HEADS):
        k_ref[0, hh] = kt[:, hh * HEAD_DIM:(hh + 1) * HEAD_DIM]
        v_ref[0, hh] = vt[:, hh * HEAD_DIM:(hh + 1) * HEAD_DIM]
    rw_ref[0] = p[:, ATT_COLS:]


def _in_proj_call(x, mod, g, w_in, cos, sin):
    nbatch, tb, _ = x.shape
    nt = tb // TM
    return pl.pallas_call(
        _in_proj_kernel,
        grid=(nbatch, nt),
        in_specs=[
            pl.BlockSpec((1, TM, D), lambda b, i: (b, i, 0)),
            pl.BlockSpec((1, 1, 6 * D), _mod_index(nbatch, CTX // TM)),
            pl.BlockSpec((1, D), lambda b, i: (0, 0)),
            pl.BlockSpec((D, IN_COLS), lambda b, i: (0, 0)),
            pl.BlockSpec((TM, LANES), lambda b, i: (i, 0)),
            pl.BlockSpec((TM, LANES), lambda b, i: (i, 0)),
        ],
        out_specs=[
            pl.BlockSpec((1, ATT_HEADS, TM, HEAD_DIM), lambda b, i: (b, 0, i, 0)),
            pl.BlockSpec((1, KV_HEADS, TM, HEAD_DIM), lambda b, i: (b, 0, i, 0)),
            pl.BlockSpec((1, KV_HEADS, TM, HEAD_DIM), lambda b, i: (b, 0, i, 0)),
            pl.BlockSpec((1, TM, RWKV_COLS), lambda b, i: (b, i, 0)),
        ],
        out_shape=[
            jax.ShapeDtypeStruct((nbatch, ATT_HEADS, tb, HEAD_DIM), bf16),
            jax.ShapeDtypeStruct((nbatch, KV_HEADS, tb, HEAD_DIM), bf16),
            jax.ShapeDtypeStruct((nbatch, KV_HEADS, tb, HEAD_DIM), bf16),
            jax.ShapeDtypeStruct((nbatch, tb, RWKV_COLS), f32),
        ],
        compiler_params=_cparams(("parallel", "arbitrary")),
        name="in_proj",
    )(x, mod, g, w_in, cos, sin)


def _attn_kernel(sink_ref, q_ref, kp_ref, kc_ref, kn_ref, vp_ref, vc_ref, vn_ref, kx_ref, vx_ref, o_ref,
                 *, nblk, ctx_blks):
    i = pl.program_id(1)
    is_lat = i >= ctx_blks
    prev_ok = jnp.logical_and(is_lat, i - 1 >= ctx_blks)
    next_ok = jnp.logical_and(is_lat, i + 1 <= nblk - 1)
    rows = ATT_GROUP * QB
    qi = _mod_pow2(lax.broadcasted_iota(i32, (rows, QB), 0), QB)
    kj = lax.broadcasted_iota(i32, (rows, QB), 1)
    mask_p = jnp.logical_and(kj >= qi, prev_ok)
    mask_n = jnp.logical_and(kj <= qi, next_ok)
    row_head = _div_pow2(lax.broadcasted_iota(i32, (rows, 1), 0), QB)
    outs = []
    for h in range(KV_HEADS):
        qh = q_ref[0, ATT_GROUP * h:ATT_GROUP * (h + 1)].reshape(rows, HEAD_DIM)
        s_p = jnp.where(mask_p, _dot_nt(qh, kp_ref[0, h]) * ATT_SCALE, NEG_INF)
        s_c = jnp.where(is_lat, _dot_nt(qh, kc_ref[0, h]) * ATT_SCALE, NEG_INF)
        s_n = jnp.where(mask_n, _dot_nt(qh, kn_ref[0, h]) * ATT_SCALE, NEG_INF)
        s_x = _dot_nt(qh, kx_ref[0, h]) * ATT_SCALE
        sink = jnp.zeros((rows, 1), f32)
        for g in range(ATT_GROUP):
            sink = jnp.where(row_head == g, sink_ref[ATT_GROUP * h + g], sink)
        s_x0, s_x1 = s_x[:, :QB], s_x[:, QB:]
        m_el = jnp.maximum(jnp.maximum(s_p, s_c), jnp.maximum(jnp.maximum(s_n, s_x0), s_x1))
        m = jnp.maximum(jnp.max(m_el, axis=-1, keepdims=True), sink)
        e_p = jnp.exp(s_p - m)
        e_c = jnp.exp(s_c - m)
        e_n = jnp.exp(s_n - m)
        e_x = jnp.exp(s_x - m)
        e_el = (e_p + e_c) + (e_n + e_x[:, :QB]) + e_x[:, QB:]
        den = jnp.sum(e_el, axis=-1, keepdims=True) + jnp.exp(sink - m)
        o = (_dot(e_p.astype(bf16), vp_ref[0, h]) + _dot(e_c.astype(bf16), vc_ref[0, h])
             + _dot(e_n.astype(bf16), vn_ref[0, h]) + _dot(e_x.astype(bf16), vx_ref[0, h]))
        o = o / den
        outs.extend(o[g * QB:(g + 1) * QB] for g in range(ATT_GROUP))
    o_ref[0] = jnp.concatenate(outs, axis=1).astype(bf16)


def _attn_call(sink, q, k, v):
    nbatch, _, tb, _ = q.shape
    nblk = tb // QB
    ctx_blks = CTX // QB
    assert CTX == 2 * QB
    kv_blk = (1, KV_HEADS, QB, HEAD_DIM)
    prev_map = lambda b, i: (b, 0, jnp.maximum(i - 1, 0), 0)
    cur_map = lambda b, i: (b, 0, i, 0)
    next_map = lambda b, i: (b, 0, jnp.minimum(i + 1, nblk - 1), 0)
    ctx_spec = pl.BlockSpec((1, KV_HEADS, CTX, HEAD_DIM), lambda b, i: (b, 0, 0, 0))
    return pl.pallas_call(
        functools.partial(_attn_kernel, nblk=nblk, ctx_blks=ctx_blks),
        grid=(nbatch, nblk),
        in_specs=[
            pl.BlockSpec(memory_space=pltpu.SMEM),
            pl.BlockSpec((1, ATT_HEADS, QB, HEAD_DIM), cur_map),
            pl.BlockSpec(kv_blk, prev_map), pl.BlockSpec(kv_blk, cur_map), pl.BlockSpec(kv_blk, next_map),
            pl.BlockSpec(kv_blk, prev_map), pl.BlockSpec(kv_blk, cur_map), pl.BlockSpec(kv_blk, next_map),
            ctx_spec, ctx_spec,
        ],
        out_specs=pl.BlockSpec((1, QB, ATT_WIDTH), lambda b, i: (b, i, 0)),
        out_shape=jax.ShapeDtypeStruct((nbatch, tb, ATT_WIDTH), bf16),
        compiler_params=_cparams(("parallel", "arbitrary")),
        name="attention",
    )(sink, q, k, k, k, v, v, v, k, v)


def _feat_kernel(*refs, nt, ctx_tiles, has_vres):
    if has_vres:
        (rw_ref, hp_ref, hn_ref, vf_ref, mu_ref, vec_ref, w2_ref, a2_ref, g2_ref, v1_ref, v2_ref,
         r_ref, v_ref, kk_ref, bv_ref, g_ref, kd_ref, lw_ref, bd_ref) = refs
    else:
        (rw_ref, hp_ref, hn_ref, mu_ref, vec_ref, w2_ref, a2_ref, g2_ref,
         r_ref, v_ref, kk_ref, bv_ref, g_ref, kd_ref, lw_ref, bd_ref) = refs
    i = pl.program_id(1)
    u0 = rw_ref[0]
    tm = u0.shape[0]
    prev_zero = jnp.logical_or(i == 0, i == ctx_tiles)
    next_zero = jnp.logical_or(i == ctx_tiles - 1, i == nt - 1)
    halo_p = jnp.where(prev_zero, 0.0, hp_ref[0, 7:8, :])
    halo_n = jnp.where(next_zero, 0.0, hn_ref[0, 0:1, :])
    row = lax.broadcasted_iota(i32, (tm, 1), 0)
    prev = jnp.where(row == 0, halo_p, pltpu.roll(u0, 1, 0))
    nxt = jnp.where(row == tm - 1, halo_n, pltpu.roll(u0, tm - 1, 0))
    u = u0 + mu_ref[0:1, :] * (prev - u0) + mu_ref[1:2, :] * (nxt - u0)

    r = u[:, 0:RW]
    k = u[:, RW:2 * RW]
    v = u[:, 2 * RW:3 * RW]
    wd = u[:, 3 * RW:3 * RW + 2 * LORA_DECAY]
    ad = u[:, 3 * RW + 2 * LORA_DECAY:3 * RW + 2 * (LORA_DECAY + LORA_ICLR)]
    gd = u[:, 3 * RW + 2 * (LORA_DECAY + LORA_ICLR):]
    k_k = vec_ref[0:1, :]
    k_a = vec_ref[1:2, :]
    r_k = vec_ref[2:3, :]
    ones = _head_ones()

    if has_vres:
        lo = _dot(v.astype(bf16), v1_ref[...])
        gate = _sigmoid(vec_ref[3:4, :] + _dot(lo.astype(bf16), v2_ref[...]))
        v = v + (vf_ref[0] - v) * gate
    decay_in = _dot(jnp.tanh(wd).astype(bf16), w2_ref[...])
    a_in = _dot(ad.astype(bf16), a2_ref[...])
    kk = k * k_k
    n2 = _dot_split_lhs(kk * kk, ones)
    kk = kk / jnp.maximum(jnp.sqrt(n2), 1e-12)
    g = _dot(_sigmoid(gd).astype(bf16), g2_ref[...])
    ksum = jnp.zeros_like(k)
    for d in range(2):
        w0 = vec_ref[4 + d:5 + d, :]
        a0 = vec_ref[6 + d:7 + d, :]
        lw = -_sigmoid(w0 + decay_in[:, d * RW:(d + 1) * RW]) * math.exp(-0.5)
        a = _sigmoid(a0 + a_in[:, d * RW:(d + 1) * RW])
        kd = k * (1.0 + (a - 1.0) * k_a)
        ksum = ksum + kd
        kd_ref[d, 0] = kd
        lw_ref[d, 0] = lw
        bd_ref[d, 0] = kk * a
    bonus = _dot_split_lhs(r * ksum * r_k, ones)
    r_ref[0] = r
    v_ref[0] = v
    kk_ref[0] = kk
    bv_ref[0] = bonus * v
    g_ref[0] = g


def _feat_call(rw, v_first, mu, vec, w2bd, a2bd, g2, v1, v2):
    nbatch, tb, _ = rw.shape
    nt = tb // TM
    has_vres = v_first is not None
    sub = TM // 8
    tile = lambda w: pl.BlockSpec((1, TM, w), lambda b, i: (b, i, 0))
    full = lambda a: pl.BlockSpec(a.shape, lambda b, i: (0,) * a.ndim)
    in_specs = [
        tile(RWKV_COLS),
        pl.BlockSpec((1, 8, RWKV_COLS), lambda b, i: (b, jnp.maximum(i * sub - 1, 0), 0)),
        pl.BlockSpec((1, 8, RWKV_COLS), lambda b, i: (b, jnp.minimum((i + 1) * sub, tb // 8 - 1), 0)),
    ]
    args = [rw, rw, rw]
    if has_vres:
        in_specs.append(tile(RW))
        args.append(v_first)
    consts = [mu, vec, w2bd, a2bd, g2] + ([v1, v2] if has_vres else [])
    in_specs += [full(a) for a in consts]
    args += consts
    dir_spec = pl.BlockSpec((2, 1, TM, RW), lambda b, i: (0, b, i, 0))
    tok = jax.ShapeDtypeStruct((nbatch, tb, RW), f32)
    dtok = jax.ShapeDtypeStruct((2, nbatch, tb, RW), f32)
    return pl.pallas_call(
        functools.partial(_feat_kernel, nt=nt, ctx_tiles=CTX // TM, has_vres=has_vres),
        grid=(nbatch, nt),
        in_specs=in_specs,
        out_specs=[tile(RW)] * 5 + [dir_spec] * 3,
        out_shape=[tok] * 5 + [dtok] * 3,
        compiler_params=_cparams(("parallel", "arbitrary")),
        name="rwkv_features",
    )(*args)


def _block_diag_rows(x, width):
    cb = _div_pow2(lax.broadcasted_iota(i32, x.shape, 1), width)
    return jnp.concatenate([jnp.where(cb == h, x, jnp.zeros_like(x)) for h in range(HG)], axis=0)


def _scan_chunks(probs):
    wide = HG * CH
    row_t = lax.broadcasted_iota(i32, (CH, wide), 0)
    col_t = _mod_pow2(lax.broadcasted_iota(i32, (CH, wide), 1), CH)
    row_g = lax.broadcasted_iota(i32, (CH, GW), 0)
    incl_t = {False: col_t <= row_t, True: col_t >= row_t}
    strict_t = {False: col_t < row_t, True: col_t > row_t}
    eye_t = (row_t == col_t).astype(f32)
    rb = _div_pow2(lax.broadcasted_iota(i32, (GW, GW), 0), HEAD_DIM)
    cb = _div_pow2(lax.broadcasted_iota(i32, (GW, GW), 1), HEAD_DIM)
    n = len(probs)
    rev = [p[7] for p in probs]
    def cumsum_rows(x, reverse):
        s = 1
        while s < CH:
            if reverse:
                x = x + jnp.where(row_g < CH - s, pltpu.roll(x, CH - s, 0), 0.0)
            else:
                x = x + jnp.where(row_g >= s, pltpu.roll(x, s, 0), 0.0)
            s *= 2
        return x

    gam = [cumsum_rows(probs[i][5], rev[i]) for i in range(n)]
    ar, bk, k_t, b_t = [], [], [], []
    for i, (s_prev, r, v, kk, k, lw, b, _) in enumerate(probs):
        e_neg = jnp.exp(-gam[i])
        a_s = (-kk * jnp.exp(gam[i] - lw)).astype(bf16)
        r_s = (r * jnp.exp(gam[i])).astype(bf16)
        b_t.append(b * e_neg)
        k_t.append(k * e_neg)
        ar.append(jnp.concatenate([a_s, r_s], axis=0))
        bk.append(jnp.concatenate([_block_diag_rows(b_t[i].astype(bf16), HEAD_DIM),
                                   _block_diag_rows(k_t[i].astype(bf16), HEAD_DIM)], axis=0))
    gram = [_dot_nt(ar[i], bk[i]) for i in range(n)]
    ars = [_dot_nt(ar[i], probs[i][0].astype(bf16)) for i in range(n)]
    v_bd = [_block_diag_rows(probs[i][2].astype(bf16), HEAD_DIM) for i in range(n)]
    p0 = [jnp.where(strict_t[rev[i]], gram[i][:CH, :wide], 0.0).astype(bf16) for i in range(n)]
    lq = [jnp.concatenate([jnp.where(strict_t[rev[i]], gram[i][:CH, wide:], 0.0),
                           jnp.where(incl_t[rev[i]], gram[i][CH:, wide:], 0.0)], axis=0).astype(bf16)
          for i in range(n)]
    lqv = [_dot(lq[i], v_bd[i]) for i in range(n)]
    rhs = [ars[i][:CH] + lqv[i][:CH] for i in range(n)]
    t = [eye_t + p0[i].astype(f32) for i in range(n)]
    p = [_dot(p0[i], _block_diag_rows(p0[i], CH)).astype(bf16) for i in range(n)]
    m = 4
    while m < CH:
        tp = [_dot(jnp.concatenate([t[i].astype(bf16), p[i]], axis=0), _block_diag_rows(p[i], CH))
              for i in range(n)]
        t = [t[i] + tp[i][:CH] for i in range(n)]
        p = [tp[i][CH:].astype(bf16) for i in range(n)]
        m *= 2
    t = [t[i] + _dot(t[i].astype(bf16), _block_diag_rows(p[i], CH)) for i in range(n)]
    u = [_dot(t[i].astype(bf16), _block_diag_rows(rhs[i].astype(bf16), HEAD_DIM)) for i in range(n)]
    out = []
    for i in range(n):
        s_prev, v, lw = probs[i][0], probs[i][2], probs[i][5]
        q_b = jnp.where(incl_t[rev[i]], gram[i][CH:, :wide], 0.0).astype(bf16)
        u_bd = _block_diag_rows(u[i].astype(bf16), HEAD_DIM)
        y = ars[i][CH:] + lqv[i][CH:] + _dot(q_b, u_bd)
        eg = jnp.exp(jnp.sum(lw, axis=0, keepdims=True))
        vu = jnp.concatenate([v, u[i]], axis=0).astype(bf16)
        kb = jnp.concatenate([k_t[i] * eg, b_t[i] * eg], axis=0).astype(bf16)
        s_add = _dot_tn(vu, kb)
        out.append((s_prev * eg + jnp.where(rb == cb, s_add, 0.0), y))
    return out


def _scan_kernel(rf_ref, vf_ref, kkf_ref, kf_ref, lwf_ref, bf_ref,
                 rb_ref, vb_ref, kkb_ref, kb_ref, lwb_ref, bb_ref,
                 yf_ref, yb_ref, s_ref):
    @pl.when(pl.program_id(1) == 0)
    def _():
        s_ref[...] = jnp.zeros_like(s_ref)

    dirs = ((rf_ref, vf_ref, kkf_ref, kf_ref, lwf_ref, bf_ref, yf_ref),
            (rb_ref, vb_ref, kkb_ref, kb_ref, lwb_ref, bb_ref, yb_ref))
    probs, dest = [], []
    for d, (r_ref, v_ref, kk_ref, k_ref, lw_ref, b_ref, y_ref) in enumerate(dirs):
        for g in range(RW // GW):
            sl = slice(g * GW, (g + 1) * GW)
            probs.append((s_ref[d, g], r_ref[0, :, sl], v_ref[0, :, sl], kk_ref[0, :, sl],
                          k_ref[0, 0, :, sl], lw_ref[0, 0, :, sl], b_ref[0, 0, :, sl], d == 1))
            dest.append((d, g, y_ref, sl))
    for (d, g, y_ref, sl), (s_new, y) in zip(dest, _scan_chunks(probs)):
        s_ref[d, g] = s_new
        y_ref[0, :, sl] = y


def _scan_call(r, v, kk, kd, lw, bd):
    nbatch, tb, _ = r.shape
    nc = tb // CH
    cc = CTX // CH
    rev = lambda j: jnp.where(j < cc, cc - 1 - j, nc - 1 + cc - j)
    tok_f = pl.BlockSpec((1, CH, RW), lambda b, j: (b, j, 0))
    tok_b = pl.BlockSpec((1, CH, RW), lambda b, j: (b, rev(j), 0))
    dir_f = pl.BlockSpec((1, 1, CH, RW), lambda b, j: (0, b, j, 0))
    dir_b = pl.BlockSpec((1, 1, CH, RW), lambda b, j: (1, b, rev(j), 0))
    out = jax.ShapeDtypeStruct((nbatch, tb, RW), f32)
    return pl.pallas_call(
        _scan_kernel,
        grid=(nbatch, nc),
        in_specs=[tok_f, tok_f, tok_f, dir_f, dir_f, dir_f, tok_b, tok_b, tok_b, dir_b, dir_b, dir_b],
        out_specs=[tok_f, tok_b],
        out_shape=[out, out],
        scratch_shapes=[pltpu.VMEM((2, RW // GW, GW, GW), f32)],
        compiler_params=_cparams(("parallel", "arbitrary")),
        name="rwkv_scan",
    )(r, v, kk, kd, lw, bd, r, v, kk, kd, lw, bd)


def _pack_bf16_pairs(x):
    bits = pltpu.bitcast(x, u32)
    half = x.shape[1] // 2
    return bits[:, :half] | lax.shift_right_logical(bits[:, half:], jnp.uint32(16))


def _unpack_bf16_pairs(p):
    hi = pltpu.bitcast(p & jnp.uint32(0xFFFF0000), f32)
    lo = pltpu.bitcast(lax.shift_left(p, jnp.uint32(16)), f32)
    return jnp.concatenate([hi, lo], axis=1).astype(bf16)


def _mix_kernel(x_ref, att_ref, yf_ref, yb_ref, bv_ref, g_ref, mod_ref, ln_ref, wo_ref, gf_ref, wr_ref, rb_ref,
                xo_ref, xl_ref, lp_ref, gt_ref, cnt_ref):
    ones = _head_ones()
    y = yf_ref[0] + yb_ref[0]
    inv = 1.0 / HEAD_DIM
    mu = _dot_split_lhs(y, ones) * inv
    dlt = y - mu
    var = _dot_split_lhs(dlt * dlt, ones) * inv
    gn = dlt * lax.rsqrt(var + GN_EPS) * ln_ref[0:1, :] + ln_ref[1:2, :]
    rwk = ((gn + bv_ref[0]) * g_ref[0]).astype(bf16)
    mix = _dot(att_ref[0], wo_ref[0:ATT_WIDTH, :]) + _dot(rwk, wo_ref[ATT_WIDTH:, :])
    gate_a = mod_ref[0, :, 2 * D:3 * D]
    xn = x_ref[0] + gate_a * mix
    xo_ref[0] = xn
    sh = mod_ref[0, :, 3 * D:4 * D]
    sc = mod_ref[0, :, 4 * D:5 * D]
    hf = _rmsnorm(xn, gf_ref[...]) * (1.0 + sc) + sh
    h_hi, h_lo = _split2(hf)
    logits = _dot_nt(wr_ref[0], h_hi) + _dot_nt(wr_ref[0], h_lo) + _dot_nt(wr_ref[1], h_hi)
    _route_and_sort(hf, _sigmoid(logits), rb_ref[...], xl_ref, lp_ref, gt_ref, cnt_ref)


def _route_and_sort(hf, scores, bias, xl_ref, lp_ref, gt_ref, cnt_ref):
    tm = hf.shape[0]
    e0, e1, g0, g1 = _route_rows(scores, bias)
    gt_ref[0] = jnp.concatenate([g0, g1], axis=0)
    eio = lax.broadcasted_iota(i32, (N_EXPERTS, tm), 0)
    oh0 = (eio == e0).astype(f32)
    oh1 = (eio == e1).astype(f32)
    both = oh0 + oh1
    upper = (lax.broadcasted_iota(i32, (tm, tm), 0) < lax.broadcasted_iota(i32, (tm, tm), 1)).astype(bf16)
    before = _dot(both.astype(bf16), upper)
    cnt = jnp.sum(both, axis=1, keepdims=True)
    seg = _round_up_pow2(cnt.astype(i32), SEG_ALIGN).astype(f32)
    lower = (lax.broadcasted_iota(i32, (N_EXPERTS, N_EXPERTS), 0)
             > lax.broadcasted_iota(i32, (N_EXPERTS, N_EXPERTS), 1)).astype(bf16)
    seg_start = _dot(lower, jnp.broadcast_to(seg, (N_EXPERTS, tm)).astype(bf16))
    pos = seg_start + before
    lp0 = jnp.sum(oh0 * pos, axis=0, keepdims=True).astype(i32)
    lp1 = jnp.sum(oh1 * pos, axis=0, keepdims=True).astype(i32)
    lp_ref[0] = jnp.concatenate([lp0, lp1], axis=0)
    cnt_ref[0] = jnp.broadcast_to(cnt, (N_EXPERTS, LANES))
    jj = lax.broadcasted_iota(i32, (LROWS, tm), 0)
    sel_t = jnp.logical_or(jj == lp0, jj == lp1).astype(bf16)
    xl_ref[0] = _pack_bf16_pairs(_dot(sel_t, hf.astype(bf16)))


def _route_rows(scores, bias):
    biased = scores + bias
    row = lambda a, e: a[e:e + 1, :]
    best = None
    for gi in range(N_GROUPS):
        m = [row(biased, gi * EXPERTS_PER_GROUP + j) for j in range(EXPERTS_PER_GROUP)]
        gs = None
        for a in range(EXPERTS_PER_GROUP):
            for b in range(a + 1, EXPERTS_PER_GROUP):
                pair = m[a] + m[b]
                gs = pair if gs is None else jnp.maximum(gs, pair)
        if best is None:
            best, g_idx = gs, jnp.zeros(gs.shape, i32)
        else:
            better = gs > best
            g_idx = jnp.where(better, gi, g_idx)
            best = jnp.where(better, gs, best)

    def pick(a, j):
        out = row(a, j)
        for gi in range(1, N_GROUPS):
            out = jnp.where(g_idx == gi, row(a, gi * EXPERTS_PER_GROUP + j), out)
        return out

    vb = [pick(biased, j) for j in range(EXPERTS_PER_GROUP)]
    vs = [pick(scores, j) for j in range(EXPERTS_PER_GROUP)]

    def argmax_first(vals):
        bv, bi = vals[0], jnp.zeros(vals[0].shape, i32)
        for j in range(1, len(vals)):
            better = vals[j] > bv
            bi = jnp.where(better, j, bi)
            bv = jnp.where(better, vals[j], bv)
        return bi

    i1 = argmax_first(vb)
    i2 = argmax_first([jnp.where(i1 == j, -jnp.inf, vb[j]) for j in range(EXPERTS_PER_GROUP)])
    sel = lambda idx: sum(jnp.where(idx == j, vs[j], 0.0) for j in range(EXPERTS_PER_GROUP))
    s1, s2 = sel(i1), sel(i2)
    tot = s1 + s2
    base = g_idx * EXPERTS_PER_GROUP
    return base + i1, base + i2, s1 / tot, s2 / tot


def _mix_call(x, att, yf, yb, bv, g, mod, ln, w_out, g_ffn, w_router, b_router):
    nbatch, tb, _ = x.shape
    nt = tb // TM
    tile = lambda w: pl.BlockSpec((1, TM, w), lambda b, i: (b, i, 0))
    full = lambda a: pl.BlockSpec(a.shape, lambda b, i: (0,) * a.ndim)
    route = pl.BlockSpec((1, TOP_K, TM), lambda b, i: (b, 0, i))
    return pl.pallas_call(
        _mix_kernel,
        grid=(nbatch, nt),
        in_specs=[tile(D), tile(ATT_WIDTH), tile(RW), tile(RW), tile(RW), tile(RW),
                  pl.BlockSpec((1, 1, 6 * D), _mod_index(nbatch, CTX // TM)),
                  full(ln), full(w_out), full(g_ffn), full(w_router), full(b_router)],
        out_specs=[tile(D), pl.BlockSpec((1, LROWS, D // 2), lambda b, i: (b, i, 0)), route, route,
                   pl.BlockSpec((1, N_EXPERTS, LANES), lambda b, i: (b, i, 0))],
        out_shape=[jax.ShapeDtypeStruct((nbatch, tb, D), f32),
                   jax.ShapeDtypeStruct((nbatch, nt * LROWS, D // 2), u32),
                   jax.ShapeDtypeStruct((nbatch, TOP_K, tb), i32),
                   jax.ShapeDtypeStruct((nbatch, TOP_K, tb), f32),
                   jax.ShapeDtypeStruct((nbatch, nt * N_EXPERTS, LANES), f32)],
        compiler_params=_cparams(("parallel", "arbitrary")),
        name="mix_out",
    )(x, att, yf, yb, bv, g, mod, ln, w_out, g_ffn, w_router, b_router)


def _segment_copies(fn, tile, base_ref, seg_ref, ls_ref, src, dst, sem, src_is_global):
    for e in range(N_EXPERTS):
        idx = tile * N_EXPERTS + e
        seg = seg_ref[idx]
        g0 = base_ref[idx]
        l0 = ls_ref[idx]
        size = TM
        while size >= SEG_ALIGN:
            done = lax.bitwise_and(seg, ~(2 * size - 1))

            @pl.when(lax.bitwise_and(seg, size) != 0)
            def _():
                g_rows = pl.ds(pl.multiple_of(g0 + done, SEG_ALIGN), size)
                l_rows = pl.ds(pl.multiple_of(l0 + done, SEG_ALIGN), size)
                s_rows, d_rows = (g_rows, l_rows) if src_is_global else (l_rows, g_rows)
                fn(pltpu.make_async_copy(src.at[s_rows, :], dst.at[d_rows, :], sem))

            size //= 2


def _dispatch_kernel(base_ref, seg_ref, ls_ref, xl_ref, xs_in_ref, xs_ref, sem):
    del xs_in_ref
    tile = pl.program_id(0)
    args = (tile, base_ref, seg_ref, ls_ref, xl_ref, xs_ref, sem, False)
    _segment_copies(lambda cp: cp.start(), *args)
    _segment_copies(lambda cp: cp.wait(), *args)


def _dispatch_call(base, seg, lstart, x_local, nrows):
    ntiles = x_local.shape[0] // LROWS
    any_spec = pl.BlockSpec(memory_space=pl.ANY)
    return pl.pallas_call(
        _dispatch_kernel,
        grid_spec=pltpu.PrefetchScalarGridSpec(
            num_scalar_prefetch=3,
            grid=(ntiles,),
            in_specs=[pl.BlockSpec((LROWS, D // 2), lambda t, *_: (t, 0)), any_spec],
            out_specs=any_spec,
            scratch_shapes=[pltpu.SemaphoreType.DMA(())],
        ),
        out_shape=jax.ShapeDtypeStruct((nrows, D // 2), u32),
        input_output_aliases={4: 0},
        compiler_params=_cparams(("arbitrary",)),
        name="moe_dispatch",
    )(base, seg, lstart, x_local, jnp.zeros((nrows, D // 2), u32))


def _ffn_kernel(be_ref, nu_ref, x_ref, wg_ref, wu_ref, wd_ref, y_ref, wg_bf, wu_bf, wd_bf):
    i = pl.program_id(0)
    used = i < nu_ref[0]
    new_expert = jnp.logical_or(i == 0, be_ref[i] != be_ref[jnp.maximum(i - 1, 0)])

    @pl.when(jnp.logical_and(used, new_expert))
    def _():
        wg_bf[...] = wg_ref[0].astype(bf16)
        wu_bf[...] = wu_ref[0].astype(bf16)
        wd_bf[...] = wd_ref[0].astype(bf16)

    @pl.when(used)
    def _():
        x = _unpack_bf16_pairs(x_ref[...])
        gt = _dot(x, wg_bf[...])
        up = _dot(x, wu_bf[...])
        hid = (gt * _sigmoid(gt) * up).astype(bf16)
        y = _dot(hid, wd_bf[...])
        y_ref[...] = _pack_bf16_pairs(y.astype(bf16).astype(f32))

    @pl.when(i >= nu_ref[0])
    def _():
        y_ref[...] = jnp.zeros_like(y_ref)


def _ffn_call(blk_expert, n_used, x_sorted, wg, wu, wd):
    nrows = x_sorted.shape[0]
    nblk = nrows // MOE_BLK
    wspec = pl.BlockSpec((1, D, D), lambda i, be, nu: (be[i], 0, 0))
    return pl.pallas_call(
        _ffn_kernel,
        grid_spec=pltpu.PrefetchScalarGridSpec(
            num_scalar_prefetch=2,
            grid=(nblk,),
            in_specs=[pl.BlockSpec((MOE_BLK, D // 2), lambda i, be, nu: (i, 0)), wspec, wspec, wspec],
            out_specs=pl.BlockSpec((MOE_BLK, D // 2), lambda i, be, nu: (i, 0)),
            scratch_shapes=[pltpu.VMEM((D, D), bf16)] * 3,
        ),
        out_shape=jax.ShapeDtypeStruct((nrows, D // 2), u32),
        compiler_params=_cparams(("arbitrary",)),
        name="moe_ffn",
    )(blk_expert, n_used, x_sorted, wg, wu, wd)


def _combine_kernel(base_ref, seg_ref, ls_ref, x_ref, lp_ref, gt_ref, mod_ref, g_ref, y_ref, o_ref, ybuf, sem,
                    *, tiles_per_batch, first_tile, final):
    b = pl.program_id(0)
    i = pl.program_id(1)
    nt = pl.num_programs(1)
    step = b * nt + i
    slot = lax.rem(step, 2)
    tile = b * tiles_per_batch + i + first_tile
    next_tile = jnp.where(i + 1 < nt, tile + 1, (b + 1) * tiles_per_batch + first_tile)

    def copies(fn, which_tile, which_slot):
        _segment_copies(fn, which_tile, base_ref, seg_ref, ls_ref, y_ref, ybuf.at[which_slot],
                        sem.at[which_slot], True)

    @pl.when(step == 0)
    def _():
        ybuf[...] = jnp.zeros_like(ybuf)
        copies(lambda cp: cp.start(), tile, slot)

    @pl.when(step + 1 < pl.num_programs(0) * nt)
    def _():
        copies(lambda cp: cp.start(), next_tile, 1 - slot)

    copies(lambda cp: cp.wait(), tile, slot)
    y_loc = _unpack_bf16_pairs(ybuf[slot])
    tm = x_ref.shape[1]
    jj = lax.broadcasted_iota(i32, (LROWS, tm), 0)
    gmat = (jnp.where(jj == lp_ref[0, 0:1, :], gt_ref[0, 0:1, :], 0.0)
            + jnp.where(jj == lp_ref[0, 1:2, :], gt_ref[0, 1:2, :], 0.0))
    g_hi, g_lo = _split2(gmat)
    moe = _dot_tn(g_hi, y_loc) + _dot_tn(g_lo, y_loc)
    xn = x_ref[0] + mod_ref[0, :, 5 * D:6 * D] * moe
    o_ref[0] = _rmsnorm(xn, g_ref[...]) if final else xn


def _combine_call(base, seg, lstart, x, lpos, gates, mod, y_sorted, final_g):
    nbatch, tb, _ = x.shape
    ctx_tiles = CTX // TM
    tiles_per_batch = tb // TM
    final = final_g is not None
    first_tile = ctx_tiles if final else 0
    nt = tiles_per_batch - first_tile
    tile = pl.BlockSpec((1, TM, D), lambda b, i, *_: (b, i + first_tile, 0))
    route = pl.BlockSpec((1, TOP_K, TM), lambda b, i, *_: (b, 0, i + first_tile))
    mod_map = _mod_index(nbatch, ctx_tiles - first_tile)
    g_arr = final_g if final else jnp.ones((1, D), f32)
    return pl.pallas_call(
        functools.partial(_combine_kernel, tiles_per_batch=tiles_per_batch, first_tile=first_tile, final=final),
        grid_spec=pltpu.PrefetchScalarGridSpec(
            num_scalar_prefetch=3,
            grid=(nbatch, nt),
            in_specs=[tile, route, route,
                      pl.BlockSpec((1, 1, 6 * D), lambda b, i, *_: mod_map(b, i)),
                      pl.BlockSpec((1, D), lambda b, i, *_: (0, 0)),
                      pl.BlockSpec(memory_space=pl.ANY)],
            out_specs=pl.BlockSpec((1, TM, D), lambda b, i, *_: (b, i, 0)),
            scratch_shapes=[pltpu.VMEM((2, LROWS, D // 2), u32), pltpu.SemaphoreType.DMA((2,))],
        ),
        out_shape=jax.ShapeDtypeStruct((nbatch, nt * TM, D), f32),
        compiler_params=_cparams(("arbitrary", "arbitrary")),
        name="ffn_residual_final" if final else "ffn_residual",
    )(base, seg, lstart, x, lpos, gates, mod, g_arr, y_sorted)


def _moe_rows(ntok):
    ntiles = ntok // TM
    worst = TOP_K * ntok + ntiles * N_EXPERTS * (SEG_ALIGN - 1)
    return (-(-worst // MOE_BLK) + N_EXPERTS) * MOE_BLK


def _segment_plan(cnt, nblk):
    seg = (cnt + SEG_ALIGN - 1) // SEG_ALIGN * SEG_ALIGN
    lstart = jnp.cumsum(seg, axis=1) - seg
    rows = jnp.sum(seg, axis=0)
    padded = (rows + MOE_BLK - 1) // MOE_BLK * MOE_BLK
    pad_ends = jnp.cumsum(padded)
    base = (pad_ends - padded)[None, :] + jnp.cumsum(seg, axis=0) - seg
    blk_start = jnp.arange(nblk, dtype=i32) * MOE_BLK
    blk_expert = jnp.minimum(jnp.sum(pad_ends[None, :] <= blk_start[:, None], axis=1), N_EXPERTS - 1)
    n_used = (pad_ends[-1] // MOE_BLK).reshape(1)
    flat = lambda a: a.reshape(-1).astype(i32)
    return flat(base), flat(seg), flat(lstart), blk_expert.astype(i32), n_used.astype(i32)


def _rope_tables(tb):
    rows = SEQ // GRID_W
    row = jnp.repeat(jnp.arange(rows, dtype=f32), GRID_W)
    col = jnp.tile(jnp.arange(GRID_W, dtype=f32), rows)
    inv_freq = ROPE_BASE ** (-jnp.arange(ROPE_FREQS, dtype=f32) / ROPE_FREQS)
    ang_r = row[:, None] * inv_freq[None, :]
    ang_c = col[:, None] * inv_freq[None, :]
    ang = jnp.concatenate([ang_r, ang_r, ang_c, ang_c], axis=-1)
    cos = jnp.concatenate([jnp.ones((CTX, HEAD_DIM), f32), jnp.cos(ang)], axis=0)
    sin = jnp.concatenate([jnp.zeros((CTX, HEAD_DIM), f32), jnp.sin(ang)], axis=0)
    return jnp.tile(cos, (1, LANES // HEAD_DIM)), jnp.tile(sin, (1, LANES // HEAD_DIM))


def _block_diag2(w):
    z = jnp.zeros_like(w[0])
    return jnp.concatenate([jnp.concatenate([w[0], z], axis=1), jnp.concatenate([z, w[1]], axis=1)], axis=0)


def kernel(x, c, ctx, c_ctx, w_mod, b_mod, norm_mix_g, norm_ffn_g, w_in, w_out, att_sink, shift_mu_prev, shift_mu_next, decay_w0, decay_w2, iclr_a0, iclr_a2, vres_v0, vres_v1, vres_v2, gate_g2, k_k, k_a, r_k, ln_x_w, ln_x_b, router_w, router_b, expert_w_gate, expert_w_up, expert_w_down, final_norm_g):
    nbatch = x.shape[0]
    depth = w_mod.shape[0]
    tb = ctx.shape[1] + x.shape[1]
    xa = jnp.concatenate([ctx, x], axis=1)
    nb_pad = -(-(nbatch + 1) // 8) * 8
    cond = jnp.zeros((nb_pad, D), f32).at[:nbatch].set(c).at[nbatch].set(c_ctx)
    mod_all = _mod_call(cond, w_mod, b_mod).reshape(depth, nb_pad, 1, 6 * D)
    cos, sin = _rope_tables(tb)
    wr_hi = router_w.T.astype(bf16)
    wr_lo = (router_w.T - wr_hi.astype(f32)).astype(bf16)
    w_router = jnp.stack([wr_hi, wr_lo])
    b_router = router_b.reshape(N_EXPERTS, 1)
    v_first = None
    for l in range(depth):
        mod = mod_all[l]
        q, k, v, rw = _in_proj_call(xa, mod, norm_mix_g[l].reshape(1, D), w_in[l].astype(bf16), cos, sin)
        att = _attn_call(att_sink[l], q, k, v)
        mu = jnp.stack([shift_mu_prev[l], shift_mu_next[l]])
        v0 = vres_v0[l - 1] if l > 0 else jnp.zeros((RW,), f32)
        vec = jnp.stack([k_k[l], k_a[l], r_k[l].reshape(RW), v0,
                         decay_w0[l, 0], decay_w0[l, 1], iclr_a0[l, 0], iclr_a0[l, 1]])
        if l > 0:
            v1 = jnp.zeros((RW, LANES), f32).at[:, :LORA_VRES].set(vres_v1[l - 1]).astype(bf16)
            v2 = jnp.zeros((LANES, RW), f32).at[:LORA_VRES].set(vres_v2[l - 1]).astype(bf16)
        else:
            v1 = v2 = None
        r_, v_, kk, bv, g, kd, lw, bd = _feat_call(
            rw, v_first, mu, vec, _block_diag2(decay_w2[l]).astype(bf16), _block_diag2(iclr_a2[l]).astype(bf16),
            gate_g2[l].astype(bf16), v1, v2)
        if l == 0:
            v_first = v_
        yf, yb = _scan_call(r_, v_, kk, kd, lw, bd)
        ln = jnp.stack([ln_x_w[l], ln_x_b[l]])
        xa, x_local, lpos, gates, cnt = _mix_call(xa, att, yf, yb, bv, g, mod, ln, w_out[l].astype(bf16),
                                                  norm_ffn_g[l].reshape(1, D), w_router, b_router)
        ntok = nbatch * tb
        nrows = _moe_rows(ntok)
        cnt = cnt[:, :, 0].astype(i32).reshape(ntok // TM, N_EXPERTS)
        base, seg, lstart, blk_expert, n_used = _segment_plan(cnt, nrows // MOE_BLK)
        x_sorted = _dispatch_call(base, seg, lstart, x_local.reshape(-1, D // 2), nrows)
        y_sorted = _ffn_call(blk_expert, n_used, x_sorted, expert_w_gate[l], expert_w_up[l], expert_w_down[l])
        xa = _combine_call(base, seg, lstart, xa, lpos, gates, mod, y_sorted,
                           final_norm_g.reshape(1, D) if l == depth - 1 else None)
    return xa
```

```python
import functools
import math

import jax
import jax.numpy as jnp
from jax import lax
from jax.experimental import pallas as pl
from jax.experimental.pallas import tpu as pltpu

f32 = jnp.float32
bf16 = jnp.bfloat16
i32 = jnp.int32
u32 = jnp.uint32

D = 1024
SEQ = 4096
CTX = 256
TB = CTX + SEQ
GRID_W = 64
HEAD_DIM = 64
ATT_WIDTH = 512
ATT_HEADS = 8
KV_HEADS = 2
ATT_GROUP = ATT_HEADS // KV_HEADS
KV_WIDTH = KV_HEADS * HEAD_DIM
RW = 512
RWKV_HEADS = 8
LORA_DECAY = 64
LORA_ICLR = 64
LORA_VRES = 32
LORA_GATE = 128
RWKV_COLS = 3 * RW + 2 * (LORA_DECAY + LORA_ICLR) + LORA_GATE
ATT_COLS = ATT_WIDTH + 2 * KV_WIDTH
IN_COLS = ATT_COLS + RWKV_COLS
N_EXPERTS = 16
N_GROUPS = 4
EXPERTS_PER_GROUP = 4
TOP_K = 2
MOE_BLK = 256
NORM_EPS = 1e-6
GN_EPS = 64e-5
NEG_INF = -1e30
ATT_SCALE = HEAD_DIM ** -0.5
ROPE_BASE = 10000.0
ROPE_FREQS = HEAD_DIM // 4

LANES = 128
TM = 256
QB = 128
CH = 64
HG = 4
GW = HG * HEAD_DIM
SEG_ALIGN = 8
LROWS = -(-(TOP_K * TM + N_EXPERTS * SEG_ALIGN) // LANES) * LANES
VMEM_LIMIT = 48 * 1024 * 1024


def _cparams(sem):
    return pltpu.CompilerParams(dimension_semantics=sem, vmem_limit_bytes=VMEM_LIMIT)


def _sigmoid(x):
    return 0.5 * jnp.tanh(0.5 * x) + 0.5


def _div_pow2(x, n):
    assert n & (n - 1) == 0
    return lax.shift_right_logical(x, n.bit_length() - 1)


def _mod_pow2(x, n):
    assert n & (n - 1) == 0
    return lax.bitwise_and(x, n - 1)


def _round_up_pow2(x, n):
    assert n & (n - 1) == 0
    return lax.bitwise_and(x + (n - 1), ~(n - 1))


def _dot(a, b):
    return jnp.dot(a, b, preferred_element_type=f32)


def _dot_nt(a, b):
    return lax.dot_general(a, b, (((1,), (1,)), ((), ())), preferred_element_type=f32)


def _dot_tn(a, b):
    return lax.dot_general(a, b, (((0,), (0,)), ((), ())), preferred_element_type=f32)


def _split2(x):
    hi = x.astype(bf16)
    lo = (x - hi.astype(f32)).astype(bf16)
    return hi, lo


def _dot_split_lhs(x, m):
    hi, lo = _split2(x)
    return _dot(hi, m) + _dot(lo, m)


def _rmsnorm(x, g):
    ms = jnp.mean(x * x, axis=-1, keepdims=True)
    return x * lax.rsqrt(ms + NORM_EPS) * g


def _head_ones():
    r = _div_pow2(lax.broadcasted_iota(i32, (RW, RW), 0), HEAD_DIM)
    c = _div_pow2(lax.broadcasted_iota(i32, (RW, RW), 1), HEAD_DIM)
    return (r == c).astype(bf16)


def _mod_kernel(c_ref, w_ref, b_ref, o_ref):
    c = c_ref[...]
    s = (c * _sigmoid(c)).astype(bf16)
    o_ref[0] = _dot(s, w_ref[0].astype(bf16)) + b_ref[0]


def _mod_call(cond, w_mod, b_mod):
    nb = cond.shape[0]
    depth = w_mod.shape[0]
    tn = 1024
    return pl.pallas_call(
        _mod_kernel,
        grid=(depth, 6 * D // tn),
        in_specs=[
            pl.BlockSpec((nb, D), lambda l, j: (0, 0)),
            pl.BlockSpec((1, D, tn), lambda l, j: (l, 0, j)),
            pl.BlockSpec((1, 1, tn), lambda l, j: (l, 0, j)),
        ],
        out_specs=pl.BlockSpec((1, nb, tn), lambda l, j: (l, 0, j)),
        out_shape=jax.ShapeDtypeStruct((depth, nb, 6 * D), f32),
        compiler_params=_cparams(("arbitrary", "arbitrary")),
        name="mod",
    )(cond, w_mod, b_mod.reshape(depth, 1, 6 * D))


def _mod_index(nbatch, ctx_tiles):
    return lambda b, i: (jnp.where(i < ctx_tiles, nbatch, b), 0, 0)


def _in_proj_kernel(x_ref, mod_ref, g_ref, w_ref, cos_ref, sin_ref, q_ref, k_ref, v_ref, rw_ref):
    x = x_ref[0]
    tm = x.shape[0]
    h = _rmsnorm(x, g_ref[...])
    sh = mod_ref[0, :, 0:D]
    sc = mod_ref[0, :, D:2 * D]
    h = (h * (1.0 + sc) + sh).astype(bf16)
    p = _dot(h, w_ref[...])
    cos = cos_ref[...]
    sin = sin_ref[...]
    lane = lax.broadcasted_iota(i32, (tm, LANES), 1)
    first_half = _mod_pow2(lane, 2 * ROPE_FREQS) < ROPE_FREQS

    def rope(t):
        rot = jnp.where(first_half, -pltpu.roll(t, LANES - ROPE_FREQS, 1), pltpu.roll(t, ROPE_FREQS, 1))
        return t * cos + rot * sin

    for j in range(ATT_WIDTH // LANES):
        t = (rope(p[:, j * LANES:(j + 1) * LANES]) * ATT_SCALE).astype(bf16)
        q_ref[0, 2 * j] = t[:, :HEAD_DIM]
        q_ref[0, 2 * j + 1] = t[:, HEAD_DIM:]
    kt = rope(p[:, ATT_WIDTH:ATT_WIDTH + KV_WIDTH]).astype(bf16)
    vt = p[:, ATT_WIDTH + KV_WIDTH:ATT_COLS].astype(bf16)
    for hh in range(KV_HEADS):
        k_ref[0, hh] = kt[:, hh * HEAD_DIM:(hh + 1) * HEAD_DIM]
        v_ref[0, hh] = vt[:, hh * HEAD_DIM:(hh + 1) * HEAD_DIM]
    rw_ref[0] = p[:, ATT_COLS:]


def _in_proj_call(x, mod, g, w_in, cos, sin):
    nbatch, tb, _ = x.shape
    nt = tb // TM
    return pl.pallas_call(
        _in_proj_kernel,
        grid=(nbatch, nt),
        in_specs=[
            pl.BlockSpec((1, TM, D), lambda b, i: (b, i, 0)),
            pl.BlockSpec((1, 1, 6 * D), _mod_index(nbatch, CTX // TM)),
            pl.BlockSpec((1, D), lambda b, i: (0, 0)),
            pl.BlockSpec((D, IN_COLS), lambda b, i: (0, 0)),
            pl.BlockSpec((TM, LANES), lambda b, i: (i, 0)),
            pl.BlockSpec((TM, LANES), lambda b, i: (i, 0)),
        ],
        out_specs=[
            pl.BlockSpec((1, ATT_HEADS, TM, HEAD_DIM), lambda b, i: (b, 0, i, 0)),
            pl.BlockSpec((1, KV_HEADS, TM, HEAD_DIM), lambda b, i: (b, 0, i, 0)),
            pl.BlockSpec((1, KV_HEADS, TM, HEAD_DIM), lambda b, i: (b, 0, i, 0)),
            pl.BlockSpec((1, TM, RWKV_COLS), lambda b, i: (b, i, 0)),
        ],
        out_shape=[
            jax.ShapeDtypeStruct((nbatch, ATT_HEADS, tb, HEAD_DIM), bf16),
            jax.ShapeDtypeStruct((nbatch, KV_HEADS, tb, HEAD_DIM), bf16),
            jax.ShapeDtypeStruct((nbatch, KV_HEADS, tb, HEAD_DIM), bf16),
            jax.ShapeDtypeStruct((nbatch, tb, RWKV_COLS), f32),
        ],
        compiler_params=_cparams(("parallel", "arbitrary")),
        name="in_proj",
    )(x, mod, g, w_in, cos, sin)


def _attn_kernel(sink_ref, q_ref, kp_ref, kc_ref, kn_ref, vp_ref, vc_ref, vn_ref, kx_ref, vx_ref, o_ref,
                 *, nblk, ctx_blks):
    i = pl.program_id(1)
    is_lat = i >= ctx_blks
    prev_ok = jnp.logical_and(is_lat, i - 1 >= ctx_blks)
    next_ok = jnp.logical_and(is_lat, i + 1 <= nblk - 1)
    rows = ATT_GROUP * QB
    qi = _mod_pow2(lax.broadcasted_iota(i32, (rows, QB), 0), QB)
    kj = lax.broadcasted_iota(i32, (rows, QB), 1)
    mask_p = jnp.logical_and(kj >= qi, prev_ok)
    mask_n = jnp.logical_and(kj <= qi, next_ok)
    row_head = _div_pow2(lax.broadcasted_iota(i32, (rows, 1), 0), QB)
    heads = range(KV_HEADS)
    qh = [q_ref[0, ATT_GROUP * h:ATT_GROUP * (h + 1)].reshape(rows, HEAD_DIM) for h in heads]
    s_p = [jnp.where(mask_p, _dot_nt(qh[h], kp_ref[0, h]), NEG_INF) for h in heads]
    s_c = [jnp.where(is_lat, _dot_nt(qh[h], kc_ref[0, h]), NEG_INF) for h in heads]
    s_n = [jnp.where(mask_n, _dot_nt(qh[h], kn_ref[0, h]), NEG_INF) for h in heads]
    s_x = [_dot_nt(qh[h], kx_ref[0, h]) for h in heads]
    sink = []
    for h in heads:
        sk = jnp.zeros((rows, 1), f32)
        for g in range(ATT_GROUP):
            sk = jnp.where(row_head == g, sink_ref[ATT_GROUP * h + g], sk)
        sink.append(sk)
    m = [jnp.maximum(jnp.max(jnp.maximum(jnp.maximum(s_p[h], s_c[h]),
                                         jnp.maximum(jnp.maximum(s_n[h], s_x[h][:, :QB]), s_x[h][:, QB:])),
                             axis=-1, keepdims=True), sink[h]) for h in heads]
    e_p = [jnp.exp(s_p[h] - m[h]) for h in heads]
    e_c = [jnp.exp(s_c[h] - m[h]) for h in heads]
    e_n = [jnp.exp(s_n[h] - m[h]) for h in heads]
    e_x = [jnp.exp(s_x[h] - m[h]) for h in heads]
    den = [jnp.sum((e_p[h] + e_c[h]) + (e_n[h] + e_x[h][:, :QB]) + e_x[h][:, QB:], axis=-1, keepdims=True)
           + jnp.exp(sink[h] - m[h]) for h in heads]
    o = [(_dot(e_p[h].astype(bf16), vp_ref[0, h]) + _dot(e_c[h].astype(bf16), vc_ref[0, h])
          + _dot(e_n[h].astype(bf16), vn_ref[0, h]) + _dot(e_x[h].astype(bf16), vx_ref[0, h])) / den[h]
         for h in heads]
    o_ref[0] = jnp.concatenate([o[h][g * QB:(g + 1) * QB] for h in heads for g in range(ATT_GROUP)],
                               axis=1).astype(bf16)


def _attn_call(sink, q, k, v):
    nbatch, _, tb, _ = q.shape
    nblk = tb // QB
    ctx_blks = CTX // QB
    assert CTX == 2 * QB
    kv_blk = (1, KV_HEADS, QB, HEAD_DIM)
    prev_map = lambda b, i: (b, 0, jnp.maximum(i - 1, 0), 0)
    cur_map = lambda b, i: (b, 0, i, 0)
    next_map = lambda b, i: (b, 0, jnp.minimum(i + 1, nblk - 1), 0)
    ctx_spec = pl.BlockSpec((1, KV_HEADS, CTX, HEAD_DIM), lambda b, i: (b, 0, 0, 0))
    return pl.pallas_call(
        functools.partial(_attn_kernel, nblk=nblk, ctx_blks=ctx_blks),
        grid=(nbatch, nblk),
        in_specs=[
            pl.BlockSpec(memory_space=pltpu.SMEM),
            pl.BlockSpec((1, ATT_HEADS, QB, HEAD_DIM), cur_map),
            pl.BlockSpec(kv_blk, prev_map), pl.BlockSpec(kv_blk, cur_map), pl.BlockSpec(kv_blk, next_map),
            pl.BlockSpec(kv_blk, prev_map), pl.BlockSpec(kv_blk, cur_map), pl.BlockSpec(kv_blk, next_map),
            ctx_spec, ctx_spec,
        ],
        out_specs=pl.BlockSpec((1, QB, ATT_WIDTH), lambda b, i: (b, i, 0)),
        out_shape=jax.ShapeDtypeStruct((nbatch, tb, ATT_WIDTH), bf16),
        compiler_params=_cparams(("parallel", "arbitrary")),
        name="attention",
    )(sink, q, k, k, k, v, v, v, k, v)


def _feat_kernel(*refs, nt, ctx_tiles, has_vres):
    if has_vres:
        (rw_ref, hp_ref, hn_ref, vf_ref, mu_ref, vec_ref, w2_ref, a2_ref, g2_ref, v1_ref, v2_ref,
         r_ref, v_ref, kk_ref, bv_ref, g_ref, kd_ref, lw_ref, bd_ref) = refs
    else:
        (rw_ref, hp_ref, hn_ref, mu_ref, vec_ref, w2_ref, a2_ref, g2_ref,
         r_ref, v_ref, kk_ref, bv_ref, g_ref, kd_ref, lw_ref, bd_ref) = refs
    i = pl.program_id(1)
    u0 = rw_ref[0]
    tm = u0.shape[0]
    prev_zero = jnp.logical_or(i == 0, i == ctx_tiles)
    next_zero = jnp.logical_or(i == ctx_tiles - 1, i == nt - 1)
    halo_p = jnp.where(prev_zero, 0.0, hp_ref[0, 7:8, :])
    halo_n = jnp.where(next_zero, 0.0, hn_ref[0, 0:1, :])
    row = lax.broadcasted_iota(i32, (tm, 1), 0)
    prev = jnp.where(row == 0, halo_p, pltpu.roll(u0, 1, 0))
    nxt = jnp.where(row == tm - 1, halo_n, pltpu.roll(u0, tm - 1, 0))
    mu_p = mu_ref[0:1, :]
    mu_n = mu_ref[1:2, :]
    u = u0 + mu_p * (prev - u0) + mu_n * (nxt - u0)

    r = u[:, 0:RW]
    k = u[:, RW:2 * RW]
    v = u[:, 2 * RW:3 * RW]
    wd = u[:, 3 * RW:3 * RW + 2 * LORA_DECAY]
    ad = u[:, 3 * RW + 2 * LORA_DECAY:3 * RW + 2 * (LORA_DECAY + LORA_ICLR)]
    gd = u[:, 3 * RW + 2 * (LORA_DECAY + LORA_ICLR):]
    k_k = vec_ref[0:1, :]
    k_a = vec_ref[1:2, :]
    r_k = vec_ref[2:3, :]
    ones = _head_ones()

    if has_vres:
        lo = _dot(v.astype(bf16), v1_ref[...])
        gate = _sigmoid(vec_ref[3:4, :] + _dot(lo.astype(bf16), v2_ref[...]))
        v = v + (vf_ref[0] - v) * gate
    decay_in = _dot(jnp.tanh(wd).astype(bf16), w2_ref[...])
    a_in = _dot(ad.astype(bf16), a2_ref[...])
    kk = k * k_k
    n2 = _dot_split_lhs(kk * kk, ones)
    kk = kk * lax.rsqrt(jnp.maximum(n2, 1e-24))
    g = _dot(_sigmoid(gd).astype(bf16), g2_ref[...])
    ksum = jnp.zeros_like(k)
    for d in range(2):
        w0 = vec_ref[4 + d:5 + d, :]
        a0 = vec_ref[6 + d:7 + d, :]
        lw = -_sigmoid(w0 + decay_in[:, d * RW:(d + 1) * RW]) * math.exp(-0.5)
        a = _sigmoid(a0 + a_in[:, d * RW:(d + 1) * RW])
        kd = k * (1.0 + (a - 1.0) * k_a)
        ksum = ksum + kd
        kd_ref[d, 0] = kd
        lw_ref[d, 0] = lw
        bd_ref[d, 0] = kk * a
    bonus = _dot_split_lhs(r * ksum * r_k, ones)
    r_ref[0] = r
    v_ref[0] = v
    kk_ref[0] = kk
    bv_ref[0] = bonus * v
    g_ref[0] = g


def _feat_call(rw, v_first, mu, vec, w2bd, a2bd, g2, v1, v2):
    nbatch, tb, _ = rw.shape
    nt = tb // TM
    has_vres = v_first is not None
    sub = TM // 8
    tile = lambda w: pl.BlockSpec((1, TM, w), lambda b, i: (b, i, 0))
    full = lambda a: pl.BlockSpec(a.shape, lambda b, i: (0,) * a.ndim)
    in_specs = [
        tile(RWKV_COLS),
        pl.BlockSpec((1, 8, RWKV_COLS), lambda b, i: (b, jnp.maximum(i * sub - 1, 0), 0)),
        pl.BlockSpec((1, 8, RWKV_COLS), lambda b, i: (b, jnp.minimum((i + 1) * sub, tb // 8 - 1), 0)),
    ]
    args = [rw, rw, rw]
    if has_vres:
        in_specs.append(tile(RW))
        args.append(v_first)
    consts = [mu, vec, w2bd, a2bd, g2] + ([v1, v2] if has_vres else [])
    in_specs += [full(a) for a in consts]
    args += consts
    dir_spec = pl.BlockSpec((2, 1, TM, RW), lambda b, i: (0, b, i, 0))
    tok = jax.ShapeDtypeStruct((nbatch, tb, RW), f32)
    dtok = jax.ShapeDtypeStruct((2, nbatch, tb, RW), f32)
    return pl.pallas_call(
        functools.partial(_feat_kernel, nt=nt, ctx_tiles=CTX // TM, has_vres=has_vres),
        grid=(nbatch, nt),
        in_specs=in_specs,
        out_specs=[tile(RW)] * 5 + [dir_spec] * 3,
        out_shape=[tok] * 5 + [dtok] * 3,
        compiler_params=_cparams(("parallel", "arbitrary")),
        name="rwkv_features",
    )(*args)


def _block_diag_rows(x, width):
    cb = _div_pow2(lax.broadcasted_iota(i32, x.shape, 1), width)
    return jnp.concatenate([jnp.where(cb == h, x, jnp.zeros_like(x)) for h in range(HG)], axis=0)


def _scan_chunks(probs):
    wide = HG * CH
    row_t = lax.broadcasted_iota(i32, (CH, wide), 0)
    col_t = _mod_pow2(lax.broadcasted_iota(i32, (CH, wide), 1), CH)
    row_g = lax.broadcasted_iota(i32, (CH, GW), 0)
    incl_t = {False: col_t <= row_t, True: col_t >= row_t}
    strict_t = {False: col_t < row_t, True: col_t > row_t}
    eye_t = (row_t == col_t).astype(f32)
    rb = _div_pow2(lax.broadcasted_iota(i32, (GW, GW), 0), HEAD_DIM)
    cb = _div_pow2(lax.broadcasted_iota(i32, (GW, GW), 1), HEAD_DIM)
    n = len(probs)
    rev = [p[7] for p in probs]
    def cumsum_rows(x, reverse):
        s = 1
        while s < CH:
            if reverse:
                x = x + jnp.where(row_g < CH - s, pltpu.roll(x, CH - s, 0), 0.0)
            else:
                x = x + jnp.where(row_g >= s, pltpu.roll(x, s, 0), 0.0)
            s *= 2
        return x

    gam = [cumsum_rows(probs[i][5], rev[i]) for i in range(n)]
    ar, bk, k_t, b_t = [], [], [], []
    for i, (s_prev, r, v, kk, k, lw, b, _) in enumerate(probs):
        e_neg = jnp.exp(-gam[i])
        a_s = (-kk * jnp.exp(gam[i] - lw)).astype(bf16)
        r_s = (r * jnp.exp(gam[i])).astype(bf16)
        b_t.append(b * e_neg)
        k_t.append(k * e_neg)
        ar.append(jnp.concatenate([a_s, r_s], axis=0))
        bk.append(jnp.concatenate([_block_diag_rows(b_t[i].astype(bf16), HEAD_DIM),
                                   _block_diag_rows(k_t[i].astype(bf16), HEAD_DIM)], axis=0))
    gram = [_dot_nt(ar[i], bk[i]) for i in range(n)]
    ars = [_dot_nt(ar[i], probs[i][0].astype(bf16)) for i in range(n)]
    v_bd = [_block_diag_rows(probs[i][2].astype(bf16), HEAD_DIM) for i in range(n)]
    p0 = [jnp.where(strict_t[rev[i]], gram[i][:CH, :wide], 0.0).astype(bf16) for i in range(n)]
    lq = [jnp.concatenate([jnp.where(strict_t[rev[i]], gram[i][:CH, wide:], 0.0),
                           jnp.where(incl_t[rev[i]], gram[i][CH:, wide:], 0.0)], axis=0).astype(bf16)
          for i in range(n)]
    lqv = [_dot(lq[i], v_bd[i]) for i in range(n)]
    rhs = [ars[i][:CH] + lqv[i][:CH] for i in range(n)]
    t = [eye_t + p0[i].astype(f32) for i in range(n)]
    p = [_dot(p0[i], _block_diag_rows(p0[i], CH)).astype(bf16) for i in range(n)]
    m = 4
    while m < CH:
        tp = [_dot(jnp.concatenate([t[i].astype(bf16), p[i]], axis=0), _block_diag_rows(p[i], CH))
              for i in range(n)]
        t = [t[i] + tp[i][:CH] for i in range(n)]
        p = [tp[i][CH:].astype(bf16) for i in range(n)]
        m *= 2
    t = [t[i] + _dot(t[i].astype(bf16), _block_diag_rows(p[i], CH)) for i in range(n)]
    u = [_dot(t[i].astype(bf16), _block_diag_rows(rhs[i].astype(bf16), HEAD_DIM)) for i in range(n)]
    out = []
    for i in range(n):
        s_prev, v, lw = probs[i][0], probs[i][2], probs[i][5]
        q_b = jnp.where(incl_t[rev[i]], gram[i][CH:, :wide], 0.0).astype(bf16)
        u_bd = _block_diag_rows(u[i].astype(bf16), HEAD_DIM)
        y = ars[i][CH:] + lqv[i][CH:] + _dot(q_b, u_bd)
        eg = jnp.exp(jnp.sum(lw, axis=0, keepdims=True))
        vu = jnp.concatenate([v, u[i]], axis=0).astype(bf16)
        kb = jnp.concatenate([k_t[i] * eg, b_t[i] * eg], axis=0).astype(bf16)
        s_add = _dot_tn(vu, kb)
        out.append((s_prev * eg + jnp.where(rb == cb, s_add, 0.0), y))
    return out


def _scan_kernel(rf_ref, vf_ref, kkf_ref, kf_ref, lwf_ref, bf_ref,
                 rb_ref, vb_ref, kkb_ref, kb_ref, lwb_ref, bb_ref,
                 yf_ref, yb_ref, s_ref):
    @pl.when(pl.program_id(1) == 0)
    def _():
        s_ref[...] = jnp.zeros_like(s_ref)

    dirs = ((rf_ref, vf_ref, kkf_ref, kf_ref, lwf_ref, bf_ref, yf_ref),
            (rb_ref, vb_ref, kkb_ref, kb_ref, lwb_ref, bb_ref, yb_ref))
    probs, dest = [], []
    for d, (r_ref, v_ref, kk_ref, k_ref, lw_ref, b_ref, y_ref) in enumerate(dirs):
        for g in range(RW // GW):
            sl = slice(g * GW, (g + 1) * GW)
            probs.append((s_ref[d, g], r_ref[0, :, sl], v_ref[0, :, sl], kk_ref[0, :, sl],
                          k_ref[0, 0, :, sl], lw_ref[0, 0, :, sl], b_ref[0, 0, :, sl], d == 1))
            dest.append((d, g, y_ref, sl))
    for (d, g, y_ref, sl), (s_new, y) in zip(dest, _scan_chunks(probs)):
        s_ref[d, g] = s_new
        y_ref[0, :, sl] = y


def _scan_call(r, v, kk, kd, lw, bd):
    nbatch, tb, _ = r.shape
    nc = tb // CH
    cc = CTX // CH
    rev = lambda j: jnp.where(j < cc, cc - 1 - j, nc - 1 + cc - j)
    tok_f = pl.BlockSpec((1, CH, RW), lambda b, j: (b, j, 0))
    tok_b = pl.BlockSpec((1, CH, RW), lambda b, j: (b, rev(j), 0))
    dir_f = pl.BlockSpec((1, 1, CH, RW), lambda b, j: (0, b, j, 0))
    dir_b = pl.BlockSpec((1, 1, CH, RW), lambda b, j: (1, b, rev(j), 0))
    out = jax.ShapeDtypeStruct((nbatch, tb, RW), f32)
    return pl.pallas_call(
        _scan_kernel,
        grid=(nbatch, nc),
        in_specs=[tok_f, tok_f, tok_f, dir_f, dir_f, dir_f, tok_b, tok_b, tok_b, dir_b, dir_b, dir_b],
        out_specs=[tok_f, tok_b],
        out_shape=[out, out],
        scratch_shapes=[pltpu.VMEM((2, RW // GW, GW, GW), f32)],
        compiler_params=_cparams(("parallel", "arbitrary")),
        name="rwkv_scan",
    )(r, v, kk, kd, lw, bd, r, v, kk, kd, lw, bd)


def _pack_bf16_pairs(x):
    bits = pltpu.bitcast(x, u32)
    half = x.shape[1] // 2
    return bits[:, :half] | lax.shift_right_logical(bits[:, half:], jnp.uint32(16))


def _unpack_bf16_pairs(p):
    hi = pltpu.bitcast(p & jnp.uint32(0xFFFF0000), f32)
    lo = pltpu.bitcast(lax.shift_left(p, jnp.uint32(16)), f32)
    return jnp.concatenate([hi, lo], axis=1).astype(bf16)


def _mix_kernel(x_ref, att_ref, yf_ref, yb_ref, bv_ref, g_ref, mod_ref, ln_ref, wo_ref, gf_ref, wr_ref, rb_ref,
                xo_ref, xl_ref, lp_ref, gt_ref, cnt_ref):
    ones = _head_ones()
    y = yf_ref[0] + yb_ref[0]
    inv = 1.0 / HEAD_DIM
    mu = _dot_split_lhs(y, ones) * inv
    dlt = y - mu
    var = _dot_split_lhs(dlt * dlt, ones) * inv
    gn = dlt * lax.rsqrt(var + GN_EPS) * ln_ref[0:1, :] + ln_ref[1:2, :]
    rwk = ((gn + bv_ref[0]) * g_ref[0]).astype(bf16)
    mix = _dot(att_ref[0], wo_ref[0:ATT_WIDTH, :]) + _dot(rwk, wo_ref[ATT_WIDTH:, :])
    gate_a = mod_ref[0, :, 2 * D:3 * D]
    xn = x_ref[0] + gate_a * mix
    xo_ref[0] = xn
    sh = mod_ref[0, :, 3 * D:4 * D]
    sc = mod_ref[0, :, 4 * D:5 * D]
    hf = _rmsnorm(xn, gf_ref[...]) * (1.0 + sc) + sh
    h_hi, h_lo = _split2(hf)
    logits = _dot_nt(wr_ref[0], h_hi) + _dot_nt(wr_ref[0], h_lo) + _dot_nt(wr_ref[1], h_hi)
    _route_and_sort(hf, _sigmoid(logits), rb_ref[...], xl_ref, lp_ref, gt_ref, cnt_ref)


def _route_and_sort(hf, scores, bias, xl_ref, lp_ref, gt_ref, cnt_ref):
    tm = hf.shape[0]
    e0, e1, g0, g1 = _route_rows(scores, bias)
    gt_ref[0] = jnp.concatenate([g0, g1], axis=0)
    eio = lax.broadcasted_iota(i32, (N_EXPERTS, tm), 0)
    oh0 = (eio == e0).astype(f32)
    oh1 = (eio == e1).astype(f32)
    both = oh0 + oh1
    upper = (lax.broadcasted_iota(i32, (tm, tm), 0) < lax.broadcasted_iota(i32, (tm, tm), 1)).astype(bf16)
    before = _dot(both.astype(bf16), upper)
    cnt = jnp.sum(both, axis=1, keepdims=True)
    seg = _round_up_pow2(cnt.astype(i32), SEG_ALIGN).astype(f32)
    lower = (lax.broadcasted_iota(i32, (N_EXPERTS, N_EXPERTS), 0)
             > lax.broadcasted_iota(i32, (N_EXPERTS, N_EXPERTS), 1)).astype(bf16)
    seg_start = _dot(lower, jnp.broadcast_to(seg, (N_EXPERTS, tm)).astype(bf16))
    pos = seg_start + before
    lp0 = jnp.sum(oh0 * pos, axis=0, keepdims=True).astype(i32)
    lp1 = jnp.sum(oh1 * pos, axis=0, keepdims=True).astype(i32)
    lp_ref[0] = jnp.concatenate([lp0, lp1], axis=0)
    cnt_ref[0] = jnp.broadcast_to(cnt, (N_EXPERTS, LANES))
    jj = lax.broadcasted_iota(i32, (LROWS, tm), 0)
    sel_t = jnp.logical_or(jj == lp0, jj == lp1).astype(bf16)
    xl_ref[0] = _pack_bf16_pairs(_dot(sel_t, hf.astype(bf16)))


def _route_rows(scores, bias):
    biased = scores + bias
    row = lambda a, e: a[e:e + 1, :]
    best = None
    for gi in range(N_GROUPS):
        m = [row(biased, gi * EXPERTS_PER_GROUP + j) for j in range(EXPERTS_PER_GROUP)]
        gs = None
        for a in range(EXPERTS_PER_GROUP):
            for b in range(a + 1, EXPERTS_PER_GROUP):
                pair = m[a] + m[b]
                gs = pair if gs is None else jnp.maximum(gs, pair)
        if best is None:
            best, g_idx = gs, jnp.zeros(gs.shape, i32)
        else:
            better = gs > best
            g_idx = jnp.where(better, gi, g_idx)
            best = jnp.where(better, gs, best)

    def pick(a, j):
        out = row(a, j)
        for gi in range(1, N_GROUPS):
            out = jnp.where(g_idx == gi, row(a, gi * EXPERTS_PER_GROUP + j), out)
        return out

    vb = [pick(biased, j) for j in range(EXPERTS_PER_GROUP)]
    vs = [pick(scores, j) for j in range(EXPERTS_PER_GROUP)]

    def argmax_first(vals):
        bv, bi = vals[0], jnp.zeros(vals[0].shape, i32)
        for j in range(1, len(vals)):
            better = vals[j] > bv
            bi = jnp.where(better, j, bi)
            bv = jnp.where(better, vals[j], bv)
        return bi

    i1 = argmax_first(vb)
    i2 = argmax_first([jnp.where(i1 == j, -jnp.inf, vb[j]) for j in range(EXPERTS_PER_GROUP)])
    sel = lambda idx: sum(jnp.where(idx == j, vs[j], 0.0) for j in range(EXPERTS_PER_GROUP))
    s1, s2 = sel(i1), sel(i2)
    tot = s1 + s2
    base = g_idx * EXPERTS_PER_GROUP
    return base + i1, base + i2, s1 / tot, s2 / tot


def _mix_call(x, att, yf, yb, bv, g, mod, ln, w_out, g_ffn, w_router, b_router):
    nbatch, tb, _ = x.shape
    nt = tb // TM
    tile = lambda w: pl.BlockSpec((1, TM, w), lambda b, i: (b, i, 0))
    full = lambda a: pl.BlockSpec(a.shape, lambda b, i: (0,) * a.ndim)
    route = pl.BlockSpec((1, TOP_K, TM), lambda b, i: (b, 0, i))
    return pl.pallas_call(
        _mix_kernel,
        grid=(nbatch, nt),
        in_specs=[tile(D), tile(ATT_WIDTH), tile(RW), tile(RW), tile(RW), tile(RW),
                  pl.BlockSpec((1, 1, 6 * D), _mod_index(nbatch, CTX // TM)),
                  full(ln), full(w_out), full(g_ffn), full(w_router), full(b_router)],
        out_specs=[tile(D), pl.BlockSpec((1, LROWS, D // 2), lambda b, i: (b, i, 0)), route, route,
                   pl.BlockSpec((1, N_EXPERTS, LANES), lambda b, i: (b, i, 0))],
        out_shape=[jax.ShapeDtypeStruct((nbatch, tb, D), f32),
                   jax.ShapeDtypeStruct((nbatch, nt * LROWS, D // 2), u32),
                   jax.ShapeDtypeStruct((nbatch, TOP_K, tb), i32),
                   jax.ShapeDtypeStruct((nbatch, TOP_K, tb), f32),
                   jax.ShapeDtypeStruct((nbatch, nt * N_EXPERTS, LANES), f32)],
        compiler_params=_cparams(("parallel", "arbitrary")),
        name="mix_out",
    )(x, att, yf, yb, bv, g, mod, ln, w_out, g_ffn, w_router, b_router)


def _segment_copies(fn, tile, base_ref, seg_ref, ls_ref, src, dst, sem, src_is_global):
    for e in range(N_EXPERTS):
        idx = tile * N_EXPERTS + e
        seg = seg_ref[idx]
        g0 = base_ref[idx]
        l0 = ls_ref[idx]
        size = TM
        while size >= SEG_ALIGN:
            done = lax.bitwise_and(seg, ~(2 * size - 1))

            @pl.when(lax.bitwise_and(seg, size) != 0)
            def _():
                g_rows = pl.ds(pl.multiple_of(g0 + done, SEG_ALIGN), size)
                l_rows = pl.ds(pl.multiple_of(l0 + done, SEG_ALIGN), size)
                s_rows, d_rows = (g_rows, l_rows) if src_is_global else (l_rows, g_rows)
                fn(pltpu.make_async_copy(src.at[s_rows, :], dst.at[d_rows, :], sem))

            size //= 2


def _dispatch_kernel(base_ref, seg_ref, ls_ref, xl_ref, xs_in_ref, xs_ref, sem):
    del xs_in_ref
    tile = pl.program_id(0)
    args = (tile, base_ref, seg_ref, ls_ref, xl_ref, xs_ref, sem, False)
    _segment_copies(lambda cp: cp.start(), *args)
    _segment_copies(lambda cp: cp.wait(), *args)


def _dispatch_call(base, seg, lstart, x_local, nrows):
    ntiles = x_local.shape[0] // LROWS
    any_spec = pl.BlockSpec(memory_space=pl.ANY)
    return pl.pallas_call(
        _dispatch_kernel,
        grid_spec=pltpu.PrefetchScalarGridSpec(
            num_scalar_prefetch=3,
            grid=(ntiles,),
            in_specs=[pl.BlockSpec((LROWS, D // 2), lambda t, *_: (t, 0)), any_spec],
            out_specs=any_spec,
            scratch_shapes=[pltpu.SemaphoreType.DMA(())],
        ),
        out_shape=jax.ShapeDtypeStruct((nrows, D // 2), u32),
        input_output_aliases={4: 0},
        compiler_params=_cparams(("arbitrary",)),
        name="moe_dispatch",
    )(base, seg, lstart, x_local, jnp.zeros((nrows, D // 2), u32))


def _ffn_kernel(be_ref, nu_ref, x_ref, wg_ref, wu_ref, wd_ref, y_ref, wg_bf, wu_bf, wd_bf):
    i = pl.program_id(0)
    used = i < nu_ref[0]
    new_expert = jnp.logical_or(i == 0, be_ref[i] != be_ref[jnp.maximum(i - 1, 0)])

    @pl.when(jnp.logical_and(used, new_expert))
    def _():
        wg_bf[...] = wg_ref[0, 0].astype(bf16)
        wu_bf[...] = wu_ref[0, 0].astype(bf16)
        wd_bf[...] = wd_ref[0, 0].astype(bf16)

    @pl.when(used)
    def _():
        x = _unpack_bf16_pairs(x_ref[...])
        gt = _dot(x, wg_bf[...])
        up = _dot(x, wu_bf[...])
        hid = (gt * _sigmoid(gt) * up).astype(bf16)
        y = _dot(hid, wd_bf[...])
        y_ref[...] = _pack_bf16_pairs(y.astype(bf16).astype(f32))

    @pl.when(i >= nu_ref[0])
    def _():
        y_ref[...] = jnp.zeros_like(y_ref)


def _ffn_call(blk_expert, n_used, x_sorted, wg, wu, wd, layer):
    nrows = x_sorted.shape[0]
    nblk = nrows // MOE_BLK
    wspec = pl.BlockSpec((1, 1, D, D), lambda i, be, nu: (layer, be[i], 0, 0))
    return pl.pallas_call(
        _ffn_kernel,
        grid_spec=pltpu.PrefetchScalarGridSpec(
            num_scalar_prefetch=2,
            grid=(nblk,),
            in_specs=[pl.BlockSpec((MOE_BLK, D // 2), lambda i, be, nu: (i, 0)), wspec, wspec, wspec],
            out_specs=pl.BlockSpec((MOE_BLK, D // 2), lambda i, be, nu: (i, 0)),
            scratch_shapes=[pltpu.VMEM((D, D), bf16)] * 3,
        ),
        out_shape=jax.ShapeDtypeStruct((nrows, D // 2), u32),
        compiler_params=_cparams(("arbitrary",)),
        name="moe_ffn",
    )(blk_expert, n_used, x_sorted, wg, wu, wd)


def _combine_kernel(base_ref, seg_ref, ls_ref, x_ref, lp_ref, gt_ref, mod_ref, g_ref, y_ref, o_ref, ybuf, sem,
                    *, tiles_per_batch, first_tile, final):
    b = pl.program_id(0)
    i = pl.program_id(1)
    nt = pl.num_programs(1)
    step = b * nt + i
    slot = lax.rem(step, 2)
    tile = b * tiles_per_batch + i + first_tile
    next_tile = jnp.where(i + 1 < nt, tile + 1, (b + 1) * tiles_per_batch + first_tile)

    def copies(fn, which_tile, which_slot):
        _segment_copies(fn, which_tile, base_ref, seg_ref, ls_ref, y_ref, ybuf.at[which_slot],
                        sem.at[which_slot], True)

    @pl.when(step == 0)
    def _():
        ybuf[...] = jnp.zeros_like(ybuf)
        copies(lambda cp: cp.start(), tile, slot)

    @pl.when(step + 1 < pl.num_programs(0) * nt)
    def _():
        copies(lambda cp: cp.start(), next_tile, 1 - slot)

    copies(lambda cp: cp.wait(), tile, slot)
    y_loc = _unpack_bf16_pairs(ybuf[slot])
    tm = x_ref.shape[1]
    jj = lax.broadcasted_iota(i32, (LROWS, tm), 0)
    gmat = (jnp.where(jj == lp_ref[0, 0:1, :], gt_ref[0, 0:1, :], 0.0)
            + jnp.where(jj == lp_ref[0, 1:2, :], gt_ref[0, 1:2, :], 0.0))
    g_hi, g_lo = _split2(gmat)
    moe = _dot_tn(g_hi, y_loc) + _dot_tn(g_lo, y_loc)
    xn = x_ref[0] + mod_ref[0, :, 5 * D:6 * D] * moe
    o_ref[0] = _rmsnorm(xn, g_ref[...]) if final else xn


def _combine_call(base, seg, lstart, x, lpos, gates, mod, y_sorted, final_g):
    nbatch, tb, _ = x.shape
    ctx_tiles = CTX // TM
    tiles_per_batch = tb // TM
    final = final_g is not None
    first_tile = ctx_tiles if final else 0
    nt = tiles_per_batch - first_tile
    tile = pl.BlockSpec((1, TM, D), lambda b, i, *_: (b, i + first_tile, 0))
    route = pl.BlockSpec((1, TOP_K, TM), lambda b, i, *_: (b, 0, i + first_tile))
    mod_map = _mod_index(nbatch, ctx_tiles - first_tile)
    g_arr = final_g if final else jnp.ones((1, D), f32)
    return pl.pallas_call(
        functools.partial(_combine_kernel, tiles_per_batch=tiles_per_batch, first_tile=first_tile, final=final),
        grid_spec=pltpu.PrefetchScalarGridSpec(
            num_scalar_prefetch=3,
            grid=(nbatch, nt),
            in_specs=[tile, route, route,
                      pl.BlockSpec((1, 1, 6 * D), lambda b, i, *_: mod_map(b, i)),
                      pl.BlockSpec((1, D), lambda b, i, *_: (0, 0)),
                      pl.BlockSpec(memory_space=pl.ANY)],
            out_specs=pl.BlockSpec((1, TM, D), lambda b, i, *_: (b, i, 0)),
            scratch_shapes=[pltpu.VMEM((2, LROWS, D // 2), u32), pltpu.SemaphoreType.DMA((2,))],
        ),
        out_shape=jax.ShapeDtypeStruct((nbatch, nt * TM, D), f32),
        compiler_params=_cparams(("arbitrary", "arbitrary")),
        name="ffn_residual_final" if final else "ffn_residual",
    )(base, seg, lstart, x, lpos, gates, mod, g_arr, y_sorted)


def _moe_rows(ntok):
    ntiles = ntok // TM
    worst = TOP_K * ntok + ntiles * N_EXPERTS * (SEG_ALIGN - 1)
    return (-(-worst // MOE_BLK) + N_EXPERTS) * MOE_BLK


def _segment_plan(cnt, nblk):
    seg = (cnt + SEG_ALIGN - 1) // SEG_ALIGN * SEG_ALIGN
    lstart = jnp.cumsum(seg, axis=1) - seg
    rows = jnp.sum(seg, axis=0)
    padded = (rows + MOE_BLK - 1) // MOE_BLK * MOE_BLK
    pad_ends = jnp.cumsum(padded)
    base = (pad_ends - padded)[None, :] + jnp.cumsum(seg, axis=0) - seg
    blk_start = jnp.arange(nblk, dtype=i32) * MOE_BLK
    blk_expert = jnp.minimum(jnp.sum(pad_ends[None, :] <= blk_start[:, None], axis=1), N_EXPERTS - 1)
    n_used = (pad_ends[-1] // MOE_BLK).reshape(1)
    flat = lambda a: a.reshape(-1).astype(i32)
    return flat(base), flat(seg), flat(lstart), blk_expert.astype(i32), n_used.astype(i32)


def _rope_tables(tb):
    rows = SEQ // GRID_W
    row = jnp.repeat(jnp.arange(rows, dtype=f32), GRID_W)
    col = jnp.tile(jnp.arange(GRID_W, dtype=f32), rows)
    inv_freq = ROPE_BASE ** (-jnp.arange(ROPE_FREQS, dtype=f32) / ROPE_FREQS)
    ang_r = row[:, None] * inv_freq[None, :]
    ang_c = col[:, None] * inv_freq[None, :]
    ang = jnp.concatenate([ang_r, ang_r, ang_c, ang_c], axis=-1)
    cos = jnp.concatenate([jnp.ones((CTX, HEAD_DIM), f32), jnp.cos(ang)], axis=0)
    sin = jnp.concatenate([jnp.zeros((CTX, HEAD_DIM), f32), jnp.sin(ang)], axis=0)
    return jnp.tile(cos, (1, LANES // HEAD_DIM)), jnp.tile(sin, (1, LANES // HEAD_DIM))


def _block_diag2(w):
    z = jnp.zeros_like(w[0])
    return jnp.concatenate([jnp.concatenate([w[0], z], axis=1), jnp.concatenate([z, w[1]], axis=1)], axis=0)


def kernel(x, c, ctx, c_ctx, w_mod, b_mod, norm_mix_g, norm_ffn_g, w_in, w_out, att_sink, shift_mu_prev, shift_mu_next, decay_w0, decay_w2, iclr_a0, iclr_a2, vres_v0, vres_v1, vres_v2, gate_g2, k_k, k_a, r_k, ln_x_w, ln_x_b, router_w, router_b, expert_w_gate, expert_w_up, expert_w_down, final_norm_g):
    nbatch = x.shape[0]
    depth = w_mod.shape[0]
    tb = ctx.shape[1] + x.shape[1]
    xa = jnp.concatenate([ctx, x], axis=1)
    nb_pad = -(-(nbatch + 1) // 8) * 8
    cond = jnp.zeros((nb_pad, D), f32).at[:nbatch].set(c).at[nbatch].set(c_ctx)
    mod_all = _mod_call(cond, w_mod, b_mod).reshape(depth, nb_pad, 1, 6 * D)
    cos, sin = _rope_tables(tb)
    wr_hi = router_w.T.astype(bf16)
    wr_lo = (router_w.T - wr_hi.astype(f32)).astype(bf16)
    w_router = jnp.stack([wr_hi, wr_lo])
    b_router = router_b.reshape(N_EXPERTS, 1)
    v_first = None
    for l in range(depth):
        mod = mod_all[l]
        q, k, v, rw = _in_proj_call(xa, mod, norm_mix_g[l].reshape(1, D), w_in[l].astype(bf16), cos, sin)
        att = _attn_call(att_sink[l], q, k, v)
        mu = jnp.stack([shift_mu_prev[l], shift_mu_next[l]])
        v0 = vres_v0[l - 1] if l > 0 else jnp.zeros((RW,), f32)
        vec = jnp.stack([k_k[l], k_a[l], r_k[l].reshape(RW), v0,
                         decay_w0[l, 0], decay_w0[l, 1], iclr_a0[l, 0], iclr_a0[l, 1]])
        if l > 0:
            v1 = jnp.zeros((RW, LANES), f32).at[:, :LORA_VRES].set(vres_v1[l - 1]).astype(bf16)
            v2 = jnp.zeros((LANES, RW), f32).at[:LORA_VRES].set(vres_v2[l - 1]).astype(bf16)
        else:
            v1 = v2 = None
        r_, v_, kk, bv, g, kd, lw, bd = _feat_call(
            rw, v_first, mu, vec, _block_diag2(decay_w2[l]).astype(bf16), _block_diag2(iclr_a2[l]).astype(bf16),
            gate_g2[l].astype(bf16), v1, v2)
        if l == 0:
            v_first = v_
        yf, yb = _scan_call(r_, v_, kk, kd, lw, bd)
        ln = jnp.stack([ln_x_w[l], ln_x_b[l]])
        xa, x_local, lpos, gates, cnt = _mix_call(xa, att, yf, yb, bv, g, mod, ln, w_out[l].astype(bf16),
                                                  norm_ffn_g[l].reshape(1, D), w_router, b_router)
        ntok = nbatch * tb
        nrows = _moe_rows(ntok)
        cnt = cnt[:, :, 0].astype(i32).reshape(ntok // TM, N_EXPERTS)
        base, seg, lstart, blk_expert, n_used = _segment_plan(cnt, nrows // MOE_BLK)
        x_sorted = _dispatch_call(base, seg, lstart, x_local.reshape(-1, D // 2), nrows)
        y_sorted = _ffn_call(blk_expert, n_used, x_sorted, expert_w_gate, expert_w_up, expert_w_down, l)
        xa = _combine_call(base, seg, lstart, xa, lpos, gates, mod, y_sorted,
                           final_norm_g.reshape(1, D) if l == depth - 1 else None)
    return xa
```

```python
import functools
import math

import jax
import jax.numpy as jnp
from jax import lax
from jax.experimental import pallas as pl
from jax.experimental.pallas import tpu as pltpu

f32 = jnp.float32
bf16 = jnp.bfloat16
i32 = jnp.int32
u32 = jnp.uint32

D = 1024
SEQ = 4096
CTX = 256
TB = CTX + SEQ
GRID_W = 64
HEAD_DIM = 64
ATT_WIDTH = 512
ATT_HEADS = 8
KV_HEADS = 2
ATT_GROUP = ATT_HEADS // KV_HEADS
KV_WIDTH = KV_HEADS * HEAD_DIM
RW = 512
RWKV_HEADS = 8
LORA_DECAY = 64
LORA_ICLR = 64
LORA_VRES = 32
LORA_GATE = 128
RWKV_COLS = 3 * RW + 2 * (LORA_DECAY + LORA_ICLR) + LORA_GATE
ATT_COLS = ATT_WIDTH + 2 * KV_WIDTH
IN_COLS = ATT_COLS + RWKV_COLS
N_EXPERTS = 16
N_GROUPS = 4
EXPERTS_PER_GROUP = 4
TOP_K = 2
MOE_BLK = 256
NORM_EPS = 1e-6
GN_EPS = 64e-5
NEG_INF = -1e30
ATT_SCALE = HEAD_DIM ** -0.5
ROPE_BASE = 10000.0
ROPE_FREQS = HEAD_DIM // 4

LANES = 128
TM = 256
QB = 128
CH = 64
HG = 4
GW = HG * HEAD_DIM
SCAN_BATCH = 4
SEG_ALIGN = 8
LROWS = -(-(TOP_K * TM + N_EXPERTS * SEG_ALIGN) // LANES) * LANES
VMEM_LIMIT = 48 * 1024 * 1024


def _cparams(sem):
    return pltpu.CompilerParams(dimension_semantics=sem, vmem_limit_bytes=VMEM_LIMIT)


def _sigmoid(x):
    return 0.5 * jnp.tanh(0.5 * x) + 0.5


def _div_pow2(x, n):
    assert n & (n - 1) == 0
    return lax.shift_right_logical(x, n.bit_length() - 1)


def _mod_pow2(x, n):
    assert n & (n - 1) == 0
    return lax.bitwise_and(x, n - 1)


def _round_up_pow2(x, n):
    assert n & (n - 1) == 0
    return lax.bitwise_and(x + (n - 1), ~(n - 1))


def _dot(a, b):
    return jnp.dot(a, b, preferred_element_type=f32)


def _dot_nt(a, b):
    return lax.dot_general(a, b, (((1,), (1,)), ((), ())), preferred_element_type=f32)


def _dot_tn(a, b):
    return lax.dot_general(a, b, (((0,), (0,)), ((), ())), preferred_element_type=f32)


def _split2(x):
    hi = x.astype(bf16)
    lo = (x - hi.astype(f32)).astype(bf16)
    return hi, lo


def _dot_split_lhs(x, m):
    hi, lo = _split2(x)
    return _dot(hi, m) + _dot(lo, m)


def _rmsnorm(x, g):
    ms = jnp.mean(x * x, axis=-1, keepdims=True)
    return x * lax.rsqrt(ms + NORM_EPS) * g


def _head_ones():
    r = _div_pow2(lax.broadcasted_iota(i32, (RW, RW), 0), HEAD_DIM)
    c = _div_pow2(lax.broadcasted_iota(i32, (RW, RW), 1), HEAD_DIM)
    return (r == c).astype(bf16)


def _mod_kernel(c_ref, w_ref, b_ref, o_ref):
    c = c_ref[...]
    s = (c * _sigmoid(c)).astype(bf16)
    o_ref[0] = _dot(s, w_ref[0].astype(bf16)) + b_ref[0]


def _mod_call(cond, w_mod, b_mod):
    nb = cond.shape[0]
    depth = w_mod.shape[0]
    tn = 1024
    return pl.pallas_call(
        _mod_kernel,
        grid=(depth, 6 * D // tn),
        in_specs=[
            pl.BlockSpec((nb, D), lambda l, j: (0, 0)),
            pl.BlockSpec((1, D, tn), lambda l, j: (l, 0, j)),
            pl.BlockSpec((1, 1, tn), lambda l, j: (l, 0, j)),
        ],
        out_specs=pl.BlockSpec((1, nb, tn), lambda l, j: (l, 0, j)),
        out_shape=jax.ShapeDtypeStruct((depth, nb, 6 * D), f32),
        compiler_params=_cparams(("arbitrary", "arbitrary")),
        name="mod",
    )(cond, w_mod, b_mod.reshape(depth, 1, 6 * D))


def _mod_index(nbatch, ctx_tiles):
    return lambda b, i: (jnp.where(i < ctx_tiles, nbatch, b), 0, 0)


def _in_proj_kernel(x_ref, mod_ref, g_ref, w_ref, cos_ref, sin_ref, q_ref, k_ref, v_ref, rw_ref):
    x = x_ref[0]
    tm = x.shape[0]
    h = _rmsnorm(x, g_ref[...])
    sh = mod_ref[0, :, 0:D]
    sc = mod_ref[0, :, D:2 * D]
    h = (h * (1.0 + sc) + sh).astype(bf16)
    p = _dot(h, w_ref[...])
    cos = cos_ref[...]
    sin = sin_ref[...]
    lane = lax.broadcasted_iota(i32, (tm, LANES), 1)
    first_half = _mod_pow2(lane, 2 * ROPE_FREQS) < ROPE_FREQS

    def rope(t):
        rot = jnp.where(first_half, -pltpu.roll(t, LANES - ROPE_FREQS, 1), pltpu.roll(t, ROPE_FREQS, 1))
        return t * cos + rot * sin

    for j in range(ATT_WIDTH // LANES):
        t = (rope(p[:, j * LANES:(j + 1) * LANES]) * ATT_SCALE).astype(bf16)
        q_ref[0, 2 * j] = t[:, :HEAD_DIM]
        q_ref[0, 2 * j + 1] = t[:, HEAD_DIM:]
    kt = rope(p[:, ATT_WIDTH:ATT_WIDTH + KV_WIDTH]).astype(bf16)
    vt = p[:, ATT_WIDTH + KV_WIDTH:ATT_COLS].astype(bf16)
    for hh in range(KV_HEADS):
        k_ref[0, hh] = kt[:, hh * HEAD_DIM:(hh + 1) * HEAD_DIM]
        v_ref[0, hh] = vt[:, hh * HEAD_DIM:(hh + 1) * HEAD_DIM]
    rw_ref[0] = p[:, ATT_COLS:]


def _in_proj_call(x, mod, g, w_in, cos, sin):
    nbatch, tb, _ = x.shape
    nt = tb // TM
    return pl.pallas_call(
        _in_proj_kernel,
        grid=(nbatch, nt),
        in_specs=[
            pl.BlockSpec((1, TM, D), lambda b, i: (b, i, 0)),
            pl.BlockSpec((1, 1, 6 * D), _mod_index(nbatch, CTX // TM)),
            pl.BlockSpec((1, D), lambda b, i: (0, 0)),
            pl.BlockSpec((D, IN_COLS), lambda b, i: (0, 0)),
            pl.BlockSpec((TM, LANES), lambda b, i: (i, 0)),
            pl.BlockSpec((TM, LANES), lambda b, i: (i, 0)),
        ],
        out_specs=[
            pl.BlockSpec((1, ATT_HEADS, TM, HEAD_DIM), lambda b, i: (b, 0, i, 0)),
            pl.BlockSpec((1, KV_HEADS, TM, HEAD_DIM), lambda b, i: (b, 0, i, 0)),
            pl.BlockSpec((1, KV_HEADS, TM, HEAD_DIM), lambda b, i: (b, 0, i, 0)),
            pl.BlockSpec((1, TM, RWKV_COLS), lambda b, i: (b, i, 0)),
        ],
        out_shape=[
            jax.ShapeDtypeStruct((nbatch, ATT_HEADS, tb, HEAD_DIM), bf16),
            jax.ShapeDtypeStruct((nbatch, KV_HEADS, tb, HEAD_DIM), bf16),
            jax.ShapeDtypeStruct((nbatch, KV_HEADS, tb, HEAD_DIM), bf16),
            jax.ShapeDtypeStruct((nbatch, tb, RWKV_COLS), f32),
        ],
        compiler_params=_cparams(("parallel", "arbitrary")),
        name="in_proj",
    )(x, mod, g, w_in, cos, sin)


def _attn_kernel(sink_ref, q_ref, kp_ref, kc_ref, kn_ref, vp_ref, vc_ref, vn_ref, kx_ref, vx_ref, o_ref,
                 *, nblk, ctx_blks):
    i = pl.program_id(1)
    is_lat = i >= ctx_blks
    prev_ok = jnp.logical_and(is_lat, i - 1 >= ctx_blks)
    next_ok = jnp.logical_and(is_lat, i + 1 <= nblk - 1)
    rows = ATT_GROUP * QB
    qi = _mod_pow2(lax.broadcasted_iota(i32, (rows, QB), 0), QB)
    kj = lax.broadcasted_iota(i32, (rows, QB), 1)
    mask_p = jnp.logical_and(kj >= qi, prev_ok)
    mask_n = jnp.logical_and(kj <= qi, next_ok)
    row_head = _div_pow2(lax.broadcasted_iota(i32, (rows, 1), 0), QB)
    heads = range(KV_HEADS)
    qh = [q_ref[0, ATT_GROUP * h:ATT_GROUP * (h + 1)].reshape(rows, HEAD_DIM) for h in heads]
    s_p = [jnp.where(mask_p, _dot_nt(qh[h], kp_ref[0, h]), NEG_INF) for h in heads]
    s_c = [jnp.where(is_lat, _dot_nt(qh[h], kc_ref[0, h]), NEG_INF) for h in heads]
    s_n = [jnp.where(mask_n, _dot_nt(qh[h], kn_ref[0, h]), NEG_INF) for h in heads]
    s_x = [_dot_nt(qh[h], kx_ref[0, h]) for h in heads]
    sink = []
    for h in heads:
        sk = jnp.zeros((rows, 1), f32)
        for g in range(ATT_GROUP):
            sk = jnp.where(row_head == g, sink_ref[ATT_GROUP * h + g], sk)
        sink.append(sk)
    m = [jnp.maximum(jnp.max(jnp.maximum(jnp.maximum(s_p[h], s_c[h]),
                                         jnp.maximum(jnp.maximum(s_n[h], s_x[h][:, :QB]), s_x[h][:, QB:])),
                             axis=-1, keepdims=True), sink[h]) for h in heads]
    e_p = [jnp.exp(s_p[h] - m[h]) for h in heads]
    e_c = [jnp.exp(s_c[h] - m[h]) for h in heads]
    e_n = [jnp.exp(s_n[h] - m[h]) for h in heads]
    e_x = [jnp.exp(s_x[h] - m[h]) for h in heads]
    den = [jnp.sum((e_p[h] + e_c[h]) + (e_n[h] + e_x[h][:, :QB]) + e_x[h][:, QB:], axis=-1, keepdims=True)
           + jnp.exp(sink[h] - m[h]) for h in heads]
    o = [(_dot(e_p[h].astype(bf16), vp_ref[0, h]) + _dot(e_c[h].astype(bf16), vc_ref[0, h])
          + _dot(e_n[h].astype(bf16), vn_ref[0, h]) + _dot(e_x[h].astype(bf16), vx_ref[0, h])) / den[h]
         for h in heads]
    o_ref[0] = jnp.concatenate([o[h][g * QB:(g + 1) * QB] for h in heads for g in range(ATT_GROUP)],
                               axis=1).astype(bf16)


def _attn_call(sink, q, k, v):
    nbatch, _, tb, _ = q.shape
    nblk = tb // QB
    ctx_blks = CTX // QB
    assert CTX == 2 * QB
    kv_blk = (1, KV_HEADS, QB, HEAD_DIM)
    prev_map = lambda b, i: (b, 0, jnp.maximum(i - 1, 0), 0)
    cur_map = lambda b, i: (b, 0, i, 0)
    next_map = lambda b, i: (b, 0, jnp.minimum(i + 1, nblk - 1), 0)
    ctx_spec = pl.BlockSpec((1, KV_HEADS, CTX, HEAD_DIM), lambda b, i: (b, 0, 0, 0))
    return pl.pallas_call(
        functools.partial(_attn_kernel, nblk=nblk, ctx_blks=ctx_blks),
        grid=(nbatch, nblk),
        in_specs=[
            pl.BlockSpec(memory_space=pltpu.SMEM),
            pl.BlockSpec((1, ATT_HEADS, QB, HEAD_DIM), cur_map),
            pl.BlockSpec(kv_blk, prev_map), pl.BlockSpec(kv_blk, cur_map), pl.BlockSpec(kv_blk, next_map),
            pl.BlockSpec(kv_blk, prev_map), pl.BlockSpec(kv_blk, cur_map), pl.BlockSpec(kv_blk, next_map),
            ctx_spec, ctx_spec,
        ],
        out_specs=pl.BlockSpec((1, QB, ATT_WIDTH), lambda b, i: (b, i, 0)),
        out_shape=jax.ShapeDtypeStruct((nbatch, tb, ATT_WIDTH), bf16),
        compiler_params=_cparams(("parallel", "arbitrary")),
        name="attention",
    )(sink, q, k, k, k, v, v, v, k, v)


def _feat_kernel(*refs, nt, ctx_tiles, has_vres):
    if has_vres:
        (rw_ref, hp_ref, hn_ref, vf_ref, mu_ref, vec_ref, w2_ref, a2_ref, g2_ref, v1_ref, v2_ref,
         r_ref, v_ref, kk_ref, bv_ref, g_ref, kd_ref, lw_ref, bd_ref) = refs
    else:
        (rw_ref, hp_ref, hn_ref, mu_ref, vec_ref, w2_ref, a2_ref, g2_ref,
         r_ref, v_ref, kk_ref, bv_ref, g_ref, kd_ref, lw_ref, bd_ref) = refs
    i = pl.program_id(1)
    u0 = rw_ref[0]
    tm = u0.shape[0]
    prev_zero = jnp.logical_or(i == 0, i == ctx_tiles)
    next_zero = jnp.logical_or(i == ctx_tiles - 1, i == nt - 1)
    halo_p = jnp.where(prev_zero, 0.0, hp_ref[0, 7:8, :])
    halo_n = jnp.where(next_zero, 0.0, hn_ref[0, 0:1, :])
    row = lax.broadcasted_iota(i32, (tm, 1), 0)
    prev = jnp.where(row == 0, halo_p, pltpu.roll(u0, 1, 0))
    nxt = jnp.where(row == tm - 1, halo_n, pltpu.roll(u0, tm - 1, 0))
    mu_p = mu_ref[0:1, :]
    mu_n = mu_ref[1:2, :]
    u = u0 + mu_p * (prev - u0) + mu_n * (nxt - u0)

    r = u[:, 0:RW]
    k = u[:, RW:2 * RW]
    v = u[:, 2 * RW:3 * RW]
    wd = u[:, 3 * RW:3 * RW + 2 * LORA_DECAY]
    ad = u[:, 3 * RW + 2 * LORA_DECAY:3 * RW + 2 * (LORA_DECAY + LORA_ICLR)]
    gd = u[:, 3 * RW + 2 * (LORA_DECAY + LORA_ICLR):]
    k_k = vec_ref[0:1, :]
    k_a = vec_ref[1:2, :]
    r_k = vec_ref[2:3, :]
    ones = _head_ones()

    if has_vres:
        lo = _dot(v.astype(bf16), v1_ref[...])
        gate = _sigmoid(vec_ref[3:4, :] + _dot(lo.astype(bf16), v2_ref[...]))
        v = v + (vf_ref[0] - v) * gate
    decay_in = _dot(jnp.tanh(wd).astype(bf16), w2_ref[...])
    a_in = _dot(ad.astype(bf16), a2_ref[...])
    kk = k * k_k
    n2 = _dot_split_lhs(kk * kk, ones)
    kk = kk * lax.rsqrt(jnp.maximum(n2, 1e-24))
    g = _dot(_sigmoid(gd).astype(bf16), g2_ref[...])
    ksum = jnp.zeros_like(k)
    for d in range(2):
        w0 = vec_ref[4 + d:5 + d, :]
        a0 = vec_ref[6 + d:7 + d, :]
        lw = -_sigmoid(w0 + decay_in[:, d * RW:(d + 1) * RW]) * math.exp(-0.5)
        a = _sigmoid(a0 + a_in[:, d * RW:(d + 1) * RW])
        kd = k * (1.0 + (a - 1.0) * k_a)
        ksum = ksum + kd
        kd_ref[d, 0] = kd
        lw_ref[d, 0] = lw
        bd_ref[d, 0] = kk * a
    bonus = _dot_split_lhs(r * ksum * r_k, ones)
    r_ref[0] = r
    v_ref[0] = v
    kk_ref[0] = kk
    bv_ref[0] = bonus * v
    g_ref[0] = g


def _feat_call(rw, v_first, mu, vec, w2bd, a2bd, g2, v1, v2):
    nbatch, tb, _ = rw.shape
    nt = tb // TM
    has_vres = v_first is not None
    sub = TM // 8
    tile = lambda w: pl.BlockSpec((1, TM, w), lambda b, i: (b, i, 0))
    full = lambda a: pl.BlockSpec(a.shape, lambda b, i: (0,) * a.ndim)
    in_specs = [
        tile(RWKV_COLS),
        pl.BlockSpec((1, 8, RWKV_COLS), lambda b, i: (b, jnp.maximum(i * sub - 1, 0), 0)),
        pl.BlockSpec((1, 8, RWKV_COLS), lambda b, i: (b, jnp.minimum((i + 1) * sub, tb // 8 - 1), 0)),
    ]
    args = [rw, rw, rw]
    if has_vres:
        in_specs.append(tile(RW))
        args.append(v_first)
    consts = [mu, vec, w2bd, a2bd, g2] + ([v1, v2] if has_vres else [])
    in_specs += [full(a) for a in consts]
    args += consts
    dir_spec = pl.BlockSpec((2, 1, TM, RW), lambda b, i: (0, b, i, 0))
    tok = jax.ShapeDtypeStruct((nbatch, tb, RW), f32)
    dtok = jax.ShapeDtypeStruct((2, nbatch, tb, RW), f32)
    return pl.pallas_call(
        functools.partial(_feat_kernel, nt=nt, ctx_tiles=CTX // TM, has_vres=has_vres),
        grid=(nbatch, nt),
        in_specs=in_specs,
        out_specs=[tile(RW)] * 5 + [dir_spec] * 3,
        out_shape=[tok] * 5 + [dtok] * 3,
        compiler_params=_cparams(("parallel", "arbitrary")),
        name="rwkv_features",
    )(*args)


def _block_diag_rows(x, width):
    cb = _div_pow2(lax.broadcasted_iota(i32, x.shape, 1), width)
    return jnp.concatenate([jnp.where(cb == h, x, jnp.zeros_like(x)) for h in range(HG)], axis=0)


def _scan_chunks(probs):
    wide = HG * CH
    row_t = lax.broadcasted_iota(i32, (CH, wide), 0)
    col_t = _mod_pow2(lax.broadcasted_iota(i32, (CH, wide), 1), CH)
    row_g = lax.broadcasted_iota(i32, (CH, GW), 0)
    incl_t = {False: col_t <= row_t, True: col_t >= row_t}
    strict_t = {False: col_t < row_t, True: col_t > row_t}
    eye_t = (row_t == col_t).astype(f32)
    rb = _div_pow2(lax.broadcasted_iota(i32, (GW, GW), 0), HEAD_DIM)
    cb = _div_pow2(lax.broadcasted_iota(i32, (GW, GW), 1), HEAD_DIM)
    n = len(probs)
    rev = [p[7] for p in probs]
    def cumsum_rows(x, reverse):
        s = 1
        while s < CH:
            if reverse:
                x = x + jnp.where(row_g < CH - s, pltpu.roll(x, CH - s, 0), 0.0)
            else:
                x = x + jnp.where(row_g >= s, pltpu.roll(x, s, 0), 0.0)
            s *= 2
        return x

    gam = [cumsum_rows(probs[i][5], rev[i]) for i in range(n)]
    ar, bk, k_t, b_t = [], [], [], []
    for i, (s_prev, r, v, kk, k, lw, b, _) in enumerate(probs):
        e_neg = jnp.exp(-gam[i])
        a_s = (-kk * jnp.exp(gam[i] - lw)).astype(bf16)
        r_s = (r * jnp.exp(gam[i])).astype(bf16)
        b_t.append(b * e_neg)
        k_t.append(k * e_neg)
        ar.append(jnp.concatenate([a_s, r_s], axis=0))
        bk.append(jnp.concatenate([_block_diag_rows(b_t[i].astype(bf16), HEAD_DIM),
                                   _block_diag_rows(k_t[i].astype(bf16), HEAD_DIM)], axis=0))
    gram = [_dot_nt(ar[i], bk[i]) for i in range(n)]
    ars = [_dot_nt(ar[i], probs[i][0].astype(bf16)) for i in range(n)]
    v_bd = [_block_diag_rows(probs[i][2].astype(bf16), HEAD_DIM) for i in range(n)]
    p0 = [jnp.where(strict_t[rev[i]], gram[i][:CH, :wide], 0.0).astype(bf16) for i in range(n)]
    lq = [jnp.concatenate([jnp.where(strict_t[rev[i]], gram[i][:CH, wide:], 0.0),
                           jnp.where(incl_t[rev[i]], gram[i][CH:, wide:], 0.0)], axis=0).astype(bf16)
          for i in range(n)]
    lqv = [_dot(lq[i], v_bd[i]) for i in range(n)]
    rhs = [ars[i][:CH] + lqv[i][:CH] for i in range(n)]
    t = [eye_t + p0[i].astype(f32) for i in range(n)]
    p = [_dot(p0[i], _block_diag_rows(p0[i], CH)).astype(bf16) for i in range(n)]
    m = 4
    while m < CH:
        tp = [_dot(jnp.concatenate([t[i].astype(bf16), p[i]], axis=0), _block_diag_rows(p[i], CH))
              for i in range(n)]
        t = [t[i] + tp[i][:CH] for i in range(n)]
        p = [tp[i][CH:].astype(bf16) for i in range(n)]
        m *= 2
    t = [t[i] + _dot(t[i].astype(bf16), _block_diag_rows(p[i], CH)) for i in range(n)]
    u = [_dot(t[i].astype(bf16), _block_diag_rows(rhs[i].astype(bf16), HEAD_DIM)) for i in range(n)]
    out = []
    for i in range(n):
        s_prev, v, lw = probs[i][0], probs[i][2], probs[i][5]
        q_b = jnp.where(incl_t[rev[i]], gram[i][CH:, :wide], 0.0).astype(bf16)
        u_bd = _block_diag_rows(u[i].astype(bf16), HEAD_DIM)
        y = ars[i][CH:] + lqv[i][CH:] + _dot(q_b, u_bd)
        eg = jnp.exp(jnp.sum(lw, axis=0, keepdims=True))
        vu = jnp.concatenate([v, u[i]], axis=0).astype(bf16)
        kb = jnp.concatenate([k_t[i] * eg, b_t[i] * eg], axis=0).astype(bf16)
        s_add = _dot_tn(vu, kb)
        out.append((s_prev * eg + jnp.where(rb == cb, s_add, 0.0), y))
    return out


def _scan_kernel(rf_ref, vf_ref, kkf_ref, kf_ref, lwf_ref, bf_ref,
                 rb_ref, vb_ref, kkb_ref, kb_ref, lwb_ref, bb_ref,
                 yf_ref, yb_ref, s_ref):
    @pl.when(pl.program_id(1) == 0)
    def _():
        s_ref[...] = jnp.zeros_like(s_ref)

    dirs = ((rf_ref, vf_ref, kkf_ref, kf_ref, lwf_ref, bf_ref, yf_ref),
            (rb_ref, vb_ref, kkb_ref, kb_ref, lwb_ref, bb_ref, yb_ref))
    probs, dest = [], []
    for bi in range(SCAN_BATCH):
        for d, (r_ref, v_ref, kk_ref, k_ref, lw_ref, b_ref, y_ref) in enumerate(dirs):
            for g in range(RW // GW):
                sl = slice(g * GW, (g + 1) * GW)
                probs.append((s_ref[bi, d, g], r_ref[bi, :, sl], v_ref[bi, :, sl], kk_ref[bi, :, sl],
                              k_ref[0, bi, :, sl], lw_ref[0, bi, :, sl], b_ref[0, bi, :, sl], d == 1))
                dest.append((bi, d, g, y_ref, sl))
    for (bi, d, g, y_ref, sl), (s_new, y) in zip(dest, _scan_chunks(probs)):
        s_ref[bi, d, g] = s_new
        y_ref[bi, :, sl] = y


def _scan_call(r, v, kk, kd, lw, bd):
    nbatch, tb, _ = r.shape
    nc = tb // CH
    cc = CTX // CH
    sb = SCAN_BATCH
    rev = lambda j: jnp.where(j < cc, cc - 1 - j, nc - 1 + cc - j)
    tok_f = pl.BlockSpec((sb, CH, RW), lambda b, j: (b, j, 0))
    tok_b = pl.BlockSpec((sb, CH, RW), lambda b, j: (b, rev(j), 0))
    dir_f = pl.BlockSpec((1, sb, CH, RW), lambda b, j: (0, b, j, 0))
    dir_b = pl.BlockSpec((1, sb, CH, RW), lambda b, j: (1, b, rev(j), 0))
    out = jax.ShapeDtypeStruct((nbatch, tb, RW), f32)
    return pl.pallas_call(
        _scan_kernel,
        grid=(nbatch // sb, nc),
        in_specs=[tok_f, tok_f, tok_f, dir_f, dir_f, dir_f, tok_b, tok_b, tok_b, dir_b, dir_b, dir_b],
        out_specs=[tok_f, tok_b],
        out_shape=[out, out],
        scratch_shapes=[pltpu.VMEM((sb, 2, RW // GW, GW, GW), f32)],
        compiler_params=_cparams(("parallel", "arbitrary")),
        name="rwkv_scan",
    )(r, v, kk, kd, lw, bd, r, v, kk, kd, lw, bd)


def _pack_bf16_pairs(x):
    bits = pltpu.bitcast(x, u32)
    half = x.shape[1] // 2
    return bits[:, :half] | lax.shift_right_logical(bits[:, half:], jnp.uint32(16))


def _unpack_bf16_pairs(p):
    hi = pltpu.bitcast(p & jnp.uint32(0xFFFF0000), f32)
    lo = pltpu.bitcast(lax.shift_left(p, jnp.uint32(16)), f32)
    return jnp.concatenate([hi, lo], axis=1).astype(bf16)


def _mix_kernel(x_ref, att_ref, yf_ref, yb_ref, bv_ref, g_ref, mod_ref, ln_ref, wo_ref, gf_ref, wr_ref, rb_ref,
                xo_ref, xl_ref, lp_ref, gt_ref, cnt_ref):
    ones = _head_ones()
    y = yf_ref[0] + yb_ref[0]
    inv = 1.0 / HEAD_DIM
    mu = _dot(y.astype(bf16), ones) * inv
    dlt = y - mu
    var = _dot((dlt * dlt).astype(bf16), ones) * inv
    gn = dlt * lax.rsqrt(var + GN_EPS) * ln_ref[0:1, :] + ln_ref[1:2, :]
    rwk = ((gn + bv_ref[0]) * g_ref[0]).astype(bf16)
    mix = _dot(att_ref[0], wo_ref[0:ATT_WIDTH, :]) + _dot(rwk, wo_ref[ATT_WIDTH:, :])
    gate_a = mod_ref[0, :, 2 * D:3 * D]
    xn = x_ref[0] + gate_a * mix
    xo_ref[0] = xn
    sh = mod_ref[0, :, 3 * D:4 * D]
    sc = mod_ref[0, :, 4 * D:5 * D]
    hf = _rmsnorm(xn, gf_ref[...]) * (1.0 + sc) + sh
    h_hi, h_lo = _split2(hf)
    logits = _dot_nt(wr_ref[0], h_hi) + _dot_nt(wr_ref[0], h_lo) + _dot_nt(wr_ref[1], h_hi)
    _route_and_sort(hf, _sigmoid(logits), rb_ref[...], xl_ref, lp_ref, gt_ref, cnt_ref)


def _route_and_sort(hf, scores, bias, xl_ref, lp_ref, gt_ref, cnt_ref):
    tm = hf.shape[0]
    e0, e1, g0, g1 = _route_rows(scores, bias)
    gt_ref[0] = jnp.concatenate([g0, g1], axis=0)
    eio = lax.broadcasted_iota(i32, (N_EXPERTS, tm), 0)
    oh0 = (eio == e0).astype(f32)
    oh1 = (eio == e1).astype(f32)
    both = oh0 + oh1
    upper = (lax.broadcasted_iota(i32, (tm, tm), 0) < lax.broadcasted_iota(i32, (tm, tm), 1)).astype(bf16)
    before = _dot(both.astype(bf16), upper)
    cnt = jnp.sum(both, axis=1, keepdims=True)
    seg = _round_up_pow2(cnt.astype(i32), SEG_ALIGN).astype(f32)
    lower = (lax.broadcasted_iota(i32, (N_EXPERTS, N_EXPERTS), 0)
             > lax.broadcasted_iota(i32, (N_EXPERTS, N_EXPERTS), 1)).astype(bf16)
    seg_start = _dot(lower, jnp.broadcast_to(seg, (N_EXPERTS, tm)).astype(bf16))
    pos = seg_start + before
    lp0 = jnp.sum(oh0 * pos, axis=0, keepdims=True).astype(i32)
    lp1 = jnp.sum(oh1 * pos, axis=0, keepdims=True).astype(i32)
    lp_ref[0] = jnp.concatenate([lp0, lp1], axis=0)
    cnt_ref[0] = jnp.broadcast_to(cnt, (N_EXPERTS, LANES))
    jj = lax.broadcasted_iota(i32, (LROWS, tm), 0)
    sel_t = jnp.logical_or(jj == lp0, jj == lp1).astype(bf16)
    xl_ref[0] = _pack_bf16_pairs(_dot(sel_t, hf.astype(bf16)))


def _route_rows(scores, bias):
    biased = scores + bias
    row = lambda a, e: a[e:e + 1, :]
    best = None
    for gi in range(N_GROUPS):
        m = [row(biased, gi * EXPERTS_PER_GROUP + j) for j in range(EXPERTS_PER_GROUP)]
        gs = None
        for a in range(EXPERTS_PER_GROUP):
            for b in range(a + 1, EXPERTS_PER_GROUP):
                pair = m[a] + m[b]
                gs = pair if gs is None else jnp.maximum(gs, pair)
        if best is None:
            best, g_idx = gs, jnp.zeros(gs.shape, i32)
        else:
            better = gs > best
            g_idx = jnp.where(better, gi, g_idx)
            best = jnp.where(better, gs, best)

    def pick(a, j):
        out = row(a, j)
        for gi in range(1, N_GROUPS):
            out = jnp.where(g_idx == gi, row(a, gi * EXPERTS_PER_GROUP + j), out)
        return out

    vb = [pick(biased, j) for j in range(EXPERTS_PER_GROUP)]
    vs = [pick(scores, j) for j in range(EXPERTS_PER_GROUP)]

    def argmax_first(vals):
        bv, bi = vals[0], jnp.zeros(vals[0].shape, i32)
        for j in range(1, len(vals)):
            better = vals[j] > bv
            bi = jnp.where(better, j, bi)
            bv = jnp.where(better, vals[j], bv)
        return bi

    i1 = argmax_first(vb)
    i2 = argmax_first([jnp.where(i1 == j, -jnp.inf, vb[j]) for j in range(EXPERTS_PER_GROUP)])
    sel = lambda idx: sum(jnp.where(idx == j, vs[j], 0.0) for j in range(EXPERTS_PER_GROUP))
    s1, s2 = sel(i1), sel(i2)
    tot = s1 + s2
    base = g_idx * EXPERTS_PER_GROUP
    return base + i1, base + i2, s1 / tot, s2 / tot


def _mix_call(x, att, yf, yb, bv, g, mod, ln, w_out, g_ffn, w_router, b_router):
    nbatch, tb, _ = x.shape
    nt = tb // TM
    tile = lambda w: pl.BlockSpec((1, TM, w), lambda b, i: (b, i, 0))
    full = lambda a: pl.BlockSpec(a.shape, lambda b, i: (0,) * a.ndim)
    route = pl.BlockSpec((1, TOP_K, TM), lambda b, i: (b, 0, i))
    return pl.pallas_call(
        _mix_kernel,
        grid=(nbatch, nt),
        in_specs=[tile(D), tile(ATT_WIDTH), tile(RW), tile(RW), tile(RW), tile(RW),
                  pl.BlockSpec((1, 1, 6 * D), _mod_index(nbatch, CTX // TM)),
                  full(ln), full(w_out), full(g_ffn), full(w_router), full(b_router)],
        out_specs=[tile(D), pl.BlockSpec((1, LROWS, D // 2), lambda b, i: (b, i, 0)), route, route,
                   pl.BlockSpec((1, N_EXPERTS, LANES), lambda b, i: (b, i, 0))],
        out_shape=[jax.ShapeDtypeStruct((nbatch, tb, D), f32),
                   jax.ShapeDtypeStruct((nbatch, nt * LROWS, D // 2), u32),
                   jax.ShapeDtypeStruct((nbatch, TOP_K, tb), i32),
                   jax.ShapeDtypeStruct((nbatch, TOP_K, tb), f32),
                   jax.ShapeDtypeStruct((nbatch, nt * N_EXPERTS, LANES), f32)],
        compiler_params=_cparams(("parallel", "arbitrary")),
        name="mix_out",
    )(x, att, yf, yb, bv, g, mod, ln, w_out, g_ffn, w_router, b_router)


def _segment_copies(fn, tile, base_ref, seg_ref, ls_ref, src, dst, sem, src_is_global):
    for e in range(N_EXPERTS):
        idx = tile * N_EXPERTS + e
        seg = seg_ref[idx]
        g0 = base_ref[idx]
        l0 = ls_ref[idx]
        size = TM
        while size >= SEG_ALIGN:
            done = lax.bitwise_and(seg, ~(2 * size - 1))

            @pl.when(lax.bitwise_and(seg, size) != 0)
            def _():
                g_rows = pl.ds(pl.multiple_of(g0 + done, SEG_ALIGN), size)
                l_rows = pl.ds(pl.multiple_of(l0 + done, SEG_ALIGN), size)
                s_rows, d_rows = (g_rows, l_rows) if src_is_global else (l_rows, g_rows)
                fn(pltpu.make_async_copy(src.at[s_rows, :], dst.at[d_rows, :], sem))

            size //= 2


def _dispatch_kernel(base_ref, seg_ref, ls_ref, xl_ref, xs_in_ref, xs_ref, sem):
    del xs_in_ref
    tile = pl.program_id(0)
    args = (tile, base_ref, seg_ref, ls_ref, xl_ref, xs_ref, sem, False)
    _segment_copies(lambda cp: cp.start(), *args)
    _segment_copies(lambda cp: cp.wait(), *args)


def _dispatch_call(base, seg, lstart, x_local, nrows):
    ntiles = x_local.shape[0] // LROWS
    any_spec = pl.BlockSpec(memory_space=pl.ANY)
    return pl.pallas_call(
        _dispatch_kernel,
        grid_spec=pltpu.PrefetchScalarGridSpec(
            num_scalar_prefetch=3,
            grid=(ntiles,),
            in_specs=[pl.BlockSpec((LROWS, D // 2), lambda t, *_: (t, 0)), any_spec],
            out_specs=any_spec,
            scratch_shapes=[pltpu.SemaphoreType.DMA(())],
        ),
        out_shape=jax.ShapeDtypeStruct((nrows, D // 2), u32),
        input_output_aliases={4: 0},
        compiler_params=_cparams(("arbitrary",)),
        name="moe_dispatch",
    )(base, seg, lstart, x_local, jnp.zeros((nrows, D // 2), u32))


def _ffn_kernel(be_ref, nu_ref, x_ref, wg_ref, wu_ref, wd_ref, y_ref, wg_bf, wu_bf, wd_bf):
    i = pl.program_id(0)
    used = i < nu_ref[0]
    new_expert = jnp.logical_or(i == 0, be_ref[i] != be_ref[jnp.maximum(i - 1, 0)])

    @pl.when(jnp.logical_and(used, new_expert))
    def _():
        wg_bf[...] = wg_ref[0, 0].astype(bf16)
        wu_bf[...] = wu_ref[0, 0].astype(bf16)
        wd_bf[...] = wd_ref[0, 0].astype(bf16)

    @pl.when(used)
    def _():
        x = _unpack_bf16_pairs(x_ref[...])
        gt = _dot(x, wg_bf[...])
        up = _dot(x, wu_bf[...])
        hid = (gt * _sigmoid(gt) * up).astype(bf16)
        y = _dot(hid, wd_bf[...])
        y_ref[...] = _pack_bf16_pairs(y.astype(bf16).astype(f32))

    @pl.when(i >= nu_ref[0])
    def _():
        y_ref[...] = jnp.zeros_like(y_ref)


def _ffn_call(blk_expert, n_used, x_sorted, wg, wu, wd, layer):
    nrows = x_sorted.shape[0]
    nblk = nrows // MOE_BLK
    wspec = pl.BlockSpec((1, 1, D, D), lambda i, be, nu: (layer, be[i], 0, 0))
    return pl.pallas_call(
        _ffn_kernel,
        grid_spec=pltpu.PrefetchScalarGridSpec(
            num_scalar_prefetch=2,
            grid=(nblk,),
            in_specs=[pl.BlockSpec((MOE_BLK, D // 2), lambda i, be, nu: (i, 0)), wspec, wspec, wspec],
            out_specs=pl.BlockSpec((MOE_BLK, D // 2), lambda i, be, nu: (i, 0)),
            scratch_shapes=[pltpu.VMEM((D, D), bf16)] * 3,
        ),
        out_shape=jax.ShapeDtypeStruct((nrows, D // 2), u32),
        compiler_params=_cparams(("arbitrary",)),
        name="moe_ffn",
    )(blk_expert, n_used, x_sorted, wg, wu, wd)


def _combine_kernel(base_ref, seg_ref, ls_ref, x_ref, lp_ref, gt_ref, mod_ref, g_ref, y_ref, o_ref, ybuf, sem,
                    *, tiles_per_batch, first_tile, final):
    b = pl.program_id(0)
    i = pl.program_id(1)
    nt = pl.num_programs(1)
    step = b * nt + i
    slot = lax.rem(step, 2)
    tile = b * tiles_per_batch + i + first_tile
    next_tile = jnp.where(i + 1 < nt, tile + 1, (b + 1) * tiles_per_batch + first_tile)

    def copies(fn, which_tile, which_slot):
        _segment_copies(fn, which_tile, base_ref, seg_ref, ls_ref, y_ref, ybuf.at[which_slot],
                        sem.at[which_slot], True)

    @pl.when(step == 0)
    def _():
        ybuf[...] = jnp.zeros_like(ybuf)
        copies(lambda cp: cp.start(), tile, slot)

    @pl.when(step + 1 < pl.num_programs(0) * nt)
    def _():
        copies(lambda cp: cp.start(), next_tile, 1 - slot)

    copies(lambda cp: cp.wait(), tile, slot)
    y_loc = _unpack_bf16_pairs(ybuf[slot])
    tm = x_ref.shape[1]
    jj = lax.broadcasted_iota(i32, (LROWS, tm), 0)
    gmat = (jnp.where(jj == lp_ref[0, 0:1, :], gt_ref[0, 0:1, :], 0.0)
            + jnp.where(jj == lp_ref[0, 1:2, :], gt_ref[0, 1:2, :], 0.0))
    g_hi, g_lo = _split2(gmat)
    moe = _dot_tn(g_hi, y_loc) + _dot_tn(g_lo, y_loc)
    xn = x_ref[0] + mod_ref[0, :, 5 * D:6 * D] * moe
    o_ref[0] = _rmsnorm(xn, g_ref[...]) if final else xn


def _combine_call(base, seg, lstart, x, lpos, gates, mod, y_sorted, final_g):
    nbatch, tb, _ = x.shape
    ctx_tiles = CTX // TM
    tiles_per_batch = tb // TM
    final = final_g is not None
    first_tile = ctx_tiles if final else 0
    nt = tiles_per_batch - first_tile
    tile = pl.BlockSpec((1, TM, D), lambda b, i, *_: (b, i + first_tile, 0))
    route = pl.BlockSpec((1, TOP_K, TM), lambda b, i, *_: (b, 0, i + first_tile))
    mod_map = _mod_index(nbatch, ctx_tiles - first_tile)
    g_arr = final_g if final else jnp.ones((1, D), f32)
    return pl.pallas_call(
        functools.partial(_combine_kernel, tiles_per_batch=tiles_per_batch, first_tile=first_tile, final=final),
        grid_spec=pltpu.PrefetchScalarGridSpec(
            num_scalar_prefetch=3,
            grid=(nbatch, nt),
            in_specs=[tile, route, route,
                      pl.BlockSpec((1, 1, 6 * D), lambda b, i, *_: mod_map(b, i)),
                      pl.BlockSpec((1, D), lambda b, i, *_: (0, 0)),
                      pl.BlockSpec(memory_space=pl.ANY)],
            out_specs=pl.BlockSpec((1, TM, D), lambda b, i, *_: (b, i, 0)),
            scratch_shapes=[pltpu.VMEM((2, LROWS, D // 2), u32), pltpu.SemaphoreType.DMA((2,))],
        ),
        out_shape=jax.ShapeDtypeStruct((nbatch, nt * TM, D), f32),
        compiler_params=_cparams(("arbitrary", "arbitrary")),
        name="ffn_residual_final" if final else "ffn_residual",
    )(base, seg, lstart, x, lpos, gates, mod, g_arr, y_sorted)


def _moe_rows(ntok):
    ntiles = ntok // TM
    worst = TOP_K * ntok + ntiles * N_EXPERTS * (SEG_ALIGN - 1)
    return (-(-worst // MOE_BLK) + N_EXPERTS) * MOE_BLK


def _segment_plan(cnt, nblk):
    seg = (cnt + SEG_ALIGN - 1) // SEG_ALIGN * SEG_ALIGN
    lstart = jnp.cumsum(seg, axis=1) - seg
    rows = jnp.sum(seg, axis=0)
    padded = (rows + MOE_BLK - 1) // MOE_BLK * MOE_BLK
    pad_ends = jnp.cumsum(padded)
    base = (pad_ends - padded)[None, :] + jnp.cumsum(seg, axis=0) - seg
    blk_start = jnp.arange(nblk, dtype=i32) * MOE_BLK
    blk_expert = jnp.minimum(jnp.sum(pad_ends[None, :] <= blk_start[:, None], axis=1), N_EXPERTS - 1)
    n_used = (pad_ends[-1] // MOE_BLK).reshape(1)
    flat = lambda a: a.reshape(-1).astype(i32)
    return flat(base), flat(seg), flat(lstart), blk_expert.astype(i32), n_used.astype(i32)


def _rope_tables(tb):
    rows = SEQ // GRID_W
    row = jnp.repeat(jnp.arange(rows, dtype=f32), GRID_W)
    col = jnp.tile(jnp.arange(GRID_W, dtype=f32), rows)
    inv_freq = ROPE_BASE ** (-jnp.arange(ROPE_FREQS, dtype=f32) / ROPE_FREQS)
    ang_r = row[:, None] * inv_freq[None, :]
    ang_c = col[:, None] * inv_freq[None, :]
    ang = jnp.concatenate([ang_r, ang_r, ang_c, ang_c], axis=-1)
    cos = jnp.concatenate([jnp.ones((CTX, HEAD_DIM), f32), jnp.cos(ang)], axis=0)
    sin = jnp.concatenate([jnp.zeros((CTX, HEAD_DIM), f32), jnp.sin(ang)], axis=0)
    return jnp.tile(cos, (1, LANES // HEAD_DIM)), jnp.tile(sin, (1, LANES // HEAD_DIM))


def _block_diag2(w):
    z = jnp.zeros_like(w[0])
    return jnp.concatenate([jnp.concatenate([w[0], z], axis=1), jnp.concatenate([z, w[1]], axis=1)], axis=0)


def kernel(x, c, ctx, c_ctx, w_mod, b_mod, norm_mix_g, norm_ffn_g, w_in, w_out, att_sink, shift_mu_prev, shift_mu_next, decay_w0, decay_w2, iclr_a0, iclr_a2, vres_v0, vres_v1, vres_v2, gate_g2, k_k, k_a, r_k, ln_x_w, ln_x_b, router_w, router_b, expert_w_gate, expert_w_up, expert_w_down, final_norm_g):
    nbatch = x.shape[0]
    depth = w_mod.shape[0]
    tb = ctx.shape[1] + x.shape[1]
    xa = jnp.concatenate([ctx, x], axis=1)
    nb_pad = -(-(nbatch + 1) // 8) * 8
    cond = jnp.zeros((nb_pad, D), f32).at[:nbatch].set(c).at[nbatch].set(c_ctx)
    mod_all = _mod_call(cond, w_mod, b_mod).reshape(depth, nb_pad, 1, 6 * D)
    cos, sin = _rope_tables(tb)
    wr_hi = router_w.T.astype(bf16)
    wr_lo = (router_w.T - wr_hi.astype(f32)).astype(bf16)
    w_router = jnp.stack([wr_hi, wr_lo])
    b_router = router_b.reshape(N_EXPERTS, 1)
    v_first = None
    for l in range(depth):
        mod = mod_all[l]
        q, k, v, rw = _in_proj_call(xa, mod, norm_mix_g[l].reshape(1, D), w_in[l].astype(bf16), cos, sin)
        att = _attn_call(att_sink[l], q, k, v)
        mu = jnp.stack([shift_mu_prev[l], shift_mu_next[l]])
        v0 = vres_v0[l - 1] if l > 0 else jnp.zeros((RW,), f32)
        vec = jnp.stack([k_k[l], k_a[l], r_k[l].reshape(RW), v0,
                         decay_w0[l, 0], decay_w0[l, 1], iclr_a0[l, 0], iclr_a0[l, 1]])
        if l > 0:
            v1 = jnp.zeros((RW, LANES), f32).at[:, :LORA_VRES].set(vres_v1[l - 1]).astype(bf16)
            v2 = jnp.zeros((LANES, RW), f32).at[:LORA_VRES].set(vres_v2[l - 1]).astype(bf16)
        else:
            v1 = v2 = None
        r_, v_, kk, bv, g, kd, lw, bd = _feat_call(
            rw, v_first, mu, vec, _block_diag2(decay_w2[l]).astype(bf16), _block_diag2(iclr_a2[l]).astype(bf16),
            gate_g2[l].astype(bf16), v1, v2)
        if l == 0:
            v_first = v_
        yf, yb = _scan_call(r_, v_, kk, kd, lw, bd)
        ln = jnp.stack([ln_x_w[l], ln_x_b[l]])
        xa, x_local, lpos, gates, cnt = _mix_call(xa, att, yf, yb, bv, g, mod, ln, w_out[l].astype(bf16),
                                                  norm_ffn_g[l].reshape(1, D), w_router, b_router)
        ntok = nbatch * tb
        nrows = _moe_rows(ntok)
        cnt = cnt[:, :, 0].astype(i32).reshape(ntok // TM, N_EXPERTS)
        base, seg, lstart, blk_expert, n_used = _segment_plan(cnt, nrows // MOE_BLK)
        x_sorted = _dispatch_call(base, seg, lstart, x_local.reshape(-1, D // 2), nrows)
        y_sorted = _ffn_call(blk_expert, n_used, x_sorted, expert_w_gate, expert_w_up, expert_w_down, l)
        xa = _combine_call(base, seg, lstart, xa, lpos, gates, mod, y_sorted,
                           final_norm_g.reshape(1, D) if l == depth - 1 else None)
    return xa
```

```python
import functools
import math

import jax
import jax.numpy as jnp
from jax import lax
from jax.experimental import pallas as pl
from jax.experimental.pallas import tpu as pltpu

f32 = jnp.float32
bf16 = jnp.bfloat16
i32 = jnp.int32
u32 = jnp.uint32

D = 1024
SEQ = 4096
CTX = 256
TB = CTX + SEQ
GRID_W = 64
HEAD_DIM = 64
ATT_WIDTH = 512
ATT_HEADS = 8
KV_HEADS = 2
ATT_GROUP = ATT_HEADS // KV_HEADS
KV_WIDTH = KV_HEADS * HEAD_DIM
RW = 512
RWKV_HEADS = 8
LORA_DECAY = 64
LORA_ICLR = 64
LORA_VRES = 32
LORA_GATE = 128
RWKV_COLS = 3 * RW + 2 * (LORA_DECAY + LORA_ICLR) + LORA_GATE
ATT_COLS = ATT_WIDTH + 2 * KV_WIDTH
IN_COLS = ATT_COLS + RWKV_COLS
N_EXPERTS = 16
N_GROUPS = 4
EXPERTS_PER_GROUP = 4
TOP_K = 2
MOE_BLK = 256
NORM_EPS = 1e-6
GN_EPS = 64e-5
NEG_INF = -1e30
ATT_SCALE = HEAD_DIM ** -0.5
ROPE_BASE = 10000.0
ROPE_FREQS = HEAD_DIM // 4

LANES = 128
TM = 256
QB = 128
CH = 64
HG = 4
GW = HG * HEAD_DIM
SCAN_BATCH = 4
SEG_ALIGN = 8
LROWS = -(-(TOP_K * TM + N_EXPERTS * SEG_ALIGN) // LANES) * LANES
VMEM_LIMIT = 48 * 1024 * 1024


def _cparams(sem):
    return pltpu.CompilerParams(dimension_semantics=sem, vmem_limit_bytes=VMEM_LIMIT)


def _sigmoid(x):
    return 0.5 * jnp.tanh(0.5 * x) + 0.5


def _div_pow2(x, n):
    assert n & (n - 1) == 0
    return lax.shift_right_logical(x, n.bit_length() - 1)


def _mod_pow2(x, n):
    assert n & (n - 1) == 0
    return lax.bitwise_and(x, n - 1)


def _round_up_pow2(x, n):
    assert n & (n - 1) == 0
    return lax.bitwise_and(x + (n - 1), ~(n - 1))


def _dot(a, b):
    return jnp.dot(a, b, preferred_element_type=f32)


def _dot_nt(a, b):
    return lax.dot_general(a, b, (((1,), (1,)), ((), ())), preferred_element_type=f32)


def _dot_tn(a, b):
    return lax.dot_general(a, b, (((0,), (0,)), ((), ())), preferred_element_type=f32)


def _split2(x):
    hi = x.astype(bf16)
    lo = (x - hi.astype(f32)).astype(bf16)
    return hi, lo


def _dot_split_lhs(x, m):
    hi, lo = _split2(x)
    return _dot(hi, m) + _dot(lo, m)


def _rmsnorm(x, g):
    ms = jnp.mean(x * x, axis=-1, keepdims=True)
    return x * lax.rsqrt(ms + NORM_EPS) * g


def _head_ones():
    r = _div_pow2(lax.broadcasted_iota(i32, (RW, RW), 0), HEAD_DIM)
    c = _div_pow2(lax.broadcasted_iota(i32, (RW, RW), 1), HEAD_DIM)
    return (r == c).astype(bf16)


def _mod_kernel(c_ref, w_ref, b_ref, o_ref):
    c = c_ref[...]
    s = (c * _sigmoid(c)).astype(bf16)
    o_ref[0] = _dot(s, w_ref[0].astype(bf16)) + b_ref[0]


def _mod_call(cond, w_mod, b_mod):
    nb = cond.shape[0]
    depth = w_mod.shape[0]
    tn = 1024
    return pl.pallas_call(
        _mod_kernel,
        grid=(depth, 6 * D // tn),
        in_specs=[
            pl.BlockSpec((nb, D), lambda l, j: (0, 0)),
            pl.BlockSpec((1, D, tn), lambda l, j: (l, 0, j)),
            pl.BlockSpec((1, 1, tn), lambda l, j: (l, 0, j)),
        ],
        out_specs=pl.BlockSpec((1, nb, tn), lambda l, j: (l, 0, j)),
        out_shape=jax.ShapeDtypeStruct((depth, nb, 6 * D), f32),
        compiler_params=_cparams(("arbitrary", "arbitrary")),
        name="mod",
    )(cond, w_mod, b_mod.reshape(depth, 1, 6 * D))


def _mod_index(nbatch, ctx_tiles):
    return lambda b, i: (jnp.where(i < ctx_tiles, nbatch, b), 0, 0)


def _in_proj_kernel(x_ref, mod_ref, g_ref, w_ref, cos_ref, sin_ref, q_ref, k_ref, v_ref, rw_ref):
    x = x_ref[0]
    tm = x.shape[0]
    h = _rmsnorm(x, g_ref[...])
    sh = mod_ref[0, :, 0:D]
    sc = mod_ref[0, :, D:2 * D]
    h = (h * (1.0 + sc) + sh).astype(bf16)
    p = _dot(h, w_ref[...])
    cos = cos_ref[...]
    sin = sin_ref[...]
    lane = lax.broadcasted_iota(i32, (tm, LANES), 1)
    first_half = _mod_pow2(lane, 2 * ROPE_FREQS) < ROPE_FREQS

    def rope(t):
        rot = jnp.where(first_half, -pltpu.roll(t, LANES - ROPE_FREQS, 1), pltpu.roll(t, ROPE_FREQS, 1))
        return t * cos + rot * sin

    for j in range(ATT_WIDTH // LANES):
        t = (rope(p[:, j * LANES:(j + 1) * LANES]) * ATT_SCALE).astype(bf16)
        q_ref[0, 2 * j] = t[:, :HEAD_DIM]
        q_ref[0, 2 * j + 1] = t[:, HEAD_DIM:]
    kt = rope(p[:, ATT_WIDTH:ATT_WIDTH + KV_WIDTH]).astype(bf16)
    vt = p[:, ATT_WIDTH + KV_WIDTH:ATT_COLS].astype(bf16)
    for hh in range(KV_HEADS):
        k_ref[0, hh] = kt[:, hh * HEAD_DIM:(hh + 1) * HEAD_DIM]
        v_ref[0, hh] = vt[:, hh * HEAD_DIM:(hh + 1) * HEAD_DIM]
    rw_ref[0] = p[:, ATT_COLS:]


def _in_proj_call(x, mod, g, w_in, cos, sin):
    nbatch, tb, _ = x.shape
    nt = tb // TM
    return pl.pallas_call(
        _in_proj_kernel,
        grid=(nbatch, nt),
        in_specs=[
            pl.BlockSpec((1, TM, D), lambda b, i: (b, i, 0)),
            pl.BlockSpec((1, 1, 6 * D), _mod_index(nbatch, CTX // TM)),
            pl.BlockSpec((1, D), lambda b, i: (0, 0)),
            pl.BlockSpec((D, IN_COLS), lambda b, i: (0, 0)),
            pl.BlockSpec((TM, LANES), lambda b, i: (i, 0)),
            pl.BlockSpec((TM, LANES), lambda b, i: (i, 0)),
        ],
        out_specs=[
            pl.BlockSpec((1, ATT_HEADS, TM, HEAD_DIM), lambda b, i: (b, 0, i, 0)),
            pl.BlockSpec((1, KV_HEADS, TM, HEAD_DIM), lambda b, i: (b, 0, i, 0)),
            pl.BlockSpec((1, KV_HEADS, TM, HEAD_DIM), lambda b, i: (b, 0, i, 0)),
            pl.BlockSpec((1, TM, RWKV_COLS), lambda b, i: (b, i, 0)),
        ],
        out_shape=[
            jax.ShapeDtypeStruct((nbatch, ATT_HEADS, tb, HEAD_DIM), bf16),
            jax.ShapeDtypeStruct((nbatch, KV_HEADS, tb, HEAD_DIM), bf16),
            jax.ShapeDtypeStruct((nbatch, KV_HEADS, tb, HEAD_DIM), bf16),
            jax.ShapeDtypeStruct((nbatch, tb, RWKV_COLS), f32),
        ],
        compiler_params=_cparams(("parallel", "arbitrary")),
        name="in_proj",
    )(x, mod, g, w_in, cos, sin)


def _attn_kernel(sink_ref, q_ref, kp_ref, km_ref, kn_ref, vp_ref, vm_ref, vn_ref, kx_ref, vx_ref, o_ref,
                 *, npairs, ctx_pairs):
    j = pl.program_id(1)
    is_lat = j >= ctx_pairs
    before_ok = jnp.logical_and(is_lat, j - 1 >= ctx_pairs)
    after_ok = jnp.logical_and(is_lat, j + 1 <= npairs - 1)
    rows = ATT_GROUP * QB
    qi = _mod_pow2(lax.broadcasted_iota(i32, (rows, QB), 0), QB)
    kj = lax.broadcasted_iota(i32, (rows, QB), 1)
    band_p = kj >= qi
    band_n = kj <= qi
    row_head = _div_pow2(lax.broadcasted_iota(i32, (rows, 1), 0), QB)
    half = (slice(0, QB), slice(QB, 2 * QB))
    chains = [(c, h) for c in range(2) for h in range(KV_HEADS)]
    qh, kprev, kcur, knext, vprev, vcur, vnext, ok_p, ok_n = [], [], [], [], [], [], [], [], []
    for c, h in chains:
        qh.append(q_ref[0, ATT_GROUP * h:ATT_GROUP * (h + 1), half[c], :].reshape(rows, HEAD_DIM))
        kcur.append(km_ref[0, h, half[c], :])
        vcur.append(vm_ref[0, h, half[c], :])
        if c == 0:
            kprev.append(kp_ref[0, h]); vprev.append(vp_ref[0, h]); ok_p.append(before_ok)
            knext.append(km_ref[0, h, half[1], :]); vnext.append(vm_ref[0, h, half[1], :]); ok_n.append(is_lat)
        else:
            kprev.append(km_ref[0, h, half[0], :]); vprev.append(vm_ref[0, h, half[0], :]); ok_p.append(is_lat)
            knext.append(kn_ref[0, h]); vnext.append(vn_ref[0, h]); ok_n.append(after_ok)
    n = range(len(chains))
    s_p = [jnp.where(jnp.logical_and(band_p, ok_p[i]), _dot_nt(qh[i], kprev[i]), NEG_INF) for i in n]
    s_c = [jnp.where(is_lat, _dot_nt(qh[i], kcur[i]), NEG_INF) for i in n]
    s_n = [jnp.where(jnp.logical_and(band_n, ok_n[i]), _dot_nt(qh[i], knext[i]), NEG_INF) for i in n]
    s_x = [_dot_nt(qh[i], kx_ref[0, chains[i][1]]) for i in n]
    sink = []
    for h in range(KV_HEADS):
        sk = jnp.zeros((rows, 1), f32)
        for g in range(ATT_GROUP):
            sk = jnp.where(row_head == g, sink_ref[ATT_GROUP * h + g], sk)
        sink.append(sk)
    sink = [sink[h] for _, h in chains]
    m = [jnp.maximum(jnp.max(jnp.maximum(jnp.maximum(s_p[i], s_c[i]),
                                         jnp.maximum(jnp.maximum(s_n[i], s_x[i][:, :QB]), s_x[i][:, QB:])),
                             axis=-1, keepdims=True), sink[i]) for i in n]
    e_p = [jnp.exp(s_p[i] - m[i]) for i in n]
    e_c = [jnp.exp(s_c[i] - m[i]) for i in n]
    e_n = [jnp.exp(s_n[i] - m[i]) for i in n]
    e_x = [jnp.exp(s_x[i] - m[i]) for i in n]
    den = [jnp.sum((e_p[i] + e_c[i]) + (e_n[i] + e_x[i][:, :QB]) + e_x[i][:, QB:], axis=-1, keepdims=True)
           + jnp.exp(sink[i] - m[i]) for i in n]
    o = [(_dot(e_p[i].astype(bf16), vprev[i]) + _dot(e_c[i].astype(bf16), vcur[i])
          + _dot(e_n[i].astype(bf16), vnext[i]) + _dot(e_x[i].astype(bf16), vx_ref[0, chains[i][1]])) / den[i]
         for i in n]
    for c in range(2):
        o_ref[0, half[c], :] = jnp.concatenate(
            [o[c * KV_HEADS + h][g * QB:(g + 1) * QB] for h in range(KV_HEADS) for g in range(ATT_GROUP)],
            axis=1).astype(bf16)


def _attn_call(sink, q, k, v):
    nbatch, _, tb, _ = q.shape
    nblk = tb // QB
    npairs = nblk // 2
    assert CTX == 2 * QB and nblk % 2 == 0
    kv_blk = (1, KV_HEADS, QB, HEAD_DIM)
    pair_blk = (1, KV_HEADS, 2 * QB, HEAD_DIM)
    before_map = lambda b, j: (b, 0, jnp.maximum(2 * j - 1, 0), 0)
    pair_map = lambda b, j: (b, 0, j, 0)
    after_map = lambda b, j: (b, 0, jnp.minimum(2 * j + 2, nblk - 1), 0)
    ctx_spec = pl.BlockSpec((1, KV_HEADS, CTX, HEAD_DIM), lambda b, j: (b, 0, 0, 0))
    return pl.pallas_call(
        functools.partial(_attn_kernel, npairs=npairs, ctx_pairs=CTX // (2 * QB)),
        grid=(nbatch, npairs),
        in_specs=[
            pl.BlockSpec(memory_space=pltpu.SMEM),
            pl.BlockSpec((1, ATT_HEADS, 2 * QB, HEAD_DIM), pair_map),
            pl.BlockSpec(kv_blk, before_map), pl.BlockSpec(pair_blk, pair_map), pl.BlockSpec(kv_blk, after_map),
            pl.BlockSpec(kv_blk, before_map), pl.BlockSpec(pair_blk, pair_map), pl.BlockSpec(kv_blk, after_map),
            ctx_spec, ctx_spec,
        ],
        out_specs=pl.BlockSpec((1, 2 * QB, ATT_WIDTH), lambda b, j: (b, j, 0)),
        out_shape=jax.ShapeDtypeStruct((nbatch, tb, ATT_WIDTH), bf16),
        compiler_params=_cparams(("parallel", "arbitrary")),
        name="attention",
    )(sink, q, k, k, k, v, v, v, k, v)


def _feat_kernel(*refs, nt, ctx_tiles, has_vres):
    if has_vres:
        (rw_ref, hp_ref, hn_ref, vf_ref, mu_ref, vec_ref, w2_ref, a2_ref, g2_ref, v1_ref, v2_ref,
         r_ref, v_ref, kk_ref, bv_ref, g_ref, kd_ref, lw_ref, bd_ref) = refs
    else:
        (rw_ref, hp_ref, hn_ref, mu_ref, vec_ref, w2_ref, a2_ref, g2_ref,
         r_ref, v_ref, kk_ref, bv_ref, g_ref, kd_ref, lw_ref, bd_ref) = refs
    i = pl.program_id(1)
    u0 = rw_ref[0]
    tm = u0.shape[0]
    prev_zero = jnp.logical_or(i == 0, i == ctx_tiles)
    next_zero = jnp.logical_or(i == ctx_tiles - 1, i == nt - 1)
    halo_p = jnp.where(prev_zero, 0.0, hp_ref[0, 7:8, :])
    halo_n = jnp.where(next_zero, 0.0, hn_ref[0, 0:1, :])
    row = lax.broadcasted_iota(i32, (tm, 1), 0)
    prev = jnp.where(row == 0, halo_p, pltpu.roll(u0, 1, 0))
    nxt = jnp.where(row == tm - 1, halo_n, pltpu.roll(u0, tm - 1, 0))
    mu_p = mu_ref[0:1, :]
    mu_n = mu_ref[1:2, :]
    u = u0 + mu_p * (prev - u0) + mu_n * (nxt - u0)

    r = u[:, 0:RW]
    k = u[:, RW:2 * RW]
    v = u[:, 2 * RW:3 * RW]
    wd = u[:, 3 * RW:3 * RW + 2 * LORA_DECAY]
    ad = u[:, 3 * RW + 2 * LORA_DECAY:3 * RW + 2 * (LORA_DECAY + LORA_ICLR)]
    gd = u[:, 3 * RW + 2 * (LORA_DECAY + LORA_ICLR):]
    k_k = vec_ref[0:1, :]
    k_a = vec_ref[1:2, :]
    r_k = vec_ref[2:3, :]
    ones = _head_ones()

    if has_vres:
        lo = _dot(v.astype(bf16), v1_ref[...])
        gate = _sigmoid(vec_ref[3:4, :] + _dot(lo.astype(bf16), v2_ref[...]))
        v = v + (vf_ref[0] - v) * gate
    decay_in = _dot(jnp.tanh(wd).astype(bf16), w2_ref[...])
    a_in = _dot(ad.astype(bf16), a2_ref[...])
    kk = k * k_k
    n2 = _dot_split_lhs(kk * kk, ones)
    kk = kk * lax.rsqrt(jnp.maximum(n2, 1e-24))
    g = _dot(_sigmoid(gd).astype(bf16), g2_ref[...])
    ksum = jnp.zeros_like(k)
    for d in range(2):
        w0 = vec_ref[4 + d:5 + d, :]
        a0 = vec_ref[6 + d:7 + d, :]
        lw = -_sigmoid(w0 + decay_in[:, d * RW:(d + 1) * RW]) * math.exp(-0.5)
        a = _sigmoid(a0 + a_in[:, d * RW:(d + 1) * RW])
        kd = k * (1.0 + (a - 1.0) * k_a)
        ksum = ksum + kd
        kd_ref[d, 0] = kd.astype(bf16)
        lw_ref[d, 0] = lw
        bd_ref[d, 0] = (kk * a).astype(bf16)
    bonus = _dot_split_lhs(r * ksum * r_k, ones)
    r_ref[0] = r.astype(bf16)
    v_ref[0] = v
    kk_ref[0] = kk.astype(bf16)
    bv_ref[0] = bonus * v
    g_ref[0] = g.astype(bf16)


def _feat_call(rw, v_first, mu, vec, w2bd, a2bd, g2, v1, v2):
    nbatch, tb, _ = rw.shape
    nt = tb // TM
    has_vres = v_first is not None
    sub = TM // 8
    tile = lambda w: pl.BlockSpec((1, TM, w), lambda b, i: (b, i, 0))
    full = lambda a: pl.BlockSpec(a.shape, lambda b, i: (0,) * a.ndim)
    in_specs = [
        tile(RWKV_COLS),
        pl.BlockSpec((1, 8, RWKV_COLS), lambda b, i: (b, jnp.maximum(i * sub - 1, 0), 0)),
        pl.BlockSpec((1, 8, RWKV_COLS), lambda b, i: (b, jnp.minimum((i + 1) * sub, tb // 8 - 1), 0)),
    ]
    args = [rw, rw, rw]
    if has_vres:
        in_specs.append(tile(RW))
        args.append(v_first)
    consts = [mu, vec, w2bd, a2bd, g2] + ([v1, v2] if has_vres else [])
    in_specs += [full(a) for a in consts]
    args += consts
    dir_spec = pl.BlockSpec((2, 1, TM, RW), lambda b, i: (0, b, i, 0))
    tok = lambda dt: jax.ShapeDtypeStruct((nbatch, tb, RW), dt)
    dtok = lambda dt: jax.ShapeDtypeStruct((2, nbatch, tb, RW), dt)
    return pl.pallas_call(
        functools.partial(_feat_kernel, nt=nt, ctx_tiles=CTX // TM, has_vres=has_vres),
        grid=(nbatch, nt),
        in_specs=in_specs,
        out_specs=[tile(RW)] * 5 + [dir_spec] * 3,
        out_shape=[tok(bf16), tok(f32), tok(bf16), tok(f32), tok(bf16), dtok(bf16), dtok(f32), dtok(bf16)],
        compiler_params=_cparams(("parallel", "arbitrary")),
        name="rwkv_features",
    )(*args)


def _block_diag_rows(x, width):
    cb = _div_pow2(lax.broadcasted_iota(i32, x.shape, 1), width)
    return jnp.concatenate([jnp.where(cb == h, x, jnp.zeros_like(x)) for h in range(HG)], axis=0)


def _scan_chunks(probs):
    wide = HG * CH
    row_t = lax.broadcasted_iota(i32, (CH, wide), 0)
    col_t = _mod_pow2(lax.broadcasted_iota(i32, (CH, wide), 1), CH)
    row_g = lax.broadcasted_iota(i32, (CH, GW), 0)
    incl_t = {False: col_t <= row_t, True: col_t >= row_t}
    strict_t = {False: col_t < row_t, True: col_t > row_t}
    eye_t = (row_t == col_t).astype(f32)
    rb = _div_pow2(lax.broadcasted_iota(i32, (GW, GW), 0), HEAD_DIM)
    cb = _div_pow2(lax.broadcasted_iota(i32, (GW, GW), 1), HEAD_DIM)
    n = len(probs)
    rev = [p[7] for p in probs]
    def cumsum_rows(x, reverse):
        s = 1
        while s < CH:
            if reverse:
                x = x + jnp.where(row_g < CH - s, pltpu.roll(x, CH - s, 0), 0.0)
            else:
                x = x + jnp.where(row_g >= s, pltpu.roll(x, s, 0), 0.0)
            s *= 2
        return x

    gam = [cumsum_rows(probs[i][5], rev[i]) for i in range(n)]
    ar, bk, k_t, b_t = [], [], [], []
    for i, (s_prev, r, v, kk, k, lw, b, _) in enumerate(probs):
        e_neg = jnp.exp(-gam[i])
        a_s = (-kk * jnp.exp(gam[i] - lw)).astype(bf16)
        r_s = (r * jnp.exp(gam[i])).astype(bf16)
        b_t.append(b * e_neg)
        k_t.append(k * e_neg)
        ar.append(jnp.concatenate([a_s, r_s], axis=0))
        bk.append(jnp.concatenate([_block_diag_rows(b_t[i].astype(bf16), HEAD_DIM),
                                   _block_diag_rows(k_t[i].astype(bf16), HEAD_DIM)], axis=0))
    gram = [_dot_nt(ar[i], bk[i]) for i in range(n)]
    ars = [_dot_nt(ar[i], probs[i][0].astype(bf16)) for i in range(n)]
    v_bd = [_block_diag_rows(probs[i][2].astype(bf16), HEAD_DIM) for i in range(n)]
    p0 = [jnp.where(strict_t[rev[i]], gram[i][:CH, :wide], 0.0).astype(bf16) for i in range(n)]
    lq = [jnp.concatenate([jnp.where(strict_t[rev[i]], gram[i][:CH, wide:], 0.0),
                           jnp.where(incl_t[rev[i]], gram[i][CH:, wide:], 0.0)], axis=0).astype(bf16)
          for i in range(n)]
    lqv = [_dot(lq[i], v_bd[i]) for i in range(n)]
    rhs = [ars[i][:CH] + lqv[i][:CH] for i in range(n)]
    t = [eye_t + p0[i].astype(f32) for i in range(n)]
    p = [_dot(p0[i], _block_diag_rows(p0[i], CH)).astype(bf16) for i in range(n)]
    m = 4
    while m < CH:
        tp = [_dot(jnp.concatenate([t[i].astype(bf16), p[i]], axis=0), _block_diag_rows(p[i], CH))
              for i in range(n)]
        t = [t[i] + tp[i][:CH] for i in range(n)]
        p = [tp[i][CH:].astype(bf16) for i in range(n)]
        m *= 2
    t = [t[i] + _dot(t[i].astype(bf16), _block_diag_rows(p[i], CH)) for i in range(n)]
    u = [_dot(t[i].astype(bf16), _block_diag_rows(rhs[i].astype(bf16), HEAD_DIM)) for i in range(n)]
    out = []
    for i in range(n):
        s_prev, v, lw = probs[i][0], probs[i][2], probs[i][5]
        q_b = jnp.where(incl_t[rev[i]], gram[i][CH:, :wide], 0.0).astype(bf16)
        u_bd = _block_diag_rows(u[i].astype(bf16), HEAD_DIM)
        y = ars[i][CH:] + lqv[i][CH:] + _dot(q_b, u_bd)
        eg = jnp.exp(jnp.sum(lw, axis=0, keepdims=True))
        vu = jnp.concatenate([v, u[i]], axis=0).astype(bf16)
        kb = jnp.concatenate([k_t[i] * eg, b_t[i] * eg], axis=0).astype(bf16)
        s_add = _dot_tn(vu, kb)
        out.append((s_prev * eg + jnp.where(rb == cb, s_add, 0.0), y))
    return out


def _scan_kernel(rf_ref, vf_ref, kkf_ref, kf_ref, lwf_ref, bf_ref,
                 rb_ref, vb_ref, kkb_ref, kb_ref, lwb_ref, bb_ref,
                 yf_ref, yb_ref, s_ref):
    @pl.when(pl.program_id(1) == 0)
    def _():
        s_ref[...] = jnp.zeros_like(s_ref)

    dirs = ((rf_ref, vf_ref, kkf_ref, kf_ref, lwf_ref, bf_ref, yf_ref),
            (rb_ref, vb_ref, kkb_ref, kb_ref, lwb_ref, bb_ref, yb_ref))
    probs, dest = [], []
    for bi in range(SCAN_BATCH):
        for d, (r_ref, v_ref, kk_ref, k_ref, lw_ref, b_ref, y_ref) in enumerate(dirs):
            for g in range(RW // GW):
                sl = slice(g * GW, (g + 1) * GW)
                probs.append((s_ref[bi, d, g], r_ref[bi, :, sl], v_ref[bi, :, sl], kk_ref[bi, :, sl],
                              k_ref[0, bi, :, sl], lw_ref[0, bi, :, sl], b_ref[0, bi, :, sl], d == 1))
                dest.append((bi, d, g, y_ref, sl))
    for (bi, d, g, y_ref, sl), (s_new, y) in zip(dest, _scan_chunks(probs)):
        s_ref[bi, d, g] = s_new
        y_ref[bi, :, sl] = y


def _scan_call(r, v, kk, kd, lw, bd):
    nbatch, tb, _ = r.shape
    nc = tb // CH
    cc = CTX // CH
    sb = SCAN_BATCH
    rev = lambda j: jnp.where(j < cc, cc - 1 - j, nc - 1 + cc - j)
    tok_f = pl.BlockSpec((sb, CH, RW), lambda b, j: (b, j, 0))
    tok_b = pl.BlockSpec((sb, CH, RW), lambda b, j: (b, rev(j), 0))
    dir_f = pl.BlockSpec((1, sb, CH, RW), lambda b, j: (0, b, j, 0))
    dir_b = pl.BlockSpec((1, sb, CH, RW), lambda b, j: (1, b, rev(j), 0))
    out = jax.ShapeDtypeStruct((nbatch, tb, RW), f32)
    return pl.pallas_call(
        _scan_kernel,
        grid=(nbatch // sb, nc),
        in_specs=[tok_f, tok_f, tok_f, dir_f, dir_f, dir_f, tok_b, tok_b, tok_b, dir_b, dir_b, dir_b],
        out_specs=[tok_f, tok_b],
        out_shape=[out, out],
        scratch_shapes=[pltpu.VMEM((sb, 2, RW // GW, GW, GW), f32)],
        compiler_params=_cparams(("parallel", "arbitrary")),
        name="rwkv_scan",
    )(r, v, kk, kd, lw, bd, r, v, kk, kd, lw, bd)


def _pack_bf16_pairs(x):
    bits = pltpu.bitcast(x, u32)
    half = x.shape[1] // 2
    return bits[:, :half] | lax.shift_right_logical(bits[:, half:], jnp.uint32(16))


def _unpack_bf16_pairs(p):
    hi = pltpu.bitcast(p & jnp.uint32(0xFFFF0000), f32)
    lo = pltpu.bitcast(lax.shift_left(p, jnp.uint32(16)), f32)
    return jnp.concatenate([hi, lo], axis=1).astype(bf16)


def _mix_kernel(x_ref, att_ref, yf_ref, yb_ref, bv_ref, g_ref, mod_ref, ln_ref, wo_ref, gf_ref, wr_ref, rb_ref,
                xo_ref, xl_ref, lp_ref, gt_ref, cnt_ref):
    ones = _head_ones()
    y = yf_ref[0] + yb_ref[0]
    inv = 1.0 / HEAD_DIM
    mu = _dot(y.astype(bf16), ones) * inv
    dlt = y - mu
    var = _dot((dlt * dlt).astype(bf16), ones) * inv
    gn = dlt * lax.rsqrt(var + GN_EPS) * ln_ref[0:1, :] + ln_ref[1:2, :]
    rwk = ((gn + bv_ref[0]) * g_ref[0]).astype(bf16)
    mix = _dot(att_ref[0], wo_ref[0:ATT_WIDTH, :]) + _dot(rwk, wo_ref[ATT_WIDTH:, :])
    gate_a = mod_ref[0, :, 2 * D:3 * D]
    xn = x_ref[0] + gate_a * mix
    xo_ref[0] = xn
    sh = mod_ref[0, :, 3 * D:4 * D]
    sc = mod_ref[0, :, 4 * D:5 * D]
    hf = _rmsnorm(xn, gf_ref[...]) * (1.0 + sc) + sh
    h_hi, h_lo = _split2(hf)
    logits = _dot_nt(wr_ref[0], h_hi) + _dot_nt(wr_ref[0], h_lo) + _dot_nt(wr_ref[1], h_hi)
    _route_and_sort(hf, _sigmoid(logits), rb_ref[...], xl_ref, lp_ref, gt_ref, cnt_ref)


def _route_and_sort(hf, scores, bias, xl_ref, lp_ref, gt_ref, cnt_ref):
    tm = hf.shape[0]
    e0, e1, g0, g1 = _route_rows(scores, bias)
    gt_ref[0] = jnp.concatenate([g0, g1], axis=0)
    eio = lax.broadcasted_iota(i32, (N_EXPERTS, tm), 0)
    oh0 = (eio == e0).astype(f32)
    oh1 = (eio == e1).astype(f32)
    both = oh0 + oh1
    upper = (lax.broadcasted_iota(i32, (tm, tm), 0) < lax.broadcasted_iota(i32, (tm, tm), 1)).astype(bf16)
    before = _dot(both.astype(bf16), upper)
    cnt = jnp.sum(both, axis=1, keepdims=True)
    seg = _round_up_pow2(cnt.astype(i32), SEG_ALIGN).astype(f32)
    lower = (lax.broadcasted_iota(i32, (N_EXPERTS, N_EXPERTS), 0)
             > lax.broadcasted_iota(i32, (N_EXPERTS, N_EXPERTS), 1)).astype(bf16)
    seg_start = _dot(lower, jnp.broadcast_to(seg, (N_EXPERTS, tm)).astype(bf16))
    pos = seg_start + before
    lp0 = jnp.sum(oh0 * pos, axis=0, keepdims=True).astype(i32)
    lp1 = jnp.sum(oh1 * pos, axis=0, keepdims=True).astype(i32)
    lp_ref[0] = jnp.concatenate([lp0, lp1], axis=0)
    cnt_ref[0] = jnp.broadcast_to(cnt, (N_EXPERTS, LANES))
    jj = lax.broadcasted_iota(i32, (LROWS, tm), 0)
    sel_t = jnp.logical_or(jj == lp0, jj == lp1).astype(bf16)
    xl_ref[0] = _pack_bf16_pairs(_dot(sel_t, hf.astype(bf16)))


def _route_rows(scores, bias):
    biased = scores + bias
    row = lambda a, e: a[e:e + 1, :]
    best = None
    for gi in range(N_GROUPS):
        m = [row(biased, gi * EXPERTS_PER_GROUP + j) for j in range(EXPERTS_PER_GROUP)]
        gs = None
        for a in range(EXPERTS_PER_GROUP):
            for b in range(a + 1, EXPERTS_PER_GROUP):
                pair = m[a] + m[b]
                gs = pair if gs is None else jnp.maximum(gs, pair)
        if best is None:
            best, g_idx = gs, jnp.zeros(gs.shape, i32)
        else:
            better = gs > best
            g_idx = jnp.where(better, gi, g_idx)
            best = jnp.where(better, gs, best)

    def pick(a, j):
        out = row(a, j)
        for gi in range(1, N_GROUPS):
            out = jnp.where(g_idx == gi, row(a, gi * EXPERTS_PER_GROUP + j), out)
        return out

    vb = [pick(biased, j) for j in range(EXPERTS_PER_GROUP)]
    vs = [pick(scores, j) for j in range(EXPERTS_PER_GROUP)]

    def argmax_first(vals):
        bv, bi = vals[0], jnp.zeros(vals[0].shape, i32)
        for j in range(1, len(vals)):
            better = vals[j] > bv
            bi = jnp.where(better, j, bi)
            bv = jnp.where(better, vals[j], bv)
        return bi

    i1 = argmax_first(vb)
    i2 = argmax_first([jnp.where(i1 == j, -jnp.inf, vb[j]) for j in range(EXPERTS_PER_GROUP)])
    sel = lambda idx: sum(jnp.where(idx == j, vs[j], 0.0) for j in range(EXPERTS_PER_GROUP))
    s1, s2 = sel(i1), sel(i2)
    tot = s1 + s2
    base = g_idx * EXPERTS_PER_GROUP
    return base + i1, base + i2, s1 / tot, s2 / tot


def _mix_call(x, att, yf, yb, bv, g, mod, ln, w_out, g_ffn, w_router, b_router):
    nbatch, tb, _ = x.shape
    nt = tb // TM
    tile = lambda w: pl.BlockSpec((1, TM, w), lambda b, i: (b, i, 0))
    full = lambda a: pl.BlockSpec(a.shape, lambda b, i: (0,) * a.ndim)
    route = pl.BlockSpec((1, TOP_K, TM), lambda b, i: (b, 0, i))
    return pl.pallas_call(
        _mix_kernel,
        grid=(nbatch, nt),
        in_specs=[tile(D), tile(ATT_WIDTH), tile(RW), tile(RW), tile(RW), tile(RW),
                  pl.BlockSpec((1, 1, 6 * D), _mod_index(nbatch, CTX // TM)),
                  full(ln), full(w_out), full(g_ffn), full(w_router), full(b_router)],
        out_specs=[tile(D), pl.BlockSpec((1, LROWS, D // 2), lambda b, i: (b, i, 0)), route, route,
                   pl.BlockSpec((1, N_EXPERTS, LANES), lambda b, i: (b, i, 0))],
        out_shape=[jax.ShapeDtypeStruct((nbatch, tb, D), f32),
                   jax.ShapeDtypeStruct((nbatch, nt * LROWS, D // 2), u32),
                   jax.ShapeDtypeStruct((nbatch, TOP_K, tb), i32),
                   jax.ShapeDtypeStruct((nbatch, TOP_K, tb), f32),
                   jax.ShapeDtypeStruct((nbatch, nt * N_EXPERTS, LANES), f32)],
        compiler_params=_cparams(("parallel", "arbitrary")),
        name="mix_out",
    )(x, att, yf, yb, bv, g, mod, ln, w_out, g_ffn, w_router, b_router)


def _segment_copies(fn, tile, base_ref, seg_ref, ls_ref, src, dst, sem, src_is_global):
    for e in range(N_EXPERTS):
        idx = tile * N_EXPERTS + e
        seg = seg_ref[idx]
        g0 = base_ref[idx]
        l0 = ls_ref[idx]
        size = TM
        while size >= SEG_ALIGN:
            done = lax.bitwise_and(seg, ~(2 * size - 1))

            @pl.when(lax.bitwise_and(seg, size) != 0)
            def _():
                g_rows = pl.ds(pl.multiple_of(g0 + done, SEG_ALIGN), size)
                l_rows = pl.ds(pl.multiple_of(l0 + done, SEG_ALIGN), size)
                s_rows, d_rows = (g_rows, l_rows) if src_is_global else (l_rows, g_rows)
                fn(pltpu.make_async_copy(src.at[s_rows, :], dst.at[d_rows, :], sem))

            size //= 2


def _dispatch_kernel(base_ref, seg_ref, ls_ref, xl_ref, xs_in_ref, xs_ref, sem):
    del xs_in_ref
    tile = pl.program_id(0)
    args = (tile, base_ref, seg_ref, ls_ref, xl_ref, xs_ref, sem, False)
    _segment_copies(lambda cp: cp.start(), *args)
    _segment_copies(lambda cp: cp.wait(), *args)


def _dispatch_call(base, seg, lstart, x_local, nrows):
    ntiles = x_local.shape[0] // LROWS
    any_spec = pl.BlockSpec(memory_space=pl.ANY)
    return pl.pallas_call(
        _dispatch_kernel,
        grid_spec=pltpu.PrefetchScalarGridSpec(
            num_scalar_prefetch=3,
            grid=(ntiles,),
            in_specs=[pl.BlockSpec((LROWS, D // 2), lambda t, *_: (t, 0)), any_spec],
            out_specs=any_spec,
            scratch_shapes=[pltpu.SemaphoreType.DMA(())],
        ),
        out_shape=jax.ShapeDtypeStruct((nrows, D // 2), u32),
        input_output_aliases={4: 0},
        compiler_params=_cparams(("arbitrary",)),
        name="moe_dispatch",
    )(base, seg, lstart, x_local, jnp.zeros((nrows, D // 2), u32))


def _ffn_kernel(be_ref, nu_ref, x_ref, wg_ref, wu_ref, wd_ref, y_ref, wg_bf, wu_bf, wd_bf):
    i = pl.program_id(0)
    used = i < nu_ref[0]
    new_expert = jnp.logical_or(i == 0, be_ref[i] != be_ref[jnp.maximum(i - 1, 0)])

    @pl.when(jnp.logical_and(used, new_expert))
    def _():
        wg_bf[...] = wg_ref[0, 0].astype(bf16)
        wu_bf[...] = wu_ref[0, 0].astype(bf16)
        wd_bf[...] = wd_ref[0, 0].astype(bf16)

    @pl.when(used)
    def _():
        x = _unpack_bf16_pairs(x_ref[...])
        gt = _dot(x, wg_bf[...])
        up = _dot(x, wu_bf[...])
        hid = (gt * _sigmoid(gt) * up).astype(bf16)
        y = _dot(hid, wd_bf[...])
        y_ref[...] = _pack_bf16_pairs(y.astype(bf16).astype(f32))

    @pl.when(i >= nu_ref[0])
    def _():
        y_ref[...] = jnp.zeros_like(y_ref)


def _ffn_call(blk_expert, n_used, x_sorted, wg, wu, wd, layer):
    nrows = x_sorted.shape[0]
    nblk = nrows // MOE_BLK
    wspec = pl.BlockSpec((1, 1, D, D), lambda i, be, nu: (layer, be[i], 0, 0))
    return pl.pallas_call(
        _ffn_kernel,
        grid_spec=pltpu.PrefetchScalarGridSpec(
            num_scalar_prefetch=2,
            grid=(nblk,),
            in_specs=[pl.BlockSpec((MOE_BLK, D // 2), lambda i, be, nu: (i, 0)), wspec, wspec, wspec],
            out_specs=pl.BlockSpec((MOE_BLK, D // 2), lambda i, be, nu: (i, 0)),
            scratch_shapes=[pltpu.VMEM((D, D), bf16)] * 3,
        ),
        out_shape=jax.ShapeDtypeStruct((nrows, D // 2), u32),
        compiler_params=_cparams(("arbitrary",)),
        name="moe_ffn",
    )(blk_expert, n_used, x_sorted, wg, wu, wd)


def _combine_kernel(base_ref, seg_ref, ls_ref, x_ref, lp_ref, gt_ref, mod_ref, g_ref, y_ref, o_ref, ybuf, sem,
                    *, tiles_per_batch, first_tile, final):
    b = pl.program_id(0)
    i = pl.program_id(1)
    nt = pl.num_programs(1)
    step = b * nt + i
    slot = lax.rem(step, 2)
    tile = b * tiles_per_batch + i + first_tile
    next_tile = jnp.where(i + 1 < nt, tile + 1, (b + 1) * tiles_per_batch + first_tile)

    def copies(fn, which_tile, which_slot):
        _segment_copies(fn, which_tile, base_ref, seg_ref, ls_ref, y_ref, ybuf.at[which_slot],
                        sem.at[which_slot], True)

    @pl.when(step == 0)
    def _():
        ybuf[...] = jnp.zeros_like(ybuf)
        copies(lambda cp: cp.start(), tile, slot)

    @pl.when(step + 1 < pl.num_programs(0) * nt)
    def _():
        copies(lambda cp: cp.start(), next_tile, 1 - slot)

    copies(lambda cp: cp.wait(), tile, slot)
    y_loc = _unpack_bf16_pairs(ybuf[slot])
    tm = x_ref.shape[1]
    jj = lax.broadcasted_iota(i32, (LROWS, tm), 0)
    gmat = (jnp.where(jj == lp_ref[0, 0:1, :], gt_ref[0, 0:1, :], 0.0)
            + jnp.where(jj == lp_ref[0, 1:2, :], gt_ref[0, 1:2, :], 0.0))
    g_hi, g_lo = _split2(gmat)
    moe = _dot_tn(g_hi, y_loc) + _dot_tn(g_lo, y_loc)
    xn = x_ref[0] + mod_ref[0, :, 5 * D:6 * D] * moe
    o_ref[0] = _rmsnorm(xn, g_ref[...]) if final else xn


def _combine_call(base, seg, lstart, x, lpos, gates, mod, y_sorted, final_g):
    nbatch, tb, _ = x.shape
    ctx_tiles = CTX // TM
    tiles_per_batch = tb // TM
    final = final_g is not None
    first_tile = ctx_tiles if final else 0
    nt = tiles_per_batch - first_tile
    tile = pl.BlockSpec((1, TM, D), lambda b, i, *_: (b, i + first_tile, 0))
    route = pl.BlockSpec((1, TOP_K, TM), lambda b, i, *_: (b, 0, i + first_tile))
    mod_map = _mod_index(nbatch, ctx_tiles - first_tile)
    g_arr = final_g if final else jnp.ones((1, D), f32)
    return pl.pallas_call(
        functools.partial(_combine_kernel, tiles_per_batch=tiles_per_batch, first_tile=first_tile, final=final),
        grid_spec=pltpu.PrefetchScalarGridSpec(
            num_scalar_prefetch=3,
            grid=(nbatch, nt),
            in_specs=[tile, route, route,
                      pl.BlockSpec((1, 1, 6 * D), lambda b, i, *_: mod_map(b, i)),
                      pl.BlockSpec((1, D), lambda b, i, *_: (0, 0)),
                      pl.BlockSpec(memory_space=pl.ANY)],
            out_specs=pl.BlockSpec((1, TM, D), lambda b, i, *_: (b, i, 0)),
            scratch_shapes=[pltpu.VMEM((2, LROWS, D // 2), u32), pltpu.SemaphoreType.DMA((2,))],
        ),
        out_shape=jax.ShapeDtypeStruct((nbatch, nt * TM, D), f32),
        compiler_params=_cparams(("arbitrary", "arbitrary")),
        name="ffn_residual_final" if final else "ffn_residual",
    )(base, seg, lstart, x, lpos, gates, mod, g_arr, y_sorted)


def _moe_rows(ntok):
    ntiles = ntok // TM
    worst = TOP_K * ntok + ntiles * N_EXPERTS * (SEG_ALIGN - 1)
    return (-(-worst // MOE_BLK) + N_EXPERTS) * MOE_BLK


def _segment_plan(cnt, nblk):
    seg = (cnt + SEG_ALIGN - 1) // SEG_ALIGN * SEG_ALIGN
    lstart = jnp.cumsum(seg, axis=1) - seg
    rows = jnp.sum(seg, axis=0)
    padded = (rows + MOE_BLK - 1) // MOE_BLK * MOE_BLK
    pad_ends = jnp.cumsum(padded)
    base = (pad_ends - padded)[None, :] + jnp.cumsum(seg, axis=0) - seg
    blk_start = jnp.arange(nblk, dtype=i32) * MOE_BLK
    blk_expert = jnp.minimum(jnp.sum(pad_ends[None, :] <= blk_start[:, None], axis=1), N_EXPERTS - 1)
    n_used = (pad_ends[-1] // MOE_BLK).reshape(1)
    flat = lambda a: a.reshape(-1).astype(i32)
    return flat(base), flat(seg), flat(lstart), blk_expert.astype(i32), n_used.astype(i32)


def _rope_tables(tb):
    rows = SEQ // GRID_W
    row = jnp.repeat(jnp.arange(rows, dtype=f32), GRID_W)
    col = jnp.tile(jnp.arange(GRID_W, dtype=f32), rows)
    inv_freq = ROPE_BASE ** (-jnp.arange(ROPE_FREQS, dtype=f32) / ROPE_FREQS)
    ang_r = row[:, None] * inv_freq[None, :]
    ang_c = col[:, None] * inv_freq[None, :]
    ang = jnp.concatenate([ang_r, ang_r, ang_c, ang_c], axis=-1)
    cos = jnp.concatenate([jnp.ones((CTX, HEAD_DIM), f32), jnp.cos(ang)], axis=0)
    sin = jnp.concatenate([jnp.zeros((CTX, HEAD_DIM), f32), jnp.sin(ang)], axis=0)
    return jnp.tile(cos, (1, LANES // HEAD_DIM)), jnp.tile(sin, (1, LANES // HEAD_DIM))


def _block_diag2(w):
    z = jnp.zeros_like(w[0])
    return jnp.concatenate([jnp.concatenate([w[0], z], axis=1), jnp.concatenate([z, w[1]], axis=1)], axis=0)


def kernel(x, c, ctx, c_ctx, w_mod, b_mod, norm_mix_g, norm_ffn_g, w_in, w_out, att_sink, shift_mu_prev, shift_mu_next, decay_w0, decay_w2, iclr_a0, iclr_a2, vres_v0, vres_v1, vres_v2, gate_g2, k_k, k_a, r_k, ln_x_w, ln_x_b, router_w, router_b, expert_w_gate, expert_w_up, expert_w_down, final_norm_g):
    nbatch = x.shape[0]
    depth = w_mod.shape[0]
    tb = ctx.shape[1] + x.shape[1]
    xa = jnp.concatenate([ctx, x], axis=1)
    nb_pad = -(-(nbatch + 1) // 8) * 8
    cond = jnp.zeros((nb_pad, D), f32).at[:nbatch].set(c).at[nbatch].set(c_ctx)
    mod_all = _mod_call(cond, w_mod, b_mod).reshape(depth, nb_pad, 1, 6 * D)
    cos, sin = _rope_tables(tb)
    wr_hi = router_w.T.astype(bf16)
    wr_lo = (router_w.T - wr_hi.astype(f32)).astype(bf16)
    w_router = jnp.stack([wr_hi, wr_lo])
    b_router = router_b.reshape(N_EXPERTS, 1)
    v_first = None
    for l in range(depth):
        mod = mod_all[l]
        q, k, v, rw = _in_proj_call(xa, mod, norm_mix_g[l].reshape(1, D), w_in[l].astype(bf16), cos, sin)
        att = _attn_call(att_sink[l], q, k, v)
        mu = jnp.stack([shift_mu_prev[l], shift_mu_next[l]])
        v0 = vres_v0[l - 1] if l > 0 else jnp.zeros((RW,), f32)
        vec = jnp.stack([k_k[l], k_a[l], r_k[l].reshape(RW), v0,
                         decay_w0[l, 0], decay_w0[l, 1], iclr_a0[l, 0], iclr_a0[l, 1]])
        if l > 0:
            v1 = jnp.zeros((RW, LANES), f32).at[:, :LORA_VRES].set(vres_v1[l - 1]).astype(bf16)
            v2 = jnp.zeros((LANES, RW), f32).at[:LORA_VRES].set(vres_v2[l - 1]).astype(bf16)
        else:
            v1 = v2 = None
        r_, v_, kk, bv, g, kd, lw, bd = _feat_call(
            rw, v_first, mu, vec, _block_diag2(decay_w2[l]).astype(bf16), _block_diag2(iclr_a2[l]).astype(bf16),
            gate_g2[l].astype(bf16), v1, v2)
        if l == 0:
            v_first = v_
        yf, yb = _scan_call(r_, v_, kk, kd, lw, bd)
        ln = jnp.stack([ln_x_w[l], ln_x_b[l]])
        xa, x_local, lpos, gates, cnt = _mix_call(xa, att, yf, yb, bv, g, mod, ln, w_out[l].astype(bf16),
                                                  norm_ffn_g[l].reshape(1, D), w_router, b_router)
        ntok = nbatch * tb
        nrows = _moe_rows(ntok)
        cnt = cnt[:, :, 0].astype(i32).reshape(ntok // TM, N_EXPERTS)
        base, seg, lstart, blk_expert, n_used = _segment_plan(cnt, nrows // MOE_BLK)
        x_sorted = _dispatch_call(base, seg, lstart, x_local.reshape(-1, D // 2), nrows)
        y_sorted = _ffn_call(blk_expert, n_used, x_sorted, expert_w_gate, expert_w_up, expert_w_down, l)
        xa = _combine_call(base, seg, lstart, xa, lpos, gates, mod, y_sorted,
                           final_norm_g.reshape(1, D) if l == depth - 1 else None)
    return xa
```

```python
import functools
import math

import jax
import jax.numpy as jnp
from jax import lax
from jax.experimental import pallas as pl
from jax.experimental.pallas import tpu as pltpu

f32 = jnp.float32
bf16 = jnp.bfloat16
i32 = jnp.int32
u32 = jnp.uint32

D = 1024
SEQ = 4096
CTX = 256
TB = CTX + SEQ
GRID_W = 64
HEAD_DIM = 64
ATT_WIDTH = 512
ATT_HEADS = 8
KV_HEADS = 2
ATT_GROUP = ATT_HEADS // KV_HEADS
KV_WIDTH = KV_HEADS * HEAD_DIM
RW = 512
RWKV_HEADS = 8
LORA_DECAY = 64
LORA_ICLR = 64
LORA_VRES = 32
LORA_GATE = 128
RWKV_COLS = 3 * RW + 2 * (LORA_DECAY + LORA_ICLR) + LORA_GATE
ATT_COLS = ATT_WIDTH + 2 * KV_WIDTH
IN_COLS = ATT_COLS + RWKV_COLS
N_EXPERTS = 16
N_GROUPS = 4
EXPERTS_PER_GROUP = 4
TOP_K = 2
MOE_BLK = 256
NORM_EPS = 1e-6
GN_EPS = 64e-5
NEG_INF = -1e30
ATT_SCALE = HEAD_DIM ** -0.5
ROPE_BASE = 10000.0
ROPE_FREQS = HEAD_DIM // 4

LANES = 128
TM = 256
QB = 128
CH = 64
HG = 4
GW = HG * HEAD_DIM
SCAN_BATCH = 4
MIX_TILES = 2
SEG_ALIGN = 8
LROWS = -(-(TOP_K * TM + N_EXPERTS * SEG_ALIGN) // LANES) * LANES
VMEM_LIMIT = 48 * 1024 * 1024


def _cparams(sem):
    return pltpu.CompilerParams(dimension_semantics=sem, vmem_limit_bytes=VMEM_LIMIT)


def _sigmoid(x):
    return 0.5 * jnp.tanh(0.5 * x) + 0.5


def _div_pow2(x, n):
    assert n & (n - 1) == 0
    return lax.shift_right_logical(x, n.bit_length() - 1)


def _mod_pow2(x, n):
    assert n & (n - 1) == 0
    return lax.bitwise_and(x, n - 1)


def _round_up_pow2(x, n):
    assert n & (n - 1) == 0
    return lax.bitwise_and(x + (n - 1), ~(n - 1))


def _dot(a, b):
    return jnp.dot(a, b, preferred_element_type=f32)


def _dot_nt(a, b):
    return lax.dot_general(a, b, (((1,), (1,)), ((), ())), preferred_element_type=f32)


def _dot_tn(a, b):
    return lax.dot_general(a, b, (((0,), (0,)), ((), ())), preferred_element_type=f32)


def _split2(x):
    hi = x.astype(bf16)
    lo = (x - hi.astype(f32)).astype(bf16)
    return hi, lo


def _dot_split_lhs(x, m):
    hi, lo = _split2(x)
    return _dot(hi, m) + _dot(lo, m)


def _rmsnorm(x, g):
    ms = jnp.mean(x * x, axis=-1, keepdims=True)
    return x * lax.rsqrt(ms + NORM_EPS) * g


def _head_ones():
    r = _div_pow2(lax.broadcasted_iota(i32, (RW, RW), 0), HEAD_DIM)
    c = _div_pow2(lax.broadcasted_iota(i32, (RW, RW), 1), HEAD_DIM)
    return (r == c).astype(bf16)


def _mod_kernel(c_ref, w_ref, b_ref, o_ref):
    c = c_ref[...]
    s = (c * _sigmoid(c)).astype(bf16)
    o_ref[0] = _dot(s, w_ref[0].astype(bf16)) + b_ref[0]


def _mod_call(cond, w_mod, b_mod):
    nb = cond.shape[0]
    depth = w_mod.shape[0]
    tn = 1024
    return pl.pallas_call(
        _mod_kernel,
        grid=(depth, 6 * D // tn),
        in_specs=[
            pl.BlockSpec((nb, D), lambda l, j: (0, 0)),
            pl.BlockSpec((1, D, tn), lambda l, j: (l, 0, j)),
            pl.BlockSpec((1, 1, tn), lambda l, j: (l, 0, j)),
        ],
        out_specs=pl.BlockSpec((1, nb, tn), lambda l, j: (l, 0, j)),
        out_shape=jax.ShapeDtypeStruct((depth, nb, 6 * D), f32),
        compiler_params=_cparams(("arbitrary", "arbitrary")),
        name="mod",
    )(cond, w_mod, b_mod.reshape(depth, 1, 6 * D))


def _mod_index(nbatch, ctx_tiles):
    return lambda b, i: (jnp.where(i < ctx_tiles, nbatch, b), 0, 0)


def _in_proj_kernel(x_ref, mod_ref, g_ref, w_ref, cos_ref, sin_ref, q_ref, k_ref, v_ref, rw_ref):
    x = x_ref[0]
    tm = x.shape[0]
    h = _rmsnorm(x, g_ref[...])
    sh = mod_ref[0, :, 0:D]
    sc = mod_ref[0, :, D:2 * D]
    h = (h * (1.0 + sc) + sh).astype(bf16)
    p = _dot(h, w_ref[...])
    cos = cos_ref[...]
    sin = sin_ref[...]
    lane = lax.broadcasted_iota(i32, (tm, LANES), 1)
    first_half = _mod_pow2(lane, 2 * ROPE_FREQS) < ROPE_FREQS

    def rope(t):
        rot = jnp.where(first_half, -pltpu.roll(t, LANES - ROPE_FREQS, 1), pltpu.roll(t, ROPE_FREQS, 1))
        return t * cos + rot * sin

    for j in range(ATT_WIDTH // LANES):
        t = (rope(p[:, j * LANES:(j + 1) * LANES]) * ATT_SCALE).astype(bf16)
        q_ref[0, 2 * j] = t[:, :HEAD_DIM]
        q_ref[0, 2 * j + 1] = t[:, HEAD_DIM:]
    kt = rope(p[:, ATT_WIDTH:ATT_WIDTH + KV_WIDTH]).astype(bf16)
    vt = p[:, ATT_WIDTH + KV_WIDTH:ATT_COLS].astype(bf16)
    for hh in range(KV_HEADS):
        k_ref[0, hh] = kt[:, hh * HEAD_DIM:(hh + 1) * HEAD_DIM]
        v_ref[0, hh] = vt[:, hh * HEAD_DIM:(hh + 1) * HEAD_DIM]
    rw_ref[0] = p[:, ATT_COLS:]


def _in_proj_call(x, mod, g, w_in, cos, sin):
    nbatch, tb, _ = x.shape
    nt = tb // TM
    return pl.pallas_call(
        _in_proj_kernel,
        grid=(nbatch, nt),
        in_specs=[
            pl.BlockSpec((1, TM, D), lambda b, i: (b, i, 0)),
            pl.BlockSpec((1, 1, 6 * D), _mod_index(nbatch, CTX // TM)),
            pl.BlockSpec((1, D), lambda b, i: (0, 0)),
            pl.BlockSpec((D, IN_COLS), lambda b, i: (0, 0)),
            pl.BlockSpec((TM, LANES), lambda b, i: (i, 0)),
            pl.BlockSpec((TM, LANES), lambda b, i: (i, 0)),
        ],
        out_specs=[
            pl.BlockSpec((1, ATT_HEADS, TM, HEAD_DIM), lambda b, i: (b, 0, i, 0)),
            pl.BlockSpec((1, KV_HEADS, TM, HEAD_DIM), lambda b, i: (b, 0, i, 0)),
            pl.BlockSpec((1, KV_HEADS, TM, HEAD_DIM), lambda b, i: (b, 0, i, 0)),
            pl.BlockSpec((1, TM, RWKV_COLS), lambda b, i: (b, i, 0)),
        ],
        out_shape=[
            jax.ShapeDtypeStruct((nbatch, ATT_HEADS, tb, HEAD_DIM), bf16),
            jax.ShapeDtypeStruct((nbatch, KV_HEADS, tb, HEAD_DIM), bf16),
            jax.ShapeDtypeStruct((nbatch, KV_HEADS, tb, HEAD_DIM), bf16),
            jax.ShapeDtypeStruct((nbatch, tb, RWKV_COLS), f32),
        ],
        compiler_params=_cparams(("parallel", "arbitrary")),
        name="in_proj",
    )(x, mod, g, w_in, cos, sin)


def _attn_kernel(sink_ref, q_ref, kp_ref, km_ref, kn_ref, vp_ref, vm_ref, vn_ref, kx_ref, vx_ref, o_ref,
                 *, npairs, ctx_pairs):
    j = pl.program_id(1)
    is_lat = j >= ctx_pairs
    before_ok = jnp.logical_and(is_lat, j - 1 >= ctx_pairs)
    after_ok = jnp.logical_and(is_lat, j + 1 <= npairs - 1)
    rows = ATT_GROUP * QB
    qi = _mod_pow2(lax.broadcasted_iota(i32, (rows, QB), 0), QB)
    kj = lax.broadcasted_iota(i32, (rows, QB), 1)
    band_p = kj >= qi
    band_n = kj <= qi
    row_head = _div_pow2(lax.broadcasted_iota(i32, (rows, 1), 0), QB)
    half = (slice(0, QB), slice(QB, 2 * QB))
    chains = [(c, h) for c in range(2) for h in range(KV_HEADS)]
    qh, kprev, kcur, knext, vprev, vcur, vnext, ok_p, ok_n = [], [], [], [], [], [], [], [], []
    for c, h in chains:
        qh.append(q_ref[0, ATT_GROUP * h:ATT_GROUP * (h + 1), half[c], :].reshape(rows, HEAD_DIM))
        kcur.append(km_ref[0, h, half[c], :])
        vcur.append(vm_ref[0, h, half[c], :])
        if c == 0:
            kprev.append(kp_ref[0, h]); vprev.append(vp_ref[0, h]); ok_p.append(before_ok)
            knext.append(km_ref[0, h, half[1], :]); vnext.append(vm_ref[0, h, half[1], :]); ok_n.append(is_lat)
        else:
            kprev.append(km_ref[0, h, half[0], :]); vprev.append(vm_ref[0, h, half[0], :]); ok_p.append(is_lat)
            knext.append(kn_ref[0, h]); vnext.append(vn_ref[0, h]); ok_n.append(after_ok)
    n = range(len(chains))
    s_p = [jnp.where(jnp.logical_and(band_p, ok_p[i]), _dot_nt(qh[i], kprev[i]), NEG_INF) for i in n]
    s_c = [jnp.where(is_lat, _dot_nt(qh[i], kcur[i]), NEG_INF) for i in n]
    s_n = [jnp.where(jnp.logical_and(band_n, ok_n[i]), _dot_nt(qh[i], knext[i]), NEG_INF) for i in n]
    s_x = [_dot_nt(qh[i], kx_ref[0, chains[i][1]]) for i in n]
    sink = []
    for h in range(KV_HEADS):
        sk = jnp.zeros((rows, 1), f32)
        for g in range(ATT_GROUP):
            sk = jnp.where(row_head == g, sink_ref[ATT_GROUP * h + g], sk)
        sink.append(sk)
    sink = [sink[h] for _, h in chains]
    m = [jnp.maximum(jnp.max(jnp.maximum(jnp.maximum(s_p[i], s_c[i]),
                                         jnp.maximum(jnp.maximum(s_n[i], s_x[i][:, :QB]), s_x[i][:, QB:])),
                             axis=-1, keepdims=True), sink[i]) for i in n]
    e_p = [jnp.exp(s_p[i] - m[i]) for i in n]
    e_c = [jnp.exp(s_c[i] - m[i]) for i in n]
    e_n = [jnp.exp(s_n[i] - m[i]) for i in n]
    e_x = [jnp.exp(s_x[i] - m[i]) for i in n]
    den = [jnp.sum((e_p[i] + e_c[i]) + (e_n[i] + e_x[i][:, :QB]) + e_x[i][:, QB:], axis=-1, keepdims=True)
           + jnp.exp(sink[i] - m[i]) for i in n]
    o = [(_dot(e_p[i].astype(bf16), vprev[i]) + _dot(e_c[i].astype(bf16), vcur[i])
          + _dot(e_n[i].astype(bf16), vnext[i]) + _dot(e_x[i].astype(bf16), vx_ref[0, chains[i][1]])) / den[i]
         for i in n]
    for c in range(2):
        o_ref[0, half[c], :] = jnp.concatenate(
            [o[c * KV_HEADS + h][g * QB:(g + 1) * QB] for h in range(KV_HEADS) for g in range(ATT_GROUP)],
            axis=1).astype(bf16)


def _attn_call(sink, q, k, v):
    nbatch, _, tb, _ = q.shape
    nblk = tb // QB
    npairs = nblk // 2
    assert CTX == 2 * QB and nblk % 2 == 0
    kv_blk = (1, KV_HEADS, QB, HEAD_DIM)
    pair_blk = (1, KV_HEADS, 2 * QB, HEAD_DIM)
    before_map = lambda b, j: (b, 0, jnp.maximum(2 * j - 1, 0), 0)
    pair_map = lambda b, j: (b, 0, j, 0)
    after_map = lambda b, j: (b, 0, jnp.minimum(2 * j + 2, nblk - 1), 0)
    ctx_spec = pl.BlockSpec((1, KV_HEADS, CTX, HEAD_DIM), lambda b, j: (b, 0, 0, 0))
    return pl.pallas_call(
        functools.partial(_attn_kernel, npairs=npairs, ctx_pairs=CTX // (2 * QB)),
        grid=(nbatch, npairs),
        in_specs=[
            pl.BlockSpec(memory_space=pltpu.SMEM),
            pl.BlockSpec((1, ATT_HEADS, 2 * QB, HEAD_DIM), pair_map),
            pl.BlockSpec(kv_blk, before_map), pl.BlockSpec(pair_blk, pair_map), pl.BlockSpec(kv_blk, after_map),
            pl.BlockSpec(kv_blk, before_map), pl.BlockSpec(pair_blk, pair_map), pl.BlockSpec(kv_blk, after_map),
            ctx_spec, ctx_spec,
        ],
        out_specs=pl.BlockSpec((1, 2 * QB, ATT_WIDTH), lambda b, j: (b, j, 0)),
        out_shape=jax.ShapeDtypeStruct((nbatch, tb, ATT_WIDTH), bf16),
        compiler_params=_cparams(("parallel", "arbitrary")),
        name="attention",
    )(sink, q, k, k, k, v, v, v, k, v)


def _feat_kernel(*refs, nt, ctx_tiles, has_vres):
    if has_vres:
        (rw_ref, hp_ref, hn_ref, vf_ref, mu_ref, vec_ref, w2_ref, a2_ref, g2_ref, v1_ref, v2_ref,
         r_ref, v_ref, kk_ref, bv_ref, g_ref, kd_ref, lw_ref, bd_ref) = refs
    else:
        (rw_ref, hp_ref, hn_ref, mu_ref, vec_ref, w2_ref, a2_ref, g2_ref,
         r_ref, v_ref, kk_ref, bv_ref, g_ref, kd_ref, lw_ref, bd_ref) = refs
    i = pl.program_id(1)
    u0 = rw_ref[0]
    tm = u0.shape[0]
    prev_zero = jnp.logical_or(i == 0, i == ctx_tiles)
    next_zero = jnp.logical_or(i == ctx_tiles - 1, i == nt - 1)
    halo_p = jnp.where(prev_zero, 0.0, hp_ref[0, 7:8, :])
    halo_n = jnp.where(next_zero, 0.0, hn_ref[0, 0:1, :])
    row = lax.broadcasted_iota(i32, (tm, 1), 0)
    prev = jnp.where(row == 0, halo_p, pltpu.roll(u0, 1, 0))
    nxt = jnp.where(row == tm - 1, halo_n, pltpu.roll(u0, tm - 1, 0))
    mu_p = mu_ref[0:1, :]
    mu_n = mu_ref[1:2, :]
    u = u0 + mu_p * (prev - u0) + mu_n * (nxt - u0)

    r = u[:, 0:RW]
    k = u[:, RW:2 * RW]
    v = u[:, 2 * RW:3 * RW]
    wd = u[:, 3 * RW:3 * RW + 2 * LORA_DECAY]
    ad = u[:, 3 * RW + 2 * LORA_DECAY:3 * RW + 2 * (LORA_DECAY + LORA_ICLR)]
    gd = u[:, 3 * RW + 2 * (LORA_DECAY + LORA_ICLR):]
    k_k = vec_ref[0:1, :]
    k_a = vec_ref[1:2, :]
    r_k = vec_ref[2:3, :]
    ones = _head_ones()

    if has_vres:
        lo = _dot(v.astype(bf16), v1_ref[...])
        gate = _sigmoid(vec_ref[3:4, :] + _dot(lo.astype(bf16), v2_ref[...]))
        v = v + (vf_ref[0] - v) * gate
    decay_in = _dot(jnp.tanh(wd).astype(bf16), w2_ref[...])
    a_in = _dot(ad.astype(bf16), a2_ref[...])
    kk = k * k_k
    n2 = _dot_split_lhs(kk * kk, ones)
    kk = kk * lax.rsqrt(jnp.maximum(n2, 1e-24))
    g = _dot(_sigmoid(gd).astype(bf16), g2_ref[...])
    ksum = jnp.zeros_like(k)
    for d in range(2):
        w0 = vec_ref[4 + d:5 + d, :]
        a0 = vec_ref[6 + d:7 + d, :]
        lw = -_sigmoid(w0 + decay_in[:, d * RW:(d + 1) * RW]) * math.exp(-0.5)
        a = _sigmoid(a0 + a_in[:, d * RW:(d + 1) * RW])
        kd = k * (1.0 + (a - 1.0) * k_a)
        ksum = ksum + kd
        kd_ref[d, 0] = kd.astype(bf16)
        lw_ref[d, 0] = lw
        bd_ref[d, 0] = (kk * a).astype(bf16)
    bonus = _dot_split_lhs(r * ksum * r_k, ones)
    r_ref[0] = r.astype(bf16)
    v_ref[0] = v
    kk_ref[0] = kk.astype(bf16)
    bv_ref[0] = bonus * v
    g_ref[0] = g.astype(bf16)


def _feat_call(rw, v_first, mu, vec, w2bd, a2bd, g2, v1, v2):
    nbatch, tb, _ = rw.shape
    nt = tb // TM
    has_vres = v_first is not None
    sub = TM // 8
    tile = lambda w: pl.BlockSpec((1, TM, w), lambda b, i: (b, i, 0))
    full = lambda a: pl.BlockSpec(a.shape, lambda b, i: (0,) * a.ndim)
    in_specs = [
        tile(RWKV_COLS),
        pl.BlockSpec((1, 8, RWKV_COLS), lambda b, i: (b, jnp.maximum(i * sub - 1, 0), 0)),
        pl.BlockSpec((1, 8, RWKV_COLS), lambda b, i: (b, jnp.minimum((i + 1) * sub, tb // 8 - 1), 0)),
    ]
    args = [rw, rw, rw]
    if has_vres:
        in_specs.append(tile(RW))
        args.append(v_first)
    consts = [mu, vec, w2bd, a2bd, g2] + ([v1, v2] if has_vres else [])
    in_specs += [full(a) for a in consts]
    args += consts
    dir_spec = pl.BlockSpec((2, 1, TM, RW), lambda b, i: (0, b, i, 0))
    tok = lambda dt: jax.ShapeDtypeStruct((nbatch, tb, RW), dt)
    dtok = lambda dt: jax.ShapeDtypeStruct((2, nbatch, tb, RW), dt)
    return pl.pallas_call(
        functools.partial(_feat_kernel, nt=nt, ctx_tiles=CTX // TM, has_vres=has_vres),
        grid=(nbatch, nt),
        in_specs=in_specs,
        out_specs=[tile(RW)] * 5 + [dir_spec] * 3,
        out_shape=[tok(bf16), tok(f32), tok(bf16), tok(f32), tok(bf16), dtok(bf16), dtok(f32), dtok(bf16)],
        compiler_params=_cparams(("parallel", "arbitrary")),
        name="rwkv_features",
    )(*args)


def _block_diag_rows(x, width):
    cb = _div_pow2(lax.broadcasted_iota(i32, x.shape, 1), width)
    return jnp.concatenate([jnp.where(cb == h, x, jnp.zeros_like(x)) for h in range(HG)], axis=0)


def _scan_chunks(probs):
    wide = HG * CH
    row_t = lax.broadcasted_iota(i32, (CH, wide), 0)
    col_t = _mod_pow2(lax.broadcasted_iota(i32, (CH, wide), 1), CH)
    row_g = lax.broadcasted_iota(i32, (CH, GW), 0)
    incl_t = {False: col_t <= row_t, True: col_t >= row_t}
    strict_t = {False: col_t < row_t, True: col_t > row_t}
    eye_t = (row_t == col_t).astype(f32)
    rb = _div_pow2(lax.broadcasted_iota(i32, (GW, GW), 0), HEAD_DIM)
    cb = _div_pow2(lax.broadcasted_iota(i32, (GW, GW), 1), HEAD_DIM)
    n = len(probs)
    rev = [p[7] for p in probs]
    def cumsum_rows(x, reverse):
        s = 1
        while s < CH:
            if reverse:
                x = x + jnp.where(row_g < CH - s, pltpu.roll(x, CH - s, 0), 0.0)
            else:
                x = x + jnp.where(row_g >= s, pltpu.roll(x, s, 0), 0.0)
            s *= 2
        return x

    gam = [cumsum_rows(probs[i][5], rev[i]) for i in range(n)]
    ar, bk, k_t, b_t = [], [], [], []
    for i, (s_prev, r, v, kk, k, lw, b, _) in enumerate(probs):
        e_neg = jnp.exp(-gam[i])
        a_s = (-kk * jnp.exp(gam[i] - lw)).astype(bf16)
        r_s = (r * jnp.exp(gam[i])).astype(bf16)
        b_t.append(b * e_neg)
        k_t.append(k * e_neg)
        ar.append(jnp.concatenate([a_s, r_s], axis=0))
        bk.append(jnp.concatenate([_block_diag_rows(b_t[i].astype(bf16), HEAD_DIM),
                                   _block_diag_rows(k_t[i].astype(bf16), HEAD_DIM)], axis=0))
    gram = [_dot_nt(ar[i], bk[i]) for i in range(n)]
    ars = [_dot_nt(ar[i], probs[i][0].astype(bf16)) for i in range(n)]
    v_bd = [_block_diag_rows(probs[i][2].astype(bf16), HEAD_DIM) for i in range(n)]
    p0 = [jnp.where(strict_t[rev[i]], gram[i][:CH, :wide], 0.0).astype(bf16) for i in range(n)]
    lq = [jnp.concatenate([jnp.where(strict_t[rev[i]], gram[i][:CH, wide:], 0.0),
                           jnp.where(incl_t[rev[i]], gram[i][CH:, wide:], 0.0)], axis=0).astype(bf16)
          for i in range(n)]
    lqv = [_dot(lq[i], v_bd[i]) for i in range(n)]
    rhs = [ars[i][:CH] + lqv[i][:CH] for i in range(n)]
    t = [eye_t + p0[i].astype(f32) for i in range(n)]
    p = [_dot(p0[i], _block_diag_rows(p0[i], CH)).astype(bf16) for i in range(n)]
    m = 4
    while m < CH:
        tp = [_dot(jnp.concatenate([t[i].astype(bf16), p[i]], axis=0), _block_diag_rows(p[i], CH))
              for i in range(n)]
        t = [t[i] + tp[i][:CH] for i in range(n)]
        p = [tp[i][CH:].astype(bf16) for i in range(n)]
        m *= 2
    t = [t[i] + _dot(t[i].astype(bf16), _block_diag_rows(p[i], CH)) for i in range(n)]
    u = [_dot(t[i].astype(bf16), _block_diag_rows(rhs[i].astype(bf16), HEAD_DIM)) for i in range(n)]
    out = []
    for i in range(n):
        s_prev, v, lw = probs[i][0], probs[i][2], probs[i][5]
        q_b = jnp.where(incl_t[rev[i]], gram[i][CH:, :wide], 0.0).astype(bf16)
        u_bd = _block_diag_rows(u[i].astype(bf16), HEAD_DIM)
        y = ars[i][CH:] + lqv[i][CH:] + _dot(q_b, u_bd)
        eg = jnp.exp(jnp.sum(lw, axis=0, keepdims=True))
        vu = jnp.concatenate([v, u[i]], axis=0).astype(bf16)
        kb = jnp.concatenate([k_t[i] * eg, b_t[i] * eg], axis=0).astype(bf16)
        s_add = _dot_tn(vu, kb)
        out.append((s_prev * eg + jnp.where(rb == cb, s_add, 0.0), y))
    return out


def _scan_kernel(rf_ref, vf_ref, kkf_ref, kf_ref, lwf_ref, bf_ref,
                 rb_ref, vb_ref, kkb_ref, kb_ref, lwb_ref, bb_ref,
                 yf_ref, yb_ref, s_ref):
    @pl.when(pl.program_id(1) == 0)
    def _():
        s_ref[...] = jnp.zeros_like(s_ref)

    dirs = ((rf_ref, vf_ref, kkf_ref, kf_ref, lwf_ref, bf_ref, yf_ref),
            (rb_ref, vb_ref, kkb_ref, kb_ref, lwb_ref, bb_ref, yb_ref))
    probs, dest = [], []
    for bi in range(SCAN_BATCH):
        for d, (r_ref, v_ref, kk_ref, k_ref, lw_ref, b_ref, y_ref) in enumerate(dirs):
            for g in range(RW // GW):
                sl = slice(g * GW, (g + 1) * GW)
                probs.append((s_ref[bi, d, g], r_ref[bi, :, sl], v_ref[bi, :, sl], kk_ref[bi, :, sl],
                              k_ref[0, bi, :, sl], lw_ref[0, bi, :, sl], b_ref[0, bi, :, sl], d == 1))
                dest.append((bi, d, g, y_ref, sl))
    for (bi, d, g, y_ref, sl), (s_new, y) in zip(dest, _scan_chunks(probs)):
        s_ref[bi, d, g] = s_new
        y_ref[bi, :, sl] = y


def _scan_call(r, v, kk, kd, lw, bd):
    nbatch, tb, _ = r.shape
    nc = tb // CH
    cc = CTX // CH
    sb = SCAN_BATCH
    rev = lambda j: jnp.where(j < cc, cc - 1 - j, nc - 1 + cc - j)
    tok_f = pl.BlockSpec((sb, CH, RW), lambda b, j: (b, j, 0))
    tok_b = pl.BlockSpec((sb, CH, RW), lambda b, j: (b, rev(j), 0))
    dir_f = pl.BlockSpec((1, sb, CH, RW), lambda b, j: (0, b, j, 0))
    dir_b = pl.BlockSpec((1, sb, CH, RW), lambda b, j: (1, b, rev(j), 0))
    out = jax.ShapeDtypeStruct((nbatch, tb, RW), f32)
    return pl.pallas_call(
        _scan_kernel,
        grid=(nbatch // sb, nc),
        in_specs=[tok_f, tok_f, tok_f, dir_f, dir_f, dir_f, tok_b, tok_b, tok_b, dir_b, dir_b, dir_b],
        out_specs=[tok_f, tok_b],
        out_shape=[out, out],
        scratch_shapes=[pltpu.VMEM((sb, 2, RW // GW, GW, GW), f32)],
        compiler_params=_cparams(("parallel", "arbitrary")),
        name="rwkv_scan",
    )(r, v, kk, kd, lw, bd, r, v, kk, kd, lw, bd)


def _pack_bf16_pairs(x):
    bits = pltpu.bitcast(x, u32)
    half = x.shape[1] // 2
    return bits[:, :half] | lax.shift_right_logical(bits[:, half:], jnp.uint32(16))


def _unpack_bf16_pairs(p):
    hi = pltpu.bitcast(p & jnp.uint32(0xFFFF0000), f32)
    lo = pltpu.bitcast(lax.shift_left(p, jnp.uint32(16)), f32)
    return jnp.concatenate([hi, lo], axis=1).astype(bf16)


def _mix_kernel(x_ref, att_ref, yf_ref, yb_ref, bv_ref, g_ref, mod0_ref, mod1_ref, ln_ref, wo_ref, gf_ref,
                wr_ref, rb_ref, xo_ref, xl_ref, lp_ref, gt_ref, cnt_ref):
    ones = _head_ones()
    inv = 1.0 / HEAD_DIM
    tiles = range(MIX_TILES)
    rows = [slice(s * TM, (s + 1) * TM) for s in tiles]
    mods = (mod0_ref, mod1_ref)
    y = [yf_ref[rows[s], :] + yb_ref[rows[s], :] for s in tiles]
    mu = [_dot(y[s].astype(bf16), ones) * inv for s in tiles]
    dlt = [y[s] - mu[s] for s in tiles]
    var = [_dot((dlt[s] * dlt[s]).astype(bf16), ones) * inv for s in tiles]
    rwk = [((dlt[s] * lax.rsqrt(var[s] + GN_EPS) * ln_ref[0:1, :] + ln_ref[1:2, :] + bv_ref[rows[s], :])
            * g_ref[rows[s], :]).astype(bf16) for s in tiles]
    mix = [_dot(att_ref[rows[s], :], wo_ref[0:ATT_WIDTH, :]) + _dot(rwk[s], wo_ref[ATT_WIDTH:, :]) for s in tiles]
    xn = [x_ref[rows[s], :] + mods[s][0, :, 2 * D:3 * D] * mix[s] for s in tiles]
    for s in tiles:
        xo_ref[rows[s], :] = xn[s]
    hf = [_rmsnorm(xn[s], gf_ref[...]) * (1.0 + mods[s][0, :, 4 * D:5 * D]) + mods[s][0, :, 3 * D:4 * D]
          for s in tiles]
    split = [_split2(hf[s]) for s in tiles]
    logits = [_dot_nt(wr_ref[0], split[s][0]) + _dot_nt(wr_ref[0], split[s][1]) + _dot_nt(wr_ref[1], split[s][0])
              for s in tiles]
    sorted_rows, lpos, gates, cnt = _route_and_sort(hf, [_sigmoid(lg) for lg in logits], rb_ref[...])
    for s in tiles:
        xl_ref[s * LROWS:(s + 1) * LROWS, :] = sorted_rows[s]
        lp_ref[:, rows[s]] = lpos[s]
        gt_ref[:, rows[s]] = gates[s]
        cnt_ref[s * N_EXPERTS:(s + 1) * N_EXPERTS, :] = cnt[s]


def _route_and_sort(hf, scores, bias):
    tiles = range(len(hf))
    tm = hf[0].shape[0]
    routed = [_route_rows(scores[s], bias) for s in tiles]
    eio = lax.broadcasted_iota(i32, (N_EXPERTS, tm), 0)
    upper = (lax.broadcasted_iota(i32, (tm, tm), 0) < lax.broadcasted_iota(i32, (tm, tm), 1)).astype(bf16)
    lower = (lax.broadcasted_iota(i32, (N_EXPERTS, N_EXPERTS), 0)
             > lax.broadcasted_iota(i32, (N_EXPERTS, N_EXPERTS), 1)).astype(bf16)
    jj = lax.broadcasted_iota(i32, (LROWS, tm), 0)
    oh0 = [(eio == routed[s][0]).astype(f32) for s in tiles]
    oh1 = [(eio == routed[s][1]).astype(f32) for s in tiles]
    both = [oh0[s] + oh1[s] for s in tiles]
    before = [_dot(both[s].astype(bf16), upper) for s in tiles]
    cnt = [jnp.sum(both[s], axis=1, keepdims=True) for s in tiles]
    seg = [_round_up_pow2(cnt[s].astype(i32), SEG_ALIGN).astype(f32) for s in tiles]
    seg_start = [_dot(lower, jnp.broadcast_to(seg[s], (N_EXPERTS, tm)).astype(bf16)) for s in tiles]
    pos = [seg_start[s] + before[s] for s in tiles]
    lp0 = [jnp.sum(oh0[s] * pos[s], axis=0, keepdims=True).astype(i32) for s in tiles]
    lp1 = [jnp.sum(oh1[s] * pos[s], axis=0, keepdims=True).astype(i32) for s in tiles]
    sel_t = [jnp.logical_or(jj == lp0[s], jj == lp1[s]).astype(bf16) for s in tiles]
    sorted_rows = [_pack_bf16_pairs(_dot(sel_t[s], hf[s].astype(bf16))) for s in tiles]
    lpos = [jnp.concatenate([lp0[s], lp1[s]], axis=0) for s in tiles]
    gates = [jnp.concatenate([routed[s][2], routed[s][3]], axis=0) for s in tiles]
    cnt_out = [jnp.broadcast_to(cnt[s], (N_EXPERTS, LANES)) for s in tiles]
    return sorted_rows, lpos, gates, cnt_out


def _route_rows(scores, bias):
    biased = scores + bias
    row = lambda a, e: a[e:e + 1, :]
    best = None
    for gi in range(N_GROUPS):
        m = [row(biased, gi * EXPERTS_PER_GROUP + j) for j in range(EXPERTS_PER_GROUP)]
        gs = None
        for a in range(EXPERTS_PER_GROUP):
            for b in range(a + 1, EXPERTS_PER_GROUP):
                pair = m[a] + m[b]
                gs = pair if gs is None else jnp.maximum(gs, pair)
        if best is None:
            best, g_idx = gs, jnp.zeros(gs.shape, i32)
        else:
            better = gs > best
            g_idx = jnp.where(better, gi, g_idx)
            best = jnp.where(better, gs, best)

    def pick(a, j):
        out = row(a, j)
        for gi in range(1, N_GROUPS):
            out = jnp.where(g_idx == gi, row(a, gi * EXPERTS_PER_GROUP + j), out)
        return out

    vb = [pick(biased, j) for j in range(EXPERTS_PER_GROUP)]
    vs = [pick(scores, j) for j in range(EXPERTS_PER_GROUP)]

    def argmax_first(vals):
        bv, bi = vals[0], jnp.zeros(vals[0].shape, i32)
        for j in range(1, len(vals)):
            better = vals[j] > bv
            bi = jnp.where(better, j, bi)
            bv = jnp.where(better, vals[j], bv)
        return bi

    i1 = argmax_first(vb)
    i2 = argmax_first([jnp.where(i1 == j, -jnp.inf, vb[j]) for j in range(EXPERTS_PER_GROUP)])
    sel = lambda idx: sum(jnp.where(idx == j, vs[j], 0.0) for j in range(EXPERTS_PER_GROUP))
    s1, s2 = sel(i1), sel(i2)
    tot = s1 + s2
    base = g_idx * EXPERTS_PER_GROUP
    return base + i1, base + i2, s1 / tot, s2 / tot


def _mix_call(x, att, yf, yb, bv, g, mod, ln, w_out, g_ffn, w_router, b_router):
    nbatch, tb, _ = x.shape
    tiles_per_batch = tb // TM
    ntiles = nbatch * tiles_per_batch
    assert ntiles % MIX_TILES == 0 and MIX_TILES == 2
    ntok = nbatch * tb
    flat = lambda a: a.reshape(ntok, a.shape[-1])
    rows = lambda w: pl.BlockSpec((MIX_TILES * TM, w), lambda p: (p, 0))
    full = lambda a: pl.BlockSpec(a.shape, lambda p: (0,) * a.ndim)
    ctx_tiles = CTX // TM

    def mod_spec(s):
        def index(p):
            tile = MIX_TILES * p + s
            return (jnp.where(tile % tiles_per_batch < ctx_tiles, nbatch, tile // tiles_per_batch), 0, 0)
        return pl.BlockSpec((1, 1, 6 * D), index)

    route = pl.BlockSpec((TOP_K, MIX_TILES * TM), lambda p: (0, p))
    return pl.pallas_call(
        _mix_kernel,
        grid=(ntiles // MIX_TILES,),
        in_specs=[rows(D), rows(ATT_WIDTH), rows(RW), rows(RW), rows(RW), rows(RW), mod_spec(0), mod_spec(1),
                  full(ln), full(w_out), full(g_ffn), full(w_router), full(b_router)],
        out_specs=[rows(D), pl.BlockSpec((MIX_TILES * LROWS, D // 2), lambda p: (p, 0)), route, route,
                   pl.BlockSpec((MIX_TILES * N_EXPERTS, LANES), lambda p: (p, 0))],
        out_shape=[jax.ShapeDtypeStruct((ntok, D), f32),
                   jax.ShapeDtypeStruct((ntiles * LROWS, D // 2), u32),
                   jax.ShapeDtypeStruct((TOP_K, ntok), i32),
                   jax.ShapeDtypeStruct((TOP_K, ntok), f32),
                   jax.ShapeDtypeStruct((ntiles * N_EXPERTS, LANES), f32)],
        compiler_params=_cparams(("arbitrary",)),
        name="mix_out",
    )(flat(x), flat(att), flat(yf), flat(yb), flat(bv), flat(g), mod, mod, ln, w_out, g_ffn, w_router, b_router)


def _segment_copies(fn, tile, base_ref, seg_ref, ls_ref, src, dst, sem, src_is_global):
    for e in range(N_EXPERTS):
        idx = tile * N_EXPERTS + e
        seg = seg_ref[idx]
        g0 = base_ref[idx]
        l0 = ls_ref[idx]
        size = TM
        while size >= SEG_ALIGN:
            done = lax.bitwise_and(seg, ~(2 * size - 1))

            @pl.when(lax.bitwise_and(seg, size) != 0)
            def _():
                g_rows = pl.ds(pl.multiple_of(g0 + done, SEG_ALIGN), size)
                l_rows = pl.ds(pl.multiple_of(l0 + done, SEG_ALIGN), size)
                s_rows, d_rows = (g_rows, l_rows) if src_is_global else (l_rows, g_rows)
                fn(pltpu.make_async_copy(src.at[s_rows, :], dst.at[d_rows, :], sem))

            size //= 2


def _dispatch_kernel(base_ref, seg_ref, ls_ref, xl_ref, xs_in_ref, xs_ref, sem):
    del xs_in_ref
    tile = pl.program_id(0)
    args = (tile, base_ref, seg_ref, ls_ref, xl_ref, xs_ref, sem, False)
    _segment_copies(lambda cp: cp.start(), *args)
    _segment_copies(lambda cp: cp.wait(), *args)


def _dispatch_call(base, seg, lstart, x_local, nrows):
    ntiles = x_local.shape[0] // LROWS
    any_spec = pl.BlockSpec(memory_space=pl.ANY)
    return pl.pallas_call(
        _dispatch_kernel,
        grid_spec=pltpu.PrefetchScalarGridSpec(
            num_scalar_prefetch=3,
            grid=(ntiles,),
            in_specs=[pl.BlockSpec((LROWS, D // 2), lambda t, *_: (t, 0)), any_spec],
            out_specs=any_spec,
            scratch_shapes=[pltpu.SemaphoreType.DMA(())],
        ),
        out_shape=jax.ShapeDtypeStruct((nrows, D // 2), u32),
        input_output_aliases={4: 0},
        compiler_params=_cparams(("arbitrary",)),
        name="moe_dispatch",
    )(base, seg, lstart, x_local, jnp.zeros((nrows, D // 2), u32))


def _ffn_kernel(be_ref, nu_ref, x_ref, wg_ref, wu_ref, wd_ref, y_ref, wg_bf, wu_bf, wd_bf):
    i = pl.program_id(0)
    used = i < nu_ref[0]
    new_expert = jnp.logical_or(i == 0, be_ref[i] != be_ref[jnp.maximum(i - 1, 0)])

    @pl.when(jnp.logical_and(used, new_expert))
    def _():
        wg_bf[...] = wg_ref[0, 0].astype(bf16)
        wu_bf[...] = wu_ref[0, 0].astype(bf16)
        wd_bf[...] = wd_ref[0, 0].astype(bf16)

    @pl.when(used)
    def _():
        x = _unpack_bf16_pairs(x_ref[...])
        gt = _dot(x, wg_bf[...])
        up = _dot(x, wu_bf[...])
        hid = (gt * _sigmoid(gt) * up).astype(bf16)
        y = _dot(hid, wd_bf[...])
        y_ref[...] = _pack_bf16_pairs(y.astype(bf16).astype(f32))

    @pl.when(i >= nu_ref[0])
    def _():
        y_ref[...] = jnp.zeros_like(y_ref)


def _ffn_call(blk_expert, n_used, x_sorted, wg, wu, wd, layer):
    nrows = x_sorted.shape[0]
    nblk = nrows // MOE_BLK
    wspec = pl.BlockSpec((1, 1, D, D), lambda i, be, nu: (layer, be[i], 0, 0))
    return pl.pallas_call(
        _ffn_kernel,
        grid_spec=pltpu.PrefetchScalarGridSpec(
            num_scalar_prefetch=2,
            grid=(nblk,),
            in_specs=[pl.BlockSpec((MOE_BLK, D // 2), lambda i, be, nu: (i, 0)), wspec, wspec, wspec],
            out_specs=pl.BlockSpec((MOE_BLK, D // 2), lambda i, be, nu: (i, 0)),
            scratch_shapes=[pltpu.VMEM((D, D), bf16)] * 3,
        ),
        out_shape=jax.ShapeDtypeStruct((nrows, D // 2), u32),
        compiler_params=_cparams(("arbitrary",)),
        name="moe_ffn",
    )(blk_expert, n_used, x_sorted, wg, wu, wd)


def _combine_kernel(base_ref, seg_ref, ls_ref, x_ref, lp_ref, gt_ref, mod_ref, g_ref, y_ref, o_ref, ybuf, sem,
                    *, tiles_per_batch, first_tile, final):
    b = pl.program_id(0)
    i = pl.program_id(1)
    nt = pl.num_programs(1)
    step = b * nt + i
    slot = lax.rem(step, 2)
    tile = b * tiles_per_batch + i + first_tile
    next_tile = jnp.where(i + 1 < nt, tile + 1, (b + 1) * tiles_per_batch + first_tile)

    def copies(fn, which_tile, which_slot):
        _segment_copies(fn, which_tile, base_ref, seg_ref, ls_ref, y_ref, ybuf.at[which_slot],
                        sem.at[which_slot], True)

    @pl.when(step == 0)
    def _():
        ybuf[...] = jnp.zeros_like(ybuf)
        copies(lambda cp: cp.start(), tile, slot)

    @pl.when(step + 1 < pl.num_programs(0) * nt)
    def _():
        copies(lambda cp: cp.start(), next_tile, 1 - slot)

    copies(lambda cp: cp.wait(), tile, slot)
    y_loc = _unpack_bf16_pairs(ybuf[slot])
    tm = x_ref.shape[1]
    jj = lax.broadcasted_iota(i32, (LROWS, tm), 0)
    gmat = (jnp.where(jj == lp_ref[0:1, :], gt_ref[0:1, :], 0.0)
            + jnp.where(jj == lp_ref[1:2, :], gt_ref[1:2, :], 0.0))
    g_hi, g_lo = _split2(gmat)
    moe = _dot_tn(g_hi, y_loc) + _dot_tn(g_lo, y_loc)
    xn = x_ref[0] + mod_ref[0, :, 5 * D:6 * D] * moe
    o_ref[0] = _rmsnorm(xn, g_ref[...]) if final else xn


def _combine_call(base, seg, lstart, x, lpos, gates, mod, y_sorted, final_g):
    nbatch, tb, _ = x.shape
    ctx_tiles = CTX // TM
    tiles_per_batch = tb // TM
    final = final_g is not None
    first_tile = ctx_tiles if final else 0
    nt = tiles_per_batch - first_tile
    tile = pl.BlockSpec((1, TM, D), lambda b, i, *_: (b, i + first_tile, 0))
    route = pl.BlockSpec((TOP_K, TM), lambda b, i, *_: (0, b * tiles_per_batch + i + first_tile))
    mod_map = _mod_index(nbatch, ctx_tiles - first_tile)
    g_arr = final_g if final else jnp.ones((1, D), f32)
    return pl.pallas_call(
        functools.partial(_combine_kernel, tiles_per_batch=tiles_per_batch, first_tile=first_tile, final=final),
        grid_spec=pltpu.PrefetchScalarGridSpec(
            num_scalar_prefetch=3,
            grid=(nbatch, nt),
            in_specs=[tile, route, route,
                      pl.BlockSpec((1, 1, 6 * D), lambda b, i, *_: mod_map(b, i)),
                      pl.BlockSpec((1, D), lambda b, i, *_: (0, 0)),
                      pl.BlockSpec(memory_space=pl.ANY)],
            out_specs=pl.BlockSpec((1, TM, D), lambda b, i, *_: (b, i, 0)),
            scratch_shapes=[pltpu.VMEM((2, LROWS, D // 2), u32), pltpu.SemaphoreType.DMA((2,))],
        ),
        out_shape=jax.ShapeDtypeStruct((nbatch, nt * TM, D), f32),
        compiler_params=_cparams(("arbitrary", "arbitrary")),
        name="ffn_residual_final" if final else "ffn_residual",
    )(base, seg, lstart, x, lpos, gates, mod, g_arr, y_sorted)


def _moe_rows(ntok):
    ntiles = ntok // TM
    worst = TOP_K * ntok + ntiles * N_EXPERTS * (SEG_ALIGN - 1)
    return (-(-worst // MOE_BLK) + N_EXPERTS) * MOE_BLK


def _segment_plan(cnt, nblk):
    seg = (cnt + SEG_ALIGN - 1) // SEG_ALIGN * SEG_ALIGN
    lstart = jnp.cumsum(seg, axis=1) - seg
    rows = jnp.sum(seg, axis=0)
    padded = (rows + MOE_BLK - 1) // MOE_BLK * MOE_BLK
    pad_ends = jnp.cumsum(padded)
    base = (pad_ends - padded)[None, :] + jnp.cumsum(seg, axis=0) - seg
    blk_start = jnp.arange(nblk, dtype=i32) * MOE_BLK
    blk_expert = jnp.minimum(jnp.sum(pad_ends[None, :] <= blk_start[:, None], axis=1), N_EXPERTS - 1)
    n_used = (pad_ends[-1] // MOE_BLK).reshape(1)
    flat = lambda a: a.reshape(-1).astype(i32)
    return flat(base), flat(seg), flat(lstart), blk_expert.astype(i32), n_used.astype(i32)


def _rope_tables(tb):
    rows = SEQ // GRID_W
    row = jnp.repeat(jnp.arange(rows, dtype=f32), GRID_W)
    col = jnp.tile(jnp.arange(GRID_W, dtype=f32), rows)
    inv_freq = ROPE_BASE ** (-jnp.arange(ROPE_FREQS, dtype=f32) / ROPE_FREQS)
    ang_r = row[:, None] * inv_freq[None, :]
    ang_c = col[:, None] * inv_freq[None, :]
    ang = jnp.concatenate([ang_r, ang_r, ang_c, ang_c], axis=-1)
    cos = jnp.concatenate([jnp.ones((CTX, HEAD_DIM), f32), jnp.cos(ang)], axis=0)
    sin = jnp.concatenate([jnp.zeros((CTX, HEAD_DIM), f32), jnp.sin(ang)], axis=0)
    return jnp.tile(cos, (1, LANES // HEAD_DIM)), jnp.tile(sin, (1, LANES // HEAD_DIM))


def _block_diag2(w):
    z = jnp.zeros_like(w[0])
    return jnp.concatenate([jnp.concatenate([w[0], z], axis=1), jnp.concatenate([z, w[1]], axis=1)], axis=0)


def kernel(x, c, ctx, c_ctx, w_mod, b_mod, norm_mix_g, norm_ffn_g, w_in, w_out, att_sink, shift_mu_prev, shift_mu_next, decay_w0, decay_w2, iclr_a0, iclr_a2, vres_v0, vres_v1, vres_v2, gate_g2, k_k, k_a, r_k, ln_x_w, ln_x_b, router_w, router_b, expert_w_gate, expert_w_up, expert_w_down, final_norm_g):
    nbatch = x.shape[0]
    depth = w_mod.shape[0]
    tb = ctx.shape[1] + x.shape[1]
    xa = jnp.concatenate([ctx, x], axis=1)
    nb_pad = -(-(nbatch + 1) // 8) * 8
    cond = jnp.zeros((nb_pad, D), f32).at[:nbatch].set(c).at[nbatch].set(c_ctx)
    mod_all = _mod_call(cond, w_mod, b_mod).reshape(depth, nb_pad, 1, 6 * D)
    cos, sin = _rope_tables(tb)
    wr_hi = router_w.T.astype(bf16)
    wr_lo = (router_w.T - wr_hi.astype(f32)).astype(bf16)
    w_router = jnp.stack([wr_hi, wr_lo])
    b_router = router_b.reshape(N_EXPERTS, 1)
    v_first = None
    for l in range(depth):
        mod = mod_all[l]
        q, k, v, rw = _in_proj_call(xa, mod, norm_mix_g[l].reshape(1, D), w_in[l].astype(bf16), cos, sin)
        att = _attn_call(att_sink[l], q, k, v)
        mu = jnp.stack([shift_mu_prev[l], shift_mu_next[l]])
        v0 = vres_v0[l - 1] if l > 0 else jnp.zeros((RW,), f32)
        vec = jnp.stack([k_k[l], k_a[l], r_k[l].reshape(RW), v0,
                         decay_w0[l, 0], decay_w0[l, 1], iclr_a0[l, 0], iclr_a0[l, 1]])
        if l > 0:
            v1 = jnp.zeros((RW, LANES), f32).at[:, :LORA_VRES].set(vres_v1[l - 1]).astype(bf16)
            v2 = jnp.zeros((LANES, RW), f32).at[:LORA_VRES].set(vres_v2[l - 1]).astype(bf16)
        else:
            v1 = v2 = None
        r_, v_, kk, bv, g, kd, lw, bd = _feat_call(
            rw, v_first, mu, vec, _block_diag2(decay_w2[l]).astype(bf16), _block_diag2(iclr_a2[l]).astype(bf16),
            gate_g2[l].astype(bf16), v1, v2)
        if l == 0:
            v_first = v_
        yf, yb = _scan_call(r_, v_, kk, kd, lw, bd)
        ln = jnp.stack([ln_x_w[l], ln_x_b[l]])
        xa, x_local, lpos, gates, cnt = _mix_call(xa, att, yf, yb, bv, g, mod, ln, w_out[l].astype(bf16),
                                                  norm_ffn_g[l].reshape(1, D), w_router, b_router)
        ntok = nbatch * tb
        nrows = _moe_rows(ntok)
        xa = xa.reshape(nbatch, tb, D)
        cnt = cnt[:, 0].astype(i32).reshape(ntok // TM, N_EXPERTS)
        base, seg, lstart, blk_expert, n_used = _segment_plan(cnt, nrows // MOE_BLK)
        x_sorted = _dispatch_call(base, seg, lstart, x_local, nrows)
        y_sorted = _ffn_call(blk_expert, n_used, x_sorted, expert_w_gate, expert_w_up, expert_w_down, l)
        xa = _combine_call(base, seg, lstart, xa, lpos, gates, mod, y_sorted,
                           final_norm_g.reshape(1, D) if l == depth - 1 else None)
    return xa
```

```python
import functools
import math

import jax
import jax.numpy as jnp
from jax import lax
from jax.experimental import pallas as pl
from jax.experimental.pallas import tpu as pltpu

f32 = jnp.float32
bf16 = jnp.bfloat16
i32 = jnp.int32
u32 = jnp.uint32

D = 1024
SEQ = 4096
CTX = 256
TB = CTX + SEQ
GRID_W = 64
HEAD_DIM = 64
ATT_WIDTH = 512
ATT_HEADS = 8
KV_HEADS = 2
ATT_GROUP = ATT_HEADS // KV_HEADS
KV_WIDTH = KV_HEADS * HEAD_DIM
RW = 512
RWKV_HEADS = 8
LORA_DECAY = 64
LORA_ICLR = 64
LORA_VRES = 32
LORA_GATE = 128
RWKV_COLS = 3 * RW + 2 * (LORA_DECAY + LORA_ICLR) + LORA_GATE
ATT_COLS = ATT_WIDTH + 2 * KV_WIDTH
IN_COLS = ATT_COLS + RWKV_COLS
N_EXPERTS = 16
N_GROUPS = 4
EXPERTS_PER_GROUP = 4
TOP_K = 2
MOE_BLK = 256
NORM_EPS = 1e-6
GN_EPS = 64e-5
NEG_INF = -1e30
ATT_SCALE = HEAD_DIM ** -0.5
ROPE_BASE = 10000.0
ROPE_FREQS = HEAD_DIM // 4

LANES = 128
TM = 256
QB = 128
CH = 64
HG = 4
GW = HG * HEAD_DIM
SCAN_BATCH = 4
MIX_TILES = 2
SEG_ALIGN = 8
LROWS = -(-(TOP_K * TM + N_EXPERTS * SEG_ALIGN) // LANES) * LANES
VMEM_LIMIT = 48 * 1024 * 1024


def _cparams(sem):
    return pltpu.CompilerParams(dimension_semantics=sem, vmem_limit_bytes=VMEM_LIMIT)


def _sigmoid(x):
    return 0.5 * jnp.tanh(0.5 * x) + 0.5


def _div_pow2(x, n):
    assert n & (n - 1) == 0
    return lax.shift_right_logical(x, n.bit_length() - 1)


def _mod_pow2(x, n):
    assert n & (n - 1) == 0
    return lax.bitwise_and(x, n - 1)


def _round_up_pow2(x, n):
    assert n & (n - 1) == 0
    return lax.bitwise_and(x + (n - 1), ~(n - 1))


def _dot(a, b):
    return jnp.dot(a, b, preferred_element_type=f32)


def _dot_nt(a, b):
    return lax.dot_general(a, b, (((1,), (1,)), ((), ())), preferred_element_type=f32)


def _dot_tn(a, b):
    return lax.dot_general(a, b, (((0,), (0,)), ((), ())), preferred_element_type=f32)


def _split2(x):
    hi = x.astype(bf16)
    lo = (x - hi.astype(f32)).astype(bf16)
    return hi, lo


def _dot_split_lhs(x, m):
    hi, lo = _split2(x)
    return _dot(hi, m) + _dot(lo, m)


def _rmsnorm(x, g):
    ms = jnp.mean(x * x, axis=-1, keepdims=True)
    return x * lax.rsqrt(ms + NORM_EPS) * g


def _head_ones():
    r = _div_pow2(lax.broadcasted_iota(i32, (RW, RW), 0), HEAD_DIM)
    c = _div_pow2(lax.broadcasted_iota(i32, (RW, RW), 1), HEAD_DIM)
    return (r == c).astype(bf16)


def _mod_kernel(c_ref, w_ref, b_ref, o_ref):
    c = c_ref[...]
    s = (c * _sigmoid(c)).astype(bf16)
    o_ref[0] = _dot(s, w_ref[0].astype(bf16)) + b_ref[0]


def _mod_call(cond, w_mod, b_mod):
    nb = cond.shape[0]
    depth = w_mod.shape[0]
    tn = 1024
    return pl.pallas_call(
        _mod_kernel,
        grid=(depth, 6 * D // tn),
        in_specs=[
            pl.BlockSpec((nb, D), lambda l, j: (0, 0)),
            pl.BlockSpec((1, D, tn), lambda l, j: (l, 0, j)),
            pl.BlockSpec((1, 1, tn), lambda l, j: (l, 0, j)),
        ],
        out_specs=pl.BlockSpec((1, nb, tn), lambda l, j: (l, 0, j)),
        out_shape=jax.ShapeDtypeStruct((depth, nb, 6 * D), f32),
        compiler_params=_cparams(("arbitrary", "arbitrary")),
        name="mod",
    )(cond, w_mod, b_mod.reshape(depth, 1, 6 * D))


def _mod_index(nbatch, ctx_tiles):
    return lambda b, i: (jnp.where(i < ctx_tiles, nbatch, b), 0, 0)


def _in_proj_kernel(x_ref, mod_ref, g_ref, w_ref, cos_ref, sin_ref, q_ref, k_ref, v_ref, rw_ref):
    x = x_ref[0]
    tm = x.shape[0]
    h = _rmsnorm(x, g_ref[...])
    sh = mod_ref[0, :, 0:D]
    sc = mod_ref[0, :, D:2 * D]
    h = (h * (1.0 + sc) + sh).astype(bf16)
    p = _dot(h, w_ref[...])
    cos = cos_ref[...]
    sin = sin_ref[...]
    lane = lax.broadcasted_iota(i32, (tm, LANES), 1)
    first_half = _mod_pow2(lane, 2 * ROPE_FREQS) < ROPE_FREQS

    def rope(t):
        rot = jnp.where(first_half, -pltpu.roll(t, LANES - ROPE_FREQS, 1), pltpu.roll(t, ROPE_FREQS, 1))
        return t * cos + rot * sin

    for j in range(ATT_WIDTH // LANES):
        t = (rope(p[:, j * LANES:(j + 1) * LANES]) * ATT_SCALE).astype(bf16)
        q_ref[0, 2 * j] = t[:, :HEAD_DIM]
        q_ref[0, 2 * j + 1] = t[:, HEAD_DIM:]
    kt = rope(p[:, ATT_WIDTH:ATT_WIDTH + KV_WIDTH]).astype(bf16)
    vt = p[:, ATT_WIDTH + KV_WIDTH:ATT_COLS].astype(bf16)
    for hh in range(KV_HEADS):
        k_ref[0, hh] = kt[:, hh * HEAD_DIM:(hh + 1) * HEAD_DIM]
        v_ref[0, hh] = vt[:, hh * HEAD_DIM:(hh + 1) * HEAD_DIM]
    rw_ref[0] = p[:, ATT_COLS:]


def _in_proj_call(x, mod, g, w_in, cos, sin):
    nbatch, tb, _ = x.shape
    nt = tb // TM
    return pl.pallas_call(
        _in_proj_kernel,
        grid=(nbatch, nt),
        in_specs=[
            pl.BlockSpec((1, TM, D), lambda b, i: (b, i, 0)),
            pl.BlockSpec((1, 1, 6 * D), _mod_index(nbatch, CTX // TM)),
            pl.BlockSpec((1, D), lambda b, i: (0, 0)),
            pl.BlockSpec((D, IN_COLS), lambda b, i: (0, 0)),
            pl.BlockSpec((TM, LANES), lambda b, i: (i, 0)),
            pl.BlockSpec((TM, LANES), lambda b, i: (i, 0)),
        ],
        out_specs=[
            pl.BlockSpec((1, ATT_HEADS, TM, HEAD_DIM), lambda b, i: (b, 0, i, 0)),
            pl.BlockSpec((1, KV_HEADS, TM, HEAD_DIM), lambda b, i: (b, 0, i, 0)),
            pl.BlockSpec((1, KV_HEADS, TM, HEAD_DIM), lambda b, i: (b, 0, i, 0)),
            pl.BlockSpec((1, TM, RWKV_COLS), lambda b, i: (b, i, 0)),
        ],
        out_shape=[
            jax.ShapeDtypeStruct((nbatch, ATT_HEADS, tb, HEAD_DIM), bf16),
            jax.ShapeDtypeStruct((nbatch, KV_HEADS, tb, HEAD_DIM), bf16),
            jax.ShapeDtypeStruct((nbatch, KV_HEADS, tb, HEAD_DIM), bf16),
            jax.ShapeDtypeStruct((nbatch, tb, RWKV_COLS), f32),
        ],
        compiler_params=_cparams(("parallel", "arbitrary")),
        name="in_proj",
    )(x, mod, g, w_in, cos, sin)


def _attn_kernel(sink_ref, q_ref, kp_ref, km_ref, kn_ref, vp_ref, vm_ref, vn_ref, kx_ref, vx_ref, o_ref,
                 *, npairs, ctx_pairs):
    j = pl.program_id(1)
    is_lat = j >= ctx_pairs
    before_ok = jnp.logical_and(is_lat, j - 1 >= ctx_pairs)
    after_ok = jnp.logical_and(is_lat, j + 1 <= npairs - 1)
    rows = ATT_GROUP * QB
    qi = _mod_pow2(lax.broadcasted_iota(i32, (rows, QB), 0), QB)
    kj = lax.broadcasted_iota(i32, (rows, QB), 1)
    band_p = kj >= qi
    band_n = kj <= qi
    row_head = _div_pow2(lax.broadcasted_iota(i32, (rows, 1), 0), QB)
    half = (slice(0, QB), slice(QB, 2 * QB))
    chains = [(c, h) for c in range(2) for h in range(KV_HEADS)]
    qh, kprev, kcur, knext, vprev, vcur, vnext, ok_p, ok_n = [], [], [], [], [], [], [], [], []
    for c, h in chains:
        qh.append(q_ref[0, ATT_GROUP * h:ATT_GROUP * (h + 1), half[c], :].reshape(rows, HEAD_DIM))
        kcur.append(km_ref[0, h, half[c], :])
        vcur.append(vm_ref[0, h, half[c], :])
        if c == 0:
            kprev.append(kp_ref[0, h]); vprev.append(vp_ref[0, h]); ok_p.append(before_ok)
            knext.append(km_ref[0, h, half[1], :]); vnext.append(vm_ref[0, h, half[1], :]); ok_n.append(is_lat)
        else:
            kprev.append(km_ref[0, h, half[0], :]); vprev.append(vm_ref[0, h, half[0], :]); ok_p.append(is_lat)
            knext.append(kn_ref[0, h]); vnext.append(vn_ref[0, h]); ok_n.append(after_ok)
    n = range(len(chains))
    s_p = [jnp.where(jnp.logical_and(band_p, ok_p[i]), _dot_nt(qh[i], kprev[i]), NEG_INF) for i in n]
    s_c = [jnp.where(is_lat, _dot_nt(qh[i], kcur[i]), NEG_INF) for i in n]
    s_n = [jnp.where(jnp.logical_and(band_n, ok_n[i]), _dot_nt(qh[i], knext[i]), NEG_INF) for i in n]
    s_x = [_dot_nt(qh[i], kx_ref[0, chains[i][1]]) for i in n]
    sink = []
    for h in range(KV_HEADS):
        sk = jnp.zeros((rows, 1), f32)
        for g in range(ATT_GROUP):
            sk = jnp.where(row_head == g, sink_ref[ATT_GROUP * h + g], sk)
        sink.append(sk)
    sink = [sink[h] for _, h in chains]
    m = [jnp.maximum(jnp.max(jnp.maximum(jnp.maximum(s_p[i], s_c[i]),
                                         jnp.maximum(jnp.maximum(s_n[i], s_x[i][:, :QB]), s_x[i][:, QB:])),
                             axis=-1, keepdims=True), sink[i]) for i in n]
    e_p = [jnp.exp(s_p[i] - m[i]) for i in n]
    e_c = [jnp.exp(s_c[i] - m[i]) for i in n]
    e_n = [jnp.exp(s_n[i] - m[i]) for i in n]
    e_x = [jnp.exp(s_x[i] - m[i]) for i in n]
    den = [jnp.sum((e_p[i] + e_c[i]) + (e_n[i] + e_x[i][:, :QB]) + e_x[i][:, QB:], axis=-1, keepdims=True)
           + jnp.exp(sink[i] - m[i]) for i in n]
    o = [(_dot(e_p[i].astype(bf16), vprev[i]) + _dot(e_c[i].astype(bf16), vcur[i])
          + _dot(e_n[i].astype(bf16), vnext[i]) + _dot(e_x[i].astype(bf16), vx_ref[0, chains[i][1]])) / den[i]
         for i in n]
    for c in range(2):
        o_ref[0, half[c], :] = jnp.concatenate(
            [o[c * KV_HEADS + h][g * QB:(g + 1) * QB] for h in range(KV_HEADS) for g in range(ATT_GROUP)],
            axis=1).astype(bf16)


def _attn_call(sink, q, k, v):
    nbatch, _, tb, _ = q.shape
    nblk = tb // QB
    npairs = nblk // 2
    assert CTX == 2 * QB and nblk % 2 == 0
    kv_blk = (1, KV_HEADS, QB, HEAD_DIM)
    pair_blk = (1, KV_HEADS, 2 * QB, HEAD_DIM)
    before_map = lambda b, j: (b, 0, jnp.maximum(2 * j - 1, 0), 0)
    pair_map = lambda b, j: (b, 0, j, 0)
    after_map = lambda b, j: (b, 0, jnp.minimum(2 * j + 2, nblk - 1), 0)
    ctx_spec = pl.BlockSpec((1, KV_HEADS, CTX, HEAD_DIM), lambda b, j: (b, 0, 0, 0))
    return pl.pallas_call(
        functools.partial(_attn_kernel, npairs=npairs, ctx_pairs=CTX // (2 * QB)),
        grid=(nbatch, npairs),
        in_specs=[
            pl.BlockSpec(memory_space=pltpu.SMEM),
            pl.BlockSpec((1, ATT_HEADS, 2 * QB, HEAD_DIM), pair_map),
            pl.BlockSpec(kv_blk, before_map), pl.BlockSpec(pair_blk, pair_map), pl.BlockSpec(kv_blk, after_map),
            pl.BlockSpec(kv_blk, before_map), pl.BlockSpec(pair_blk, pair_map), pl.BlockSpec(kv_blk, after_map),
            ctx_spec, ctx_spec,
        ],
        out_specs=pl.BlockSpec((1, 2 * QB, ATT_WIDTH), lambda b, j: (b, j, 0)),
        out_shape=jax.ShapeDtypeStruct((nbatch, tb, ATT_WIDTH), bf16),
        compiler_params=_cparams(("parallel", "arbitrary")),
        name="attention",
    )(sink, q, k, k, k, v, v, v, k, v)


def _feat_kernel(*refs, nt, ctx_tiles, has_vres):
    if has_vres:
        (rw_ref, hp_ref, hn_ref, vf_ref, mu_ref, vec_ref, w2_ref, a2_ref, g2_ref, v1_ref, v2_ref,
         r_ref, v_ref, kk_ref, bv_ref, g_ref, kd_ref, lw_ref, bd_ref) = refs
    else:
        (rw_ref, hp_ref, hn_ref, mu_ref, vec_ref, w2_ref, a2_ref, g2_ref,
         r_ref, v_ref, kk_ref, bv_ref, g_ref, kd_ref, lw_ref, bd_ref) = refs
    i = pl.program_id(1)
    u0 = rw_ref[0]
    tm = u0.shape[0]
    prev_zero = jnp.logical_or(i == 0, i == ctx_tiles)
    next_zero = jnp.logical_or(i == ctx_tiles - 1, i == nt - 1)
    halo_p = jnp.where(prev_zero, 0.0, hp_ref[0, 7:8, :])
    halo_n = jnp.where(next_zero, 0.0, hn_ref[0, 0:1, :])
    row = lax.broadcasted_iota(i32, (tm, 1), 0)
    prev = jnp.where(row == 0, halo_p, pltpu.roll(u0, 1, 0))
    nxt = jnp.where(row == tm - 1, halo_n, pltpu.roll(u0, tm - 1, 0))
    mu_p = mu_ref[0:1, :]
    mu_n = mu_ref[1:2, :]
    u = u0 + mu_p * (prev - u0) + mu_n * (nxt - u0)

    r = u[:, 0:RW]
    k = u[:, RW:2 * RW]
    v = u[:, 2 * RW:3 * RW]
    wd = u[:, 3 * RW:3 * RW + 2 * LORA_DECAY]
    ad = u[:, 3 * RW + 2 * LORA_DECAY:3 * RW + 2 * (LORA_DECAY + LORA_ICLR)]
    gd = u[:, 3 * RW + 2 * (LORA_DECAY + LORA_ICLR):]
    k_k = vec_ref[0:1, :]
    k_a = vec_ref[1:2, :]
    r_k = vec_ref[2:3, :]
    ones = _head_ones()

    if has_vres:
        lo = _dot(v.astype(bf16), v1_ref[...])
        gate = _sigmoid(vec_ref[3:4, :] + _dot(lo.astype(bf16), v2_ref[...]))
        v = v + (vf_ref[0] - v) * gate
    decay_in = _dot(jnp.tanh(wd).astype(bf16), w2_ref[...])
    a_in = _dot(ad.astype(bf16), a2_ref[...])
    kk = k * k_k
    n2 = _dot_split_lhs(kk * kk, ones)
    kk = kk * lax.rsqrt(jnp.maximum(n2, 1e-24))
    g = _dot(_sigmoid(gd).astype(bf16), g2_ref[...])
    ksum = jnp.zeros_like(k)
    for d in range(2):
        w0 = vec_ref[4 + d:5 + d, :]
        a0 = vec_ref[6 + d:7 + d, :]
        lw = -_sigmoid(w0 + decay_in[:, d * RW:(d + 1) * RW]) * math.exp(-0.5)
        a = _sigmoid(a0 + a_in[:, d * RW:(d + 1) * RW])
        kd = k * (1.0 + (a - 1.0) * k_a)
        ksum = ksum + kd
        kd_ref[d, 0] = kd.astype(bf16)
        lw_ref[d, 0] = lw
        bd_ref[d, 0] = (kk * a).astype(bf16)
    bonus = _dot_split_lhs(r * ksum * r_k, ones)
    r_ref[0] = r.astype(bf16)
    v_ref[0] = v
    kk_ref[0] = kk.astype(bf16)
    bv_ref[0] = bonus * v
    g_ref[0] = g.astype(bf16)


def _feat_call(rw, v_first, mu, vec, w2bd, a2bd, g2, v1, v2):
    nbatch, tb, _ = rw.shape
    nt = tb // TM
    has_vres = v_first is not None
    sub = TM // 8
    tile = lambda w: pl.BlockSpec((1, TM, w), lambda b, i: (b, i, 0))
    full = lambda a: pl.BlockSpec(a.shape, lambda b, i: (0,) * a.ndim)
    in_specs = [
        tile(RWKV_COLS),
        pl.BlockSpec((1, 8, RWKV_COLS), lambda b, i: (b, jnp.maximum(i * sub - 1, 0), 0)),
        pl.BlockSpec((1, 8, RWKV_COLS), lambda b, i: (b, jnp.minimum((i + 1) * sub, tb // 8 - 1), 0)),
    ]
    args = [rw, rw, rw]
    if has_vres:
        in_specs.append(tile(RW))
        args.append(v_first)
    consts = [mu, vec, w2bd, a2bd, g2] + ([v1, v2] if has_vres else [])
    in_specs += [full(a) for a in consts]
    args += consts
    dir_spec = pl.BlockSpec((2, 1, TM, RW), lambda b, i: (0, b, i, 0))
    tok = lambda dt: jax.ShapeDtypeStruct((nbatch, tb, RW), dt)
    dtok = lambda dt: jax.ShapeDtypeStruct((2, nbatch, tb, RW), dt)
    return pl.pallas_call(
        functools.partial(_feat_kernel, nt=nt, ctx_tiles=CTX // TM, has_vres=has_vres),
        grid=(nbatch, nt),
        in_specs=in_specs,
        out_specs=[tile(RW)] * 5 + [dir_spec] * 3,
        out_shape=[tok(bf16), tok(f32), tok(bf16), tok(f32), tok(bf16), dtok(bf16), dtok(f32), dtok(bf16)],
        compiler_params=_cparams(("parallel", "arbitrary")),
        name="rwkv_features",
    )(*args)


def _block_diag_rows(x, width):
    cb = _div_pow2(lax.broadcasted_iota(i32, x.shape, 1), width)
    return jnp.concatenate([jnp.where(cb == h, x, jnp.zeros_like(x)) for h in range(HG)], axis=0)


def _scan_chunks(probs):
    wide = HG * CH
    row_t = lax.broadcasted_iota(i32, (CH, wide), 0)
    col_t = _mod_pow2(lax.broadcasted_iota(i32, (CH, wide), 1), CH)
    row_g = lax.broadcasted_iota(i32, (CH, GW), 0)
    incl_t = {False: col_t <= row_t, True: col_t >= row_t}
    strict_t = {False: col_t < row_t, True: col_t > row_t}
    eye_t = (row_t == col_t).astype(f32)
    rb = _div_pow2(lax.broadcasted_iota(i32, (GW, GW), 0), HEAD_DIM)
    cb = _div_pow2(lax.broadcasted_iota(i32, (GW, GW), 1), HEAD_DIM)
    n = len(probs)
    rev = [p[7] for p in probs]
    def cumsum_rows(x, reverse):
        s = 1
        while s < CH:
            if reverse:
                x = x + jnp.where(row_g < CH - s, pltpu.roll(x, CH - s, 0), 0.0)
            else:
                x = x + jnp.where(row_g >= s, pltpu.roll(x, s, 0), 0.0)
            s *= 2
        return x

    gam = [cumsum_rows(probs[i][5], rev[i]) for i in range(n)]
    ar, bk, k_t, b_t = [], [], [], []
    for i, (s_prev, r, v, kk, k, lw, b, _) in enumerate(probs):
        e_neg = jnp.exp(-gam[i])
        a_s = (-kk * jnp.exp(gam[i] - lw)).astype(bf16)
        r_s = (r * jnp.exp(gam[i])).astype(bf16)
        b_t.append(b * e_neg)
        k_t.append(k * e_neg)
        ar.append(jnp.concatenate([a_s, r_s], axis=0))
        bk.append(jnp.concatenate([_block_diag_rows(b_t[i].astype(bf16), HEAD_DIM),
                                   _block_diag_rows(k_t[i].astype(bf16), HEAD_DIM)], axis=0))
    gram = [_dot_nt(ar[i], bk[i]) for i in range(n)]
    ars = [_dot_nt(ar[i], probs[i][0].astype(bf16)) for i in range(n)]
    v_bd = [_block_diag_rows(probs[i][2].astype(bf16), HEAD_DIM) for i in range(n)]
    p0 = [jnp.where(strict_t[rev[i]], gram[i][:CH, :wide], 0.0).astype(bf16) for i in range(n)]
    lq = [jnp.concatenate([jnp.where(strict_t[rev[i]], gram[i][:CH, wide:], 0.0),
                           jnp.where(incl_t[rev[i]], gram[i][CH:, wide:], 0.0)], axis=0).astype(bf16)
          for i in range(n)]
    lqv = [_dot(lq[i], v_bd[i]) for i in range(n)]
    rhs = [ars[i][:CH] + lqv[i][:CH] for i in range(n)]
    t = [eye_t + p0[i].astype(f32) for i in range(n)]
    p = [_dot(p0[i], _block_diag_rows(p0[i], CH)).astype(bf16) for i in range(n)]
    m = 4
    while m < CH:
        tp = [_dot(jnp.concatenate([t[i].astype(bf16), p[i]], axis=0), _block_diag_rows(p[i], CH))
              for i in range(n)]
        t = [t[i] + tp[i][:CH] for i in range(n)]
        p = [tp[i][CH:].astype(bf16) for i in range(n)]
        m *= 2
    t = [t[i] + _dot(t[i].astype(bf16), _block_diag_rows(p[i], CH)) for i in range(n)]
    u = [_dot(t[i].astype(bf16), _block_diag_rows(rhs[i].astype(bf16), HEAD_DIM)) for i in range(n)]
    out = []
    for i in range(n):
        s_prev, v, lw = probs[i][0], probs[i][2], probs[i][5]
        q_b = jnp.where(incl_t[rev[i]], gram[i][CH:, :wide], 0.0).astype(bf16)
        u_bd = _block_diag_rows(u[i].astype(bf16), HEAD_DIM)
        y = ars[i][CH:] + lqv[i][CH:] + _dot(q_b, u_bd)
        eg = jnp.exp(jnp.sum(lw, axis=0, keepdims=True))
        vu = jnp.concatenate([v, u[i]], axis=0).astype(bf16)
        kb = jnp.concatenate([k_t[i] * eg, b_t[i] * eg], axis=0).astype(bf16)
        s_add = _dot_tn(vu, kb)
        out.append((s_prev * eg + jnp.where(rb == cb, s_add, 0.0), y))
    return out


def _scan_kernel(rf_ref, vf_ref, kkf_ref, kf_ref, lwf_ref, bf_ref,
                 rb_ref, vb_ref, kkb_ref, kb_ref, lwb_ref, bb_ref,
                 yf_ref, yb_ref, s_ref):
    @pl.when(pl.program_id(1) == 0)
    def _():
        s_ref[...] = jnp.zeros_like(s_ref)

    dirs = ((rf_ref, vf_ref, kkf_ref, kf_ref, lwf_ref, bf_ref, yf_ref),
            (rb_ref, vb_ref, kkb_ref, kb_ref, lwb_ref, bb_ref, yb_ref))
    probs, dest = [], []
    for bi in range(SCAN_BATCH):
        for d, (r_ref, v_ref, kk_ref, k_ref, lw_ref, b_ref, y_ref) in enumerate(dirs):
            for g in range(RW // GW):
                sl = slice(g * GW, (g + 1) * GW)
                probs.append((s_ref[bi, d, g], r_ref[bi, :, sl], v_ref[bi, :, sl], kk_ref[bi, :, sl],
                              k_ref[0, bi, :, sl], lw_ref[0, bi, :, sl], b_ref[0, bi, :, sl], d == 1))
                dest.append((bi, d, g, y_ref, sl))
    for (bi, d, g, y_ref, sl), (s_new, y) in zip(dest, _scan_chunks(probs)):
        s_ref[bi, d, g] = s_new
        y_ref[bi, :, sl] = y


def _scan_call(r, v, kk, kd, lw, bd):
    nbatch, tb, _ = r.shape
    nc = tb // CH
    cc = CTX // CH
    sb = SCAN_BATCH
    rev = lambda j: jnp.where(j < cc, cc - 1 - j, nc - 1 + cc - j)
    tok_f = pl.BlockSpec((sb, CH, RW), lambda b, j: (b, j, 0))
    tok_b = pl.BlockSpec((sb, CH, RW), lambda b, j: (b, rev(j), 0))
    dir_f = pl.BlockSpec((1, sb, CH, RW), lambda b, j: (0, b, j, 0))
    dir_b = pl.BlockSpec((1, sb, CH, RW), lambda b, j: (1, b, rev(j), 0))
    out = jax.ShapeDtypeStruct((nbatch, tb, RW), f32)
    return pl.pallas_call(
        _scan_kernel,
        grid=(nbatch // sb, nc),
        in_specs=[tok_f, tok_f, tok_f, dir_f, dir_f, dir_f, tok_b, tok_b, tok_b, dir_b, dir_b, dir_b],
        out_specs=[tok_f, tok_b],
        out_shape=[out, out],
        scratch_shapes=[pltpu.VMEM((sb, 2, RW // GW, GW, GW), f32)],
        compiler_params=_cparams(("parallel", "arbitrary")),
        name="rwkv_scan",
    )(r, v, kk, kd, lw, bd, r, v, kk, kd, lw, bd)


def _pack_bf16_pairs(x):
    bits = pltpu.bitcast(x, u32)
    half = x.shape[1] // 2
    return bits[:, :half] | lax.shift_right_logical(bits[:, half:], jnp.uint32(16))


def _unpack_bf16_pairs(p):
    hi = pltpu.bitcast(p & jnp.uint32(0xFFFF0000), f32)
    lo = pltpu.bitcast(lax.shift_left(p, jnp.uint32(16)), f32)
    return jnp.concatenate([hi, lo], axis=1).astype(bf16)


def _mix_kernel(x_ref, att_ref, yf_ref, yb_ref, bv_ref, g_ref, mod0_ref, mod1_ref, ln_ref, wo_ref, gf_ref,
                wr_ref, rb_ref, xo_ref, xl_ref, lp_ref, gt_ref, cnt_ref):
    ones = _head_ones()
    inv = 1.0 / HEAD_DIM
    tiles = range(MIX_TILES)
    rows = [slice(s * TM, (s + 1) * TM) for s in tiles]
    mods = (mod0_ref, mod1_ref)
    y = [yf_ref[rows[s], :] + yb_ref[rows[s], :] for s in tiles]
    mu = [_dot(y[s].astype(bf16), ones) * inv for s in tiles]
    dlt = [y[s] - mu[s] for s in tiles]
    var = [_dot((dlt[s] * dlt[s]).astype(bf16), ones) * inv for s in tiles]
    rwk = [((dlt[s] * lax.rsqrt(var[s] + GN_EPS) * ln_ref[0:1, :] + ln_ref[1:2, :] + bv_ref[rows[s], :])
            * g_ref[rows[s], :]).astype(bf16) for s in tiles]
    mix = [_dot(att_ref[rows[s], :], wo_ref[0:ATT_WIDTH, :]) + _dot(rwk[s], wo_ref[ATT_WIDTH:, :]) for s in tiles]
    xn = [x_ref[rows[s], :] + mods[s][0, :, 2 * D:3 * D] * mix[s] for s in tiles]
    for s in tiles:
        xo_ref[rows[s], :] = xn[s]
    hf = [_rmsnorm(xn[s], gf_ref[...]) * (1.0 + mods[s][0, :, 4 * D:5 * D]) + mods[s][0, :, 3 * D:4 * D]
          for s in tiles]
    split = [_split2(hf[s]) for s in tiles]
    by_hi = [_dot_nt(wr_ref[...], split[s][0]) for s in tiles]
    logits = [by_hi[s][:N_EXPERTS] + by_hi[s][N_EXPERTS:] + _dot_nt(wr_ref[0:N_EXPERTS, :], split[s][1])
              for s in tiles]
    sorted_rows, lpos, gates, cnt = _route_and_sort(hf, [_sigmoid(lg) for lg in logits], rb_ref[...])
    for s in tiles:
        xl_ref[s * LROWS:(s + 1) * LROWS, :] = sorted_rows[s]
        lp_ref[:, rows[s]] = lpos[s]
        gt_ref[:, rows[s]] = gates[s]
        cnt_ref[s * N_EXPERTS:(s + 1) * N_EXPERTS, :] = cnt[s]


def _route_and_sort(hf, scores, bias):
    tiles = range(len(hf))
    tm = hf[0].shape[0]
    routed = [_route_rows(scores[s], bias) for s in tiles]
    eio = lax.broadcasted_iota(i32, (N_EXPERTS, tm), 0)
    upper = (lax.broadcasted_iota(i32, (tm, tm), 0) < lax.broadcasted_iota(i32, (tm, tm), 1)).astype(bf16)
    lower = (lax.broadcasted_iota(i32, (N_EXPERTS, N_EXPERTS), 0)
             > lax.broadcasted_iota(i32, (N_EXPERTS, N_EXPERTS), 1)).astype(bf16)
    jj = lax.broadcasted_iota(i32, (LROWS, tm), 0)
    oh0 = [(eio == routed[s][0]).astype(f32) for s in tiles]
    oh1 = [(eio == routed[s][1]).astype(f32) for s in tiles]
    both = [oh0[s] + oh1[s] for s in tiles]
    before = [_dot(both[s].astype(bf16), upper) for s in tiles]
    cnt = [jnp.sum(both[s], axis=1, keepdims=True) for s in tiles]
    seg = [_round_up_pow2(cnt[s].astype(i32), SEG_ALIGN).astype(f32) for s in tiles]
    seg_start = [_dot(lower, jnp.broadcast_to(seg[s], (N_EXPERTS, tm)).astype(bf16)) for s in tiles]
    pos = [seg_start[s] + before[s] for s in tiles]
    lp0 = [jnp.sum(oh0[s] * pos[s], axis=0, keepdims=True).astype(i32) for s in tiles]
    lp1 = [jnp.sum(oh1[s] * pos[s], axis=0, keepdims=True).astype(i32) for s in tiles]
    sel_t = [jnp.logical_or(jj == lp0[s], jj == lp1[s]).astype(bf16) for s in tiles]
    sorted_rows = [_pack_bf16_pairs(_dot(sel_t[s], hf[s].astype(bf16))) for s in tiles]
    lpos = [jnp.concatenate([lp0[s], lp1[s]], axis=0) for s in tiles]
    gates = [jnp.concatenate([routed[s][2], routed[s][3]], axis=0) for s in tiles]
    cnt_out = [jnp.broadcast_to(cnt[s], (N_EXPERTS, LANES)) for s in tiles]
    return sorted_rows, lpos, gates, cnt_out


def _route_rows(scores, bias):
    biased = scores + bias
    row = lambda a, e: a[e:e + 1, :]
    best = None
    for gi in range(N_GROUPS):
        m = [row(biased, gi * EXPERTS_PER_GROUP + j) for j in range(EXPERTS_PER_GROUP)]
        gs = None
        for a in range(EXPERTS_PER_GROUP):
            for b in range(a + 1, EXPERTS_PER_GROUP):
                pair = m[a] + m[b]
                gs = pair if gs is None else jnp.maximum(gs, pair)
        if best is None:
            best, g_idx = gs, jnp.zeros(gs.shape, i32)
        else:
            better = gs > best
            g_idx = jnp.where(better, gi, g_idx)
            best = jnp.where(better, gs, best)

    def pick(a, j):
        out = row(a, j)
        for gi in range(1, N_GROUPS):
            out = jnp.where(g_idx == gi, row(a, gi * EXPERTS_PER_GROUP + j), out)
        return out

    vb = [pick(biased, j) for j in range(EXPERTS_PER_GROUP)]
    vs = [pick(scores, j) for j in range(EXPERTS_PER_GROUP)]

    def argmax_first(vals):
        bv, bi = vals[0], jnp.zeros(vals[0].shape, i32)
        for j in range(1, len(vals)):
            better = vals[j] > bv
            bi = jnp.where(better, j, bi)
            bv = jnp.where(better, vals[j], bv)
        return bi

    i1 = argmax_first(vb)
    i2 = argmax_first([jnp.where(i1 == j, -jnp.inf, vb[j]) for j in range(EXPERTS_PER_GROUP)])
    sel = lambda idx: sum(jnp.where(idx == j, vs[j], 0.0) for j in range(EXPERTS_PER_GROUP))
    s1, s2 = sel(i1), sel(i2)
    tot = s1 + s2
    base = g_idx * EXPERTS_PER_GROUP
    return base + i1, base + i2, s1 / tot, s2 / tot


def _mix_call(x, att, yf, yb, bv, g, mod, ln, w_out, g_ffn, w_router, b_router):
    nbatch, tb, _ = x.shape
    tiles_per_batch = tb // TM
    ntiles = nbatch * tiles_per_batch
    assert ntiles % MIX_TILES == 0 and MIX_TILES == 2
    ntok = nbatch * tb
    flat = lambda a: a.reshape(ntok, a.shape[-1])
    rows = lambda w: pl.BlockSpec((MIX_TILES * TM, w), lambda p: (p, 0))
    full = lambda a: pl.BlockSpec(a.shape, lambda p: (0,) * a.ndim)
    ctx_tiles = CTX // TM

    def mod_spec(s):
        def index(p):
            tile = MIX_TILES * p + s
            return (jnp.where(tile % tiles_per_batch < ctx_tiles, nbatch, tile // tiles_per_batch), 0, 0)
        return pl.BlockSpec((1, 1, 6 * D), index)

    route = pl.BlockSpec((TOP_K, MIX_TILES * TM), lambda p: (0, p))
    return pl.pallas_call(
        _mix_kernel,
        grid=(ntiles // MIX_TILES,),
        in_specs=[rows(D), rows(ATT_WIDTH), rows(RW), rows(RW), rows(RW), rows(RW), mod_spec(0), mod_spec(1),
                  full(ln), full(w_out), full(g_ffn), full(w_router), full(b_router)],
        out_specs=[rows(D), pl.BlockSpec((MIX_TILES * LROWS, D // 2), lambda p: (p, 0)), route, route,
                   pl.BlockSpec((MIX_TILES * N_EXPERTS, LANES), lambda p: (p, 0))],
        out_shape=[jax.ShapeDtypeStruct((ntok, D), f32),
                   jax.ShapeDtypeStruct((ntiles * LROWS, D // 2), u32),
                   jax.ShapeDtypeStruct((TOP_K, ntok), i32),
                   jax.ShapeDtypeStruct((TOP_K, ntok), f32),
                   jax.ShapeDtypeStruct((ntiles * N_EXPERTS, LANES), f32)],
        compiler_params=_cparams(("arbitrary",)),
        name="mix_out",
    )(flat(x), flat(att), flat(yf), flat(yb), flat(bv), flat(g), mod, mod, ln, w_out, g_ffn, w_router, b_router)


def _segment_copies(fn, tile, base_ref, seg_ref, ls_ref, src, dst, sem, src_is_global):
    for e in range(N_EXPERTS):
        idx = tile * N_EXPERTS + e
        seg = seg_ref[idx]
        g0 = base_ref[idx]
        l0 = ls_ref[idx]
        size = TM
        while size >= SEG_ALIGN:
            done = lax.bitwise_and(seg, ~(2 * size - 1))

            @pl.when(lax.bitwise_and(seg, size) != 0)
            def _():
                g_rows = pl.ds(pl.multiple_of(g0 + done, SEG_ALIGN), size)
                l_rows = pl.ds(pl.multiple_of(l0 + done, SEG_ALIGN), size)
                s_rows, d_rows = (g_rows, l_rows) if src_is_global else (l_rows, g_rows)
                fn(pltpu.make_async_copy(src.at[s_rows, :], dst.at[d_rows, :], sem))

            size //= 2


def _dispatch_kernel(base_ref, seg_ref, ls_ref, ends_ref, xl_ref, xs_ref, zeros, sem):
    tile = pl.program_id(0)

    def zero_tails(fn):
        for e in range(N_EXPERTS):
            end = ends_ref[e]
            start = ends_ref[e - 1] if e else 0

            @pl.when(end > start)
            def _():
                rows = pl.ds(pl.multiple_of(end - MOE_BLK, MOE_BLK), MOE_BLK)
                fn(pltpu.make_async_copy(zeros, xs_ref.at[rows, :], sem))

    def zero_unused(fn):
        def body(blk, carry):
            fn(pltpu.make_async_copy(zeros, xs_ref.at[pl.ds(pl.multiple_of(blk * MOE_BLK, MOE_BLK), MOE_BLK), :],
                                     sem))
            return carry
        lax.fori_loop(ends_ref[N_EXPERTS - 1] // MOE_BLK, xs_ref.shape[0] // MOE_BLK, body, 0)

    @pl.when(tile == 0)
    def _():
        zeros[...] = jnp.zeros_like(zeros)
        zero_tails(lambda cp: cp.start())
        zero_unused(lambda cp: cp.start())
        zero_tails(lambda cp: cp.wait())
        zero_unused(lambda cp: cp.wait())

    args = (tile, base_ref, seg_ref, ls_ref, xl_ref, xs_ref, sem, False)
    _segment_copies(lambda cp: cp.start(), *args)
    _segment_copies(lambda cp: cp.wait(), *args)


def _dispatch_call(base, seg, lstart, pad_ends, x_local, nrows):
    ntiles = x_local.shape[0] // LROWS
    return pl.pallas_call(
        _dispatch_kernel,
        grid_spec=pltpu.PrefetchScalarGridSpec(
            num_scalar_prefetch=4,
            grid=(ntiles,),
            in_specs=[pl.BlockSpec((LROWS, D // 2), lambda t, *_: (t, 0))],
            out_specs=pl.BlockSpec(memory_space=pl.ANY),
            scratch_shapes=[pltpu.VMEM((MOE_BLK, D // 2), u32), pltpu.SemaphoreType.DMA(())],
        ),
        out_shape=jax.ShapeDtypeStruct((nrows, D // 2), u32),
        compiler_params=_cparams(("arbitrary",)),
        name="moe_dispatch",
    )(base, seg, lstart, pad_ends, x_local)


def _ffn_kernel(be_ref, nu_ref, x_ref, wg_ref, wu_ref, wd_ref, y_ref, wg_bf, wu_bf, wd_bf):
    i = pl.program_id(0)
    used = i < nu_ref[0]
    new_expert = jnp.logical_or(i == 0, be_ref[i] != be_ref[jnp.maximum(i - 1, 0)])

    @pl.when(jnp.logical_and(used, new_expert))
    def _():
        wg_bf[...] = wg_ref[0, 0].astype(bf16)
        wu_bf[...] = wu_ref[0, 0].astype(bf16)
        wd_bf[...] = wd_ref[0, 0].astype(bf16)

    @pl.when(used)
    def _():
        x = _unpack_bf16_pairs(x_ref[...])
        gt = _dot(x, wg_bf[...])
        up = _dot(x, wu_bf[...])
        hid = (gt * _sigmoid(gt) * up).astype(bf16)
        y = _dot(hid, wd_bf[...])
        y_ref[...] = _pack_bf16_pairs(y.astype(bf16).astype(f32))

    @pl.when(i >= nu_ref[0])
    def _():
        y_ref[...] = jnp.zeros_like(y_ref)


def _ffn_call(blk_expert, n_used, x_sorted, wg, wu, wd, layer):
    nrows = x_sorted.shape[0]
    nblk = nrows // MOE_BLK
    wspec = pl.BlockSpec((1, 1, D, D), lambda i, be, nu: (layer, be[i], 0, 0))
    return pl.pallas_call(
        _ffn_kernel,
        grid_spec=pltpu.PrefetchScalarGridSpec(
            num_scalar_prefetch=2,
            grid=(nblk,),
            in_specs=[pl.BlockSpec((MOE_BLK, D // 2), lambda i, be, nu: (jnp.minimum(i, nu[0] - 1), 0)),
                      wspec, wspec, wspec],
            out_specs=pl.BlockSpec((MOE_BLK, D // 2), lambda i, be, nu: (i, 0)),
            scratch_shapes=[pltpu.VMEM((D, D), bf16)] * 3,
        ),
        out_shape=jax.ShapeDtypeStruct((nrows, D // 2), u32),
        compiler_params=_cparams(("arbitrary",)),
        name="moe_ffn",
    )(blk_expert, n_used, x_sorted, wg, wu, wd)


def _combine_kernel(base_ref, seg_ref, ls_ref, x_ref, lp_ref, gt_ref, mod_ref, g_ref, y_ref, o_ref, ybuf, sem,
                    *, tiles_per_batch, first_tile, final):
    b = pl.program_id(0)
    i = pl.program_id(1)
    nt = pl.num_programs(1)
    step = b * nt + i
    slot = lax.rem(step, 2)
    tile = b * tiles_per_batch + i + first_tile
    next_tile = jnp.where(i + 1 < nt, tile + 1, (b + 1) * tiles_per_batch + first_tile)

    def copies(fn, which_tile, which_slot):
        _segment_copies(fn, which_tile, base_ref, seg_ref, ls_ref, y_ref, ybuf.at[which_slot],
                        sem.at[which_slot], True)

    @pl.when(step == 0)
    def _():
        ybuf[...] = jnp.zeros_like(ybuf)
        copies(lambda cp: cp.start(), tile, slot)

    @pl.when(step + 1 < pl.num_programs(0) * nt)
    def _():
        copies(lambda cp: cp.start(), next_tile, 1 - slot)

    copies(lambda cp: cp.wait(), tile, slot)
    y_loc = _unpack_bf16_pairs(ybuf[slot])
    tm = x_ref.shape[1]
    jj = lax.broadcasted_iota(i32, (LROWS, tm), 0)
    gmat = (jnp.where(jj == lp_ref[0:1, :], gt_ref[0:1, :], 0.0)
            + jnp.where(jj == lp_ref[1:2, :], gt_ref[1:2, :], 0.0))
    g_hi, g_lo = _split2(gmat)
    moe = _dot_tn(g_hi, y_loc) + _dot_tn(g_lo, y_loc)
    xn = x_ref[0] + mod_ref[0, :, 5 * D:6 * D] * moe
    o_ref[0] = _rmsnorm(xn, g_ref[...]) if final else xn


def _combine_call(base, seg, lstart, x, lpos, gates, mod, y_sorted, final_g):
    nbatch, tb, _ = x.shape
    ctx_tiles = CTX // TM
    tiles_per_batch = tb // TM
    final = final_g is not None
    first_tile = ctx_tiles if final else 0
    nt = tiles_per_batch - first_tile
    tile = pl.BlockSpec((1, TM, D), lambda b, i, *_: (b, i + first_tile, 0))
    route = pl.BlockSpec((TOP_K, TM), lambda b, i, *_: (0, b * tiles_per_batch + i + first_tile))
    mod_map = _mod_index(nbatch, ctx_tiles - first_tile)
    g_arr = final_g if final else jnp.ones((1, D), f32)
    return pl.pallas_call(
        functools.partial(_combine_kernel, tiles_per_batch=tiles_per_batch, first_tile=first_tile, final=final),
        grid_spec=pltpu.PrefetchScalarGridSpec(
            num_scalar_prefetch=3,
            grid=(nbatch, nt),
            in_specs=[tile, route, route,
                      pl.BlockSpec((1, 1, 6 * D), lambda b, i, *_: mod_map(b, i)),
                      pl.BlockSpec((1, D), lambda b, i, *_: (0, 0)),
                      pl.BlockSpec(memory_space=pl.ANY)],
            out_specs=pl.BlockSpec((1, TM, D), lambda b, i, *_: (b, i, 0)),
            scratch_shapes=[pltpu.VMEM((2, LROWS, D // 2), u32), pltpu.SemaphoreType.DMA((2,))],
        ),
        out_shape=jax.ShapeDtypeStruct((nbatch, nt * TM, D), f32),
        compiler_params=_cparams(("arbitrary", "arbitrary")),
        name="ffn_residual_final" if final else "ffn_residual",
    )(base, seg, lstart, x, lpos, gates, mod, g_arr, y_sorted)


def _moe_rows(ntok):
    ntiles = ntok // TM
    worst = TOP_K * ntok + ntiles * N_EXPERTS * (SEG_ALIGN - 1)
    return (-(-worst // MOE_BLK) + N_EXPERTS) * MOE_BLK


def _segment_plan(cnt, nblk):
    seg = (cnt + SEG_ALIGN - 1) // SEG_ALIGN * SEG_ALIGN
    lstart = jnp.cumsum(seg, axis=1) - seg
    rows = jnp.sum(seg, axis=0)
    padded = (rows + MOE_BLK - 1) // MOE_BLK * MOE_BLK
    pad_ends = jnp.cumsum(padded)
    base = (pad_ends - padded)[None, :] + jnp.cumsum(seg, axis=0) - seg
    blk_start = jnp.arange(nblk, dtype=i32) * MOE_BLK
    blk_expert = jnp.minimum(jnp.sum(pad_ends[None, :] <= blk_start[:, None], axis=1), N_EXPERTS - 1)
    n_used = (pad_ends[-1] // MOE_BLK).reshape(1)
    flat = lambda a: a.reshape(-1).astype(i32)
    return flat(base), flat(seg), flat(lstart), flat(pad_ends), blk_expert.astype(i32), n_used.astype(i32)


def _rope_tables(tb):
    rows = SEQ // GRID_W
    row = jnp.repeat(jnp.arange(rows, dtype=f32), GRID_W)
    col = jnp.tile(jnp.arange(GRID_W, dtype=f32), rows)
    inv_freq = ROPE_BASE ** (-jnp.arange(ROPE_FREQS, dtype=f32) / ROPE_FREQS)
    ang_r = row[:, None] * inv_freq[None, :]
    ang_c = col[:, None] * inv_freq[None, :]
    ang = jnp.concatenate([ang_r, ang_r, ang_c, ang_c], axis=-1)
    cos = jnp.concatenate([jnp.ones((CTX, HEAD_DIM), f32), jnp.cos(ang)], axis=0)
    sin = jnp.concatenate([jnp.zeros((CTX, HEAD_DIM), f32), jnp.sin(ang)], axis=0)
    return jnp.tile(cos, (1, LANES // HEAD_DIM)), jnp.tile(sin, (1, LANES // HEAD_DIM))


def _block_diag2(w):
    z = jnp.zeros_like(w[0])
    return jnp.concatenate([jnp.concatenate([w[0], z], axis=1), jnp.concatenate([z, w[1]], axis=1)], axis=0)


def kernel(x, c, ctx, c_ctx, w_mod, b_mod, norm_mix_g, norm_ffn_g, w_in, w_out, att_sink, shift_mu_prev, shift_mu_next, decay_w0, decay_w2, iclr_a0, iclr_a2, vres_v0, vres_v1, vres_v2, gate_g2, k_k, k_a, r_k, ln_x_w, ln_x_b, router_w, router_b, expert_w_gate, expert_w_up, expert_w_down, final_norm_g):
    nbatch = x.shape[0]
    depth = w_mod.shape[0]
    tb = ctx.shape[1] + x.shape[1]
    xa = jnp.concatenate([ctx, x], axis=1)
    nb_pad = -(-(nbatch + 1) // 8) * 8
    cond = jnp.zeros((nb_pad, D), f32).at[:nbatch].set(c).at[nbatch].set(c_ctx)
    mod_all = _mod_call(cond, w_mod, b_mod).reshape(depth, nb_pad, 1, 6 * D)
    cos, sin = _rope_tables(tb)
    wr_hi = router_w.T.astype(bf16)
    wr_lo = (router_w.T - wr_hi.astype(f32)).astype(bf16)
    w_router = jnp.concatenate([wr_hi, wr_lo], axis=0)
    b_router = router_b.reshape(N_EXPERTS, 1)
    v_first = None
    for l in range(depth):
        mod = mod_all[l]
        q, k, v, rw = _in_proj_call(xa, mod, norm_mix_g[l].reshape(1, D), w_in[l].astype(bf16), cos, sin)
        att = _attn_call(att_sink[l], q, k, v)
        mu = jnp.stack([shift_mu_prev[l], shift_mu_next[l]])
        v0 = vres_v0[l - 1] if l > 0 else jnp.zeros((RW,), f32)
        vec = jnp.stack([k_k[l], k_a[l], r_k[l].reshape(RW), v0,
                         decay_w0[l, 0], decay_w0[l, 1], iclr_a0[l, 0], iclr_a0[l, 1]])
        if l > 0:
            v1 = jnp.zeros((RW, LANES), f32).at[:, :LORA_VRES].set(vres_v1[l - 1]).astype(bf16)
            v2 = jnp.zeros((LANES, RW), f32).at[:LORA_VRES].set(vres_v2[l - 1]).astype(bf16)
        else:
            v1 = v2 = None
        r_, v_, kk, bv, g, kd, lw, bd = _feat_call(
            rw, v_first, mu, vec, _block_diag2(decay_w2[l]).astype(bf16), _block_diag2(iclr_a2[l]).astype(bf16),
            gate_g2[l].astype(bf16), v1, v2)
        if l == 0:
            v_first = v_
        yf, yb = _scan_call(r_, v_, kk, kd, lw, bd)
        ln = jnp.stack([ln_x_w[l], ln_x_b[l]])
        xa, x_local, lpos, gates, cnt = _mix_call(xa, att, yf, yb, bv, g, mod, ln, w_out[l].astype(bf16),
                                                  norm_ffn_g[l].reshape(1, D), w_router, b_router)
        ntok = nbatch * tb
        nrows = _moe_rows(ntok)
        xa = xa.reshape(nbatch, tb, D)
        cnt = cnt[:, 0].astype(i32).reshape(ntok // TM, N_EXPERTS)
        base, seg, lstart, pad_ends, blk_expert, n_used = _segment_plan(cnt, nrows // MOE_BLK)
        x_sorted = _dispatch_call(base, seg, lstart, pad_ends, x_local, nrows)
        y_sorted = _ffn_call(blk_expert, n_used, x_sorted, expert_w_gate, expert_w_up, expert_w_down, l)
        xa = _combine_call(base, seg, lstart, xa, lpos, gates, mod, y_sorted,
                           final_norm_g.reshape(1, D) if l == depth - 1 else None)
    return xa
```

```python
import functools
import math

import jax
import jax.numpy as jnp
from jax import lax
from jax.experimental import pallas as pl
from jax.experimental.pallas import tpu as pltpu

f32 = jnp.float32
bf16 = jnp.bfloat16
i32 = jnp.int32
u32 = jnp.uint32

D = 1024
SEQ = 4096
CTX = 256
TB = CTX + SEQ
GRID_W = 64
HEAD_DIM = 64
ATT_WIDTH = 512
ATT_HEADS = 8
KV_HEADS = 2
ATT_GROUP = ATT_HEADS // KV_HEADS
KV_WIDTH = KV_HEADS * HEAD_DIM
RW = 512
RWKV_HEADS = 8
LORA_DECAY = 64
LORA_ICLR = 64
LORA_VRES = 32
LORA_GATE = 128
RWKV_COLS = 3 * RW + 2 * (LORA_DECAY + LORA_ICLR) + LORA_GATE
ATT_COLS = ATT_WIDTH + 2 * KV_WIDTH
IN_COLS = ATT_COLS + RWKV_COLS
N_EXPERTS = 16
N_GROUPS = 4
EXPERTS_PER_GROUP = 4
TOP_K = 2
MOE_BLK = 256
NORM_EPS = 1e-6
GN_EPS = 64e-5
NEG_INF = -1e30
ATT_SCALE = HEAD_DIM ** -0.5
ROPE_BASE = 10000.0
ROPE_FREQS = HEAD_DIM // 4

LANES = 128
TM = 256
QB = 128
CH = 64
HG = 4
GW = HG * HEAD_DIM
SCAN_BATCH = 4
MIX_TILES = 2
SEG_ALIGN = 8
LROWS = -(-(TOP_K * TM + N_EXPERTS * SEG_ALIGN) // LANES) * LANES
VMEM_LIMIT = 48 * 1024 * 1024


def _cparams(sem):
    return pltpu.CompilerParams(dimension_semantics=sem, vmem_limit_bytes=VMEM_LIMIT)


def _sigmoid(x):
    return 0.5 * jnp.tanh(0.5 * x) + 0.5


def _div_pow2(x, n):
    assert n & (n - 1) == 0
    return lax.shift_right_logical(x, n.bit_length() - 1)


def _mod_pow2(x, n):
    assert n & (n - 1) == 0
    return lax.bitwise_and(x, n - 1)


def _round_up_pow2(x, n):
    assert n & (n - 1) == 0
    return lax.bitwise_and(x + (n - 1), ~(n - 1))


def _dot(a, b):
    return jnp.dot(a, b, preferred_element_type=f32)


def _dot_nt(a, b):
    return lax.dot_general(a, b, (((1,), (1,)), ((), ())), preferred_element_type=f32)


def _dot_tn(a, b):
    return lax.dot_general(a, b, (((0,), (0,)), ((), ())), preferred_element_type=f32)


def _split2(x):
    hi = x.astype(bf16)
    lo = (x - hi.astype(f32)).astype(bf16)
    return hi, lo


def _dot_split_lhs(x, m):
    hi, lo = _split2(x)
    return _dot(hi, m) + _dot(lo, m)


def _rmsnorm(x, g):
    ms = jnp.mean(x * x, axis=-1, keepdims=True)
    return x * lax.rsqrt(ms + NORM_EPS) * g


def _head_ones():
    r = _div_pow2(lax.broadcasted_iota(i32, (RW, RW), 0), HEAD_DIM)
    c = _div_pow2(lax.broadcasted_iota(i32, (RW, RW), 1), HEAD_DIM)
    return (r == c).astype(bf16)


def _mod_kernel(c_ref, w_ref, b_ref, o_ref):
    c = c_ref[...]
    s = (c * _sigmoid(c)).astype(bf16)
    o_ref[0] = _dot(s, w_ref[0].astype(bf16)) + b_ref[0]


def _mod_call(cond, w_mod, b_mod):
    nb = cond.shape[0]
    depth = w_mod.shape[0]
    tn = 1024
    return pl.pallas_call(
        _mod_kernel,
        grid=(depth, 6 * D // tn),
        in_specs=[
            pl.BlockSpec((nb, D), lambda l, j: (0, 0)),
            pl.BlockSpec((1, D, tn), lambda l, j: (l, 0, j)),
            pl.BlockSpec((1, 1, tn), lambda l, j: (l, 0, j)),
        ],
        out_specs=pl.BlockSpec((1, nb, tn), lambda l, j: (l, 0, j)),
        out_shape=jax.ShapeDtypeStruct((depth, nb, 6 * D), f32),
        compiler_params=_cparams(("arbitrary", "arbitrary")),
        name="mod",
    )(cond, w_mod, b_mod.reshape(depth, 1, 6 * D))


def _mod_index(nbatch, ctx_tiles):
    return lambda b, i: (jnp.where(i < ctx_tiles, nbatch, b), 0, 0)


def _in_proj_kernel(x_ref, mod_ref, g_ref, w_ref, cos_ref, sin_ref, q_ref, k_ref, v_ref, rw_ref):
    x = x_ref[0]
    tm = x.shape[0]
    h = _rmsnorm(x, g_ref[...])
    sh = mod_ref[0, :, 0:D]
    sc = mod_ref[0, :, D:2 * D]
    h = (h * (1.0 + sc) + sh).astype(bf16)
    p = _dot(h, w_ref[...])
    cos = cos_ref[...]
    sin = sin_ref[...]
    lane = lax.broadcasted_iota(i32, (tm, LANES), 1)
    first_half = _mod_pow2(lane, 2 * ROPE_FREQS) < ROPE_FREQS

    def rope(t):
        rot = jnp.where(first_half, -pltpu.roll(t, LANES - ROPE_FREQS, 1), pltpu.roll(t, ROPE_FREQS, 1))
        return t * cos + rot * sin

    for j in range(ATT_WIDTH // LANES):
        t = (rope(p[:, j * LANES:(j + 1) * LANES]) * ATT_SCALE).astype(bf16)
        q_ref[0, 2 * j] = t[:, :HEAD_DIM]
        q_ref[0, 2 * j + 1] = t[:, HEAD_DIM:]
    kt = rope(p[:, ATT_WIDTH:ATT_WIDTH + KV_WIDTH]).astype(bf16)
    vt = p[:, ATT_WIDTH + KV_WIDTH:ATT_COLS].astype(bf16)
    for hh in range(KV_HEADS):
        k_ref[0, hh] = kt[:, hh * HEAD_DIM:(hh + 1) * HEAD_DIM]
        v_ref[0, hh] = vt[:, hh * HEAD_DIM:(hh + 1) * HEAD_DIM]
    rw_ref[0] = p[:, ATT_COLS:]


def _in_proj_call(x, mod, g, w_in, cos, sin):
    nbatch, tb, _ = x.shape
    nt = tb // TM
    return pl.pallas_call(
        _in_proj_kernel,
        grid=(nbatch, nt),
        in_specs=[
            pl.BlockSpec((1, TM, D), lambda b, i: (b, i, 0)),
            pl.BlockSpec((1, 1, 6 * D), _mod_index(nbatch, CTX // TM)),
            pl.BlockSpec((1, D), lambda b, i: (0, 0)),
            pl.BlockSpec((D, IN_COLS), lambda b, i: (0, 0)),
            pl.BlockSpec((TM, LANES), lambda b, i: (i, 0)),
            pl.BlockSpec((TM, LANES), lambda b, i: (i, 0)),
        ],
        out_specs=[
            pl.BlockSpec((1, ATT_HEADS, TM, HEAD_DIM), lambda b, i: (b, 0, i, 0)),
            pl.BlockSpec((1, KV_HEADS, TM, HEAD_DIM), lambda b, i: (b, 0, i, 0)),
            pl.BlockSpec((1, KV_HEADS, TM, HEAD_DIM), lambda b, i: (b, 0, i, 0)),
            pl.BlockSpec((1, TM, RWKV_COLS), lambda b, i: (b, i, 0)),
        ],
        out_shape=[
            jax.ShapeDtypeStruct((nbatch, ATT_HEADS, tb, HEAD_DIM), bf16),
            jax.ShapeDtypeStruct((nbatch, KV_HEADS, tb, HEAD_DIM), bf16),
            jax.ShapeDtypeStruct((nbatch, KV_HEADS, tb, HEAD_DIM), bf16),
            jax.ShapeDtypeStruct((nbatch, tb, RWKV_COLS), f32),
        ],
        compiler_params=_cparams(("parallel", "arbitrary")),
        name="in_proj",
    )(x, mod, g, w_in, cos, sin)


def _attn_kernel(sink_ref, q_ref, kp_ref, km_ref, kn_ref, vp_ref, vm_ref, vn_ref, kx_ref, vx_ref, o_ref,
                 *, npairs, ctx_pairs):
    j = pl.program_id(1)
    is_lat = j >= ctx_pairs
    before_ok = jnp.logical_and(is_lat, j - 1 >= ctx_pairs)
    after_ok = jnp.logical_and(is_lat, j + 1 <= npairs - 1)
    rows = ATT_GROUP * QB
    qi = _mod_pow2(lax.broadcasted_iota(i32, (rows, QB), 0), QB)
    kj = lax.broadcasted_iota(i32, (rows, QB), 1)
    band_p = kj >= qi
    band_n = kj <= qi
    row_head = _div_pow2(lax.broadcasted_iota(i32, (rows, 1), 0), QB)
    half = (slice(0, QB), slice(QB, 2 * QB))
    chains = [(c, h) for c in range(2) for h in range(KV_HEADS)]
    qh, kprev, kcur, knext, vprev, vcur, vnext, ok_p, ok_n = [], [], [], [], [], [], [], [], []
    for c, h in chains:
        qh.append(q_ref[0, ATT_GROUP * h:ATT_GROUP * (h + 1), half[c], :].reshape(rows, HEAD_DIM))
        kcur.append(km_ref[0, h, half[c], :])
        vcur.append(vm_ref[0, h, half[c], :])
        if c == 0:
            kprev.append(kp_ref[0, h]); vprev.append(vp_ref[0, h]); ok_p.append(before_ok)
            knext.append(km_ref[0, h, half[1], :]); vnext.append(vm_ref[0, h, half[1], :]); ok_n.append(is_lat)
        else:
            kprev.append(km_ref[0, h, half[0], :]); vprev.append(vm_ref[0, h, half[0], :]); ok_p.append(is_lat)
            knext.append(kn_ref[0, h]); vnext.append(vn_ref[0, h]); ok_n.append(after_ok)
    n = range(len(chains))
    s_p = [jnp.where(jnp.logical_and(band_p, ok_p[i]), _dot_nt(qh[i], kprev[i]), NEG_INF) for i in n]
    s_c = [jnp.where(is_lat, _dot_nt(qh[i], kcur[i]), NEG_INF) for i in n]
    s_n = [jnp.where(jnp.logical_and(band_n, ok_n[i]), _dot_nt(qh[i], knext[i]), NEG_INF) for i in n]
    s_x = [_dot_nt(qh[i], kx_ref[0, chains[i][1]]) for i in n]
    sink = []
    for h in range(KV_HEADS):
        sk = jnp.zeros((rows, 1), f32)
        for g in range(ATT_GROUP):
            sk = jnp.where(row_head == g, sink_ref[ATT_GROUP * h + g], sk)
        sink.append(sk)
    sink = [sink[h] for _, h in chains]
    m = [jnp.maximum(jnp.max(jnp.maximum(jnp.maximum(s_p[i], s_c[i]),
                                         jnp.maximum(jnp.maximum(s_n[i], s_x[i][:, :QB]), s_x[i][:, QB:])),
                             axis=-1, keepdims=True), sink[i]) for i in n]
    e_p = [jnp.exp(s_p[i] - m[i]) for i in n]
    e_c = [jnp.exp(s_c[i] - m[i]) for i in n]
    e_n = [jnp.exp(s_n[i] - m[i]) for i in n]
    e_x = [jnp.exp(s_x[i] - m[i]) for i in n]
    den = [jnp.sum((e_p[i] + e_c[i]) + (e_n[i] + e_x[i][:, :QB]) + e_x[i][:, QB:], axis=-1, keepdims=True)
           + jnp.exp(sink[i] - m[i]) for i in n]
    o = [(_dot(e_p[i].astype(bf16), vprev[i]) + _dot(e_c[i].astype(bf16), vcur[i])
          + _dot(e_n[i].astype(bf16), vnext[i]) + _dot(e_x[i].astype(bf16), vx_ref[0, chains[i][1]])) / den[i]
         for i in n]
    for c in range(2):
        o_ref[0, half[c], :] = jnp.concatenate(
            [o[c * KV_HEADS + h][g * QB:(g + 1) * QB] for h in range(KV_HEADS) for g in range(ATT_GROUP)],
            axis=1).astype(bf16)


def _attn_call(sink, q, k, v):
    nbatch, _, tb, _ = q.shape
    nblk = tb // QB
    npairs = nblk // 2
    assert CTX == 2 * QB and nblk % 2 == 0
    kv_blk = (1, KV_HEADS, QB, HEAD_DIM)
    pair_blk = (1, KV_HEADS, 2 * QB, HEAD_DIM)
    before_map = lambda b, j: (b, 0, jnp.maximum(2 * j - 1, 0), 0)
    pair_map = lambda b, j: (b, 0, j, 0)
    after_map = lambda b, j: (b, 0, jnp.minimum(2 * j + 2, nblk - 1), 0)
    ctx_spec = pl.BlockSpec((1, KV_HEADS, CTX, HEAD_DIM), lambda b, j: (b, 0, 0, 0))
    return pl.pallas_call(
        functools.partial(_attn_kernel, npairs=npairs, ctx_pairs=CTX // (2 * QB)),
        grid=(nbatch, npairs),
        in_specs=[
            pl.BlockSpec(memory_space=pltpu.SMEM),
            pl.BlockSpec((1, ATT_HEADS, 2 * QB, HEAD_DIM), pair_map),
            pl.BlockSpec(kv_blk, before_map), pl.BlockSpec(pair_blk, pair_map), pl.BlockSpec(kv_blk, after_map),
            pl.BlockSpec(kv_blk, before_map), pl.BlockSpec(pair_blk, pair_map), pl.BlockSpec(kv_blk, after_map),
            ctx_spec, ctx_spec,
        ],
        out_specs=pl.BlockSpec((1, 2 * QB, ATT_WIDTH), lambda b, j: (b, j, 0)),
        out_shape=jax.ShapeDtypeStruct((nbatch, tb, ATT_WIDTH), bf16),
        compiler_params=_cparams(("parallel", "arbitrary")),
        name="attention",
    )(sink, q, k, k, k, v, v, v, k, v)


def _feat_kernel(*refs, nt, ctx_tiles, has_vres):
    if has_vres:
        (rw_ref, hp_ref, hn_ref, vf_ref, mu_ref, vec_ref, w2_ref, a2_ref, g2_ref, v1_ref, v2_ref,
         r_ref, v_ref, kk_ref, bv_ref, g_ref, kd_ref, lw_ref, bd_ref) = refs
    else:
        (rw_ref, hp_ref, hn_ref, mu_ref, vec_ref, w2_ref, a2_ref, g2_ref,
         r_ref, v_ref, kk_ref, bv_ref, g_ref, kd_ref, lw_ref, bd_ref) = refs
    i = pl.program_id(1)
    u0 = rw_ref[0]
    tm = u0.shape[0]
    prev_zero = jnp.logical_or(i == 0, i == ctx_tiles)
    next_zero = jnp.logical_or(i == ctx_tiles - 1, i == nt - 1)
    halo_p = jnp.where(prev_zero, 0.0, hp_ref[0, 7:8, :])
    halo_n = jnp.where(next_zero, 0.0, hn_ref[0, 0:1, :])
    row = lax.broadcasted_iota(i32, (tm, 1), 0)
    prev = jnp.where(row == 0, halo_p, pltpu.roll(u0, 1, 0))
    nxt = jnp.where(row == tm - 1, halo_n, pltpu.roll(u0, tm - 1, 0))
    mu_p = mu_ref[0:1, :]
    mu_n = mu_ref[1:2, :]
    u = u0 + mu_p * (prev - u0) + mu_n * (nxt - u0)

    r = u[:, 0:RW]
    k = u[:, RW:2 * RW]
    v = u[:, 2 * RW:3 * RW]
    wd = u[:, 3 * RW:3 * RW + 2 * LORA_DECAY]
    ad = u[:, 3 * RW + 2 * LORA_DECAY:3 * RW + 2 * (LORA_DECAY + LORA_ICLR)]
    gd = u[:, 3 * RW + 2 * (LORA_DECAY + LORA_ICLR):]
    k_k = vec_ref[0:1, :]
    k_a = vec_ref[1:2, :]
    r_k = vec_ref[2:3, :]
    ones = _head_ones()

    if has_vres:
        lo = _dot(v.astype(bf16), v1_ref[...])
        gate = _sigmoid(vec_ref[3:4, :] + _dot(lo.astype(bf16), v2_ref[...]))
        v = v + (vf_ref[0] - v) * gate
    decay_in = _dot(jnp.tanh(wd).astype(bf16), w2_ref[...])
    a_in = _dot(ad.astype(bf16), a2_ref[...])
    kk = k * k_k
    n2 = _dot_split_lhs(kk * kk, ones)
    kk = kk * lax.rsqrt(jnp.maximum(n2, 1e-24))
    g = _dot(_sigmoid(gd).astype(bf16), g2_ref[...])
    ksum = jnp.zeros_like(k)
    for d in range(2):
        w0 = vec_ref[4 + d:5 + d, :]
        a0 = vec_ref[6 + d:7 + d, :]
        lw = -_sigmoid(w0 + decay_in[:, d * RW:(d + 1) * RW]) * math.exp(-0.5)
        a = _sigmoid(a0 + a_in[:, d * RW:(d + 1) * RW])
        kd = k * (1.0 + (a - 1.0) * k_a)
        ksum = ksum + kd
        kd_ref[d, 0] = kd.astype(bf16)
        lw_ref[d, 0] = lw
        bd_ref[d, 0] = (kk * a).astype(bf16)
    bonus = _dot_split_lhs(r * ksum * r_k, ones)
    r_ref[0] = r.astype(bf16)
    v_ref[0] = v
    kk_ref[0] = kk.astype(bf16)
    bv_ref[0] = bonus * v
    g_ref[0] = g.astype(bf16)


def _feat_call(rw, v_first, mu, vec, w2bd, a2bd, g2, v1, v2):
    nbatch, tb, _ = rw.shape
    nt = tb // TM
    has_vres = v_first is not None
    sub = TM // 8
    tile = lambda w: pl.BlockSpec((1, TM, w), lambda b, i: (b, i, 0))
    full = lambda a: pl.BlockSpec(a.shape, lambda b, i: (0,) * a.ndim)
    in_specs = [
        tile(RWKV_COLS),
        pl.BlockSpec((1, 8, RWKV_COLS), lambda b, i: (b, jnp.maximum(i * sub - 1, 0), 0)),
        pl.BlockSpec((1, 8, RWKV_COLS), lambda b, i: (b, jnp.minimum((i + 1) * sub, tb // 8 - 1), 0)),
    ]
    args = [rw, rw, rw]
    if has_vres:
        in_specs.append(tile(RW))
        args.append(v_first)
    consts = [mu, vec, w2bd, a2bd, g2] + ([v1, v2] if has_vres else [])
    in_specs += [full(a) for a in consts]
    args += consts
    dir_spec = pl.BlockSpec((2, 1, TM, RW), lambda b, i: (0, b, i, 0))
    tok = lambda dt: jax.ShapeDtypeStruct((nbatch, tb, RW), dt)
    dtok = lambda dt: jax.ShapeDtypeStruct((2, nbatch, tb, RW), dt)
    return pl.pallas_call(
        functools.partial(_feat_kernel, nt=nt, ctx_tiles=CTX // TM, has_vres=has_vres),
        grid=(nbatch, nt),
        in_specs=in_specs,
        out_specs=[tile(RW)] * 5 + [dir_spec] * 3,
        out_shape=[tok(bf16), tok(f32), tok(bf16), tok(f32), tok(bf16), dtok(bf16), dtok(f32), dtok(bf16)],
        compiler_params=_cparams(("parallel", "arbitrary")),
        name="rwkv_features",
    )(*args)


def _block_diag_rows(x, width):
    cb = _div_pow2(lax.broadcasted_iota(i32, x.shape, 1), width)
    return jnp.concatenate([jnp.where(cb == h, x, jnp.zeros_like(x)) for h in range(HG)], axis=0)


def _scan_chunks(probs):
    wide = HG * CH
    row_t = lax.broadcasted_iota(i32, (CH, wide), 0)
    col_t = _mod_pow2(lax.broadcasted_iota(i32, (CH, wide), 1), CH)
    row_g = lax.broadcasted_iota(i32, (CH, GW), 0)
    incl_t = {False: col_t <= row_t, True: col_t >= row_t}
    strict_t = {False: col_t < row_t, True: col_t > row_t}
    eye_t = (row_t == col_t).astype(f32)
    rb = _div_pow2(lax.broadcasted_iota(i32, (GW, GW), 0), HEAD_DIM)
    cb = _div_pow2(lax.broadcasted_iota(i32, (GW, GW), 1), HEAD_DIM)
    n = len(probs)
    rev = [p[7] for p in probs]
    def cumsum_rows(x, reverse):
        s = 1
        while s < CH:
            if reverse:
                x = x + jnp.where(row_g < CH - s, pltpu.roll(x, CH - s, 0), 0.0)
            else:
                x = x + jnp.where(row_g >= s, pltpu.roll(x, s, 0), 0.0)
            s *= 2
        return x

    gam = [cumsum_rows(probs[i][5], rev[i]) for i in range(n)]
    ar, bk, k_t, b_t = [], [], [], []
    for i, (s_prev, r, v, kk, k, lw, b, _) in enumerate(probs):
        e_neg = jnp.exp(-gam[i])
        a_s = (-kk * jnp.exp(gam[i] - lw)).astype(bf16)
        r_s = (r * jnp.exp(gam[i])).astype(bf16)
        b_t.append(b * e_neg)
        k_t.append(k * e_neg)
        ar.append(jnp.concatenate([a_s, r_s], axis=0))
        bk.append(jnp.concatenate([_block_diag_rows(b_t[i].astype(bf16), HEAD_DIM),
                                   _block_diag_rows(k_t[i].astype(bf16), HEAD_DIM)], axis=0))
    gram = [_dot_nt(ar[i], bk[i]) for i in range(n)]
    ars = [_dot_nt(ar[i], probs[i][0].astype(bf16)) for i in range(n)]
    v_bd = [_block_diag_rows(probs[i][2].astype(bf16), HEAD_DIM) for i in range(n)]
    p0 = [jnp.where(strict_t[rev[i]], gram[i][:CH, :wide], 0.0).astype(bf16) for i in range(n)]
    lq = [jnp.concatenate([jnp.where(strict_t[rev[i]], gram[i][:CH, wide:], 0.0),
                           jnp.where(incl_t[rev[i]], gram[i][CH:, wide:], 0.0)], axis=0).astype(bf16)
          for i in range(n)]
    lqv = [_dot(lq[i], v_bd[i]) for i in range(n)]
    rhs = [ars[i][:CH] + lqv[i][:CH] for i in range(n)]
    t = [eye_t + p0[i].astype(f32) for i in range(n)]
    p = [_dot(p0[i], _block_diag_rows(p0[i], CH)).astype(bf16) for i in range(n)]
    m = 4
    while m < CH:
        tp = [_dot(jnp.concatenate([t[i].astype(bf16), p[i]], axis=0), _block_diag_rows(p[i], CH))
              for i in range(n)]
        t = [t[i] + tp[i][:CH] for i in range(n)]
        p = [tp[i][CH:].astype(bf16) for i in range(n)]
        m *= 2
    t = [t[i] + _dot(t[i].astype(bf16), _block_diag_rows(p[i], CH)) for i in range(n)]
    u = [_dot(t[i].astype(bf16), _block_diag_rows(rhs[i].astype(bf16), HEAD_DIM)) for i in range(n)]
    out = []
    for i in range(n):
        s_prev, v, lw = probs[i][0], probs[i][2], probs[i][5]
        q_b = jnp.where(incl_t[rev[i]], gram[i][CH:, :wide], 0.0).astype(bf16)
        u_bd = _block_diag_rows(u[i].astype(bf16), HEAD_DIM)
        y = ars[i][CH:] + lqv[i][CH:] + _dot(q_b, u_bd)
        eg = jnp.exp(jnp.sum(lw, axis=0, keepdims=True))
        vu = jnp.concatenate([v, u[i]], axis=0).astype(bf16)
        kb = jnp.concatenate([k_t[i] * eg, b_t[i] * eg], axis=0).astype(bf16)
        s_add = _dot_tn(vu, kb)
        out.append((s_prev * eg + jnp.where(rb == cb, s_add, 0.0), y))
    return out


def _scan_kernel(rf_ref, vf_ref, kkf_ref, kf_ref, lwf_ref, bf_ref,
                 rb_ref, vb_ref, kkb_ref, kb_ref, lwb_ref, bb_ref,
                 yf_ref, yb_ref, s_ref):
    @pl.when(pl.program_id(1) == 0)
    def _():
        s_ref[...] = jnp.zeros_like(s_ref)

    dirs = ((rf_ref, vf_ref, kkf_ref, kf_ref, lwf_ref, bf_ref, yf_ref),
            (rb_ref, vb_ref, kkb_ref, kb_ref, lwb_ref, bb_ref, yb_ref))
    probs, dest = [], []
    for bi in range(SCAN_BATCH):
        for d, (r_ref, v_ref, kk_ref, k_ref, lw_ref, b_ref, y_ref) in enumerate(dirs):
            for g in range(RW // GW):
                sl = slice(g * GW, (g + 1) * GW)
                probs.append((s_ref[bi, d, g], r_ref[bi, :, sl], v_ref[bi, :, sl], kk_ref[bi, :, sl],
                              k_ref[0, bi, :, sl], lw_ref[0, bi, :, sl], b_ref[0, bi, :, sl], d == 1))
                dest.append((bi, d, g, y_ref, sl))
    for (bi, d, g, y_ref, sl), (s_new, y) in zip(dest, _scan_chunks(probs)):
        s_ref[bi, d, g] = s_new
        y_ref[bi, :, sl] = y


def _scan_call(r, v, kk, kd, lw, bd):
    nbatch, tb, _ = r.shape
    nc = tb // CH
    cc = CTX // CH
    sb = SCAN_BATCH
    rev = lambda j: jnp.where(j < cc, cc - 1 - j, nc - 1 + cc - j)
    tok_f = pl.BlockSpec((sb, CH, RW), lambda b, j: (b, j, 0))
    tok_b = pl.BlockSpec((sb, CH, RW), lambda b, j: (b, rev(j), 0))
    dir_f = pl.BlockSpec((1, sb, CH, RW), lambda b, j: (0, b, j, 0))
    dir_b = pl.BlockSpec((1, sb, CH, RW), lambda b, j: (1, b, rev(j), 0))
    out = jax.ShapeDtypeStruct((nbatch, tb, RW), f32)
    return pl.pallas_call(
        _scan_kernel,
        grid=(nbatch // sb, nc),
        in_specs=[tok_f, tok_f, tok_f, dir_f, dir_f, dir_f, tok_b, tok_b, tok_b, dir_b, dir_b, dir_b],
        out_specs=[tok_f, tok_b],
        out_shape=[out, out],
        scratch_shapes=[pltpu.VMEM((sb, 2, RW // GW, GW, GW), f32)],
        compiler_params=_cparams(("parallel", "arbitrary")),
        name="rwkv_scan",
    )(r, v, kk, kd, lw, bd, r, v, kk, kd, lw, bd)


def _pack_bf16_pairs(x):
    bits = pltpu.bitcast(x, u32)
    half = x.shape[1] // 2
    return bits[:, :half] | lax.shift_right_logical(bits[:, half:], jnp.uint32(16))


def _unpack_bf16_pairs(p):
    hi = pltpu.bitcast(p & jnp.uint32(0xFFFF0000), f32)
    lo = pltpu.bitcast(lax.shift_left(p, jnp.uint32(16)), f32)
    return jnp.concatenate([hi, lo], axis=1).astype(bf16)


def _mix_kernel(x_ref, att_ref, yf_ref, yb_ref, bv_ref, g_ref, mod0_ref, mod1_ref, ln_ref, wo_ref, gf_ref,
                wr_ref, rb_ref, xo_ref, xl_ref, lp_ref, gt_ref, cnt_ref):
    ones = _head_ones()
    inv = 1.0 / HEAD_DIM
    tiles = range(MIX_TILES)
    rows = [slice(s * TM, (s + 1) * TM) for s in tiles]
    mods = (mod0_ref, mod1_ref)
    y = [yf_ref[rows[s], :] + yb_ref[rows[s], :] for s in tiles]
    mu = [_dot(y[s].astype(bf16), ones) * inv for s in tiles]
    dlt = [y[s] - mu[s] for s in tiles]
    var = [_dot((dlt[s] * dlt[s]).astype(bf16), ones) * inv for s in tiles]
    rwk = [((dlt[s] * lax.rsqrt(var[s] + GN_EPS) * ln_ref[0:1, :] + ln_ref[1:2, :] + bv_ref[rows[s], :])
            * g_ref[rows[s], :]).astype(bf16) for s in tiles]
    mix = [_dot(att_ref[rows[s], :], wo_ref[0:ATT_WIDTH, :]) + _dot(rwk[s], wo_ref[ATT_WIDTH:, :]) for s in tiles]
    xn = [x_ref[rows[s], :] + mods[s][0, :, 2 * D:3 * D] * mix[s] for s in tiles]
    for s in tiles:
        xo_ref[rows[s], :] = xn[s]
    hf = [_rmsnorm(xn[s], gf_ref[...]) * (1.0 + mods[s][0, :, 4 * D:5 * D]) + mods[s][0, :, 3 * D:4 * D]
          for s in tiles]
    split = [_split2(hf[s]) for s in tiles]
    by_hi = [_dot_nt(wr_ref[...], split[s][0]) for s in tiles]
    logits = [by_hi[s][:N_EXPERTS] + by_hi[s][N_EXPERTS:] + _dot_nt(wr_ref[0:N_EXPERTS, :], split[s][1])
              for s in tiles]
    sorted_rows, lpos, gates, cnt = _route_and_sort(hf, [_sigmoid(lg) for lg in logits], rb_ref[...])
    for s in tiles:
        xl_ref[s * LROWS:(s + 1) * LROWS, :] = sorted_rows[s]
        lp_ref[:, rows[s]] = lpos[s]
        gt_ref[:, rows[s]] = gates[s]
        cnt_ref[s * N_EXPERTS:(s + 1) * N_EXPERTS, :] = cnt[s]


def _route_and_sort(hf, scores, bias):
    tiles = range(len(hf))
    tm = hf[0].shape[0]
    routed = [_route_rows(scores[s], bias) for s in tiles]
    eio = lax.broadcasted_iota(i32, (N_EXPERTS, tm), 0)
    upper = (lax.broadcasted_iota(i32, (tm, tm), 0) < lax.broadcasted_iota(i32, (tm, tm), 1)).astype(bf16)
    lower = (lax.broadcasted_iota(i32, (N_EXPERTS, N_EXPERTS), 0)
             > lax.broadcasted_iota(i32, (N_EXPERTS, N_EXPERTS), 1)).astype(bf16)
    jj = lax.broadcasted_iota(i32, (LROWS, tm), 0)
    oh0 = [(eio == routed[s][0]).astype(f32) for s in tiles]
    oh1 = [(eio == routed[s][1]).astype(f32) for s in tiles]
    both = [oh0[s] + oh1[s] for s in tiles]
    before = [_dot(both[s].astype(bf16), upper) for s in tiles]
    cnt = [jnp.sum(both[s], axis=1, keepdims=True) for s in tiles]
    seg = [_round_up_pow2(cnt[s].astype(i32), SEG_ALIGN).astype(f32) for s in tiles]
    seg_start = [_dot(lower, jnp.broadcast_to(seg[s], (N_EXPERTS, tm)).astype(bf16)) for s in tiles]
    pos = [seg_start[s] + before[s] for s in tiles]
    lp0 = [jnp.sum(oh0[s] * pos[s], axis=0, keepdims=True).astype(i32) for s in tiles]
    lp1 = [jnp.sum(oh1[s] * pos[s], axis=0, keepdims=True).astype(i32) for s in tiles]
    sel_t = [jnp.logical_or(jj == lp0[s], jj == lp1[s]).astype(bf16) for s in tiles]
    sorted_rows = [_pack_bf16_pairs(_dot(sel_t[s], hf[s].astype(bf16))) for s in tiles]
    lpos = [jnp.concatenate([lp0[s], lp1[s]], axis=0) for s in tiles]
    gates = [jnp.concatenate([routed[s][2], routed[s][3]], axis=0) for s in tiles]
    cnt_out = [jnp.broadcast_to(cnt[s], (N_EXPERTS, LANES)) for s in tiles]
    return sorted_rows, lpos, gates, cnt_out


def _route_rows(scores, bias):
    biased = scores + bias
    row = lambda a, e: a[e:e + 1, :]
    best = None
    for gi in range(N_GROUPS):
        m = [row(biased, gi * EXPERTS_PER_GROUP + j) for j in range(EXPERTS_PER_GROUP)]
        gs = None
        for a in range(EXPERTS_PER_GROUP):
            for b in range(a + 1, EXPERTS_PER_GROUP):
                pair = m[a] + m[b]
                gs = pair if gs is None else jnp.maximum(gs, pair)
        if best is None:
            best, g_idx = gs, jnp.zeros(gs.shape, i32)
        else:
            better = gs > best
            g_idx = jnp.where(better, gi, g_idx)
            best = jnp.where(better, gs, best)

    def pick(a, j):
        out = row(a, j)
        for gi in range(1, N_GROUPS):
            out = jnp.where(g_idx == gi, row(a, gi * EXPERTS_PER_GROUP + j), out)
        return out

    vb = [pick(biased, j) for j in range(EXPERTS_PER_GROUP)]
    vs = [pick(scores, j) for j in range(EXPERTS_PER_GROUP)]

    def argmax_first(vals):
        bv, bi = vals[0], jnp.zeros(vals[0].shape, i32)
        for j in range(1, len(vals)):
            better = vals[j] > bv
            bi = jnp.where(better, j, bi)
            bv = jnp.where(better, vals[j], bv)
        return bi

    i1 = argmax_first(vb)
    i2 = argmax_first([jnp.where(i1 == j, -jnp.inf, vb[j]) for j in range(EXPERTS_PER_GROUP)])
    sel = lambda idx: sum(jnp.where(idx == j, vs[j], 0.0) for j in range(EXPERTS_PER_GROUP))
    s1, s2 = sel(i1), sel(i2)
    tot = s1 + s2
    base = g_idx * EXPERTS_PER_GROUP
    return base + i1, base + i2, s1 / tot, s2 / tot


def _mix_call(x, att, yf, yb, bv, g, mod, ln, w_out, g_ffn, w_router, b_router):
    nbatch, tb, _ = x.shape
    tiles_per_batch = tb // TM
    ntiles = nbatch * tiles_per_batch
    assert ntiles % MIX_TILES == 0 and MIX_TILES == 2
    ntok = nbatch * tb
    flat = lambda a: a.reshape(ntok, a.shape[-1])
    rows = lambda w: pl.BlockSpec((MIX_TILES * TM, w), lambda p: (p, 0))
    full = lambda a: pl.BlockSpec(a.shape, lambda p: (0,) * a.ndim)
    ctx_tiles = CTX // TM

    def mod_spec(s):
        def index(p):
            tile = MIX_TILES * p + s
            return (jnp.where(tile % tiles_per_batch < ctx_tiles, nbatch, tile // tiles_per_batch), 0, 0)
        return pl.BlockSpec((1, 1, 6 * D), index)

    route = pl.BlockSpec((TOP_K, MIX_TILES * TM), lambda p: (0, p))
    return pl.pallas_call(
        _mix_kernel,
        grid=(ntiles // MIX_TILES,),
        in_specs=[rows(D), rows(ATT_WIDTH), rows(RW), rows(RW), rows(RW), rows(RW), mod_spec(0), mod_spec(1),
                  full(ln), full(w_out), full(g_ffn), full(w_router), full(b_router)],
        out_specs=[rows(D), pl.BlockSpec((MIX_TILES * LROWS, D // 2), lambda p: (p, 0)), route, route,
                   pl.BlockSpec((MIX_TILES * N_EXPERTS, LANES), lambda p: (p, 0))],
        out_shape=[jax.ShapeDtypeStruct((ntok, D), f32),
                   jax.ShapeDtypeStruct((ntiles * LROWS, D // 2), u32),
                   jax.ShapeDtypeStruct((TOP_K, ntok), i32),
                   jax.ShapeDtypeStruct((TOP_K, ntok), f32),
                   jax.ShapeDtypeStruct((ntiles * N_EXPERTS, LANES), f32)],
        compiler_params=_cparams(("arbitrary",)),
        name="mix_out",
    )(flat(x), flat(att), flat(yf), flat(yb), flat(bv), flat(g), mod, mod, ln, w_out, g_ffn, w_router, b_router)


def _segment_copies(fn, tile, base_ref, seg_ref, ls_ref, src, dst, sem, src_is_global):
    for e in range(N_EXPERTS):
        idx = tile * N_EXPERTS + e
        seg = seg_ref[idx]
        g0 = base_ref[idx]
        l0 = ls_ref[idx]
        size = TM
        while size >= SEG_ALIGN:
            done = lax.bitwise_and(seg, ~(2 * size - 1))

            @pl.when(lax.bitwise_and(seg, size) != 0)
            def _():
                g_rows = pl.ds(pl.multiple_of(g0 + done, SEG_ALIGN), size)
                l_rows = pl.ds(pl.multiple_of(l0 + done, SEG_ALIGN), size)
                s_rows, d_rows = (g_rows, l_rows) if src_is_global else (l_rows, g_rows)
                fn(pltpu.make_async_copy(src.at[s_rows, :], dst.at[d_rows, :], sem))

            size //= 2


def _segment_waits(tile, seg_ref, src, dst, sem):
    total = seg_ref[tile * N_EXPERTS]
    for e in range(1, N_EXPERTS):
        total = total + seg_ref[tile * N_EXPERTS + e]
    size = 2 * TM
    assert LROWS < 2 * size
    while size >= SEG_ALIGN:
        @pl.when(lax.bitwise_and(total, size) != 0)
        def _():
            pltpu.make_async_copy(src.at[pl.ds(0, size), :], dst.at[pl.ds(0, size), :], sem).wait()

        size //= 2


def _dispatch_kernel(base_ref, seg_ref, ls_ref, ends_ref, xl_ref, xs_ref, zeros, sem):
    tile = pl.program_id(0)

    def zero_tails(fn):
        for e in range(N_EXPERTS):
            end = ends_ref[e]
            start = ends_ref[e - 1] if e else 0

            @pl.when(end > start)
            def _():
                rows = pl.ds(pl.multiple_of(end - MOE_BLK, MOE_BLK), MOE_BLK)
                fn(pltpu.make_async_copy(zeros, xs_ref.at[rows, :], sem))

    def zero_unused(fn):
        def body(blk, carry):
            fn(pltpu.make_async_copy(zeros, xs_ref.at[pl.ds(pl.multiple_of(blk * MOE_BLK, MOE_BLK), MOE_BLK), :],
                                     sem))
            return carry
        lax.fori_loop(ends_ref[N_EXPERTS - 1] // MOE_BLK, xs_ref.shape[0] // MOE_BLK, body, 0)

    @pl.when(tile == 0)
    def _():
        zeros[...] = jnp.zeros_like(zeros)
        zero_tails(lambda cp: cp.start())
        zero_unused(lambda cp: cp.start())
        zero_tails(lambda cp: cp.wait())
        zero_unused(lambda cp: cp.wait())

    args = (tile, base_ref, seg_ref, ls_ref, xl_ref, xs_ref, sem, False)
    _segment_copies(lambda cp: cp.start(), *args)
    _segment_waits(tile, seg_ref, xl_ref, xs_ref, sem)


def _dispatch_call(base, seg, lstart, pad_ends, x_local, nrows):
    ntiles = x_local.shape[0] // LROWS
    return pl.pallas_call(
        _dispatch_kernel,
        grid_spec=pltpu.PrefetchScalarGridSpec(
            num_scalar_prefetch=4,
            grid=(ntiles,),
            in_specs=[pl.BlockSpec((LROWS, D // 2), lambda t, *_: (t, 0))],
            out_specs=pl.BlockSpec(memory_space=pl.ANY),
            scratch_shapes=[pltpu.VMEM((MOE_BLK, D // 2), u32), pltpu.SemaphoreType.DMA(())],
        ),
        out_shape=jax.ShapeDtypeStruct((nrows, D // 2), u32),
        compiler_params=_cparams(("arbitrary",)),
        name="moe_dispatch",
    )(base, seg, lstart, pad_ends, x_local)


def _ffn_kernel(be_ref, nu_ref, x_ref, wg_ref, wu_ref, wd_ref, y_ref, wg_bf, wu_bf, wd_bf):
    i = pl.program_id(0)
    used = i < nu_ref[0]
    new_expert = jnp.logical_or(i == 0, be_ref[i] != be_ref[jnp.maximum(i - 1, 0)])

    @pl.when(jnp.logical_and(used, new_expert))
    def _():
        wg_bf[...] = wg_ref[0, 0].astype(bf16)
        wu_bf[...] = wu_ref[0, 0].astype(bf16)
        wd_bf[...] = wd_ref[0, 0].astype(bf16)

    @pl.when(used)
    def _():
        x = _unpack_bf16_pairs(x_ref[...])
        gt = _dot(x, wg_bf[...])
        up = _dot(x, wu_bf[...])
        hid = (gt * _sigmoid(gt) * up).astype(bf16)
        y = _dot(hid, wd_bf[...])
        y_ref[...] = _pack_bf16_pairs(y.astype(bf16).astype(f32))

    @pl.when(i >= nu_ref[0])
    def _():
        y_ref[...] = jnp.zeros_like(y_ref)


def _ffn_call(blk_expert, n_used, x_sorted, wg, wu, wd, layer):
    nrows = x_sorted.shape[0]
    nblk = nrows // MOE_BLK
    wspec = pl.BlockSpec((1, 1, D, D), lambda i, be, nu: (layer, be[i], 0, 0))
    return pl.pallas_call(
        _ffn_kernel,
        grid_spec=pltpu.PrefetchScalarGridSpec(
            num_scalar_prefetch=2,
            grid=(nblk,),
            in_specs=[pl.BlockSpec((MOE_BLK, D // 2), lambda i, be, nu: (jnp.minimum(i, nu[0] - 1), 0)),
                      wspec, wspec, wspec],
            out_specs=pl.BlockSpec((MOE_BLK, D // 2), lambda i, be, nu: (i, 0)),
            scratch_shapes=[pltpu.VMEM((D, D), bf16)] * 3,
        ),
        out_shape=jax.ShapeDtypeStruct((nrows, D // 2), u32),
        compiler_params=_cparams(("arbitrary",)),
        name="moe_ffn",
    )(blk_expert, n_used, x_sorted, wg, wu, wd)


def _combine_kernel(base_ref, seg_ref, ls_ref, x_ref, lp_ref, gt_ref, mod_ref, g_ref, y_ref, o_ref, ybuf, sem,
                    *, tiles_per_batch, first_tile, final):
    b = pl.program_id(0)
    i = pl.program_id(1)
    nt = pl.num_programs(1)
    step = b * nt + i
    slot = lax.rem(step, 2)
    tile = b * tiles_per_batch + i + first_tile
    next_tile = jnp.where(i + 1 < nt, tile + 1, (b + 1) * tiles_per_batch + first_tile)

    def copies(fn, which_tile, which_slot):
        _segment_copies(fn, which_tile, base_ref, seg_ref, ls_ref, y_ref, ybuf.at[which_slot],
                        sem.at[which_slot], True)

    @pl.when(step == 0)
    def _():
        ybuf[...] = jnp.zeros_like(ybuf)
        copies(lambda cp: cp.start(), tile, slot)

    @pl.when(step + 1 < pl.num_programs(0) * nt)
    def _():
        copies(lambda cp: cp.start(), next_tile, 1 - slot)

    _segment_waits(tile, seg_ref, y_ref, ybuf.at[slot], sem.at[slot])
    y_loc = _unpack_bf16_pairs(ybuf[slot])
    tm = x_ref.shape[1]
    jj = lax.broadcasted_iota(i32, (LROWS, tm), 0)
    gmat = (jnp.where(jj == lp_ref[0:1, :], gt_ref[0:1, :], 0.0)
            + jnp.where(jj == lp_ref[1:2, :], gt_ref[1:2, :], 0.0))
    g_hi, g_lo = _split2(gmat)
    moe = _dot_tn(g_hi, y_loc) + _dot_tn(g_lo, y_loc)
    xn = x_ref[0] + mod_ref[0, :, 5 * D:6 * D] * moe
    o_ref[0] = _rmsnorm(xn, g_ref[...]) if final else xn


def _combine_call(base, seg, lstart, x, lpos, gates, mod, y_sorted, final_g):
    nbatch, tb, _ = x.shape
    ctx_tiles = CTX // TM
    tiles_per_batch = tb // TM
    final = final_g is not None
    first_tile = ctx_tiles if final else 0
    nt = tiles_per_batch - first_tile
    tile = pl.BlockSpec((1, TM, D), lambda b, i, *_: (b, i + first_tile, 0))
    route = pl.BlockSpec((TOP_K, TM), lambda b, i, *_: (0, b * tiles_per_batch + i + first_tile))
    mod_map = _mod_index(nbatch, ctx_tiles - first_tile)
    g_arr = final_g if final else jnp.ones((1, D), f32)
    return pl.pallas_call(
        functools.partial(_combine_kernel, tiles_per_batch=tiles_per_batch, first_tile=first_tile, final=final),
        grid_spec=pltpu.PrefetchScalarGridSpec(
            num_scalar_prefetch=3,
            grid=(nbatch, nt),
            in_specs=[tile, route, route,
                      pl.BlockSpec((1, 1, 6 * D), lambda b, i, *_: mod_map(b, i)),
                      pl.BlockSpec((1, D), lambda b, i, *_: (0, 0)),
                      pl.BlockSpec(memory_space=pl.ANY)],
            out_specs=pl.BlockSpec((1, TM, D), lambda b, i, *_: (b, i, 0)),
            scratch_shapes=[pltpu.VMEM((2, LROWS, D // 2), u32), pltpu.SemaphoreType.DMA((2,))],
        ),
        out_shape=jax.ShapeDtypeStruct((nbatch, nt * TM, D), f32),
        compiler_params=_cparams(("arbitrary", "arbitrary")),
        name="ffn_residual_final" if final else "ffn_residual",
    )(base, seg, lstart, x, lpos, gates, mod, g_arr, y_sorted)


def _moe_rows(ntok):
    ntiles = ntok // TM
    worst = TOP_K * ntok + ntiles * N_EXPERTS * (SEG_ALIGN - 1)
    return (-(-worst // MOE_BLK) + N_EXPERTS) * MOE_BLK


def _segment_plan(cnt, nblk):
    seg = (cnt + SEG_ALIGN - 1) // SEG_ALIGN * SEG_ALIGN
    lstart = jnp.cumsum(seg, axis=1) - seg
    rows = jnp.sum(seg, axis=0)
    padded = (rows + MOE_BLK - 1) // MOE_BLK * MOE_BLK
    pad_ends = jnp.cumsum(padded)
    base = (pad_ends - padded)[None, :] + jnp.cumsum(seg, axis=0) - seg
    blk_start = jnp.arange(nblk, dtype=i32) * MOE_BLK
    blk_expert = jnp.minimum(jnp.sum(pad_ends[None, :] <= blk_start[:, None], axis=1), N_EXPERTS - 1)
    n_used = (pad_ends[-1] // MOE_BLK).reshape(1)
    flat = lambda a: a.reshape(-1).astype(i32)
    return flat(base), flat(seg), flat(lstart), flat(pad_ends), blk_expert.astype(i32), n_used.astype(i32)


def _rope_tables(tb):
    rows = SEQ // GRID_W
    row = jnp.repeat(jnp.arange(rows, dtype=f32), GRID_W)
    col = jnp.tile(jnp.arange(GRID_W, dtype=f32), rows)
    inv_freq = ROPE_BASE ** (-jnp.arange(ROPE_FREQS, dtype=f32) / ROPE_FREQS)
    ang_r = row[:, None] * inv_freq[None, :]
    ang_c = col[:, None] * inv_freq[None, :]
    ang = jnp.concatenate([ang_r, ang_r, ang_c, ang_c], axis=-1)
    cos = jnp.concatenate([jnp.ones((CTX, HEAD_DIM), f32), jnp.cos(ang)], axis=0)
    sin = jnp.concatenate([jnp.zeros((CTX, HEAD_DIM), f32), jnp.sin(ang)], axis=0)
    return jnp.tile(cos, (1, LANES // HEAD_DIM)), jnp.tile(sin, (1, LANES // HEAD_DIM))


def _block_diag2(w):
    z = jnp.zeros_like(w[0])
    return jnp.concatenate([jnp.concatenate([w[0], z], axis=1), jnp.concatenate([z, w[1]], axis=1)], axis=0)


def kernel(x, c, ctx, c_ctx, w_mod, b_mod, norm_mix_g, norm_ffn_g, w_in, w_out, att_sink, shift_mu_prev, shift_mu_next, decay_w0, decay_w2, iclr_a0, iclr_a2, vres_v0, vres_v1, vres_v2, gate_g2, k_k, k_a, r_k, ln_x_w, ln_x_b, router_w, router_b, expert_w_gate, expert_w_up, expert_w_down, final_norm_g):
    nbatch = x.shape[0]
    depth = w_mod.shape[0]
    tb = ctx.shape[1] + x.shape[1]
    xa = jnp.concatenate([ctx, x], axis=1)
    nb_pad = -(-(nbatch + 1) // 8) * 8
    cond = jnp.zeros((nb_pad, D), f32).at[:nbatch].set(c).at[nbatch].set(c_ctx)
    mod_all = _mod_call(cond, w_mod, b_mod).reshape(depth, nb_pad, 1, 6 * D)
    cos, sin = _rope_tables(tb)
    wr_hi = router_w.T.astype(bf16)
    wr_lo = (router_w.T - wr_hi.astype(f32)).astype(bf16)
    w_router = jnp.concatenate([wr_hi, wr_lo], axis=0)
    b_router = router_b.reshape(N_EXPERTS, 1)
    v_first = None
    for l in range(depth):
        mod = mod_all[l]
        q, k, v, rw = _in_proj_call(xa, mod, norm_mix_g[l].reshape(1, D), w_in[l].astype(bf16), cos, sin)
        att = _attn_call(att_sink[l], q, k, v)
        mu = jnp.stack([shift_mu_prev[l], shift_mu_next[l]])
        v0 = vres_v0[l - 1] if l > 0 else jnp.zeros((RW,), f32)
        vec = jnp.stack([k_k[l], k_a[l], r_k[l].reshape(RW), v0,
                         decay_w0[l, 0], decay_w0[l, 1], iclr_a0[l, 0], iclr_a0[l, 1]])
        if l > 0:
            v1 = jnp.zeros((RW, LANES), f32).at[:, :LORA_VRES].set(vres_v1[l - 1]).astype(bf16)
            v2 = jnp.zeros((LANES, RW), f32).at[:LORA_VRES].set(vres_v2[l - 1]).astype(bf16)
        else:
            v1 = v2 = None
        r_, v_, kk, bv, g, kd, lw, bd = _feat_call(
            rw, v_first, mu, vec, _block_diag2(decay_w2[l]).astype(bf16), _block_diag2(iclr_a2[l]).astype(bf16),
            gate_g2[l].astype(bf16), v1, v2)
        if l == 0:
            v_first = v_
        yf, yb = _scan_call(r_, v_, kk, kd, lw, bd)
        ln = jnp.stack([ln_x_w[l], ln_x_b[l]])
        xa, x_local, lpos, gates, cnt = _mix_call(xa, att, yf, yb, bv, g, mod, ln, w_out[l].astype(bf16),
                                                  norm_ffn_g[l].reshape(1, D), w_router, b_router)
        ntok = nbatch * tb
        nrows = _moe_rows(ntok)
        xa = xa.reshape(nbatch, tb, D)
        cnt = cnt[:, 0].astype(i32).reshape(ntok // TM, N_EXPERTS)
        base, seg, lstart, pad_ends, blk_expert, n_used = _segment_plan(cnt, nrows // MOE_BLK)
        x_sorted = _dispatch_call(base, seg, lstart, pad_ends, x_local, nrows)
        y_sorted = _ffn_call(blk_expert, n_used, x_sorted, expert_w_gate, expert_w_up, expert_w_down, l)
        xa = _combine_call(base, seg, lstart, xa, lpos, gates, mod, y_sorted,
                           final_norm_g.reshape(1, D) if l == depth - 1 else None)
    return xa
```

```python
import functools
import math

import jax
import jax.numpy as jnp
from jax import lax
from jax.experimental import pallas as pl
from jax.experimental.pallas import tpu as pltpu

f32 = jnp.float32
bf16 = jnp.bfloat16
i32 = jnp.int32
u32 = jnp.uint32

D = 1024
SEQ = 4096
CTX = 256
TB = CTX + SEQ
GRID_W = 64
HEAD_DIM = 64
ATT_WIDTH = 512
ATT_HEADS = 8
KV_HEADS = 2
ATT_GROUP = ATT_HEADS // KV_HEADS
KV_WIDTH = KV_HEADS * HEAD_DIM
RW = 512
RWKV_HEADS = 8
LORA_DECAY = 64
LORA_ICLR = 64
LORA_VRES = 32
LORA_GATE = 128
RWKV_COLS = 3 * RW + 2 * (LORA_DECAY + LORA_ICLR) + LORA_GATE
ATT_COLS = ATT_WIDTH + 2 * KV_WIDTH
IN_COLS = ATT_COLS + RWKV_COLS
N_EXPERTS = 16
N_GROUPS = 4
EXPERTS_PER_GROUP = 4
TOP_K = 2
MOE_BLK = 256
NORM_EPS = 1e-6
GN_EPS = 64e-5
NEG_INF = -1e30
ATT_SCALE = HEAD_DIM ** -0.5
ROPE_BASE = 10000.0
ROPE_FREQS = HEAD_DIM // 4

LANES = 128
TM = 256
QB = 128
CH = 64
HG = 4
GW = HG * HEAD_DIM
SCAN_BATCH = 4
MIX_TILES = 2
SEG_ALIGN = 8
LROWS = -(-(TOP_K * TM + N_EXPERTS * SEG_ALIGN) // LANES) * LANES
VMEM_LIMIT = 48 * 1024 * 1024


def _cparams(sem):
    return pltpu.CompilerParams(dimension_semantics=sem, vmem_limit_bytes=VMEM_LIMIT)


def _sigmoid(x):
    return 0.5 * jnp.tanh(0.5 * x) + 0.5


def _div_pow2(x, n):
    assert n & (n - 1) == 0
    return lax.shift_right_logical(x, n.bit_length() - 1)


def _mod_pow2(x, n):
    assert n & (n - 1) == 0
    return lax.bitwise_and(x, n - 1)


def _round_up_pow2(x, n):
    assert n & (n - 1) == 0
    return lax.bitwise_and(x + (n - 1), ~(n - 1))


def _dot(a, b):
    return jnp.dot(a, b, preferred_element_type=f32)


def _dot_nt(a, b):
    return lax.dot_general(a, b, (((1,), (1,)), ((), ())), preferred_element_type=f32)


def _dot_tn(a, b):
    return lax.dot_general(a, b, (((0,), (0,)), ((), ())), preferred_element_type=f32)


def _split2(x):
    hi = x.astype(bf16)
    lo = (x - hi.astype(f32)).astype(bf16)
    return hi, lo


def _dot_split_lhs(x, m):
    hi, lo = _split2(x)
    return _dot(hi, m) + _dot(lo, m)


def _rmsnorm(x, g):
    ms = jnp.mean(x * x, axis=-1, keepdims=True)
    return x * lax.rsqrt(ms + NORM_EPS) * g


def _head_ones():
    r = _div_pow2(lax.broadcasted_iota(i32, (RW, RW), 0), HEAD_DIM)
    c = _div_pow2(lax.broadcasted_iota(i32, (RW, RW), 1), HEAD_DIM)
    return (r == c).astype(bf16)


def _mod_kernel(c_ref, w_ref, b_ref, o_ref):
    c = c_ref[...]
    s = (c * _sigmoid(c)).astype(bf16)
    o_ref[0] = _dot(s, w_ref[0].astype(bf16)) + b_ref[0]


def _mod_call(cond, w_mod, b_mod):
    nb = cond.shape[0]
    depth = w_mod.shape[0]
    tn = 1024
    return pl.pallas_call(
        _mod_kernel,
        grid=(depth, 6 * D // tn),
        in_specs=[
            pl.BlockSpec((nb, D), lambda l, j: (0, 0)),
            pl.BlockSpec((1, D, tn), lambda l, j: (l, 0, j)),
            pl.BlockSpec((1, 1, tn), lambda l, j: (l, 0, j)),
        ],
        out_specs=pl.BlockSpec((1, nb, tn), lambda l, j: (l, 0, j)),
        out_shape=jax.ShapeDtypeStruct((depth, nb, 6 * D), f32),
        compiler_params=_cparams(("arbitrary", "arbitrary")),
        name="mod",
    )(cond, w_mod, b_mod.reshape(depth, 1, 6 * D))


def _mod_index(nbatch, ctx_tiles):
    return lambda b, i: (jnp.where(i < ctx_tiles, nbatch, b), 0, 0)


def _in_proj_kernel(x_ref, mod_ref, g_ref, w_ref, cos_ref, sin_ref, q_ref, k_ref, v_ref, rw_ref):
    x = x_ref[0]
    tm = x.shape[0]
    h = _rmsnorm(x, g_ref[...])
    sh = mod_ref[0, :, 0:D]
    sc = mod_ref[0, :, D:2 * D]
    h = (h * (1.0 + sc) + sh).astype(bf16)
    p = _dot(h, w_ref[...])
    cos = cos_ref[...]
    sin = sin_ref[...]
    lane = lax.broadcasted_iota(i32, (tm, LANES), 1)
    first_half = _mod_pow2(lane, 2 * ROPE_FREQS) < ROPE_FREQS

    def rope(t):
        rot = jnp.where(first_half, -pltpu.roll(t, LANES - ROPE_FREQS, 1), pltpu.roll(t, ROPE_FREQS, 1))
        return t * cos + rot * sin

    for j in range(ATT_WIDTH // LANES):
        t = (rope(p[:, j * LANES:(j + 1) * LANES]) * ATT_SCALE).astype(bf16)
        q_ref[0, 2 * j] = t[:, :HEAD_DIM]
        q_ref[0, 2 * j + 1] = t[:, HEAD_DIM:]
    kt = rope(p[:, ATT_WIDTH:ATT_WIDTH + KV_WIDTH]).astype(bf16)
    vt = p[:, ATT_WIDTH + KV_WIDTH:ATT_COLS].astype(bf16)
    for hh in range(KV_HEADS):
        k_ref[0, hh] = kt[:, hh * HEAD_DIM:(hh + 1) * HEAD_DIM]
        v_ref[0, hh] = vt[:, hh * HEAD_DIM:(hh + 1) * HEAD_DIM]
    rw_ref[0] = p[:, ATT_COLS:]


def _in_proj_call(x, mod, g, w_in, cos, sin):
    nbatch, tb, _ = x.shape
    nt = tb // TM
    return pl.pallas_call(
        _in_proj_kernel,
        grid=(nbatch, nt),
        in_specs=[
            pl.BlockSpec((1, TM, D), lambda b, i: (b, i, 0)),
            pl.BlockSpec((1, 1, 6 * D), _mod_index(nbatch, CTX // TM)),
            pl.BlockSpec((1, D), lambda b, i: (0, 0)),
            pl.BlockSpec((D, IN_COLS), lambda b, i: (0, 0)),
            pl.BlockSpec((TM, LANES), lambda b, i: (i, 0)),
            pl.BlockSpec((TM, LANES), lambda b, i: (i, 0)),
        ],
        out_specs=[
            pl.BlockSpec((1, ATT_HEADS, TM, HEAD_DIM), lambda b, i: (b, 0, i, 0)),
            pl.BlockSpec((1, KV_HEADS, TM, HEAD_DIM), lambda b, i: (b, 0, i, 0)),
            pl.BlockSpec((1, KV_HEADS, TM, HEAD_DIM), lambda b, i: (b, 0, i, 0)),
            pl.BlockSpec((1, TM, RWKV_COLS), lambda b, i: (b, i, 0)),
        ],
        out_shape=[
            jax.ShapeDtypeStruct((nbatch, ATT_HEADS, tb, HEAD_DIM), bf16),
            jax.ShapeDtypeStruct((nbatch, KV_HEADS, tb, HEAD_DIM), bf16),
            jax.ShapeDtypeStruct((nbatch, KV_HEADS, tb, HEAD_DIM), bf16),
            jax.ShapeDtypeStruct((nbatch, tb, RWKV_COLS), f32),
        ],
        compiler_params=_cparams(("parallel", "arbitrary")),
        name="in_proj",
    )(x, mod, g, w_in, cos, sin)


def _attn_kernel(sink_ref, q_ref, kp_ref, km_ref, kn_ref, vp_ref, vm_ref, vn_ref, kx_ref, vx_ref, o_ref,
                 *, npairs, ctx_pairs):
    j = pl.program_id(1)
    is_lat = j >= ctx_pairs
    before_ok = jnp.logical_and(is_lat, j - 1 >= ctx_pairs)
    after_ok = jnp.logical_and(is_lat, j + 1 <= npairs - 1)
    rows = ATT_GROUP * QB
    qi = _mod_pow2(lax.broadcasted_iota(i32, (rows, QB), 0), QB)
    kj = lax.broadcasted_iota(i32, (rows, QB), 1)
    band_p = kj >= qi
    band_n = kj <= qi
    row_head = _div_pow2(lax.broadcasted_iota(i32, (rows, 1), 0), QB)
    half = (slice(0, QB), slice(QB, 2 * QB))
    chains = [(c, h) for c in range(2) for h in range(KV_HEADS)]
    qh, kprev, kcur, knext, vprev, vcur, vnext, ok_p, ok_n = [], [], [], [], [], [], [], [], []
    for c, h in chains:
        qh.append(q_ref[0, ATT_GROUP * h:ATT_GROUP * (h + 1), half[c], :].reshape(rows, HEAD_DIM))
        kcur.append(km_ref[0, h, half[c], :])
        vcur.append(vm_ref[0, h, half[c], :])
        if c == 0:
            kprev.append(kp_ref[0, h]); vprev.append(vp_ref[0, h]); ok_p.append(before_ok)
            knext.append(km_ref[0, h, half[1], :]); vnext.append(vm_ref[0, h, half[1], :]); ok_n.append(is_lat)
        else:
            kprev.append(km_ref[0, h, half[0], :]); vprev.append(vm_ref[0, h, half[0], :]); ok_p.append(is_lat)
            knext.append(kn_ref[0, h]); vnext.append(vn_ref[0, h]); ok_n.append(after_ok)
    n = range(len(chains))
    s_p = [jnp.where(jnp.logical_and(band_p, ok_p[i]), _dot_nt(qh[i], kprev[i]), NEG_INF) for i in n]
    s_c = [jnp.where(is_lat, _dot_nt(qh[i], kcur[i]), NEG_INF) for i in n]
    s_n = [jnp.where(jnp.logical_and(band_n, ok_n[i]), _dot_nt(qh[i], knext[i]), NEG_INF) for i in n]
    s_x = [_dot_nt(qh[i], kx_ref[0, chains[i][1]]) for i in n]
    sink = []
    for h in range(KV_HEADS):
        sk = jnp.zeros((rows, 1), f32)
        for g in range(ATT_GROUP):
            sk = jnp.where(row_head == g, sink_ref[ATT_GROUP * h + g], sk)
        sink.append(sk)
    sink = [sink[h] for _, h in chains]
    m = [jnp.maximum(jnp.max(jnp.maximum(jnp.maximum(s_p[i], s_c[i]),
                                         jnp.maximum(jnp.maximum(s_n[i], s_x[i][:, :QB]), s_x[i][:, QB:])),
                             axis=-1, keepdims=True), sink[i]) for i in n]
    e_p = [jnp.exp(s_p[i] - m[i]) for i in n]
    e_c = [jnp.exp(s_c[i] - m[i]) for i in n]
    e_n = [jnp.exp(s_n[i] - m[i]) for i in n]
    e_x = [jnp.exp(s_x[i] - m[i]) for i in n]
    den = [jnp.sum((e_p[i] + e_c[i]) + (e_n[i] + e_x[i][:, :QB]) + e_x[i][:, QB:], axis=-1, keepdims=True)
           + jnp.exp(sink[i] - m[i]) for i in n]
    o = [(_dot(e_p[i].astype(bf16), vprev[i]) + _dot(e_c[i].astype(bf16), vcur[i])
          + _dot(e_n[i].astype(bf16), vnext[i]) + _dot(e_x[i].astype(bf16), vx_ref[0, chains[i][1]])) / den[i]
         for i in n]
    for c in range(2):
        o_ref[0, half[c], :] = jnp.concatenate(
            [o[c * KV_HEADS + h][g * QB:(g + 1) * QB] for h in range(KV_HEADS) for g in range(ATT_GROUP)],
            axis=1).astype(bf16)


def _attn_call(sink, q, k, v):
    nbatch, _, tb, _ = q.shape
    nblk = tb // QB
    npairs = nblk // 2
    assert CTX == 2 * QB and nblk % 2 == 0
    kv_blk = (1, KV_HEADS, QB, HEAD_DIM)
    pair_blk = (1, KV_HEADS, 2 * QB, HEAD_DIM)
    before_map = lambda b, j: (b, 0, jnp.maximum(2 * j - 1, 0), 0)
    pair_map = lambda b, j: (b, 0, j, 0)
    after_map = lambda b, j: (b, 0, jnp.minimum(2 * j + 2, nblk - 1), 0)
    ctx_spec = pl.BlockSpec((1, KV_HEADS, CTX, HEAD_DIM), lambda b, j: (b, 0, 0, 0))
    return pl.pallas_call(
        functools.partial(_attn_kernel, npairs=npairs, ctx_pairs=CTX // (2 * QB)),
        grid=(nbatch, npairs),
        in_specs=[
            pl.BlockSpec(memory_space=pltpu.SMEM),
            pl.BlockSpec((1, ATT_HEADS, 2 * QB, HEAD_DIM), pair_map),
            pl.BlockSpec(kv_blk, before_map), pl.BlockSpec(pair_blk, pair_map), pl.BlockSpec(kv_blk, after_map),
            pl.BlockSpec(kv_blk, before_map), pl.BlockSpec(pair_blk, pair_map), pl.BlockSpec(kv_blk, after_map),
            ctx_spec, ctx_spec,
        ],
        out_specs=pl.BlockSpec((1, 2 * QB, ATT_WIDTH), lambda b, j: (b, j, 0)),
        out_shape=jax.ShapeDtypeStruct((nbatch, tb, ATT_WIDTH), bf16),
        compiler_params=_cparams(("parallel", "arbitrary")),
        name="attention",
    )(sink, q, k, k, k, v, v, v, k, v)


def _feat_kernel(*refs, nt, ctx_tiles, has_vres):
    if has_vres:
        (rw_ref, hp_ref, hn_ref, vf_ref, mu_ref, vec_ref, w2_ref, a2_ref, g2_ref, v1_ref, v2_ref,
         r_ref, v_ref, kk_ref, bv_ref, g_ref, kd_ref, lw_ref, bd_ref) = refs
    else:
        (rw_ref, hp_ref, hn_ref, mu_ref, vec_ref, w2_ref, a2_ref, g2_ref,
         r_ref, v_ref, kk_ref, bv_ref, g_ref, kd_ref, lw_ref, bd_ref) = refs
    i = pl.program_id(1)
    u0 = rw_ref[0]
    tm = u0.shape[0]
    prev_zero = jnp.logical_or(i == 0, i == ctx_tiles)
    next_zero = jnp.logical_or(i == ctx_tiles - 1, i == nt - 1)
    halo_p = jnp.where(prev_zero, 0.0, hp_ref[0, 7:8, :])
    halo_n = jnp.where(next_zero, 0.0, hn_ref[0, 0:1, :])
    row = lax.broadcasted_iota(i32, (tm, 1), 0)
    prev = jnp.where(row == 0, halo_p, pltpu.roll(u0, 1, 0))
    nxt = jnp.where(row == tm - 1, halo_n, pltpu.roll(u0, tm - 1, 0))
    mu_p = mu_ref[0:1, :]
    mu_n = mu_ref[1:2, :]
    u = u0 + mu_p * (prev - u0) + mu_n * (nxt - u0)

    r = u[:, 0:RW]
    k = u[:, RW:2 * RW]
    v = u[:, 2 * RW:3 * RW]
    wd = u[:, 3 * RW:3 * RW + 2 * LORA_DECAY]
    ad = u[:, 3 * RW + 2 * LORA_DECAY:3 * RW + 2 * (LORA_DECAY + LORA_ICLR)]
    gd = u[:, 3 * RW + 2 * (LORA_DECAY + LORA_ICLR):]
    k_k = vec_ref[0:1, :]
    k_a = vec_ref[1:2, :]
    r_k = vec_ref[2:3, :]
    ones = _head_ones()

    if has_vres:
        lo = _dot(v.astype(bf16), v1_ref[...])
        gate = _sigmoid(vec_ref[3:4, :] + _dot(lo.astype(bf16), v2_ref[...]))
        v = v + (vf_ref[0] - v) * gate
    decay_in = _dot(jnp.tanh(wd).astype(bf16), w2_ref[...])
    a_in = _dot(ad.astype(bf16), a2_ref[...])
    kk = k * k_k
    n2 = _dot_split_lhs(kk * kk, ones)
    kk = kk * lax.rsqrt(jnp.maximum(n2, 1e-24))
    g = _dot(_sigmoid(gd).astype(bf16), g2_ref[...])
    ksum = jnp.zeros_like(k)
    for d in range(2):
        w0 = vec_ref[4 + d:5 + d, :]
        a0 = vec_ref[6 + d:7 + d, :]
        lw = -_sigmoid(w0 + decay_in[:, d * RW:(d + 1) * RW]) * math.exp(-0.5)
        a = _sigmoid(a0 + a_in[:, d * RW:(d + 1) * RW])
        kd = k * (1.0 + (a - 1.0) * k_a)
        ksum = ksum + kd
        kd_ref[d, 0] = kd.astype(bf16)
        lw_ref[d, 0] = lw
        bd_ref[d, 0] = (kk * a).astype(bf16)
    bonus = _dot_split_lhs(r * ksum * r_k, ones)
    r_ref[0] = r.astype(bf16)
    v_ref[0] = v
    kk_ref[0] = kk.astype(bf16)
    bv_ref[0] = bonus * v
    g_ref[0] = g.astype(bf16)


def _feat_call(rw, v_first, mu, vec, w2bd, a2bd, g2, v1, v2):
    nbatch, tb, _ = rw.shape
    nt = tb // TM
    has_vres = v_first is not None
    sub = TM // 8
    tile = lambda w: pl.BlockSpec((1, TM, w), lambda b, i: (b, i, 0))
    full = lambda a: pl.BlockSpec(a.shape, lambda b, i: (0,) * a.ndim)
    in_specs = [
        tile(RWKV_COLS),
        pl.BlockSpec((1, 8, RWKV_COLS), lambda b, i: (b, jnp.maximum(i * sub - 1, 0), 0)),
        pl.BlockSpec((1, 8, RWKV_COLS), lambda b, i: (b, jnp.minimum((i + 1) * sub, tb // 8 - 1), 0)),
    ]
    args = [rw, rw, rw]
    if has_vres:
        in_specs.append(tile(RW))
        args.append(v_first)
    consts = [mu, vec, w2bd, a2bd, g2] + ([v1, v2] if has_vres else [])
    in_specs += [full(a) for a in consts]
    args += consts
    dir_spec = pl.BlockSpec((2, 1, TM, RW), lambda b, i: (0, b, i, 0))
    tok = lambda dt: jax.ShapeDtypeStruct((nbatch, tb, RW), dt)
    dtok = lambda dt: jax.ShapeDtypeStruct((2, nbatch, tb, RW), dt)
    return pl.pallas_call(
        functools.partial(_feat_kernel, nt=nt, ctx_tiles=CTX // TM, has_vres=has_vres),
        grid=(nbatch, nt),
        in_specs=in_specs,
        out_specs=[tile(RW)] * 5 + [dir_spec] * 3,
        out_shape=[tok(bf16), tok(f32), tok(bf16), tok(f32), tok(bf16), dtok(bf16), dtok(f32), dtok(bf16)],
        compiler_params=_cparams(("parallel", "arbitrary")),
        name="rwkv_features",
    )(*args)


def _block_diag_rows(x, width):
    cb = _div_pow2(lax.broadcasted_iota(i32, x.shape, 1), width)
    return jnp.concatenate([jnp.where(cb == h, x, jnp.zeros_like(x)) for h in range(HG)], axis=0)


def _scan_chunks(probs):
    wide = HG * CH
    row_t = lax.broadcasted_iota(i32, (CH, wide), 0)
    col_t = _mod_pow2(lax.broadcasted_iota(i32, (CH, wide), 1), CH)
    row_g = lax.broadcasted_iota(i32, (CH, GW), 0)
    incl_t = {False: col_t <= row_t, True: col_t >= row_t}
    strict_t = {False: col_t < row_t, True: col_t > row_t}
    eye_t = (row_t == col_t).astype(f32)
    rb = _div_pow2(lax.broadcasted_iota(i32, (GW, GW), 0), HEAD_DIM)
    cb = _div_pow2(lax.broadcasted_iota(i32, (GW, GW), 1), HEAD_DIM)
    n = len(probs)
    rev = [p[7] for p in probs]
    def cumsum_rows(x, reverse):
        s = 1
        while s < CH:
            if reverse:
                x = x + jnp.where(row_g < CH - s, pltpu.roll(x, CH - s, 0), 0.0)
            else:
                x = x + jnp.where(row_g >= s, pltpu.roll(x, s, 0), 0.0)
            s *= 2
        return x

    gam = [cumsum_rows(probs[i][5], rev[i]) for i in range(n)]
    ar, bk, k_t, b_t = [], [], [], []
    for i, (s_prev, r, v, kk, k, lw, b, _) in enumerate(probs):
        e_neg = jnp.exp(-gam[i])
        a_s = (-kk * jnp.exp(gam[i] - lw)).astype(bf16)
        r_s = (r * jnp.exp(gam[i])).astype(bf16)
        b_t.append(b * e_neg)
        k_t.append(k * e_neg)
        ar.append(jnp.concatenate([a_s, r_s], axis=0))
        bk.append(jnp.concatenate([_block_diag_rows(b_t[i].astype(bf16), HEAD_DIM),
                                   _block_diag_rows(k_t[i].astype(bf16), HEAD_DIM)], axis=0))
    gram = [_dot_nt(ar[i], bk[i]) for i in range(n)]
    ars = [_dot_nt(ar[i], probs[i][0].astype(bf16)) for i in range(n)]
    v_bd = [_block_diag_rows(probs[i][2].astype(bf16), HEAD_DIM) for i in range(n)]
    p0 = [jnp.where(strict_t[rev[i]], gram[i][:CH, :wide], 0.0).astype(bf16) for i in range(n)]
    lq = [jnp.concatenate([jnp.where(strict_t[rev[i]], gram[i][:CH, wide:], 0.0),
                           jnp.where(incl_t[rev[i]], gram[i][CH:, wide:], 0.0)], axis=0).astype(bf16)
          for i in range(n)]
    lqv = [_dot(lq[i], v_bd[i]) for i in range(n)]
    rhs = [ars[i][:CH] + lqv[i][:CH] for i in range(n)]
    t = [eye_t + p0[i].astype(f32) for i in range(n)]
    p = [_dot(p0[i], _block_diag_rows(p0[i], CH)).astype(bf16) for i in range(n)]
    m = 4
    while m < CH:
        tp = [_dot(jnp.concatenate([t[i].astype(bf16), p[i]], axis=0), _block_diag_rows(p[i], CH))
              for i in range(n)]
        t = [t[i] + tp[i][:CH] for i in range(n)]
        p = [tp[i][CH:].astype(bf16) for i in range(n)]
        m *= 2
    t = [t[i] + _dot(t[i].astype(bf16), _block_diag_rows(p[i], CH)) for i in range(n)]
    u = [_dot(t[i].astype(bf16), _block_diag_rows(rhs[i].astype(bf16), HEAD_DIM)) for i in range(n)]
    out = []
    for i in range(n):
        s_prev, v, lw = probs[i][0], probs[i][2], probs[i][5]
        q_b = jnp.where(incl_t[rev[i]], gram[i][CH:, :wide], 0.0).astype(bf16)
        u_bd = _block_diag_rows(u[i].astype(bf16), HEAD_DIM)
        y = ars[i][CH:] + lqv[i][CH:] + _dot(q_b, u_bd)
        eg = jnp.exp(jnp.sum(lw, axis=0, keepdims=True))
        vu = jnp.concatenate([v, u[i]], axis=0).astype(bf16)
        kb = jnp.concatenate([k_t[i] * eg, b_t[i] * eg], axis=0).astype(bf16)
        s_add = _dot_tn(vu, kb)
        out.append((s_prev * eg + jnp.where(rb == cb, s_add, 0.0), y))
    return out


def _scan_kernel(rf_ref, vf_ref, kkf_ref, kf_ref, lwf_ref, bf_ref,
                 rb_ref, vb_ref, kkb_ref, kb_ref, lwb_ref, bb_ref,
                 yf_ref, yb_ref, s_ref):
    @pl.when(pl.program_id(1) == 0)
    def _():
        s_ref[...] = jnp.zeros_like(s_ref)

    dirs = ((rf_ref, vf_ref, kkf_ref, kf_ref, lwf_ref, bf_ref, yf_ref),
            (rb_ref, vb_ref, kkb_ref, kb_ref, lwb_ref, bb_ref, yb_ref))
    probs, dest = [], []
    for bi in range(SCAN_BATCH):
        for d, (r_ref, v_ref, kk_ref, k_ref, lw_ref, b_ref, y_ref) in enumerate(dirs):
            for g in range(RW // GW):
                sl = slice(g * GW, (g + 1) * GW)
                probs.append((s_ref[bi, d, g], r_ref[bi, :, sl], v_ref[bi, :, sl], kk_ref[bi, :, sl],
                              k_ref[0, bi, :, sl], lw_ref[0, bi, :, sl], b_ref[0, bi, :, sl], d == 1))
                dest.append((bi, d, g, y_ref, sl))
    for (bi, d, g, y_ref, sl), (s_new, y) in zip(dest, _scan_chunks(probs)):
        s_ref[bi, d, g] = s_new
        y_ref[bi, :, sl] = y


def _scan_call(r, v, kk, kd, lw, bd):
    nbatch, tb, _ = r.shape
    nc = tb // CH
    cc = CTX // CH
    sb = SCAN_BATCH
    rev = lambda j: jnp.where(j < cc, cc - 1 - j, nc - 1 + cc - j)
    tok_f = pl.BlockSpec((sb, CH, RW), lambda b, j: (b, j, 0))
    tok_b = pl.BlockSpec((sb, CH, RW), lambda b, j: (b, rev(j), 0))
    dir_f = pl.BlockSpec((1, sb, CH, RW), lambda b, j: (0, b, j, 0))
    dir_b = pl.BlockSpec((1, sb, CH, RW), lambda b, j: (1, b, rev(j), 0))
    out = jax.ShapeDtypeStruct((nbatch, tb, RW), f32)
    return pl.pallas_call(
        _scan_kernel,
        grid=(nbatch // sb, nc),
        in_specs=[tok_f, tok_f, tok_f, dir_f, dir_f, dir_f, tok_b, tok_b, tok_b, dir_b, dir_b, dir_b],
        out_specs=[tok_f, tok_b],
        out_shape=[out, out],
        scratch_shapes=[pltpu.VMEM((sb, 2, RW // GW, GW, GW), f32)],
        compiler_params=_cparams(("parallel", "arbitrary")),
        name="rwkv_scan",
    )(r, v, kk, kd, lw, bd, r, v, kk, kd, lw, bd)


def _pack_bf16_pairs(x):
    bits = pltpu.bitcast(x, u32)
    half = x.shape[1] // 2
    return bits[:, :half] | lax.shift_right_logical(bits[:, half:], jnp.uint32(16))


def _unpack_bf16_pairs(p):
    hi = pltpu.bitcast(p & jnp.uint32(0xFFFF0000), f32)
    lo = pltpu.bitcast(lax.shift_left(p, jnp.uint32(16)), f32)
    return jnp.concatenate([hi, lo], axis=1).astype(bf16)


def _mix_kernel(x_ref, att_ref, yf_ref, yb_ref, bv_ref, g_ref, mod0_ref, mod1_ref, ln_ref, wo_ref, gf_ref,
                wr_ref, rb_ref, xo_ref, xl_ref, lp_ref, gt_ref, cnt_ref):
    ones = _head_ones()
    inv = 1.0 / HEAD_DIM
    tiles = range(MIX_TILES)
    rows = [slice(s * TM, (s + 1) * TM) for s in tiles]
    mods = (mod0_ref, mod1_ref)
    y = [yf_ref[rows[s], :] + yb_ref[rows[s], :] for s in tiles]
    mu = [_dot(y[s].astype(bf16), ones) * inv for s in tiles]
    dlt = [y[s] - mu[s] for s in tiles]
    var = [_dot((dlt[s] * dlt[s]).astype(bf16), ones) * inv for s in tiles]
    rwk = [((dlt[s] * lax.rsqrt(var[s] + GN_EPS) * ln_ref[0:1, :] + ln_ref[1:2, :] + bv_ref[rows[s], :])
            * g_ref[rows[s], :]).astype(bf16) for s in tiles]
    mix = [_dot(att_ref[rows[s], :], wo_ref[0:ATT_WIDTH, :]) + _dot(rwk[s], wo_ref[ATT_WIDTH:, :]) for s in tiles]
    xn = [x_ref[rows[s], :] + mods[s][0, :, 2 * D:3 * D] * mix[s] for s in tiles]
    for s in tiles:
        xo_ref[rows[s], :] = xn[s]
    hf = [_rmsnorm(xn[s], gf_ref[...]) * (1.0 + mods[s][0, :, 4 * D:5 * D]) + mods[s][0, :, 3 * D:4 * D]
          for s in tiles]
    split = [_split2(hf[s]) for s in tiles]
    by_hi = [_dot_nt(wr_ref[...], split[s][0]) for s in tiles]
    logits = [by_hi[s][:N_EXPERTS] + by_hi[s][N_EXPERTS:] + _dot_nt(wr_ref[0:N_EXPERTS, :], split[s][1])
              for s in tiles]
    sorted_rows, lpos, gates, cnt = _route_and_sort(hf, [_sigmoid(lg) for lg in logits], rb_ref[...])
    for s in tiles:
        xl_ref[s * LROWS:(s + 1) * LROWS, :] = sorted_rows[s]
        lp_ref[:, rows[s]] = lpos[s]
        gt_ref[:, rows[s]] = gates[s]
        cnt_ref[s * N_EXPERTS:(s + 1) * N_EXPERTS, :] = cnt[s]


def _route_and_sort(hf, scores, bias):
    tiles = range(len(hf))
    tm = hf[0].shape[0]
    routed = [_route_rows(scores[s], bias) for s in tiles]
    eio = lax.broadcasted_iota(i32, (N_EXPERTS, tm), 0)
    upper = (lax.broadcasted_iota(i32, (tm, tm), 0) < lax.broadcasted_iota(i32, (tm, tm), 1)).astype(bf16)
    lower = (lax.broadcasted_iota(i32, (N_EXPERTS, N_EXPERTS), 0)
             > lax.broadcasted_iota(i32, (N_EXPERTS, N_EXPERTS), 1)).astype(bf16)
    jj = lax.broadcasted_iota(i32, (LROWS, tm), 0)
    oh0 = [(eio == routed[s][0]).astype(f32) for s in tiles]
    oh1 = [(eio == routed[s][1]).astype(f32) for s in tiles]
    both = [oh0[s] + oh1[s] for s in tiles]
    before = [_dot(both[s].astype(bf16), upper) for s in tiles]
    cnt = [jnp.sum(both[s], axis=1, keepdims=True) for s in tiles]
    seg = [_round_up_pow2(cnt[s].astype(i32), SEG_ALIGN).astype(f32) for s in tiles]
    seg_start = [_dot(lower, jnp.broadcast_to(seg[s], (N_EXPERTS, tm)).astype(bf16)) for s in tiles]
    pos = [seg_start[s] + before[s] for s in tiles]
    lp0 = [jnp.sum(oh0[s] * pos[s], axis=0, keepdims=True).astype(i32) for s in tiles]
    lp1 = [jnp.sum(oh1[s] * pos[s], axis=0, keepdims=True).astype(i32) for s in tiles]
    sel_t = [jnp.logical_or(jj == lp0[s], jj == lp1[s]).astype(bf16) for s in tiles]
    sorted_rows = [_pack_bf16_pairs(_dot(sel_t[s], hf[s].astype(bf16))) for s in tiles]
    lpos = [jnp.concatenate([lp0[s], lp1[s]], axis=0) for s in tiles]
    gates = [jnp.concatenate([routed[s][2], routed[s][3]], axis=0) for s in tiles]
    cnt_out = [jnp.broadcast_to(cnt[s], (N_EXPERTS, LANES)) for s in tiles]
    return sorted_rows, lpos, gates, cnt_out


def _route_rows(scores, bias):
    biased = scores + bias
    row = lambda a, e: a[e:e + 1, :]
    best = None
    for gi in range(N_GROUPS):
        m = [row(biased, gi * EXPERTS_PER_GROUP + j) for j in range(EXPERTS_PER_GROUP)]
        gs = None
        for a in range(EXPERTS_PER_GROUP):
            for b in range(a + 1, EXPERTS_PER_GROUP):
                pair = m[a] + m[b]
                gs = pair if gs is None else jnp.maximum(gs, pair)
        if best is None:
            best, g_idx = gs, jnp.zeros(gs.shape, i32)
        else:
            better = gs > best
            g_idx = jnp.where(better, gi, g_idx)
            best = jnp.where(better, gs, best)

    def pick(a, j):
        out = row(a, j)
        for gi in range(1, N_GROUPS):
            out = jnp.where(g_idx == gi, row(a, gi * EXPERTS_PER_GROUP + j), out)
        return out

    vb = [pick(biased, j) for j in range(EXPERTS_PER_GROUP)]
    vs = [pick(scores, j) for j in range(EXPERTS_PER_GROUP)]

    def argmax_first(vals):
        bv, bi = vals[0], jnp.zeros(vals[0].shape, i32)
        for j in range(1, len(vals)):
            better = vals[j] > bv
            bi = jnp.where(better, j, bi)
            bv = jnp.where(better, vals[j], bv)
        return bi

    i1 = argmax_first(vb)
    i2 = argmax_first([jnp.where(i1 == j, -jnp.inf, vb[j]) for j in range(EXPERTS_PER_GROUP)])
    sel = lambda idx: sum(jnp.where(idx == j, vs[j], 0.0) for j in range(EXPERTS_PER_GROUP))
    s1, s2 = sel(i1), sel(i2)
    tot = s1 + s2
    base = g_idx * EXPERTS_PER_GROUP
    return base + i1, base + i2, s1 / tot, s2 / tot


def _mix_call(x, att, yf, yb, bv, g, mod, ln, w_out, g_ffn, w_router, b_router):
    nbatch, tb, _ = x.shape
    tiles_per_batch = tb // TM
    ntiles = nbatch * tiles_per_batch
    assert ntiles % MIX_TILES == 0 and MIX_TILES == 2
    ntok = nbatch * tb
    flat = lambda a: a.reshape(ntok, a.shape[-1])
    rows = lambda w: pl.BlockSpec((MIX_TILES * TM, w), lambda p: (p, 0))
    full = lambda a: pl.BlockSpec(a.shape, lambda p: (0,) * a.ndim)
    ctx_tiles = CTX // TM

    def mod_spec(s):
        def index(p):
            tile = MIX_TILES * p + s
            return (jnp.where(tile % tiles_per_batch < ctx_tiles, nbatch, tile // tiles_per_batch), 0, 0)
        return pl.BlockSpec((1, 1, 6 * D), index)

    route = pl.BlockSpec((TOP_K, MIX_TILES * TM), lambda p: (0, p))
    return pl.pallas_call(
        _mix_kernel,
        grid=(ntiles // MIX_TILES,),
        in_specs=[rows(D), rows(ATT_WIDTH), rows(RW), rows(RW), rows(RW), rows(RW), mod_spec(0), mod_spec(1),
                  full(ln), full(w_out), full(g_ffn), full(w_router), full(b_router)],
        out_specs=[rows(D), pl.BlockSpec((MIX_TILES * LROWS, D // 2), lambda p: (p, 0)), route, route,
                   pl.BlockSpec((MIX_TILES * N_EXPERTS, LANES), lambda p: (p, 0))],
        out_shape=[jax.ShapeDtypeStruct((ntok, D), f32),
                   jax.ShapeDtypeStruct((ntiles * LROWS, D // 2), u32),
                   jax.ShapeDtypeStruct((TOP_K, ntok), i32),
                   jax.ShapeDtypeStruct((TOP_K, ntok), f32),
                   jax.ShapeDtypeStruct((ntiles * N_EXPERTS, LANES), f32)],
        compiler_params=_cparams(("arbitrary",)),
        name="mix_out",
    )(flat(x), flat(att), flat(yf), flat(yb), flat(bv), flat(g), mod, mod, ln, w_out, g_ffn, w_router, b_router)


def _segment_copies(fn, tile, base_ref, seg_ref, ls_ref, src, dst, sem, src_is_global):
    for e in range(N_EXPERTS):
        idx = tile * N_EXPERTS + e
        seg = seg_ref[idx]
        g0 = base_ref[idx]
        l0 = ls_ref[idx]
        size = TM
        while size >= SEG_ALIGN:
            done = lax.bitwise_and(seg, ~(2 * size - 1))

            @pl.when(lax.bitwise_and(seg, size) != 0)
            def _():
                g_rows = pl.ds(pl.multiple_of(g0 + done, SEG_ALIGN), size)
                l_rows = pl.ds(pl.multiple_of(l0 + done, SEG_ALIGN), size)
                s_rows, d_rows = (g_rows, l_rows) if src_is_global else (l_rows, g_rows)
                fn(pltpu.make_async_copy(src.at[s_rows, :], dst.at[d_rows, :], sem))

            size //= 2


def _segment_waits(tile, seg_ref, src, dst, sem):
    total = seg_ref[tile * N_EXPERTS]
    for e in range(1, N_EXPERTS):
        total = total + seg_ref[tile * N_EXPERTS + e]
    size = 2 * TM
    assert LROWS < 2 * size
    while size >= SEG_ALIGN:
        @pl.when(lax.bitwise_and(total, size) != 0)
        def _():
            pltpu.make_async_copy(src.at[pl.ds(0, size), :], dst.at[pl.ds(0, size), :], sem).wait()

        size //= 2


def _dispatch_kernel(base_ref, seg_ref, ls_ref, ends_ref, xl_ref, xs_ref, zeros, stage, sem, in_sem, out_sem):
    tile = pl.program_id(0)
    ntiles = pl.num_programs(0)

    def load(t):
        return pltpu.make_async_copy(xl_ref.at[pl.ds(pl.multiple_of(t * LROWS, SEG_ALIGN), LROWS), :],
                                     stage.at[lax.rem(t, 3)], in_sem.at[lax.rem(t, 3)])

    def zero_tails(fn):
        for e in range(N_EXPERTS):
            end = ends_ref[e]
            start = ends_ref[e - 1] if e else 0

            @pl.when(end > start)
            def _():
                rows = pl.ds(pl.multiple_of(end - MOE_BLK, MOE_BLK), MOE_BLK)
                fn(pltpu.make_async_copy(zeros, xs_ref.at[rows, :], sem))

    def zero_unused(fn):
        def body(blk, carry):
            fn(pltpu.make_async_copy(zeros, xs_ref.at[pl.ds(pl.multiple_of(blk * MOE_BLK, MOE_BLK), MOE_BLK), :],
                                     sem))
            return carry
        lax.fori_loop(ends_ref[N_EXPERTS - 1] // MOE_BLK, xs_ref.shape[0] // MOE_BLK, body, 0)

    @pl.when(tile == 0)
    def _():
        load(tile).start()
        zeros[...] = jnp.zeros_like(zeros)
        zero_tails(lambda cp: cp.start())
        zero_unused(lambda cp: cp.start())
        zero_tails(lambda cp: cp.wait())
        zero_unused(lambda cp: cp.wait())

    @pl.when(tile + 1 < ntiles)
    def _():
        load(tile + 1).start()

    load(tile).wait()
    cur = lax.rem(tile, 2)
    _segment_copies(lambda cp: cp.start(), tile, base_ref, seg_ref, ls_ref, stage.at[lax.rem(tile, 3)], xs_ref,
                    out_sem.at[cur], False)

    @pl.when(tile > 0)
    def _():
        _segment_waits(tile - 1, seg_ref, stage.at[0], xs_ref, out_sem.at[1 - cur])

    @pl.when(tile == ntiles - 1)
    def _():
        _segment_waits(tile, seg_ref, stage.at[0], xs_ref, out_sem.at[cur])


def _dispatch_call(base, seg, lstart, pad_ends, x_local, nrows):
    ntiles = x_local.shape[0] // LROWS
    return pl.pallas_call(
        _dispatch_kernel,
        grid_spec=pltpu.PrefetchScalarGridSpec(
            num_scalar_prefetch=4,
            grid=(ntiles,),
            in_specs=[pl.BlockSpec(memory_space=pl.ANY)],
            out_specs=pl.BlockSpec(memory_space=pl.ANY),
            scratch_shapes=[pltpu.VMEM((MOE_BLK, D // 2), u32), pltpu.VMEM((3, LROWS, D // 2), u32),
                            pltpu.SemaphoreType.DMA(()), pltpu.SemaphoreType.DMA((3,)),
                            pltpu.SemaphoreType.DMA((2,))],
        ),
        out_shape=jax.ShapeDtypeStruct((nrows, D // 2), u32),
        compiler_params=_cparams(("arbitrary",)),
        name="moe_dispatch",
    )(base, seg, lstart, pad_ends, x_local)


def _ffn_kernel(be_ref, nu_ref, x_ref, wg_ref, wu_ref, wd_ref, y_ref, wg_bf, wu_bf, wd_bf):
    i = pl.program_id(0)
    used = i < nu_ref[0]
    new_expert = jnp.logical_or(i == 0, be_ref[i] != be_ref[jnp.maximum(i - 1, 0)])

    @pl.when(jnp.logical_and(used, new_expert))
    def _():
        wg_bf[...] = wg_ref[0, 0].astype(bf16)
        wu_bf[...] = wu_ref[0, 0].astype(bf16)
        wd_bf[...] = wd_ref[0, 0].astype(bf16)

    @pl.when(used)
    def _():
        x = _unpack_bf16_pairs(x_ref[...])
        gt = _dot(x, wg_bf[...])
        up = _dot(x, wu_bf[...])
        hid = (gt * _sigmoid(gt) * up).astype(bf16)
        y = _dot(hid, wd_bf[...])
        y_ref[...] = _pack_bf16_pairs(y.astype(bf16).astype(f32))

    @pl.when(i >= nu_ref[0])
    def _():
        y_ref[...] = jnp.zeros_like(y_ref)


def _ffn_call(blk_expert, n_used, x_sorted, wg, wu, wd, layer):
    nrows = x_sorted.shape[0]
    nblk = nrows // MOE_BLK
    wspec = pl.BlockSpec((1, 1, D, D), lambda i, be, nu: (layer, be[i], 0, 0))
    return pl.pallas_call(
        _ffn_kernel,
        grid_spec=pltpu.PrefetchScalarGridSpec(
            num_scalar_prefetch=2,
            grid=(nblk,),
            in_specs=[pl.BlockSpec((MOE_BLK, D // 2), lambda i, be, nu: (jnp.minimum(i, nu[0] - 1), 0)),
                      wspec, wspec, wspec],
            out_specs=pl.BlockSpec((MOE_BLK, D // 2), lambda i, be, nu: (i, 0)),
            scratch_shapes=[pltpu.VMEM((D, D), bf16)] * 3,
        ),
        out_shape=jax.ShapeDtypeStruct((nrows, D // 2), u32),
        compiler_params=_cparams(("arbitrary",)),
        name="moe_ffn",
    )(blk_expert, n_used, x_sorted, wg, wu, wd)


def _combine_kernel(base_ref, seg_ref, ls_ref, x_ref, lp_ref, gt_ref, mod_ref, g_ref, y_ref, o_ref, ybuf, sem,
                    *, tiles_per_batch, first_tile, final):
    b = pl.program_id(0)
    i = pl.program_id(1)
    nt = pl.num_programs(1)
    step = b * nt + i
    slot = lax.rem(step, 2)
    tile = b * tiles_per_batch + i + first_tile
    next_tile = jnp.where(i + 1 < nt, tile + 1, (b + 1) * tiles_per_batch + first_tile)

    def copies(fn, which_tile, which_slot):
        _segment_copies(fn, which_tile, base_ref, seg_ref, ls_ref, y_ref, ybuf.at[which_slot],
                        sem.at[which_slot], True)

    @pl.when(step == 0)
    def _():
        ybuf[...] = jnp.zeros_like(ybuf)
        copies(lambda cp: cp.start(), tile, slot)

    @pl.when(step + 1 < pl.num_programs(0) * nt)
    def _():
        copies(lambda cp: cp.start(), next_tile, 1 - slot)

    _segment_waits(tile, seg_ref, y_ref, ybuf.at[slot], sem.at[slot])
    y_loc = _unpack_bf16_pairs(ybuf[slot])
    tm = x_ref.shape[1]
    jj = lax.broadcasted_iota(i32, (LROWS, tm), 0)
    gmat = (jnp.where(jj == lp_ref[0:1, :], gt_ref[0:1, :], 0.0)
            + jnp.where(jj == lp_ref[1:2, :], gt_ref[1:2, :], 0.0))
    g_hi, g_lo = _split2(gmat)
    moe = _dot_tn(g_hi, y_loc) + _dot_tn(g_lo, y_loc)
    xn = x_ref[0] + mod_ref[0, :, 5 * D:6 * D] * moe
    o_ref[0] = _rmsnorm(xn, g_ref[...]) if final else xn


def _combine_call(base, seg, lstart, x, lpos, gates, mod, y_sorted, final_g):
    nbatch, tb, _ = x.shape
    ctx_tiles = CTX // TM
    tiles_per_batch = tb // TM
    final = final_g is not None
    first_tile = ctx_tiles if final else 0
    nt = tiles_per_batch - first_tile
    tile = pl.BlockSpec((1, TM, D), lambda b, i, *_: (b, i + first_tile, 0))
    route = pl.BlockSpec((TOP_K, TM), lambda b, i, *_: (0, b * tiles_per_batch + i + first_tile))
    mod_map = _mod_index(nbatch, ctx_tiles - first_tile)
    g_arr = final_g if final else jnp.ones((1, D), f32)
    return pl.pallas_call(
        functools.partial(_combine_kernel, tiles_per_batch=tiles_per_batch, first_tile=first_tile, final=final),
        grid_spec=pltpu.PrefetchScalarGridSpec(
            num_scalar_prefetch=3,
            grid=(nbatch, nt),
            in_specs=[tile, route, route,
                      pl.BlockSpec((1, 1, 6 * D), lambda b, i, *_: mod_map(b, i)),
                      pl.BlockSpec((1, D), lambda b, i, *_: (0, 0)),
                      pl.BlockSpec(memory_space=pl.ANY)],
            out_specs=pl.BlockSpec((1, TM, D), lambda b, i, *_: (b, i, 0)),
            scratch_shapes=[pltpu.VMEM((2, LROWS, D // 2), u32), pltpu.SemaphoreType.DMA((2,))],
        ),
        out_shape=jax.ShapeDtypeStruct((nbatch, nt * TM, D), f32),
        compiler_params=_cparams(("arbitrary", "arbitrary")),
        name="ffn_residual_final" if final else "ffn_residual",
    )(base, seg, lstart, x, lpos, gates, mod, g_arr, y_sorted)


def _moe_rows(ntok):
    ntiles = ntok // TM
    worst = TOP_K * ntok + ntiles * N_EXPERTS * (SEG_ALIGN - 1)
    return (-(-worst // MOE_BLK) + N_EXPERTS) * MOE_BLK


def _segment_plan(cnt, nblk):
    seg = (cnt + SEG_ALIGN - 1) // SEG_ALIGN * SEG_ALIGN
    lstart = jnp.cumsum(seg, axis=1) - seg
    rows = jnp.sum(seg, axis=0)
    padded = (rows + MOE_BLK - 1) // MOE_BLK * MOE_BLK
    pad_ends = jnp.cumsum(padded)
    base = (pad_ends - padded)[None, :] + jnp.cumsum(seg, axis=0) - seg
    blk_start = jnp.arange(nblk, dtype=i32) * MOE_BLK
    blk_expert = jnp.minimum(jnp.sum(pad_ends[None, :] <= blk_start[:, None], axis=1), N_EXPERTS - 1)
    n_used = (pad_ends[-1] // MOE_BLK).reshape(1)
    flat = lambda a: a.reshape(-1).astype(i32)
    return flat(base), flat(seg), flat(lstart), flat(pad_ends), blk_expert.astype(i32), n_used.astype(i32)


def _rope_tables(tb):
    rows = SEQ // GRID_W
    row = jnp.repeat(jnp.arange(rows, dtype=f32), GRID_W)
    col = jnp.tile(jnp.arange(GRID_W, dtype=f32), rows)
    inv_freq = ROPE_BASE ** (-jnp.arange(ROPE_FREQS, dtype=f32) / ROPE_FREQS)
    ang_r = row[:, None] * inv_freq[None, :]
    ang_c = col[:, None] * inv_freq[None, :]
    ang = jnp.concatenate([ang_r, ang_r, ang_c, ang_c], axis=-1)
    cos = jnp.concatenate([jnp.ones((CTX, HEAD_DIM), f32), jnp.cos(ang)], axis=0)
    sin = jnp.concatenate([jnp.zeros((CTX, HEAD_DIM), f32), jnp.sin(ang)], axis=0)
    return jnp.tile(cos, (1, LANES // HEAD_DIM)), jnp.tile(sin, (1, LANES // HEAD_DIM))


def _block_diag2(w):
    z = jnp.zeros_like(w[0])
    return jnp.concatenate([jnp.concatenate([w[0], z], axis=1), jnp.concatenate([z, w[1]], axis=1)], axis=0)


def kernel(x, c, ctx, c_ctx, w_mod, b_mod, norm_mix_g, norm_ffn_g, w_in, w_out, att_sink, shift_mu_prev, shift_mu_next, decay_w0, decay_w2, iclr_a0, iclr_a2, vres_v0, vres_v1, vres_v2, gate_g2, k_k, k_a, r_k, ln_x_w, ln_x_b, router_w, router_b, expert_w_gate, expert_w_up, expert_w_down, final_norm_g):
    nbatch = x.shape[0]
    depth = w_mod.shape[0]
    tb = ctx.shape[1] + x.shape[1]
    xa = jnp.concatenate([ctx, x], axis=1)
    nb_pad = -(-(nbatch + 1) // 8) * 8
    cond = jnp.zeros((nb_pad, D), f32).at[:nbatch].set(c).at[nbatch].set(c_ctx)
    mod_all = _mod_call(cond, w_mod, b_mod).reshape(depth, nb_pad, 1, 6 * D)
    cos, sin = _rope_tables(tb)
    wr_hi = router_w.T.astype(bf16)
    wr_lo = (router_w.T - wr_hi.astype(f32)).astype(bf16)
    w_router = jnp.concatenate([wr_hi, wr_lo], axis=0)
    b_router = router_b.reshape(N_EXPERTS, 1)
    v_first = None
    for l in range(depth):
        mod = mod_all[l]
        q, k, v, rw = _in_proj_call(xa, mod, norm_mix_g[l].reshape(1, D), w_in[l].astype(bf16), cos, sin)
        att = _attn_call(att_sink[l], q, k, v)
        mu = jnp.stack([shift_mu_prev[l], shift_mu_next[l]])
        v0 = vres_v0[l - 1] if l > 0 else jnp.zeros((RW,), f32)
        vec = jnp.stack([k_k[l], k_a[l], r_k[l].reshape(RW), v0,
                         decay_w0[l, 0], decay_w0[l, 1], iclr_a0[l, 0], iclr_a0[l, 1]])
        if l > 0:
            v1 = jnp.zeros((RW, LANES), f32).at[:, :LORA_VRES].set(vres_v1[l - 1]).astype(bf16)
            v2 = jnp.zeros((LANES, RW), f32).at[:LORA_VRES].set(vres_v2[l - 1]).astype(bf16)
        else:
            v1 = v2 = None
        r_, v_, kk, bv, g, kd, lw, bd = _feat_call(
            rw, v_first, mu, vec, _block_diag2(decay_w2[l]).astype(bf16), _block_diag2(iclr_a2[l]).astype(bf16),
            gate_g2[l].astype(bf16), v1, v2)
        if l == 0:
            v_first = v_
        yf, yb = _scan_call(r_, v_, kk, kd, lw, bd)
        ln = jnp.stack([ln_x_w[l], ln_x_b[l]])
        xa, x_local, lpos, gates, cnt = _mix_call(xa, att, yf, yb, bv, g, mod, ln, w_out[l].astype(bf16),
                                                  norm_ffn_g[l].reshape(1, D), w_router, b_router)
        ntok = nbatch * tb
        nrows = _moe_rows(ntok)
        xa = xa.reshape(nbatch, tb, D)
        cnt = cnt[:, 0].astype(i32).reshape(ntok // TM, N_EXPERTS)
        base, seg, lstart, pad_ends, blk_expert, n_used = _segment_plan(cnt, nrows // MOE_BLK)
        x_sorted = _dispatch_call(base, seg, lstart, pad_ends, x_local, nrows)
        y_sorted = _ffn_call(blk_expert, n_used, x_sorted, expert_w_gate, expert_w_up, expert_w_down, l)
        xa = _combine_call(base, seg, lstart, xa, lpos, gates, mod, y_sorted,
                           final_norm_g.reshape(1, D) if l == depth - 1 else None)
    return xa
```

```python
import functools
import math

import jax
import jax.numpy as jnp
from jax import lax
from jax.experimental import pallas as pl
from jax.experimental.pallas import tpu as pltpu

f32 = jnp.float32
bf16 = jnp.bfloat16
i32 = jnp.int32
u32 = jnp.uint32

D = 1024
SEQ = 4096
CTX = 256
TB = CTX + SEQ
GRID_W = 64
HEAD_DIM = 64
ATT_WIDTH = 512
ATT_HEADS = 8
KV_HEADS = 2
ATT_GROUP = ATT_HEADS // KV_HEADS
KV_WIDTH = KV_HEADS * HEAD_DIM
RW = 512
LORA_DECAY = 64
LORA_ICLR = 64
LORA_VRES = 32
LORA_GATE = 128
RWKV_COLS = 3 * RW + 2 * (LORA_DECAY + LORA_ICLR) + LORA_GATE
ATT_COLS = ATT_WIDTH + 2 * KV_WIDTH
IN_COLS = ATT_COLS + RWKV_COLS
N_EXPERTS = 16
N_GROUPS = 4
EXPERTS_PER_GROUP = 4
TOP_K = 2
MOE_BLK = 256
NORM_EPS = 1e-6
GN_EPS = 64e-5
NEG_INF = -1e30
ATT_SCALE = HEAD_DIM ** -0.5
ROPE_BASE = 10000.0
ROPE_FREQS = HEAD_DIM // 4

LANES = 128
SUBLANES = 8
TM = 256
QB = 128
CH = 64
HG = 4
GW = HG * HEAD_DIM
SCAN_BATCH = 4
MIX_TILES = 2
SEG_ALIGN = SUBLANES
LROWS = -(-(TOP_K * TM + N_EXPERTS * SEG_ALIGN) // LANES) * LANES
VMEM_LIMIT = 48 * 1024 * 1024


def _cparams(sem):
    return pltpu.CompilerParams(dimension_semantics=sem, vmem_limit_bytes=VMEM_LIMIT)


def _sigmoid(x):
    return 0.5 * jnp.tanh(0.5 * x) + 0.5


def _div_pow2(x, n):
    assert n & (n - 1) == 0
    return lax.shift_right_logical(x, n.bit_length() - 1)


def _mod_pow2(x, n):
    assert n & (n - 1) == 0
    return lax.bitwise_and(x, n - 1)


def _round_up_pow2(x, n):
    assert n & (n - 1) == 0
    return lax.bitwise_and(x + (n - 1), ~(n - 1))


def _dot(a, b):
    return jnp.dot(a, b, preferred_element_type=f32)


def _dot_nt(a, b):
    return lax.dot_general(a, b, (((1,), (1,)), ((), ())), preferred_element_type=f32)


def _dot_tn(a, b):
    return lax.dot_general(a, b, (((0,), (0,)), ((), ())), preferred_element_type=f32)


def _split2(x):
    hi = x.astype(bf16)
    lo = (x - hi.astype(f32)).astype(bf16)
    return hi, lo


def _rmsnorm(x, g):
    ms = jnp.mean(x * x, axis=-1, keepdims=True)
    return x * lax.rsqrt(ms + NORM_EPS) * g


def _head_ones():
    r = _div_pow2(lax.broadcasted_iota(i32, (RW, RW), 0), HEAD_DIM)
    c = _div_pow2(lax.broadcasted_iota(i32, (RW, RW), 1), HEAD_DIM)
    return (r == c).astype(bf16)


def _mod_kernel(c_ref, w_ref, b_ref, o_ref):
    c = c_ref[...]
    s = (c * _sigmoid(c)).astype(bf16)
    o_ref[0] = _dot(s, w_ref[0].astype(bf16)) + b_ref[0]


def _mod_call(cond, w_mod, b_mod):
    nb = cond.shape[0]
    depth = w_mod.shape[0]
    tn = 1024
    return pl.pallas_call(
        _mod_kernel,
        grid=(depth, 6 * D // tn),
        in_specs=[
            pl.BlockSpec((nb, D), lambda l, j: (0, 0)),
            pl.BlockSpec((1, D, tn), lambda l, j: (l, 0, j)),
            pl.BlockSpec((1, 1, tn), lambda l, j: (l, 0, j)),
        ],
        out_specs=pl.BlockSpec((1, nb, tn), lambda l, j: (l, 0, j)),
        out_shape=jax.ShapeDtypeStruct((depth, nb, 6 * D), f32),
        compiler_params=_cparams(("arbitrary", "arbitrary")),
        name="mod",
    )(cond, w_mod, b_mod.reshape(depth, 1, 6 * D))


def _mod_index(nbatch, ctx_tiles):
    return lambda b, i: (jnp.where(i < ctx_tiles, nbatch, b), 0, 0)


def _in_proj_kernel(x_ref, mod_ref, g_ref, w_ref, cos_ref, sin_ref, q_ref, k_ref, v_ref, rw_ref):
    x = x_ref[0]
    tm = x.shape[0]
    h = _rmsnorm(x, g_ref[...])
    sh = mod_ref[0, :, 0:D]
    sc = mod_ref[0, :, D:2 * D]
    h = (h * (1.0 + sc) + sh).astype(bf16)
    p = _dot(h, w_ref[...])
    cos = cos_ref[...]
    sin = sin_ref[...]
    lane = lax.broadcasted_iota(i32, (tm, LANES), 1)
    first_half = _mod_pow2(lane, 2 * ROPE_FREQS) < ROPE_FREQS

    def rope(t):
        rot = jnp.where(first_half, -pltpu.roll(t, LANES - ROPE_FREQS, 1), pltpu.roll(t, ROPE_FREQS, 1))
        return t * cos + rot * sin

    for j in range(ATT_WIDTH // LANES):
        t = (rope(p[:, j * LANES:(j + 1) * LANES]) * ATT_SCALE).astype(bf16)
        q_ref[0, 2 * j] = t[:, :HEAD_DIM]
        q_ref[0, 2 * j + 1] = t[:, HEAD_DIM:]
    kt = rope(p[:, ATT_WIDTH:ATT_WIDTH + KV_WIDTH]).astype(bf16)
    vt = p[:, ATT_WIDTH + KV_WIDTH:ATT_COLS].astype(bf16)
    for hh in range(KV_HEADS):
        k_ref[0, hh] = kt[:, hh * HEAD_DIM:(hh + 1) * HEAD_DIM]
        v_ref[0, hh] = vt[:, hh * HEAD_DIM:(hh + 1) * HEAD_DIM]
    rw_ref[0] = p[:, ATT_COLS:]


def _in_proj_call(x, mod, g, w_in, cos, sin):
    nbatch, tb, _ = x.shape
    nt = tb // TM
    return pl.pallas_call(
        _in_proj_kernel,
        grid=(nbatch, nt),
        in_specs=[
            pl.BlockSpec((1, TM, D), lambda b, i: (b, i, 0)),
            pl.BlockSpec((1, 1, 6 * D), _mod_index(nbatch, CTX // TM)),
            pl.BlockSpec((1, D), lambda b, i: (0, 0)),
            pl.BlockSpec((D, IN_COLS), lambda b, i: (0, 0)),
            pl.BlockSpec((TM, LANES), lambda b, i: (i, 0)),
            pl.BlockSpec((TM, LANES), lambda b, i: (i, 0)),
        ],
        out_specs=[
            pl.BlockSpec((1, ATT_HEADS, TM, HEAD_DIM), lambda b, i: (b, 0, i, 0)),
            pl.BlockSpec((1, KV_HEADS, TM, HEAD_DIM), lambda b, i: (b, 0, i, 0)),
            pl.BlockSpec((1, KV_HEADS, TM, HEAD_DIM), lambda b, i: (b, 0, i, 0)),
            pl.BlockSpec((1, TM, RWKV_COLS), lambda b, i: (b, i, 0)),
        ],
        out_shape=[
            jax.ShapeDtypeStruct((nbatch, ATT_HEADS, tb, HEAD_DIM), bf16),
            jax.ShapeDtypeStruct((nbatch, KV_HEADS, tb, HEAD_DIM), bf16),
            jax.ShapeDtypeStruct((nbatch, KV_HEADS, tb, HEAD_DIM), bf16),
            jax.ShapeDtypeStruct((nbatch, tb, RWKV_COLS), f32),
        ],
        compiler_params=_cparams(("parallel", "arbitrary")),
        name="in_proj",
    )(x, mod, g, w_in, cos, sin)


def _attn_kernel(sink_ref, q_ref, kp_ref, km_ref, kn_ref, vp_ref, vm_ref, vn_ref, kx_ref, vx_ref, o_ref,
                 *, npairs, ctx_pairs):
    j = pl.program_id(1)
    is_lat = j >= ctx_pairs
    before_ok = jnp.logical_and(is_lat, j - 1 >= ctx_pairs)
    after_ok = jnp.logical_and(is_lat, j + 1 <= npairs - 1)
    rows = ATT_GROUP * QB
    qi = _mod_pow2(lax.broadcasted_iota(i32, (rows, QB), 0), QB)
    kj = lax.broadcasted_iota(i32, (rows, QB), 1)
    band_p = kj >= qi
    band_n = kj <= qi
    row_head = _div_pow2(lax.broadcasted_iota(i32, (rows, 1), 0), QB)
    half = (slice(0, QB), slice(QB, 2 * QB))
    chains = [(c, h) for c in range(2) for h in range(KV_HEADS)]
    qh, kprev, kcur, knext, vprev, vcur, vnext, ok_p, ok_n = [], [], [], [], [], [], [], [], []
    for c, h in chains:
        qh.append(q_ref[0, ATT_GROUP * h:ATT_GROUP * (h + 1), half[c], :].reshape(rows, HEAD_DIM))
        kcur.append(km_ref[0, h, half[c], :])
        vcur.append(vm_ref[0, h, half[c], :])
        if c == 0:
            kprev.append(kp_ref[0, h]); vprev.append(vp_ref[0, h]); ok_p.append(before_ok)
            knext.append(km_ref[0, h, half[1], :]); vnext.append(vm_ref[0, h, half[1], :]); ok_n.append(is_lat)
        else:
            kprev.append(km_ref[0, h, half[0], :]); vprev.append(vm_ref[0, h, half[0], :]); ok_p.append(is_lat)
            knext.append(kn_ref[0, h]); vnext.append(vn_ref[0, h]); ok_n.append(after_ok)
    n = range(len(chains))
    s_p = [jnp.where(jnp.logical_and(band_p, ok_p[i]), _dot_nt(qh[i], kprev[i]), NEG_INF) for i in n]
    s_c = [jnp.where(is_lat, _dot_nt(qh[i], kcur[i]), NEG_INF) for i in n]
    s_n = [jnp.where(jnp.logical_and(band_n, ok_n[i]), _dot_nt(qh[i], knext[i]), NEG_INF) for i in n]
    s_x = [_dot_nt(qh[i], kx_ref[0, chains[i][1]]) for i in n]
    sink = []
    for h in range(KV_HEADS):
        sk = jnp.zeros((rows, 1), f32)
        for g in range(ATT_GROUP):
            sk = jnp.where(row_head == g, sink_ref[ATT_GROUP * h + g], sk)
        sink.append(sk)
    sink = [sink[h] for _, h in chains]
    m = [jnp.maximum(jnp.max(jnp.maximum(jnp.maximum(s_p[i], s_c[i]),
                                         jnp.maximum(jnp.maximum(s_n[i], s_x[i][:, :QB]), s_x[i][:, QB:])),
                             axis=-1, keepdims=True), sink[i]) for i in n]
    e_p = [jnp.exp(s_p[i] - m[i]) for i in n]
    e_c = [jnp.exp(s_c[i] - m[i]) for i in n]
    e_n = [jnp.exp(s_n[i] - m[i]) for i in n]
    e_x = [jnp.exp(s_x[i] - m[i]) for i in n]
    den = [jnp.sum((e_p[i] + e_c[i]) + (e_n[i] + e_x[i][:, :QB]) + e_x[i][:, QB:], axis=-1, keepdims=True)
           + jnp.exp(sink[i] - m[i]) for i in n]
    o = [(_dot(e_p[i].astype(bf16), vprev[i]) + _dot(e_c[i].astype(bf16), vcur[i])
          + _dot(e_n[i].astype(bf16), vnext[i]) + _dot(e_x[i].astype(bf16), vx_ref[0, chains[i][1]])) / den[i]
         for i in n]
    for c in range(2):
        o_ref[0, half[c], :] = jnp.concatenate(
            [o[c * KV_HEADS + h][g * QB:(g + 1) * QB] for h in range(KV_HEADS) for g in range(ATT_GROUP)],
            axis=1).astype(bf16)


def _attn_call(sink, q, k, v):
    nbatch, _, tb, _ = q.shape
    nblk = tb // QB
    npairs = nblk // 2
    assert CTX == 2 * QB and nblk % 2 == 0
    kv_blk = (1, KV_HEADS, QB, HEAD_DIM)
    pair_blk = (1, KV_HEADS, 2 * QB, HEAD_DIM)
    before_map = lambda b, j: (b, 0, jnp.maximum(2 * j - 1, 0), 0)
    pair_map = lambda b, j: (b, 0, j, 0)
    after_map = lambda b, j: (b, 0, jnp.minimum(2 * j + 2, nblk - 1), 0)
    ctx_spec = pl.BlockSpec((1, KV_HEADS, CTX, HEAD_DIM), lambda b, j: (b, 0, 0, 0))
    return pl.pallas_call(
        functools.partial(_attn_kernel, npairs=npairs, ctx_pairs=CTX // (2 * QB)),
        grid=(nbatch, npairs),
        in_specs=[
            pl.BlockSpec(memory_space=pltpu.SMEM),
            pl.BlockSpec((1, ATT_HEADS, 2 * QB, HEAD_DIM), pair_map),
            pl.BlockSpec(kv_blk, before_map), pl.BlockSpec(pair_blk, pair_map), pl.BlockSpec(kv_blk, after_map),
            pl.BlockSpec(kv_blk, before_map), pl.BlockSpec(pair_blk, pair_map), pl.BlockSpec(kv_blk, after_map),
            ctx_spec, ctx_spec,
        ],
        out_specs=pl.BlockSpec((1, 2 * QB, ATT_WIDTH), lambda b, j: (b, j, 0)),
        out_shape=jax.ShapeDtypeStruct((nbatch, tb, ATT_WIDTH), bf16),
        compiler_params=_cparams(("parallel", "arbitrary")),
        name="attention",
    )(sink, q, k, k, k, v, v, v, k, v)


def _feat_kernel(*refs, nt, ctx_tiles, has_vres):
    if has_vres:
        (rw_ref, hp_ref, hn_ref, vf_ref, mu_ref, vec_ref, w2_ref, a2_ref, g2_ref, v1_ref, v2_ref,
         r_ref, v_ref, kk_ref, bv_ref, g_ref, kd_ref, lw_ref, bd_ref) = refs
    else:
        (rw_ref, hp_ref, hn_ref, mu_ref, vec_ref, w2_ref, a2_ref, g2_ref,
         r_ref, v_ref, kk_ref, bv_ref, g_ref, kd_ref, lw_ref, bd_ref) = refs
    i = pl.program_id(1)
    u0 = rw_ref[0]
    tm = u0.shape[0]
    prev_zero = jnp.logical_or(i == 0, i == ctx_tiles)
    next_zero = jnp.logical_or(i == ctx_tiles - 1, i == nt - 1)
    halo_p = jnp.where(prev_zero, 0.0, hp_ref[0, SUBLANES - 1:SUBLANES, :])
    halo_n = jnp.where(next_zero, 0.0, hn_ref[0, 0:1, :])
    row = lax.broadcasted_iota(i32, (tm, 1), 0)
    prev = jnp.where(row == 0, halo_p, pltpu.roll(u0, 1, 0))
    nxt = jnp.where(row == tm - 1, halo_n, pltpu.roll(u0, tm - 1, 0))
    mu_p = mu_ref[0:1, :]
    mu_n = mu_ref[1:2, :]
    u = u0 + mu_p * (prev - u0) + mu_n * (nxt - u0)

    r = u[:, 0:RW]
    k = u[:, RW:2 * RW]
    v = u[:, 2 * RW:3 * RW]
    wd = u[:, 3 * RW:3 * RW + 2 * LORA_DECAY]
    ad = u[:, 3 * RW + 2 * LORA_DECAY:3 * RW + 2 * (LORA_DECAY + LORA_ICLR)]
    gd = u[:, 3 * RW + 2 * (LORA_DECAY + LORA_ICLR):]
    k_k = vec_ref[0:1, :]
    k_a = vec_ref[1:2, :]
    r_k = vec_ref[2:3, :]
    ones = _head_ones()

    if has_vres:
        lo = _dot(v.astype(bf16), v1_ref[...])
        gate = _sigmoid(vec_ref[3:4, :] + _dot(lo.astype(bf16), v2_ref[...]))
        v = v + (vf_ref[0] - v) * gate
    decay_in = _dot(jnp.tanh(wd).astype(bf16), w2_ref[...])
    a_in = _dot(ad.astype(bf16), a2_ref[...])
    kk = k * k_k
    n2 = _dot((kk * kk).astype(bf16), ones)
    kk = kk * lax.rsqrt(jnp.maximum(n2, 1e-24))
    g = _dot(_sigmoid(gd).astype(bf16), g2_ref[...])
    ksum = jnp.zeros_like(k)
    for d in range(2):
        w0 = vec_ref[4 + d:5 + d, :]
        a0 = vec_ref[6 + d:7 + d, :]
        lw = -_sigmoid(w0 + decay_in[:, d * RW:(d + 1) * RW]) * math.exp(-0.5)
        a = _sigmoid(a0 + a_in[:, d * RW:(d + 1) * RW])
        kd = k * (1.0 + (a - 1.0) * k_a)
        ksum = ksum + kd
        kd_ref[d, 0] = kd.astype(bf16)
        lw_ref[d, 0] = lw
        bd_ref[d, 0] = (kk * a).astype(bf16)
    bonus = _dot((r * ksum * r_k).astype(bf16), ones)
    r_ref[0] = r.astype(bf16)
    v_ref[0] = v
    kk_ref[0] = kk.astype(bf16)
    bv_ref[0] = bonus * v
    g_ref[0] = g.astype(bf16)


def _feat_call(rw, v_first, mu, vec, w2bd, a2bd, g2, v1, v2):
    nbatch, tb, _ = rw.shape
    nt = tb // TM
    has_vres = v_first is not None
    sub = TM // SUBLANES
    tile = lambda w: pl.BlockSpec((1, TM, w), lambda b, i: (b, i, 0))
    full = lambda a: pl.BlockSpec(a.shape, lambda b, i: (0,) * a.ndim)
    halo = (1, SUBLANES, RWKV_COLS)
    in_specs = [
        tile(RWKV_COLS),
        pl.BlockSpec(halo, lambda b, i: (b, jnp.maximum(i * sub - 1, 0), 0)),
        pl.BlockSpec(halo, lambda b, i: (b, jnp.minimum((i + 1) * sub, tb // SUBLANES - 1), 0)),
    ]
    args = [rw, rw, rw]
    if has_vres:
        in_specs.append(tile(RW))
        args.append(v_first)
    consts = [mu, vec, w2bd, a2bd, g2] + ([v1, v2] if has_vres else [])
    in_specs += [full(a) for a in consts]
    args += consts
    dir_spec = pl.BlockSpec((2, 1, TM, RW), lambda b, i: (0, b, i, 0))
    tok = lambda dt: jax.ShapeDtypeStruct((nbatch, tb, RW), dt)
    dtok = lambda dt: jax.ShapeDtypeStruct((2, nbatch, tb, RW), dt)
    return pl.pallas_call(
        functools.partial(_feat_kernel, nt=nt, ctx_tiles=CTX // TM, has_vres=has_vres),
        grid=(nbatch, nt),
        in_specs=in_specs,
        out_specs=[tile(RW)] * 5 + [dir_spec] * 3,
        out_shape=[tok(bf16), tok(f32), tok(bf16), tok(f32), tok(bf16), dtok(bf16), dtok(f32), dtok(bf16)],
        compiler_params=_cparams(("parallel", "arbitrary")),
        name="rwkv_features",
    )(*args)


def _block_diag_rows(x, width):
    cb = _div_pow2(lax.broadcasted_iota(i32, x.shape, 1), width)
    return jnp.concatenate([jnp.where(cb == h, x, jnp.zeros_like(x)) for h in range(HG)], axis=0)


def _scan_chunks(probs):
    wide = HG * CH
    row_t = lax.broadcasted_iota(i32, (CH, wide), 0)
    col_t = _mod_pow2(lax.broadcasted_iota(i32, (CH, wide), 1), CH)
    row_g = lax.broadcasted_iota(i32, (CH, GW), 0)
    incl_t = {False: col_t <= row_t, True: col_t >= row_t}
    strict_t = {False: col_t < row_t, True: col_t > row_t}
    eye_t = (row_t == col_t).astype(f32)
    rb = _div_pow2(lax.broadcasted_iota(i32, (GW, GW), 0), HEAD_DIM)
    cb = _div_pow2(lax.broadcasted_iota(i32, (GW, GW), 1), HEAD_DIM)
    n = len(probs)
    rev = [p[7] for p in probs]
    def cumsum_rows(x, reverse):
        s = 1
        while s < CH:
            if reverse:
                x = x + jnp.where(row_g < CH - s, pltpu.roll(x, CH - s, 0), 0.0)
            else:
                x = x + jnp.where(row_g >= s, pltpu.roll(x, s, 0), 0.0)
            s *= 2
        return x

    gam = [cumsum_rows(probs[i][5], rev[i]) for i in range(n)]
    ar, bk, k_t, b_t = [], [], [], []
    for i, (s_prev, r, v, kk, k, lw, b, _) in enumerate(probs):
        e_neg = jnp.exp(-gam[i])
        a_s = (-kk * jnp.exp(gam[i] - lw)).astype(bf16)
        r_s = (r * jnp.exp(gam[i])).astype(bf16)
        b_t.append(b * e_neg)
        k_t.append(k * e_neg)
        ar.append(jnp.concatenate([a_s, r_s], axis=0))
        bk.append(jnp.concatenate([_block_diag_rows(b_t[i].astype(bf16), HEAD_DIM),
                                   _block_diag_rows(k_t[i].astype(bf16), HEAD_DIM)], axis=0))
    gram = [_dot_nt(ar[i], bk[i]) for i in range(n)]
    ars = [_dot_nt(ar[i], probs[i][0].astype(bf16)) for i in range(n)]
    v_bd = [_block_diag_rows(probs[i][2].astype(bf16), HEAD_DIM) for i in range(n)]
    p0 = [jnp.where(strict_t[rev[i]], gram[i][:CH, :wide], 0.0).astype(bf16) for i in range(n)]
    lq = [jnp.concatenate([jnp.where(strict_t[rev[i]], gram[i][:CH, wide:], 0.0),
                           jnp.where(incl_t[rev[i]], gram[i][CH:, wide:], 0.0)], axis=0).astype(bf16)
          for i in range(n)]
    lqv = [_dot(lq[i], v_bd[i]) for i in range(n)]
    rhs = [ars[i][:CH] + lqv[i][:CH] for i in range(n)]
    t = [eye_t + p0[i].astype(f32) for i in range(n)]
    p = [_dot(p0[i], _block_diag_rows(p0[i], CH)).astype(bf16) for i in range(n)]
    m = 4
    while m < CH:
        tp = [_dot(jnp.concatenate([t[i].astype(bf16), p[i]], axis=0), _block_diag_rows(p[i], CH))
              for i in range(n)]
        t = [t[i] + tp[i][:CH] for i in range(n)]
        p = [tp[i][CH:].astype(bf16) for i in range(n)]
        m *= 2
    t = [t[i] + _dot(t[i].astype(bf16), _block_diag_rows(p[i], CH)) for i in range(n)]
    u = [_dot(t[i].astype(bf16), _block_diag_rows(rhs[i].astype(bf16), HEAD_DIM)) for i in range(n)]
    out = []
    for i in range(n):
        s_prev, v, lw = probs[i][0], probs[i][2], probs[i][5]
        q_b = jnp.where(incl_t[rev[i]], gram[i][CH:, :wide], 0.0).astype(bf16)
        u_bd = _block_diag_rows(u[i].astype(bf16), HEAD_DIM)
        y = ars[i][CH:] + lqv[i][CH:] + _dot(q_b, u_bd)
        eg = jnp.exp(jnp.sum(lw, axis=0, keepdims=True))
        vu = jnp.concatenate([v, u[i]], axis=0).astype(bf16)
        kb = jnp.concatenate([k_t[i] * eg, b_t[i] * eg], axis=0).astype(bf16)
        s_add = _dot_tn(vu, kb)
        out.append((s_prev * eg + jnp.where(rb == cb, s_add, 0.0), y))
    return out


def _scan_kernel(rf_ref, vf_ref, kkf_ref, kf_ref, lwf_ref, bf_ref,
                 rb_ref, vb_ref, kkb_ref, kb_ref, lwb_ref, bb_ref,
                 yf_ref, yb_ref, s_ref):
    @pl.when(pl.program_id(1) == 0)
    def _():
        s_ref[...] = jnp.zeros_like(s_ref)

    dirs = ((rf_ref, vf_ref, kkf_ref, kf_ref, lwf_ref, bf_ref, yf_ref),
            (rb_ref, vb_ref, kkb_ref, kb_ref, lwb_ref, bb_ref, yb_ref))
    probs, dest = [], []
    for bi in range(SCAN_BATCH):
        for d, (r_ref, v_ref, kk_ref, k_ref, lw_ref, b_ref, y_ref) in enumerate(dirs):
            for g in range(RW // GW):
                sl = slice(g * GW, (g + 1) * GW)
                probs.append((s_ref[bi, d, g], r_ref[bi, :, sl], v_ref[bi, :, sl], kk_ref[bi, :, sl],
                              k_ref[0, bi, :, sl], lw_ref[0, bi, :, sl], b_ref[0, bi, :, sl], d == 1))
                dest.append((bi, d, g, y_ref, sl))
    for (bi, d, g, y_ref, sl), (s_new, y) in zip(dest, _scan_chunks(probs)):
        s_ref[bi, d, g] = s_new
        y_ref[bi, :, sl] = y


def _scan_call(r, v, kk, kd, lw, bd):
    nbatch, tb, _ = r.shape
    nc = tb // CH
    cc = CTX // CH
    sb = SCAN_BATCH
    rev = lambda j: jnp.where(j < cc, cc - 1 - j, nc - 1 + cc - j)
    tok_f = pl.BlockSpec((sb, CH, RW), lambda b, j: (b, j, 0))
    tok_b = pl.BlockSpec((sb, CH, RW), lambda b, j: (b, rev(j), 0))
    dir_f = pl.BlockSpec((1, sb, CH, RW), lambda b, j: (0, b, j, 0))
    dir_b = pl.BlockSpec((1, sb, CH, RW), lambda b, j: (1, b, rev(j), 0))
    out = jax.ShapeDtypeStruct((nbatch, tb, RW), f32)
    return pl.pallas_call(
        _scan_kernel,
        grid=(nbatch // sb, nc),
        in_specs=[tok_f, tok_f, tok_f, dir_f, dir_f, dir_f, tok_b, tok_b, tok_b, dir_b, dir_b, dir_b],
        out_specs=[tok_f, tok_b],
        out_shape=[out, out],
        scratch_shapes=[pltpu.VMEM((sb, 2, RW // GW, GW, GW), f32)],
        compiler_params=_cparams(("parallel", "arbitrary")),
        name="rwkv_scan",
    )(r, v, kk, kd, lw, bd, r, v, kk, kd, lw, bd)


def _pack_bf16_pairs(x):
    bits = pltpu.bitcast(x, u32)
    half = x.shape[1] // 2
    return bits[:, :half] | lax.shift_right_logical(bits[:, half:], jnp.uint32(16))


def _unpack_bf16_pairs(p):
    hi = pltpu.bitcast(p & jnp.uint32(0xFFFF0000), f32)
    lo = pltpu.bitcast(lax.shift_left(p, jnp.uint32(16)), f32)
    return jnp.concatenate([hi, lo], axis=1).astype(bf16)


def _mix_kernel(x_ref, att_ref, yf_ref, yb_ref, bv_ref, g_ref, mod0_ref, mod1_ref, ln_ref, wo_ref, gf_ref,
                wr_ref, rb_ref, xo_ref, xl_ref, lp_ref, gt_ref, cnt_ref):
    ones = _head_ones()
    inv = 1.0 / HEAD_DIM
    tiles = range(MIX_TILES)
    rows = [slice(s * TM, (s + 1) * TM) for s in tiles]
    mods = (mod0_ref, mod1_ref)
    y = [yf_ref[rows[s], :] + yb_ref[rows[s], :] for s in tiles]
    mu = [_dot(y[s].astype(bf16), ones) * inv for s in tiles]
    dlt = [y[s] - mu[s] for s in tiles]
    var = [_dot((dlt[s] * dlt[s]).astype(bf16), ones) * inv for s in tiles]
    rwk = [((dlt[s] * lax.rsqrt(var[s] + GN_EPS) * ln_ref[0:1, :] + ln_ref[1:2, :] + bv_ref[rows[s], :])
            * g_ref[rows[s], :]).astype(bf16) for s in tiles]
    mix = [_dot(att_ref[rows[s], :], wo_ref[0:ATT_WIDTH, :]) + _dot(rwk[s], wo_ref[ATT_WIDTH:, :]) for s in tiles]
    xn = [x_ref[rows[s], :] + mods[s][0, :, 2 * D:3 * D] * mix[s] for s in tiles]
    for s in tiles:
        xo_ref[rows[s], :] = xn[s]
    hf = [_rmsnorm(xn[s], gf_ref[...]) * (1.0 + mods[s][0, :, 4 * D:5 * D]) + mods[s][0, :, 3 * D:4 * D]
          for s in tiles]
    split = [_split2(hf[s]) for s in tiles]
    by_hi = [_dot_nt(wr_ref[...], split[s][0]) for s in tiles]
    logits = [by_hi[s][:N_EXPERTS] + by_hi[s][N_EXPERTS:] + _dot_nt(wr_ref[0:N_EXPERTS, :], split[s][1])
              for s in tiles]
    sorted_rows, lpos, gates, cnt = _route_and_sort(hf, [_sigmoid(lg) for lg in logits], rb_ref[...])
    for s in tiles:
        xl_ref[s * LROWS:(s + 1) * LROWS, :] = sorted_rows[s]
        lp_ref[:, rows[s]] = lpos[s]
        gt_ref[:, rows[s]] = gates[s]
        cnt_ref[s * N_EXPERTS:(s + 1) * N_EXPERTS, :] = cnt[s]


def _route_and_sort(hf, scores, bias):
    tiles = range(len(hf))
    tm = hf[0].shape[0]
    routed = [_route_rows(scores[s], bias) for s in tiles]
    eio = lax.broadcasted_iota(i32, (N_EXPERTS, tm), 0)
    upper = (lax.broadcasted_iota(i32, (tm, tm), 0) < lax.broadcasted_iota(i32, (tm, tm), 1)).astype(bf16)
    lower = (lax.broadcasted_iota(i32, (N_EXPERTS, N_EXPERTS), 0)
             > lax.broadcasted_iota(i32, (N_EXPERTS, N_EXPERTS), 1)).astype(bf16)
    jj = lax.broadcasted_iota(i32, (LROWS, tm), 0)
    oh0 = [(eio == routed[s][0]).astype(f32) for s in tiles]
    oh1 = [(eio == routed[s][1]).astype(f32) for s in tiles]
    both = [oh0[s] + oh1[s] for s in tiles]
    before = [_dot(both[s].astype(bf16), upper) for s in tiles]
    cnt = [jnp.sum(both[s], axis=1, keepdims=True) for s in tiles]
    seg = [_round_up_pow2(cnt[s].astype(i32), SEG_ALIGN).astype(f32) for s in tiles]
    seg_start = [_dot(lower, jnp.broadcast_to(seg[s], (N_EXPERTS, tm)).astype(bf16)) for s in tiles]
    pos = [seg_start[s] + before[s] for s in tiles]
    lp0 = [jnp.sum(oh0[s] * pos[s], axis=0, keepdims=True).astype(i32) for s in tiles]
    lp1 = [jnp.sum(oh1[s] * pos[s], axis=0, keepdims=True).astype(i32) for s in tiles]
    sel_t = [jnp.logical_or(jj == lp0[s], jj == lp1[s]).astype(bf16) for s in tiles]
    sorted_rows = [_pack_bf16_pairs(_dot(sel_t[s], hf[s].astype(bf16))) for s in tiles]
    lpos = [jnp.concatenate([lp0[s], lp1[s]], axis=0) for s in tiles]
    gates = [jnp.concatenate([routed[s][2], routed[s][3]], axis=0) for s in tiles]
    cnt_out = [jnp.broadcast_to(cnt[s], (N_EXPERTS, LANES)) for s in tiles]
    return sorted_rows, lpos, gates, cnt_out


def _route_rows(scores, bias):
    biased = scores + bias
    row = lambda a, e: a[e:e + 1, :]
    best = None
    for gi in range(N_GROUPS):
        m = [row(biased, gi * EXPERTS_PER_GROUP + j) for j in range(EXPERTS_PER_GROUP)]
        gs = None
        for a in range(EXPERTS_PER_GROUP):
            for b in range(a + 1, EXPERTS_PER_GROUP):
                pair = m[a] + m[b]
                gs = pair if gs is None else jnp.maximum(gs, pair)
        if best is None:
            best, g_idx = gs, jnp.zeros(gs.shape, i32)
        else:
            better = gs > best
            g_idx = jnp.where(better, gi, g_idx)
            best = jnp.where(better, gs, best)

    def pick(a, j):
        out = row(a, j)
        for gi in range(1, N_GROUPS):
            out = jnp.where(g_idx == gi, row(a, gi * EXPERTS_PER_GROUP + j), out)
        return out

    vb = [pick(biased, j) for j in range(EXPERTS_PER_GROUP)]
    vs = [pick(scores, j) for j in range(EXPERTS_PER_GROUP)]

    def argmax_first(vals):
        bv, bi = vals[0], jnp.zeros(vals[0].shape, i32)
        for j in range(1, len(vals)):
            better = vals[j] > bv
            bi = jnp.where(better, j, bi)
            bv = jnp.where(better, vals[j], bv)
        return bi

    i1 = argmax_first(vb)
    i2 = argmax_first([jnp.where(i1 == j, -jnp.inf, vb[j]) for j in range(EXPERTS_PER_GROUP)])
    sel = lambda idx: sum(jnp.where(idx == j, vs[j], 0.0) for j in range(EXPERTS_PER_GROUP))
    s1, s2 = sel(i1), sel(i2)
    tot = s1 + s2
    base = g_idx * EXPERTS_PER_GROUP
    return base + i1, base + i2, s1 / tot, s2 / tot


def _mix_call(x, att, yf, yb, bv, g, mod, ln, w_out, g_ffn, w_router, b_router):
    nbatch, tb, _ = x.shape
    tiles_per_batch = tb // TM
    ntiles = nbatch * tiles_per_batch
    assert ntiles % MIX_TILES == 0 and MIX_TILES == 2
    ntok = nbatch * tb
    flat = lambda a: a.reshape(ntok, a.shape[-1])
    rows = lambda w: pl.BlockSpec((MIX_TILES * TM, w), lambda p: (p, 0))
    full = lambda a: pl.BlockSpec(a.shape, lambda p: (0,) * a.ndim)
    ctx_tiles = CTX // TM

    def mod_spec(s):
        def index(p):
            tile = MIX_TILES * p + s
            return (jnp.where(tile % tiles_per_batch < ctx_tiles, nbatch, tile // tiles_per_batch), 0, 0)
        return pl.BlockSpec((1, 1, 6 * D), index)

    route = pl.BlockSpec((TOP_K, MIX_TILES * TM), lambda p: (0, p))
    return pl.pallas_call(
        _mix_kernel,
        grid=(ntiles // MIX_TILES,),
        in_specs=[rows(D), rows(ATT_WIDTH), rows(RW), rows(RW), rows(RW), rows(RW), mod_spec(0), mod_spec(1),
                  full(ln), full(w_out), full(g_ffn), full(w_router), full(b_router)],
        out_specs=[rows(D), pl.BlockSpec((MIX_TILES * LROWS, D // 2), lambda p: (p, 0)), route, route,
                   pl.BlockSpec((MIX_TILES * N_EXPERTS, LANES), lambda p: (p, 0))],
        out_shape=[jax.ShapeDtypeStruct((ntok, D), f32),
                   jax.ShapeDtypeStruct((ntiles * LROWS, D // 2), u32),
                   jax.ShapeDtypeStruct((TOP_K, ntok), i32),
                   jax.ShapeDtypeStruct((TOP_K, ntok), f32),
                   jax.ShapeDtypeStruct((ntiles * N_EXPERTS, LANES), f32)],
        compiler_params=_cparams(("arbitrary",)),
        name="mix_out",
    )(flat(x), flat(att), flat(yf), flat(yb), flat(bv), flat(g), mod, mod, ln, w_out, g_ffn, w_router, b_router)


def _segment_copies(fn, tile, base_ref, seg_ref, ls_ref, src, dst, sem, src_is_global):
    for e in range(N_EXPERTS):
        idx = tile * N_EXPERTS + e
        seg = seg_ref[idx]
        g0 = base_ref[idx]
        l0 = ls_ref[idx]
        size = TM
        while size >= SEG_ALIGN:
            done = lax.bitwise_and(seg, ~(2 * size - 1))

            @pl.when(lax.bitwise_and(seg, size) != 0)
            def _():
                g_rows = pl.ds(pl.multiple_of(g0 + done, SEG_ALIGN), size)
                l_rows = pl.ds(pl.multiple_of(l0 + done, SEG_ALIGN), size)
                s_rows, d_rows = (g_rows, l_rows) if src_is_global else (l_rows, g_rows)
                fn(pltpu.make_async_copy(src.at[s_rows, :], dst.at[d_rows, :], sem))

            size //= 2


def _segment_waits(tile, seg_ref, src, dst, sem):
    total = seg_ref[tile * N_EXPERTS]
    for e in range(1, N_EXPERTS):
        total = total + seg_ref[tile * N_EXPERTS + e]
    size = 2 * TM
    assert LROWS < 2 * size
    while size >= SEG_ALIGN:
        @pl.when(lax.bitwise_and(total, size) != 0)
        def _():
            pltpu.make_async_copy(src.at[pl.ds(0, size), :], dst.at[pl.ds(0, size), :], sem).wait()

        size //= 2


def _dispatch_kernel(base_ref, seg_ref, ls_ref, ends_ref, xl_ref, xs_ref, zeros, stage, sem, in_sem, out_sem):
    tile = pl.program_id(0)
    ntiles = pl.num_programs(0)

    def load(t):
        return pltpu.make_async_copy(xl_ref.at[pl.ds(pl.multiple_of(t * LROWS, SEG_ALIGN), LROWS), :],
                                     stage.at[lax.rem(t, 3)], in_sem.at[lax.rem(t, 3)])

    def zero_tails(fn):
        for e in range(N_EXPERTS):
            end = ends_ref[e]
            start = ends_ref[e - 1] if e else 0

            @pl.when(end > start)
            def _():
                rows = pl.ds(pl.multiple_of(end - MOE_BLK, MOE_BLK), MOE_BLK)
                fn(pltpu.make_async_copy(zeros, xs_ref.at[rows, :], sem))

    def zero_unused(fn):
        def body(blk, carry):
            fn(pltpu.make_async_copy(zeros, xs_ref.at[pl.ds(pl.multiple_of(blk * MOE_BLK, MOE_BLK), MOE_BLK), :],
                                     sem))
            return carry
        lax.fori_loop(ends_ref[N_EXPERTS - 1] // MOE_BLK, xs_ref.shape[0] // MOE_BLK, body, 0)

    @pl.when(tile == 0)
    def _():
        load(tile).start()
        zeros[...] = jnp.zeros_like(zeros)
        zero_tails(lambda cp: cp.start())
        zero_unused(lambda cp: cp.start())
        zero_tails(lambda cp: cp.wait())
        zero_unused(lambda cp: cp.wait())

    @pl.when(tile + 1 < ntiles)
    def _():
        load(tile + 1).start()

    load(tile).wait()
    cur = lax.rem(tile, 2)
    _segment_copies(lambda cp: cp.start(), tile, base_ref, seg_ref, ls_ref, stage.at[lax.rem(tile, 3)], xs_ref,
                    out_sem.at[cur], False)

    @pl.when(tile > 0)
    def _():
        _segment_waits(tile - 1, seg_ref, stage.at[0], xs_ref, out_sem.at[1 - cur])

    @pl.when(tile == ntiles - 1)
    def _():
        _segment_waits(tile, seg_ref, stage.at[0], xs_ref, out_sem.at[cur])


def _dispatch_call(base, seg, lstart, pad_ends, x_local, nrows):
    ntiles = x_local.shape[0] // LROWS
    return pl.pallas_call(
        _dispatch_kernel,
        grid_spec=pltpu.PrefetchScalarGridSpec(
            num_scalar_prefetch=4,
            grid=(ntiles,),
            in_specs=[pl.BlockSpec(memory_space=pl.ANY)],
            out_specs=pl.BlockSpec(memory_space=pl.ANY),
            scratch_shapes=[pltpu.VMEM((MOE_BLK, D // 2), u32), pltpu.VMEM((3, LROWS, D // 2), u32),
                            pltpu.SemaphoreType.DMA(()), pltpu.SemaphoreType.DMA((3,)),
                            pltpu.SemaphoreType.DMA((2,))],
        ),
        out_shape=jax.ShapeDtypeStruct((nrows, D // 2), u32),
        compiler_params=_cparams(("arbitrary",)),
        name="moe_dispatch",
    )(base, seg, lstart, pad_ends, x_local)


def _ffn_kernel(be_ref, nu_ref, x_ref, wg_ref, wu_ref, wd_ref, y_ref, wg_bf, wu_bf, wd_bf):
    i = pl.program_id(0)
    used = i < nu_ref[0]
    new_expert = jnp.logical_or(i == 0, be_ref[i] != be_ref[jnp.maximum(i - 1, 0)])

    @pl.when(jnp.logical_and(used, new_expert))
    def _():
        wg_bf[...] = wg_ref[0, 0].astype(bf16)
        wu_bf[...] = wu_ref[0, 0].astype(bf16)
        wd_bf[...] = wd_ref[0, 0].astype(bf16)

    @pl.when(used)
    def _():
        x = _unpack_bf16_pairs(x_ref[...])
        gt = _dot(x, wg_bf[...])
        up = _dot(x, wu_bf[...])
        hid = (gt * _sigmoid(gt) * up).astype(bf16)
        y = _dot(hid, wd_bf[...])
        y_ref[...] = _pack_bf16_pairs(y.astype(bf16).astype(f32))

    @pl.when(i >= nu_ref[0])
    def _():
        y_ref[...] = jnp.zeros_like(y_ref)


def _ffn_call(blk_expert, n_used, x_sorted, wg, wu, wd, layer):
    nrows = x_sorted.shape[0]
    nblk = nrows // MOE_BLK
    wspec = pl.BlockSpec((1, 1, D, D), lambda i, be, nu: (layer, be[i], 0, 0))
    return pl.pallas_call(
        _ffn_kernel,
        grid_spec=pltpu.PrefetchScalarGridSpec(
            num_scalar_prefetch=2,
            grid=(nblk,),
            in_specs=[pl.BlockSpec((MOE_BLK, D // 2), lambda i, be, nu: (jnp.minimum(i, nu[0] - 1), 0)),
                      wspec, wspec, wspec],
            out_specs=pl.BlockSpec((MOE_BLK, D // 2), lambda i, be, nu: (i, 0)),
            scratch_shapes=[pltpu.VMEM((D, D), bf16)] * 3,
        ),
        out_shape=jax.ShapeDtypeStruct((nrows, D // 2), u32),
        compiler_params=_cparams(("arbitrary",)),
        name="moe_ffn",
    )(blk_expert, n_used, x_sorted, wg, wu, wd)


def _combine_kernel(base_ref, seg_ref, ls_ref, x_ref, lp_ref, gt_ref, mod_ref, *rest,
                    tiles_per_batch, first_tile, final):
    g_ref = rest[0] if final else None
    y_ref, o_ref, ybuf, sem = rest[-4:]
    b = pl.program_id(0)
    i = pl.program_id(1)
    nt = pl.num_programs(1)
    step = b * nt + i
    slot = lax.rem(step, 2)
    tile = b * tiles_per_batch + i + first_tile
    next_tile = jnp.where(i + 1 < nt, tile + 1, (b + 1) * tiles_per_batch + first_tile)

    def copies(fn, which_tile, which_slot):
        _segment_copies(fn, which_tile, base_ref, seg_ref, ls_ref, y_ref, ybuf.at[which_slot],
                        sem.at[which_slot], True)

    @pl.when(step == 0)
    def _():
        ybuf[...] = jnp.zeros_like(ybuf)
        copies(lambda cp: cp.start(), tile, slot)

    @pl.when(step + 1 < pl.num_programs(0) * nt)
    def _():
        copies(lambda cp: cp.start(), next_tile, 1 - slot)

    _segment_waits(tile, seg_ref, y_ref, ybuf.at[slot], sem.at[slot])
    y_loc = _unpack_bf16_pairs(ybuf[slot])
    tm = x_ref.shape[1]
    jj = lax.broadcasted_iota(i32, (LROWS, tm), 0)
    gmat = (jnp.where(jj == lp_ref[0:1, :], gt_ref[0:1, :], 0.0)
            + jnp.where(jj == lp_ref[1:2, :], gt_ref[1:2, :], 0.0))
    g_hi, g_lo = _split2(gmat)
    moe = _dot_tn(g_hi, y_loc) + _dot_tn(g_lo, y_loc)
    xn = x_ref[0] + mod_ref[0, :, 5 * D:6 * D] * moe
    o_ref[0] = _rmsnorm(xn, g_ref[...]) if final else xn


def _combine_call(base, seg, lstart, x, lpos, gates, mod, y_sorted, final_g):
    nbatch, tb, _ = x.shape
    ctx_tiles = CTX // TM
    tiles_per_batch = tb // TM
    final = final_g is not None
    first_tile = ctx_tiles if final else 0
    nt = tiles_per_batch - first_tile
    tile = pl.BlockSpec((1, TM, D), lambda b, i, *_: (b, i + first_tile, 0))
    route = pl.BlockSpec((TOP_K, TM), lambda b, i, *_: (0, b * tiles_per_batch + i + first_tile))
    mod_map = _mod_index(nbatch, ctx_tiles - first_tile)
    in_specs = [tile, route, route, pl.BlockSpec((1, 1, 6 * D), lambda b, i, *_: mod_map(b, i))]
    args = [x, lpos, gates, mod]
    if final:
        in_specs.append(pl.BlockSpec((1, D), lambda b, i, *_: (0, 0)))
        args.append(final_g)
    return pl.pallas_call(
        functools.partial(_combine_kernel, tiles_per_batch=tiles_per_batch, first_tile=first_tile, final=final),
        grid_spec=pltpu.PrefetchScalarGridSpec(
            num_scalar_prefetch=3,
            grid=(nbatch, nt),
            in_specs=in_specs + [pl.BlockSpec(memory_space=pl.ANY)],
            out_specs=pl.BlockSpec((1, TM, D), lambda b, i, *_: (b, i, 0)),
            scratch_shapes=[pltpu.VMEM((2, LROWS, D // 2), u32), pltpu.SemaphoreType.DMA((2,))],
        ),
        out_shape=jax.ShapeDtypeStruct((nbatch, nt * TM, D), f32),
        compiler_params=_cparams(("arbitrary", "arbitrary")),
        name="ffn_residual_final" if final else "ffn_residual",
    )(base, seg, lstart, *args, y_sorted)


def _moe_rows(ntok):
    ntiles = ntok // TM
    worst = TOP_K * ntok + ntiles * N_EXPERTS * (SEG_ALIGN - 1)
    return (-(-worst // MOE_BLK) + N_EXPERTS) * MOE_BLK


def _segment_plan(cnt, nblk):
    seg = (cnt + SEG_ALIGN - 1) // SEG_ALIGN * SEG_ALIGN
    lstart = jnp.cumsum(seg, axis=1) - seg
    rows = jnp.sum(seg, axis=0)
    padded = (rows + MOE_BLK - 1) // MOE_BLK * MOE_BLK
    pad_ends = jnp.cumsum(padded)
    base = (pad_ends - padded)[None, :] + jnp.cumsum(seg, axis=0) - seg
    blk_start = jnp.arange(nblk, dtype=i32) * MOE_BLK
    blk_expert = jnp.minimum(jnp.sum(pad_ends[None, :] <= blk_start[:, None], axis=1), N_EXPERTS - 1)
    n_used = (pad_ends[-1] // MOE_BLK).reshape(1)
    flat = lambda a: a.reshape(-1).astype(i32)
    return flat(base), flat(seg), flat(lstart), flat(pad_ends), blk_expert.astype(i32), n_used.astype(i32)


def _rope_tables(tb):
    rows = SEQ // GRID_W
    row = jnp.repeat(jnp.arange(rows, dtype=f32), GRID_W)
    col = jnp.tile(jnp.arange(GRID_W, dtype=f32), rows)
    inv_freq = ROPE_BASE ** (-jnp.arange(ROPE_FREQS, dtype=f32) / ROPE_FREQS)
    ang_r = row[:, None] * inv_freq[None, :]
    ang_c = col[:, None] * inv_freq[None, :]
    ang = jnp.concatenate([ang_r, ang_r, ang_c, ang_c], axis=-1)
    cos = jnp.concatenate([jnp.ones((CTX, HEAD_DIM), f32), jnp.cos(ang)], axis=0)
    sin = jnp.concatenate([jnp.zeros((CTX, HEAD_DIM), f32), jnp.sin(ang)], axis=0)
    return jnp.tile(cos, (1, LANES // HEAD_DIM)), jnp.tile(sin, (1, LANES // HEAD_DIM))


def _block_diag2(w):
    z = jnp.zeros_like(w[0])
    return jnp.concatenate([jnp.concatenate([w[0], z], axis=1), jnp.concatenate([z, w[1]], axis=1)], axis=0)


def kernel(x, c, ctx, c_ctx, w_mod, b_mod, norm_mix_g, norm_ffn_g, w_in, w_out, att_sink, shift_mu_prev, shift_mu_next, decay_w0, decay_w2, iclr_a0, iclr_a2, vres_v0, vres_v1, vres_v2, gate_g2, k_k, k_a, r_k, ln_x_w, ln_x_b, router_w, router_b, expert_w_gate, expert_w_up, expert_w_down, final_norm_g):
    nbatch = x.shape[0]
    depth = w_mod.shape[0]
    tb = ctx.shape[1] + x.shape[1]
    xa = jnp.concatenate([ctx, x], axis=1)
    nb_pad = -(-(nbatch + 1) // SUBLANES) * SUBLANES
    cond = jnp.zeros((nb_pad, D), f32).at[:nbatch].set(c).at[nbatch].set(c_ctx)
    mod_all = _mod_call(cond, w_mod, b_mod).reshape(depth, nb_pad, 1, 6 * D)
    cos, sin = _rope_tables(tb)
    wr_hi = router_w.T.astype(bf16)
    wr_lo = (router_w.T - wr_hi.astype(f32)).astype(bf16)
    w_router = jnp.concatenate([wr_hi, wr_lo], axis=0)
    b_router = router_b.reshape(N_EXPERTS, 1)
    v_first = None
    for l in range(depth):
        mod = mod_all[l]
        q, k, v, rw = _in_proj_call(xa, mod, norm_mix_g[l].reshape(1, D), w_in[l].astype(bf16), cos, sin)
        att = _attn_call(att_sink[l], q, k, v)
        mu = jnp.stack([shift_mu_prev[l], shift_mu_next[l]])
        v0 = vres_v0[l - 1] if l > 0 else jnp.zeros((RW,), f32)
        vec = jnp.stack([k_k[l], k_a[l], r_k[l].reshape(RW), v0,
                         decay_w0[l, 0], decay_w0[l, 1], iclr_a0[l, 0], iclr_a0[l, 1]])
        if l > 0:
            v1 = jnp.zeros((RW, LANES), f32).at[:, :LORA_VRES].set(vres_v1[l - 1]).astype(bf16)
            v2 = jnp.zeros((LANES, RW), f32).at[:LORA_VRES].set(vres_v2[l - 1]).astype(bf16)
        else:
            v1 = v2 = None
        r_, v_, kk, bv, g, kd, lw, bd = _feat_call(
            rw, v_first, mu, vec, _block_diag2(decay_w2[l]).astype(bf16), _block_diag2(iclr_a2[l]).astype(bf16),
            gate_g2[l].astype(bf16), v1, v2)
        if l == 0:
            v_first = v_
        yf, yb = _scan_call(r_, v_, kk, kd, lw, bd)
        ln = jnp.stack([ln_x_w[l], ln_x_b[l]])
        xa, x_local, lpos, gates, cnt = _mix_call(xa, att, yf, yb, bv, g, mod, ln, w_out[l].astype(bf16),
                                                  norm_ffn_g[l].reshape(1, D), w_router, b_router)
        ntok = nbatch * tb
        nrows = _moe_rows(ntok)
        xa = xa.reshape(nbatch, tb, D)
        cnt = cnt[:, 0].astype(i32).reshape(ntok // TM, N_EXPERTS)
        base, seg, lstart, pad_ends, blk_expert, n_used = _segment_plan(cnt, nrows // MOE_BLK)
        x_sorted = _dispatch_call(base, seg, lstart, pad_ends, x_local, nrows)
        y_sorted = _ffn_call(blk_expert, n_used, x_sorted, expert_w_gate, expert_w_up, expert_w_down, l)
        xa = _combine_call(base, seg, lstart, xa, lpos, gates, mod, y_sorted,
                           final_norm_g.reshape(1, D) if l == depth - 1 else None)
    return xa
```

```python
import functools
import math

import jax
import jax.numpy as jnp
from jax import lax
from jax.experimental import pallas as pl
from jax.experimental.pallas import tpu as pltpu

f32 = jnp.float32
bf16 = jnp.bfloat16
i32 = jnp.int32
u32 = jnp.uint32

D = 1024
SEQ = 4096
CTX = 256
TB = CTX + SEQ
GRID_W = 64
HEAD_DIM = 64
ATT_WIDTH = 512
ATT_HEADS = 8
KV_HEADS = 2
ATT_GROUP = ATT_HEADS // KV_HEADS
KV_WIDTH = KV_HEADS * HEAD_DIM
RW = 512
LORA_DECAY = 64
LORA_ICLR = 64
LORA_VRES = 32
LORA_GATE = 128
RWKV_COLS = 3 * RW + 2 * (LORA_DECAY + LORA_ICLR) + LORA_GATE
ATT_COLS = ATT_WIDTH + 2 * KV_WIDTH
IN_COLS = ATT_COLS + RWKV_COLS
N_EXPERTS = 16
N_GROUPS = 4
EXPERTS_PER_GROUP = 4
TOP_K = 2
MOE_BLK = 256
NORM_EPS = 1e-6
GN_EPS = 64e-5
NEG_INF = -1e30
ATT_SCALE = HEAD_DIM ** -0.5
ROPE_BASE = 10000.0
ROPE_FREQS = HEAD_DIM // 4

LANES = 128
SUBLANES = 8
TM = 256
QB = 128
CH = 64
HG = 4
GW = HG * HEAD_DIM
SCAN_BATCH = 4
MIX_TILES = 2
SEG_ALIGN = SUBLANES
LONG_SEG = 64
LROWS = -(-(TOP_K * TM + N_EXPERTS * SEG_ALIGN) // LANES) * LANES
VMEM_LIMIT = 48 * 1024 * 1024


def _cparams(sem):
    return pltpu.CompilerParams(dimension_semantics=sem, vmem_limit_bytes=VMEM_LIMIT)


def _sigmoid(x):
    return 0.5 * jnp.tanh(0.5 * x) + 0.5


def _div_pow2(x, n):
    assert n & (n - 1) == 0
    return lax.shift_right_logical(x, n.bit_length() - 1)


def _mod_pow2(x, n):
    assert n & (n - 1) == 0
    return lax.bitwise_and(x, n - 1)


def _round_up_pow2(x, n):
    assert n & (n - 1) == 0
    return lax.bitwise_and(x + (n - 1), ~(n - 1))


def _dot(a, b):
    return jnp.dot(a, b, preferred_element_type=f32)


def _dot_nt(a, b):
    return lax.dot_general(a, b, (((1,), (1,)), ((), ())), preferred_element_type=f32)


def _dot_tn(a, b):
    return lax.dot_general(a, b, (((0,), (0,)), ((), ())), preferred_element_type=f32)


def _split2(x):
    hi = x.astype(bf16)
    lo = (x - hi.astype(f32)).astype(bf16)
    return hi, lo


def _rmsnorm(x, g):
    ms = jnp.mean(x * x, axis=-1, keepdims=True)
    return x * lax.rsqrt(ms + NORM_EPS) * g


def _head_ones():
    r = _div_pow2(lax.broadcasted_iota(i32, (RW, RW), 0), HEAD_DIM)
    c = _div_pow2(lax.broadcasted_iota(i32, (RW, RW), 1), HEAD_DIM)
    return (r == c).astype(bf16)


def _mod_kernel(c_ref, w_ref, b_ref, o_ref):
    c = c_ref[...]
    s = (c * _sigmoid(c)).astype(bf16)
    o_ref[0] = _dot(s, w_ref[0].astype(bf16)) + b_ref[0]


def _mod_call(cond, w_mod, b_mod):
    nb = cond.shape[0]
    depth = w_mod.shape[0]
    tn = 1024
    return pl.pallas_call(
        _mod_kernel,
        grid=(depth, 6 * D // tn),
        in_specs=[
            pl.BlockSpec((nb, D), lambda l, j: (0, 0)),
            pl.BlockSpec((1, D, tn), lambda l, j: (l, 0, j)),
            pl.BlockSpec((1, 1, tn), lambda l, j: (l, 0, j)),
        ],
        out_specs=pl.BlockSpec((1, nb, tn), lambda l, j: (l, 0, j)),
        out_shape=jax.ShapeDtypeStruct((depth, nb, 6 * D), f32),
        compiler_params=_cparams(("arbitrary", "arbitrary")),
        name="mod",
    )(cond, w_mod, b_mod.reshape(depth, 1, 6 * D))


def _mod_index(nbatch, ctx_tiles):
    return lambda b, i: (jnp.where(i < ctx_tiles, nbatch, b), 0, 0)


def _in_proj_kernel(x_ref, mod_ref, g_ref, w_ref, cos_ref, sin_ref, q_ref, k_ref, v_ref, rw_ref):
    x = x_ref[0]
    tm = x.shape[0]
    h = _rmsnorm(x, g_ref[...])
    sh = mod_ref[0, :, 0:D]
    sc = mod_ref[0, :, D:2 * D]
    h = (h * (1.0 + sc) + sh).astype(bf16)
    p = _dot(h, w_ref[...])
    cos = cos_ref[...]
    sin = sin_ref[...]
    lane = lax.broadcasted_iota(i32, (tm, LANES), 1)
    first_half = _mod_pow2(lane, 2 * ROPE_FREQS) < ROPE_FREQS

    def rope(t):
        rot = jnp.where(first_half, -pltpu.roll(t, LANES - ROPE_FREQS, 1), pltpu.roll(t, ROPE_FREQS, 1))
        return t * cos + rot * sin

    for j in range(ATT_WIDTH // LANES):
        t = (rope(p[:, j * LANES:(j + 1) * LANES]) * ATT_SCALE).astype(bf16)
        q_ref[0, 2 * j] = t[:, :HEAD_DIM]
        q_ref[0, 2 * j + 1] = t[:, HEAD_DIM:]
    kt = rope(p[:, ATT_WIDTH:ATT_WIDTH + KV_WIDTH]).astype(bf16)
    vt = p[:, ATT_WIDTH + KV_WIDTH:ATT_COLS].astype(bf16)
    for hh in range(KV_HEADS):
        k_ref[0, hh] = kt[:, hh * HEAD_DIM:(hh + 1) * HEAD_DIM]
        v_ref[0, hh] = vt[:, hh * HEAD_DIM:(hh + 1) * HEAD_DIM]
    rw_ref[0] = p[:, ATT_COLS:]


def _in_proj_call(x, mod, g, w_in, cos, sin):
    nbatch, tb, _ = x.shape
    nt = tb // TM
    return pl.pallas_call(
        _in_proj_kernel,
        grid=(nbatch, nt),
        in_specs=[
            pl.BlockSpec((1, TM, D), lambda b, i: (b, i, 0)),
            pl.BlockSpec((1, 1, 6 * D), _mod_index(nbatch, CTX // TM)),
            pl.BlockSpec((1, D), lambda b, i: (0, 0)),
            pl.BlockSpec((D, IN_COLS), lambda b, i: (0, 0)),
            pl.BlockSpec((TM, LANES), lambda b, i: (i, 0)),
            pl.BlockSpec((TM, LANES), lambda b, i: (i, 0)),
        ],
        out_specs=[
            pl.BlockSpec((1, ATT_HEADS, TM, HEAD_DIM), lambda b, i: (b, 0, i, 0)),
            pl.BlockSpec((1, KV_HEADS, TM, HEAD_DIM), lambda b, i: (b, 0, i, 0)),
            pl.BlockSpec((1, KV_HEADS, TM, HEAD_DIM), lambda b, i: (b, 0, i, 0)),
            pl.BlockSpec((1, TM, RWKV_COLS), lambda b, i: (b, i, 0)),
        ],
        out_shape=[
            jax.ShapeDtypeStruct((nbatch, ATT_HEADS, tb, HEAD_DIM), bf16),
            jax.ShapeDtypeStruct((nbatch, KV_HEADS, tb, HEAD_DIM), bf16),
            jax.ShapeDtypeStruct((nbatch, KV_HEADS, tb, HEAD_DIM), bf16),
            jax.ShapeDtypeStruct((nbatch, tb, RWKV_COLS), f32),
        ],
        compiler_params=_cparams(("parallel", "arbitrary")),
        name="in_proj",
    )(x, mod, g, w_in, cos, sin)


def _attn_kernel(sink_ref, q_ref, kp_ref, km_ref, kn_ref, vp_ref, vm_ref, vn_ref, kx_ref, vx_ref, o_ref,
                 *, npairs, ctx_pairs):
    j = pl.program_id(1)
    is_lat = j >= ctx_pairs
    before_ok = jnp.logical_and(is_lat, j - 1 >= ctx_pairs)
    after_ok = jnp.logical_and(is_lat, j + 1 <= npairs - 1)
    rows = ATT_GROUP * QB
    qi = _mod_pow2(lax.broadcasted_iota(i32, (rows, QB), 0), QB)
    kj = lax.broadcasted_iota(i32, (rows, QB), 1)
    band_p = kj >= qi
    band_n = kj <= qi
    row_head = _div_pow2(lax.broadcasted_iota(i32, (rows, 1), 0), QB)
    half = (slice(0, QB), slice(QB, 2 * QB))
    chains = [(c, h) for c in range(2) for h in range(KV_HEADS)]
    qh, kprev, kcur, knext, vprev, vcur, vnext, ok_p, ok_n = [], [], [], [], [], [], [], [], []
    for c, h in chains:
        qh.append(q_ref[0, ATT_GROUP * h:ATT_GROUP * (h + 1), half[c], :].reshape(rows, HEAD_DIM))
        kcur.append(km_ref[0, h, half[c], :])
        vcur.append(vm_ref[0, h, half[c], :])
        if c == 0:
            kprev.append(kp_ref[0, h]); vprev.append(vp_ref[0, h]); ok_p.append(before_ok)
            knext.append(km_ref[0, h, half[1], :]); vnext.append(vm_ref[0, h, half[1], :]); ok_n.append(is_lat)
        else:
            kprev.append(km_ref[0, h, half[0], :]); vprev.append(vm_ref[0, h, half[0], :]); ok_p.append(is_lat)
            knext.append(kn_ref[0, h]); vnext.append(vn_ref[0, h]); ok_n.append(after_ok)
    n = range(len(chains))
    s_p = [jnp.where(jnp.logical_and(band_p, ok_p[i]), _dot_nt(qh[i], kprev[i]), NEG_INF) for i in n]
    s_c = [jnp.where(is_lat, _dot_nt(qh[i], kcur[i]), NEG_INF) for i in n]
    s_n = [jnp.where(jnp.logical_and(band_n, ok_n[i]), _dot_nt(qh[i], knext[i]), NEG_INF) for i in n]
    s_x = [_dot_nt(qh[i], kx_ref[0, chains[i][1]]) for i in n]
    sink = []
    for h in range(KV_HEADS):
        sk = jnp.zeros((rows, 1), f32)
        for g in range(ATT_GROUP):
            sk = jnp.where(row_head == g, sink_ref[ATT_GROUP * h + g], sk)
        sink.append(sk)
    sink = [sink[h] for _, h in chains]
    m = [jnp.maximum(jnp.max(jnp.maximum(jnp.maximum(s_p[i], s_c[i]),
                                         jnp.maximum(jnp.maximum(s_n[i], s_x[i][:, :QB]), s_x[i][:, QB:])),
                             axis=-1, keepdims=True), sink[i]) for i in n]
    e_p = [jnp.exp(s_p[i] - m[i]) for i in n]
    e_c = [jnp.exp(s_c[i] - m[i]) for i in n]
    e_n = [jnp.exp(s_n[i] - m[i]) for i in n]
    e_x = [jnp.exp(s_x[i] - m[i]) for i in n]
    den = [jnp.sum((e_p[i] + e_c[i]) + (e_n[i] + e_x[i][:, :QB]) + e_x[i][:, QB:], axis=-1, keepdims=True)
           + jnp.exp(sink[i] - m[i]) for i in n]
    o = [(_dot(e_p[i].astype(bf16), vprev[i]) + _dot(e_c[i].astype(bf16), vcur[i])
          + _dot(e_n[i].astype(bf16), vnext[i]) + _dot(e_x[i].astype(bf16), vx_ref[0, chains[i][1]])) / den[i]
         for i in n]
    for c in range(2):
        o_ref[0, half[c], :] = jnp.concatenate(
            [o[c * KV_HEADS + h][g * QB:(g + 1) * QB] for h in range(KV_HEADS) for g in range(ATT_GROUP)],
            axis=1).astype(bf16)


def _attn_call(sink, q, k, v):
    nbatch, _, tb, _ = q.shape
    nblk = tb // QB
    npairs = nblk // 2
    assert CTX == 2 * QB and nblk % 2 == 0
    kv_blk = (1, KV_HEADS, QB, HEAD_DIM)
    pair_blk = (1, KV_HEADS, 2 * QB, HEAD_DIM)
    before_map = lambda b, j: (b, 0, jnp.maximum(2 * j - 1, 0), 0)
    pair_map = lambda b, j: (b, 0, j, 0)
    after_map = lambda b, j: (b, 0, jnp.minimum(2 * j + 2, nblk - 1), 0)
    ctx_spec = pl.BlockSpec((1, KV_HEADS, CTX, HEAD_DIM), lambda b, j: (b, 0, 0, 0))
    return pl.pallas_call(
        functools.partial(_attn_kernel, npairs=npairs, ctx_pairs=CTX // (2 * QB)),
        grid=(nbatch, npairs),
        in_specs=[
            pl.BlockSpec(memory_space=pltpu.SMEM),
            pl.BlockSpec((1, ATT_HEADS, 2 * QB, HEAD_DIM), pair_map),
            pl.BlockSpec(kv_blk, before_map), pl.BlockSpec(pair_blk, pair_map), pl.BlockSpec(kv_blk, after_map),
            pl.BlockSpec(kv_blk, before_map), pl.BlockSpec(pair_blk, pair_map), pl.BlockSpec(kv_blk, after_map),
            ctx_spec, ctx_spec,
        ],
        out_specs=pl.BlockSpec((1, 2 * QB, ATT_WIDTH), lambda b, j: (b, j, 0)),
        out_shape=jax.ShapeDtypeStruct((nbatch, tb, ATT_WIDTH), bf16),
        compiler_params=_cparams(("parallel", "arbitrary")),
        name="attention",
    )(sink, q, k, k, k, v, v, v, k, v)


def _feat_kernel(*refs, nt, ctx_tiles, has_vres):
    if has_vres:
        (rw_ref, hp_ref, hn_ref, vf_ref, mu_ref, vec_ref, w2_ref, a2_ref, g2_ref, v1_ref, v2_ref,
         r_ref, v_ref, kk_ref, bv_ref, g_ref, kd_ref, lw_ref, bd_ref) = refs
    else:
        (rw_ref, hp_ref, hn_ref, mu_ref, vec_ref, w2_ref, a2_ref, g2_ref,
         r_ref, v_ref, kk_ref, bv_ref, g_ref, kd_ref, lw_ref, bd_ref) = refs
    i = pl.program_id(1)
    u0 = rw_ref[0]
    tm = u0.shape[0]
    prev_zero = jnp.logical_or(i == 0, i == ctx_tiles)
    next_zero = jnp.logical_or(i == ctx_tiles - 1, i == nt - 1)
    halo_p = jnp.where(prev_zero, 0.0, hp_ref[0, SUBLANES - 1:SUBLANES, :])
    halo_n = jnp.where(next_zero, 0.0, hn_ref[0, 0:1, :])
    row = lax.broadcasted_iota(i32, (tm, 1), 0)
    prev = jnp.where(row == 0, halo_p, pltpu.roll(u0, 1, 0))
    nxt = jnp.where(row == tm - 1, halo_n, pltpu.roll(u0, tm - 1, 0))
    mu_p = mu_ref[0:1, :]
    mu_n = mu_ref[1:2, :]
    u = u0 + mu_p * (prev - u0) + mu_n * (nxt - u0)

    r = u[:, 0:RW]
    k = u[:, RW:2 * RW]
    v = u[:, 2 * RW:3 * RW]
    wd = u[:, 3 * RW:3 * RW + 2 * LORA_DECAY]
    ad = u[:, 3 * RW + 2 * LORA_DECAY:3 * RW + 2 * (LORA_DECAY + LORA_ICLR)]
    gd = u[:, 3 * RW + 2 * (LORA_DECAY + LORA_ICLR):]
    k_k = vec_ref[0:1, :]
    k_a = vec_ref[1:2, :]
    r_k = vec_ref[2:3, :]
    ones = _head_ones()

    if has_vres:
        lo = _dot(v.astype(bf16), v1_ref[...])
        gate = _sigmoid(vec_ref[3:4, :] + _dot(lo.astype(bf16), v2_ref[...]))
        v = v + (vf_ref[0] - v) * gate
    decay_in = _dot(jnp.tanh(wd).astype(bf16), w2_ref[...])
    a_in = _dot(ad.astype(bf16), a2_ref[...])
    kk = k * k_k
    n2 = _dot((kk * kk).astype(bf16), ones)
    kk = kk * lax.rsqrt(jnp.maximum(n2, 1e-24))
    g = _dot(_sigmoid(gd).astype(bf16), g2_ref[...])
    ksum = jnp.zeros_like(k)
    for d in range(2):
        w0 = vec_ref[4 + d:5 + d, :]
        a0 = vec_ref[6 + d:7 + d, :]
        lw = -_sigmoid(w0 + decay_in[:, d * RW:(d + 1) * RW]) * math.exp(-0.5)
        a = _sigmoid(a0 + a_in[:, d * RW:(d + 1) * RW])
        kd = k * (1.0 + (a - 1.0) * k_a)
        ksum = ksum + kd
        kd_ref[d, 0] = kd.astype(bf16)
        lw_ref[d, 0] = lw
        bd_ref[d, 0] = (kk * a).astype(bf16)
    bonus = _dot((r * ksum * r_k).astype(bf16), ones)
    r_ref[0] = r.astype(bf16)
    v_ref[0] = v
    kk_ref[0] = kk.astype(bf16)
    bv_ref[0] = bonus * v
    g_ref[0] = g.astype(bf16)


def _feat_call(rw, v_first, mu, vec, w2bd, a2bd, g2, v1, v2):
    nbatch, tb, _ = rw.shape
    nt = tb // TM
    has_vres = v_first is not None
    sub = TM // SUBLANES
    tile = lambda w: pl.BlockSpec((1, TM, w), lambda b, i: (b, i, 0))
    full = lambda a: pl.BlockSpec(a.shape, lambda b, i: (0,) * a.ndim)
    halo = (1, SUBLANES, RWKV_COLS)
    in_specs = [
        tile(RWKV_COLS),
        pl.BlockSpec(halo, lambda b, i: (b, jnp.maximum(i * sub - 1, 0), 0)),
        pl.BlockSpec(halo, lambda b, i: (b, jnp.minimum((i + 1) * sub, tb // SUBLANES - 1), 0)),
    ]
    args = [rw, rw, rw]
    if has_vres:
        in_specs.append(tile(RW))
        args.append(v_first)
    consts = [mu, vec, w2bd, a2bd, g2] + ([v1, v2] if has_vres else [])
    in_specs += [full(a) for a in consts]
    args += consts
    dir_spec = pl.BlockSpec((2, 1, TM, RW), lambda b, i: (0, b, i, 0))
    tok = lambda dt: jax.ShapeDtypeStruct((nbatch, tb, RW), dt)
    dtok = lambda dt: jax.ShapeDtypeStruct((2, nbatch, tb, RW), dt)
    return pl.pallas_call(
        functools.partial(_feat_kernel, nt=nt, ctx_tiles=CTX // TM, has_vres=has_vres),
        grid=(nbatch, nt),
        in_specs=in_specs,
        out_specs=[tile(RW)] * 5 + [dir_spec] * 3,
        out_shape=[tok(bf16), tok(f32), tok(bf16), tok(f32), tok(bf16), dtok(bf16), dtok(f32), dtok(bf16)],
        compiler_params=_cparams(("parallel", "arbitrary")),
        name="rwkv_features",
    )(*args)


def _block_diag_rows(x, width):
    cb = _div_pow2(lax.broadcasted_iota(i32, x.shape, 1), width)
    return jnp.concatenate([jnp.where(cb == h, x, jnp.zeros_like(x)) for h in range(HG)], axis=0)


def _scan_chunks(probs):
    wide = HG * CH
    row_t = lax.broadcasted_iota(i32, (CH, wide), 0)
    col_t = _mod_pow2(lax.broadcasted_iota(i32, (CH, wide), 1), CH)
    row_g = lax.broadcasted_iota(i32, (CH, GW), 0)
    incl_t = {False: col_t <= row_t, True: col_t >= row_t}
    strict_t = {False: col_t < row_t, True: col_t > row_t}
    eye_t = (row_t == col_t).astype(f32)
    rb = _div_pow2(lax.broadcasted_iota(i32, (GW, GW), 0), HEAD_DIM)
    cb = _div_pow2(lax.broadcasted_iota(i32, (GW, GW), 1), HEAD_DIM)
    n = len(probs)
    rev = [p[7] for p in probs]
    def cumsum_rows(x, reverse):
        s = 1
        while s < CH:
            if reverse:
                x = x + jnp.where(row_g < CH - s, pltpu.roll(x, CH - s, 0), 0.0)
            else:
                x = x + jnp.where(row_g >= s, pltpu.roll(x, s, 0), 0.0)
            s *= 2
        return x

    gam = [cumsum_rows(probs[i][5], rev[i]) for i in range(n)]
    ar, bk, k_t, b_t = [], [], [], []
    for i, (s_prev, r, v, kk, k, lw, b, _) in enumerate(probs):
        e_neg = jnp.exp(-gam[i])
        a_s = (-kk * jnp.exp(gam[i] - lw)).astype(bf16)
        r_s = (r * jnp.exp(gam[i])).astype(bf16)
        b_t.append(b * e_neg)
        k_t.append(k * e_neg)
        ar.append(jnp.concatenate([a_s, r_s], axis=0))
        bk.append(jnp.concatenate([_block_diag_rows(b_t[i].astype(bf16), HEAD_DIM),
                                   _block_diag_rows(k_t[i].astype(bf16), HEAD_DIM)], axis=0))
    gram = [_dot_nt(ar[i], bk[i]) for i in range(n)]
    ars = [_dot_nt(ar[i], probs[i][0].astype(bf16)) for i in range(n)]
    v_bd = [_block_diag_rows(probs[i][2].astype(bf16), HEAD_DIM) for i in range(n)]
    p0 = [jnp.where(strict_t[rev[i]], gram[i][:CH, :wide], 0.0).astype(bf16) for i in range(n)]
    lq = [jnp.concatenate([jnp.where(strict_t[rev[i]], gram[i][:CH, wide:], 0.0),
                           jnp.where(incl_t[rev[i]], gram[i][CH:, wide:], 0.0)], axis=0).astype(bf16)
          for i in range(n)]
    lqv = [_dot(lq[i], v_bd[i]) for i in range(n)]
    rhs = [ars[i][:CH] + lqv[i][:CH] for i in range(n)]
    t = [eye_t + p0[i].astype(f32) for i in range(n)]
    p = [_dot(p0[i], _block_diag_rows(p0[i], CH)).astype(bf16) for i in range(n)]
    m = 4
    while m < CH:
        tp = [_dot(jnp.concatenate([t[i].astype(bf16), p[i]], axis=0), _block_diag_rows(p[i], CH))
              for i in range(n)]
        t = [t[i] + tp[i][:CH] for i in range(n)]
        p = [tp[i][CH:].astype(bf16) for i in range(n)]
        m *= 2
    t = [t[i] + _dot(t[i].astype(bf16), _block_diag_rows(p[i], CH)) for i in range(n)]
    u = [_dot(t[i].astype(bf16), _block_diag_rows(rhs[i].astype(bf16), HEAD_DIM)) for i in range(n)]
    out = []
    for i in range(n):
        s_prev, v, lw = probs[i][0], probs[i][2], probs[i][5]
        q_b = jnp.where(incl_t[rev[i]], gram[i][CH:, :wide], 0.0).astype(bf16)
        u_bd = _block_diag_rows(u[i].astype(bf16), HEAD_DIM)
        y = ars[i][CH:] + lqv[i][CH:] + _dot(q_b, u_bd)
        eg = jnp.exp(jnp.sum(lw, axis=0, keepdims=True))
        vu = jnp.concatenate([v, u[i]], axis=0).astype(bf16)
        kb = jnp.concatenate([k_t[i] * eg, b_t[i] * eg], axis=0).astype(bf16)
        s_add = _dot_tn(vu, kb)
        out.append((s_prev * eg + jnp.where(rb == cb, s_add, 0.0), y))
    return out


def _scan_kernel(rf_ref, vf_ref, kkf_ref, kf_ref, lwf_ref, bf_ref,
                 rb_ref, vb_ref, kkb_ref, kb_ref, lwb_ref, bb_ref,
                 yf_ref, yb_ref, s_ref):
    @pl.when(pl.program_id(1) == 0)
    def _():
        s_ref[...] = jnp.zeros_like(s_ref)

    dirs = ((rf_ref, vf_ref, kkf_ref, kf_ref, lwf_ref, bf_ref, yf_ref),
            (rb_ref, vb_ref, kkb_ref, kb_ref, lwb_ref, bb_ref, yb_ref))
    probs, dest = [], []
    for bi in range(SCAN_BATCH):
        for d, (r_ref, v_ref, kk_ref, k_ref, lw_ref, b_ref, y_ref) in enumerate(dirs):
            for g in range(RW // GW):
                sl = slice(g * GW, (g + 1) * GW)
                probs.append((s_ref[bi, d, g], r_ref[bi, :, sl], v_ref[bi, :, sl], kk_ref[bi, :, sl],
                              k_ref[0, bi, :, sl], lw_ref[0, bi, :, sl], b_ref[0, bi, :, sl], d == 1))
                dest.append((bi, d, g, y_ref, sl))
    for (bi, d, g, y_ref, sl), (s_new, y) in zip(dest, _scan_chunks(probs)):
        s_ref[bi, d, g] = s_new
        y_ref[bi, :, sl] = y


def _scan_call(r, v, kk, kd, lw, bd):
    nbatch, tb, _ = r.shape
    nc = tb // CH
    cc = CTX // CH
    sb = SCAN_BATCH
    rev = lambda j: jnp.where(j < cc, cc - 1 - j, nc - 1 + cc - j)
    tok_f = pl.BlockSpec((sb, CH, RW), lambda b, j: (b, j, 0))
    tok_b = pl.BlockSpec((sb, CH, RW), lambda b, j: (b, rev(j), 0))
    dir_f = pl.BlockSpec((1, sb, CH, RW), lambda b, j: (0, b, j, 0))
    dir_b = pl.BlockSpec((1, sb, CH, RW), lambda b, j: (1, b, rev(j), 0))
    out = jax.ShapeDtypeStruct((nbatch, tb, RW), f32)
    return pl.pallas_call(
        _scan_kernel,
        grid=(nbatch // sb, nc),
        in_specs=[tok_f, tok_f, tok_f, dir_f, dir_f, dir_f, tok_b, tok_b, tok_b, dir_b, dir_b, dir_b],
        out_specs=[tok_f, tok_b],
        out_shape=[out, out],
        scratch_shapes=[pltpu.VMEM((sb, 2, RW // GW, GW, GW), f32)],
        compiler_params=_cparams(("parallel", "arbitrary")),
        name="rwkv_scan",
    )(r, v, kk, kd, lw, bd, r, v, kk, kd, lw, bd)


def _pack_bf16_pairs(x):
    bits = pltpu.bitcast(x, u32)
    half = x.shape[1] // 2
    return bits[:, :half] | lax.shift_right_logical(bits[:, half:], jnp.uint32(16))


def _unpack_bf16_pairs(p):
    hi = pltpu.bitcast(p & jnp.uint32(0xFFFF0000), f32)
    lo = pltpu.bitcast(lax.shift_left(p, jnp.uint32(16)), f32)
    return jnp.concatenate([hi, lo], axis=1).astype(bf16)


def _mix_kernel(x_ref, att_ref, yf_ref, yb_ref, bv_ref, g_ref, mod0_ref, mod1_ref, ln_ref, wo_ref, gf_ref,
                wr_ref, rb_ref, xo_ref, xl_ref, lp_ref, gt_ref, cnt_ref):
    ones = _head_ones()
    inv = 1.0 / HEAD_DIM
    tiles = range(MIX_TILES)
    rows = [slice(s * TM, (s + 1) * TM) for s in tiles]
    mods = (mod0_ref, mod1_ref)
    y = [yf_ref[rows[s], :] + yb_ref[rows[s], :] for s in tiles]
    mu = [_dot(y[s].astype(bf16), ones) * inv for s in tiles]
    dlt = [y[s] - mu[s] for s in tiles]
    var = [_dot((dlt[s] * dlt[s]).astype(bf16), ones) * inv for s in tiles]
    rwk = [((dlt[s] * lax.rsqrt(var[s] + GN_EPS) * ln_ref[0:1, :] + ln_ref[1:2, :] + bv_ref[rows[s], :])
            * g_ref[rows[s], :]).astype(bf16) for s in tiles]
    mix = [_dot(att_ref[rows[s], :], wo_ref[0:ATT_WIDTH, :]) + _dot(rwk[s], wo_ref[ATT_WIDTH:, :]) for s in tiles]
    xn = [x_ref[rows[s], :] + mods[s][0, :, 2 * D:3 * D] * mix[s] for s in tiles]
    for s in tiles:
        xo_ref[rows[s], :] = xn[s]
    hf = [_rmsnorm(xn[s], gf_ref[...]) * (1.0 + mods[s][0, :, 4 * D:5 * D]) + mods[s][0, :, 3 * D:4 * D]
          for s in tiles]
    split = [_split2(hf[s]) for s in tiles]
    by_hi = [_dot_nt(wr_ref[...], split[s][0]) for s in tiles]
    logits = [by_hi[s][:N_EXPERTS] + by_hi[s][N_EXPERTS:] + _dot_nt(wr_ref[0:N_EXPERTS, :], split[s][1])
              for s in tiles]
    sorted_rows, lpos, gates, cnt = _route_and_sort(hf, [_sigmoid(lg) for lg in logits], rb_ref[...])
    for s in tiles:
        xl_ref[s * LROWS:(s + 1) * LROWS, :] = sorted_rows[s]
        lp_ref[:, rows[s]] = lpos[s]
        gt_ref[:, rows[s]] = gates[s]
        cnt_ref[s * N_EXPERTS:(s + 1) * N_EXPERTS, :] = cnt[s]


def _route_and_sort(hf, scores, bias):
    tiles = range(len(hf))
    tm = hf[0].shape[0]
    routed = [_route_rows(scores[s], bias) for s in tiles]
    eio = lax.broadcasted_iota(i32, (N_EXPERTS, tm), 0)
    upper = (lax.broadcasted_iota(i32, (tm, tm), 0) < lax.broadcasted_iota(i32, (tm, tm), 1)).astype(bf16)
    lower = (lax.broadcasted_iota(i32, (N_EXPERTS, N_EXPERTS), 0)
             > lax.broadcasted_iota(i32, (N_EXPERTS, N_EXPERTS), 1)).astype(bf16)
    jj = lax.broadcasted_iota(i32, (LROWS, tm), 0)
    oh0 = [(eio == routed[s][0]).astype(f32) for s in tiles]
    oh1 = [(eio == routed[s][1]).astype(f32) for s in tiles]
    both = [oh0[s] + oh1[s] for s in tiles]
    before = [_dot(both[s].astype(bf16), upper) for s in tiles]
    cnt = [jnp.sum(both[s], axis=1, keepdims=True) for s in tiles]
    seg = [_round_up_pow2(cnt[s].astype(i32), SEG_ALIGN).astype(f32) for s in tiles]
    seg_start = [_dot(lower, jnp.broadcast_to(seg[s], (N_EXPERTS, tm)).astype(bf16)) for s in tiles]
    pos = [seg_start[s] + before[s] for s in tiles]
    lp0 = [jnp.sum(oh0[s] * pos[s], axis=0, keepdims=True).astype(i32) for s in tiles]
    lp1 = [jnp.sum(oh1[s] * pos[s], axis=0, keepdims=True).astype(i32) for s in tiles]
    sel_t = [jnp.logical_or(jj == lp0[s], jj == lp1[s]).astype(bf16) for s in tiles]
    sorted_rows = [_pack_bf16_pairs(_dot(sel_t[s], hf[s].astype(bf16))) for s in tiles]
    lpos = [jnp.concatenate([lp0[s], lp1[s]], axis=0) for s in tiles]
    gates = [jnp.concatenate([routed[s][2], routed[s][3]], axis=0) for s in tiles]
    cnt_out = [jnp.broadcast_to(cnt[s], (N_EXPERTS, LANES)) for s in tiles]
    return sorted_rows, lpos, gates, cnt_out


def _route_rows(scores, bias):
    biased = scores + bias
    row = lambda a, e: a[e:e + 1, :]
    best = None
    for gi in range(N_GROUPS):
        m = [row(biased, gi * EXPERTS_PER_GROUP + j) for j in range(EXPERTS_PER_GROUP)]
        gs = None
        for a in range(EXPERTS_PER_GROUP):
            for b in range(a + 1, EXPERTS_PER_GROUP):
                pair = m[a] + m[b]
                gs = pair if gs is None else jnp.maximum(gs, pair)
        if best is None:
            best, g_idx = gs, jnp.zeros(gs.shape, i32)
        else:
            better = gs > best
            g_idx = jnp.where(better, gi, g_idx)
            best = jnp.where(better, gs, best)

    def pick(a, j):
        out = row(a, j)
        for gi in range(1, N_GROUPS):
            out = jnp.where(g_idx == gi, row(a, gi * EXPERTS_PER_GROUP + j), out)
        return out

    vb = [pick(biased, j) for j in range(EXPERTS_PER_GROUP)]
    vs = [pick(scores, j) for j in range(EXPERTS_PER_GROUP)]

    def argmax_first(vals):
        bv, bi = vals[0], jnp.zeros(vals[0].shape, i32)
        for j in range(1, len(vals)):
            better = vals[j] > bv
            bi = jnp.where(better, j, bi)
            bv = jnp.where(better, vals[j], bv)
        return bi

    i1 = argmax_first(vb)
    i2 = argmax_first([jnp.where(i1 == j, -jnp.inf, vb[j]) for j in range(EXPERTS_PER_GROUP)])
    sel = lambda idx: sum(jnp.where(idx == j, vs[j], 0.0) for j in range(EXPERTS_PER_GROUP))
    s1, s2 = sel(i1), sel(i2)
    tot = s1 + s2
    base = g_idx * EXPERTS_PER_GROUP
    return base + i1, base + i2, s1 / tot, s2 / tot


def _mix_call(x, att, yf, yb, bv, g, mod, ln, w_out, g_ffn, w_router, b_router):
    nbatch, tb, _ = x.shape
    tiles_per_batch = tb // TM
    ntiles = nbatch * tiles_per_batch
    assert ntiles % MIX_TILES == 0 and MIX_TILES == 2
    ntok = nbatch * tb
    flat = lambda a: a.reshape(ntok, a.shape[-1])
    rows = lambda w: pl.BlockSpec((MIX_TILES * TM, w), lambda p: (p, 0))
    full = lambda a: pl.BlockSpec(a.shape, lambda p: (0,) * a.ndim)
    ctx_tiles = CTX // TM

    def mod_spec(s):
        def index(p):
            tile = MIX_TILES * p + s
            return (jnp.where(tile % tiles_per_batch < ctx_tiles, nbatch, tile // tiles_per_batch), 0, 0)
        return pl.BlockSpec((1, 1, 6 * D), index)

    route = pl.BlockSpec((TOP_K, MIX_TILES * TM), lambda p: (0, p))
    return pl.pallas_call(
        _mix_kernel,
        grid=(ntiles // MIX_TILES,),
        in_specs=[rows(D), rows(ATT_WIDTH), rows(RW), rows(RW), rows(RW), rows(RW), mod_spec(0), mod_spec(1),
                  full(ln), full(w_out), full(g_ffn), full(w_router), full(b_router)],
        out_specs=[rows(D), pl.BlockSpec((MIX_TILES * LROWS, D // 2), lambda p: (p, 0)), route, route,
                   pl.BlockSpec((MIX_TILES * N_EXPERTS, LANES), lambda p: (p, 0))],
        out_shape=[jax.ShapeDtypeStruct((ntok, D), f32),
                   jax.ShapeDtypeStruct((ntiles * LROWS, D // 2), u32),
                   jax.ShapeDtypeStruct((TOP_K, ntok), i32),
                   jax.ShapeDtypeStruct((TOP_K, ntok), f32),
                   jax.ShapeDtypeStruct((ntiles * N_EXPERTS, LANES), f32)],
        compiler_params=_cparams(("arbitrary",)),
        name="mix_out",
    )(flat(x), flat(att), flat(yf), flat(yb), flat(bv), flat(g), mod, mod, ln, w_out, g_ffn, w_router, b_router)


def _segment_copies(fn, tile, base_ref, seg_ref, ls_ref, src, dst, sem, src_is_global):
    def copy_bits(seg, g0, l0, sizes):
        for size in sizes:
            done = lax.bitwise_and(seg, ~(2 * size - 1))

            @pl.when(lax.bitwise_and(seg, size) != 0)
            def _():
                g_rows = pl.ds(pl.multiple_of(g0 + done, SEG_ALIGN), size)
                l_rows = pl.ds(pl.multiple_of(l0 + done, SEG_ALIGN), size)
                s_rows, d_rows = (g_rows, l_rows) if src_is_global else (l_rows, g_rows)
                fn(pltpu.make_async_copy(src.at[s_rows, :], dst.at[d_rows, :], sem))

    sizes = [TM >> s for s in range((TM // SEG_ALIGN).bit_length())]
    big = [s for s in sizes if s >= LONG_SEG]
    for e in range(N_EXPERTS):
        idx = tile * N_EXPERTS + e
        seg = seg_ref[idx]
        g0 = base_ref[idx]
        l0 = ls_ref[idx]

        @pl.when(seg >= LONG_SEG)
        def _():
            copy_bits(seg, g0, l0, big)

        copy_bits(seg, g0, l0, [s for s in sizes if s < LONG_SEG])


def _segment_waits(tile, seg_ref, src, dst, sem):
    total = seg_ref[tile * N_EXPERTS]
    for e in range(1, N_EXPERTS):
        total = total + seg_ref[tile * N_EXPERTS + e]
    size = 2 * TM
    assert LROWS < 2 * size
    while size >= SEG_ALIGN:
        @pl.when(lax.bitwise_and(total, size) != 0)
        def _():
            pltpu.make_async_copy(src.at[pl.ds(0, size), :], dst.at[pl.ds(0, size), :], sem).wait()

        size //= 2


def _dispatch_kernel(base_ref, seg_ref, ls_ref, ends_ref, xl_ref, xs_ref, zeros, stage, sem, in_sem, out_sem):
    tile = pl.program_id(0)
    ntiles = pl.num_programs(0)

    def load(t):
        return pltpu.make_async_copy(xl_ref.at[pl.ds(pl.multiple_of(t * LROWS, SEG_ALIGN), LROWS), :],
                                     stage.at[lax.rem(t, 3)], in_sem.at[lax.rem(t, 3)])

    def zero_tails(fn):
        for e in range(N_EXPERTS):
            end = ends_ref[e]
            start = ends_ref[e - 1] if e else 0

            @pl.when(end > start)
            def _():
                rows = pl.ds(pl.multiple_of(end - MOE_BLK, MOE_BLK), MOE_BLK)
                fn(pltpu.make_async_copy(zeros, xs_ref.at[rows, :], sem))

    def zero_unused(fn):
        def body(blk, carry):
            fn(pltpu.make_async_copy(zeros, xs_ref.at[pl.ds(pl.multiple_of(blk * MOE_BLK, MOE_BLK), MOE_BLK), :],
                                     sem))
            return carry
        lax.fori_loop(ends_ref[N_EXPERTS - 1] // MOE_BLK, xs_ref.shape[0] // MOE_BLK, body, 0)

    @pl.when(tile == 0)
    def _():
        load(tile).start()
        zeros[...] = jnp.zeros_like(zeros)
        zero_tails(lambda cp: cp.start())
        zero_unused(lambda cp: cp.start())
        zero_tails(lambda cp: cp.wait())
        zero_unused(lambda cp: cp.wait())

    @pl.when(tile + 1 < ntiles)
    def _():
        load(tile + 1).start()

    load(tile).wait()
    cur = lax.rem(tile, 2)
    _segment_copies(lambda cp: cp.start(), tile, base_ref, seg_ref, ls_ref, stage.at[lax.rem(tile, 3)], xs_ref,
                    out_sem.at[cur], False)

    @pl.when(tile > 0)
    def _():
        _segment_waits(tile - 1, seg_ref, stage.at[0], xs_ref, out_sem.at[1 - cur])

    @pl.when(tile == ntiles - 1)
    def _():
        _segment_waits(tile, seg_ref, stage.at[0], xs_ref, out_sem.at[cur])


def _dispatch_call(base, seg, lstart, pad_ends, x_local, nrows):
    ntiles = x_local.shape[0] // LROWS
    return pl.pallas_call(
        _dispatch_kernel,
        grid_spec=pltpu.PrefetchScalarGridSpec(
            num_scalar_prefetch=4,
            grid=(ntiles,),
            in_specs=[pl.BlockSpec(memory_space=pl.ANY)],
            out_specs=pl.BlockSpec(memory_space=pl.ANY),
            scratch_shapes=[pltpu.VMEM((MOE_BLK, D // 2), u32), pltpu.VMEM((3, LROWS, D // 2), u32),
                            pltpu.SemaphoreType.DMA(()), pltpu.SemaphoreType.DMA((3,)),
                            pltpu.SemaphoreType.DMA((2,))],
        ),
        out_shape=jax.ShapeDtypeStruct((nrows, D // 2), u32),
        compiler_params=_cparams(("arbitrary",)),
        name="moe_dispatch",
    )(base, seg, lstart, pad_ends, x_local)


def _ffn_kernel(be_ref, nu_ref, x_ref, wg_ref, wu_ref, wd_ref, y_ref, wg_bf, wu_bf, wd_bf):
    i = pl.program_id(0)
    used = i < nu_ref[0]
    new_expert = jnp.logical_or(i == 0, be_ref[i] != be_ref[jnp.maximum(i - 1, 0)])

    @pl.when(jnp.logical_and(used, new_expert))
    def _():
        wg_bf[...] = wg_ref[0, 0].astype(bf16)
        wu_bf[...] = wu_ref[0, 0].astype(bf16)
        wd_bf[...] = wd_ref[0, 0].astype(bf16)

    @pl.when(used)
    def _():
        x = _unpack_bf16_pairs(x_ref[...])
        gt = _dot(x, wg_bf[...])
        up = _dot(x, wu_bf[...])
        hid = (gt * _sigmoid(gt) * up).astype(bf16)
        y = _dot(hid, wd_bf[...])
        y_ref[...] = _pack_bf16_pairs(y.astype(bf16).astype(f32))

    @pl.when(i >= nu_ref[0])
    def _():
        y_ref[...] = jnp.zeros_like(y_ref)


def _ffn_call(blk_expert, n_used, x_sorted, wg, wu, wd, layer):
    nrows = x_sorted.shape[0]
    nblk = nrows // MOE_BLK
    wspec = pl.BlockSpec((1, 1, D, D), lambda i, be, nu: (layer, be[i], 0, 0))
    return pl.pallas_call(
        _ffn_kernel,
        grid_spec=pltpu.PrefetchScalarGridSpec(
            num_scalar_prefetch=2,
            grid=(nblk,),
            in_specs=[pl.BlockSpec((MOE_BLK, D // 2), lambda i, be, nu: (jnp.minimum(i, nu[0] - 1), 0)),
                      wspec, wspec, wspec],
            out_specs=pl.BlockSpec((MOE_BLK, D // 2), lambda i, be, nu: (i, 0)),
            scratch_shapes=[pltpu.VMEM((D, D), bf16)] * 3,
        ),
        out_shape=jax.ShapeDtypeStruct((nrows, D // 2), u32),
        compiler_params=_cparams(("arbitrary",)),
        name="moe_ffn",
    )(blk_expert, n_used, x_sorted, wg, wu, wd)


def _combine_kernel(base_ref, seg_ref, ls_ref, x_ref, lp_ref, gt_ref, mod_ref, *rest,
                    tiles_per_batch, first_tile, final):
    g_ref = rest[0] if final else None
    y_ref, o_ref, ybuf, sem = rest[-4:]
    b = pl.program_id(0)
    i = pl.program_id(1)
    nt = pl.num_programs(1)
    step = b * nt + i
    slot = lax.rem(step, 2)
    tile = b * tiles_per_batch + i + first_tile
    next_tile = jnp.where(i + 1 < nt, tile + 1, (b + 1) * tiles_per_batch + first_tile)

    def copies(fn, which_tile, which_slot):
        _segment_copies(fn, which_tile, base_ref, seg_ref, ls_ref, y_ref, ybuf.at[which_slot],
                        sem.at[which_slot], True)

    @pl.when(step == 0)
    def _():
        ybuf[...] = jnp.zeros_like(ybuf)
        copies(lambda cp: cp.start(), tile, slot)

    @pl.when(step + 1 < pl.num_programs(0) * nt)
    def _():
        copies(lambda cp: cp.start(), next_tile, 1 - slot)

    _segment_waits(tile, seg_ref, y_ref, ybuf.at[slot], sem.at[slot])
    y_loc = _unpack_bf16_pairs(ybuf[slot])
    tm = x_ref.shape[1]
    jj = lax.broadcasted_iota(i32, (LROWS, tm), 0)
    gmat = (jnp.where(jj == lp_ref[0:1, :], gt_ref[0:1, :], 0.0)
            + jnp.where(jj == lp_ref[1:2, :], gt_ref[1:2, :], 0.0))
    g_hi, g_lo = _split2(gmat)
    moe = _dot_tn(g_hi, y_loc) + _dot_tn(g_lo, y_loc)
    xn = x_ref[0] + mod_ref[0, :, 5 * D:6 * D] * moe
    o_ref[0] = _rmsnorm(xn, g_ref[...]) if final else xn


def _combine_call(base, seg, lstart, x, lpos, gates, mod, y_sorted, final_g):
    nbatch, tb, _ = x.shape
    ctx_tiles = CTX // TM
    tiles_per_batch = tb // TM
    final = final_g is not None
    first_tile = ctx_tiles if final else 0
    nt = tiles_per_batch - first_tile
    tile = pl.BlockSpec((1, TM, D), lambda b, i, *_: (b, i + first_tile, 0))
    route = pl.BlockSpec((TOP_K, TM), lambda b, i, *_: (0, b * tiles_per_batch + i + first_tile))
    mod_map = _mod_index(nbatch, ctx_tiles - first_tile)
    in_specs = [tile, route, route, pl.BlockSpec((1, 1, 6 * D), lambda b, i, *_: mod_map(b, i))]
    args = [x, lpos, gates, mod]
    if final:
        in_specs.append(pl.BlockSpec((1, D), lambda b, i, *_: (0, 0)))
        args.append(final_g)
    return pl.pallas_call(
        functools.partial(_combine_kernel, tiles_per_batch=tiles_per_batch, first_tile=first_tile, final=final),
        grid_spec=pltpu.PrefetchScalarGridSpec(
            num_scalar_prefetch=3,
            grid=(nbatch, nt),
            in_specs=in_specs + [pl.BlockSpec(memory_space=pl.ANY)],
            out_specs=pl.BlockSpec((1, TM, D), lambda b, i, *_: (b, i, 0)),
            scratch_shapes=[pltpu.VMEM((2, LROWS, D // 2), u32), pltpu.SemaphoreType.DMA((2,))],
        ),
        out_shape=jax.ShapeDtypeStruct((nbatch, nt * TM, D), f32),
        compiler_params=_cparams(("arbitrary", "arbitrary")),
        name="ffn_residual_final" if final else "ffn_residual",
    )(base, seg, lstart, *args, y_sorted)


def _moe_rows(ntok):
    ntiles = ntok // TM
    worst = TOP_K * ntok + ntiles * N_EXPERTS * (SEG_ALIGN - 1)
    return (-(-worst // MOE_BLK) + N_EXPERTS) * MOE_BLK


def _segment_plan(cnt, nblk):
    seg = (cnt + SEG_ALIGN - 1) // SEG_ALIGN * SEG_ALIGN
    lstart = jnp.cumsum(seg, axis=1) - seg
    rows = jnp.sum(seg, axis=0)
    padded = (rows + MOE_BLK - 1) // MOE_BLK * MOE_BLK
    pad_ends = jnp.cumsum(padded)
    base = (pad_ends - padded)[None, :] + jnp.cumsum(seg, axis=0) - seg
    blk_start = jnp.arange(nblk, dtype=i32) * MOE_BLK
    blk_expert = jnp.minimum(jnp.sum(pad_ends[None, :] <= blk_start[:, None], axis=1), N_EXPERTS - 1)
    n_used = (pad_ends[-1] // MOE_BLK).reshape(1)
    flat = lambda a: a.reshape(-1).astype(i32)
    return flat(base), flat(seg), flat(lstart), flat(pad_ends), blk_expert.astype(i32), n_used.astype(i32)


def _rope_tables(tb):
    rows = SEQ // GRID_W
    row = jnp.repeat(jnp.arange(rows, dtype=f32), GRID_W)
    col = jnp.tile(jnp.arange(GRID_W, dtype=f32), rows)
    inv_freq = ROPE_BASE ** (-jnp.arange(ROPE_FREQS, dtype=f32) / ROPE_FREQS)
    ang_r = row[:, None] * inv_freq[None, :]
    ang_c = col[:, None] * inv_freq[None, :]
    ang = jnp.concatenate([ang_r, ang_r, ang_c, ang_c], axis=-1)
    cos = jnp.concatenate([jnp.ones((CTX, HEAD_DIM), f32), jnp.cos(ang)], axis=0)
    sin = jnp.concatenate([jnp.zeros((CTX, HEAD_DIM), f32), jnp.sin(ang)], axis=0)
    return jnp.tile(cos, (1, LANES // HEAD_DIM)), jnp.tile(sin, (1, LANES // HEAD_DIM))


def _block_diag2(w):
    z = jnp.zeros_like(w[0])
    return jnp.concatenate([jnp.concatenate([w[0], z], axis=1), jnp.concatenate([z, w[1]], axis=1)], axis=0)


def kernel(x, c, ctx, c_ctx, w_mod, b_mod, norm_mix_g, norm_ffn_g, w_in, w_out, att_sink, shift_mu_prev, shift_mu_next, decay_w0, decay_w2, iclr_a0, iclr_a2, vres_v0, vres_v1, vres_v2, gate_g2, k_k, k_a, r_k, ln_x_w, ln_x_b, router_w, router_b, expert_w_gate, expert_w_up, expert_w_down, final_norm_g):
    nbatch = x.shape[0]
    depth = w_mod.shape[0]
    tb = ctx.shape[1] + x.shape[1]
    xa = jnp.concatenate([ctx, x], axis=1)
    nb_pad = -(-(nbatch + 1) // SUBLANES) * SUBLANES
    cond = jnp.zeros((nb_pad, D), f32).at[:nbatch].set(c).at[nbatch].set(c_ctx)
    mod_all = _mod_call(cond, w_mod, b_mod).reshape(depth, nb_pad, 1, 6 * D)
    cos, sin = _rope_tables(tb)
    wr_hi = router_w.T.astype(bf16)
    wr_lo = (router_w.T - wr_hi.astype(f32)).astype(bf16)
    w_router = jnp.concatenate([wr_hi, wr_lo], axis=0)
    b_router = router_b.reshape(N_EXPERTS, 1)
    v_first = None
    for l in range(depth):
        mod = mod_all[l]
        q, k, v, rw = _in_proj_call(xa, mod, norm_mix_g[l].reshape(1, D), w_in[l].astype(bf16), cos, sin)
        att = _attn_call(att_sink[l], q, k, v)
        mu = jnp.stack([shift_mu_prev[l], shift_mu_next[l]])
        v0 = vres_v0[l - 1] if l > 0 else jnp.zeros((RW,), f32)
        vec = jnp.stack([k_k[l], k_a[l], r_k[l].reshape(RW), v0,
                         decay_w0[l, 0], decay_w0[l, 1], iclr_a0[l, 0], iclr_a0[l, 1]])
        if l > 0:
            v1 = jnp.zeros((RW, LANES), f32).at[:, :LORA_VRES].set(vres_v1[l - 1]).astype(bf16)
            v2 = jnp.zeros((LANES, RW), f32).at[:LORA_VRES].set(vres_v2[l - 1]).astype(bf16)
        else:
            v1 = v2 = None
        r_, v_, kk, bv, g, kd, lw, bd = _feat_call(
            rw, v_first, mu, vec, _block_diag2(decay_w2[l]).astype(bf16), _block_diag2(iclr_a2[l]).astype(bf16),
            gate_g2[l].astype(bf16), v1, v2)
        if l == 0:
            v_first = v_
        yf, yb = _scan_call(r_, v_, kk, kd, lw, bd)
        ln = jnp.stack([ln_x_w[l], ln_x_b[l]])
        xa, x_local, lpos, gates, cnt = _mix_call(xa, att, yf, yb, bv, g, mod, ln, w_out[l].astype(bf16),
                                                  norm_ffn_g[l].reshape(1, D), w_router, b_router)
        ntok = nbatch * tb
        nrows = _moe_rows(ntok)
        xa = xa.reshape(nbatch, tb, D)
        cnt = cnt[:, 0].astype(i32).reshape(ntok // TM, N_EXPERTS)
        base, seg, lstart, pad_ends, blk_expert, n_used = _segment_plan(cnt, nrows // MOE_BLK)
        x_sorted = _dispatch_call(base, seg, lstart, pad_ends, x_local, nrows)
        y_sorted = _ffn_call(blk_expert, n_used, x_sorted, expert_w_gate, expert_w_up, expert_w_down, l)
        xa = _combine_call(base, seg, lstart, xa, lpos, gates, mod, y_sorted,
                           final_norm_g.reshape(1, D) if l == depth - 1 else None)
    return xa
```

```python
import functools
import math

import jax
import jax.numpy as jnp
from jax import lax
from jax.experimental import pallas as pl
from jax.experimental.pallas import tpu as pltpu

f32 = jnp.float32
bf16 = jnp.bfloat16
i32 = jnp.int32
u32 = jnp.uint32

D = 1024
SEQ = 4096
CTX = 256
TB = CTX + SEQ
GRID_W = 64
HEAD_DIM = 64
ATT_WIDTH = 512
ATT_HEADS = 8
KV_HEADS = 2
ATT_GROUP = ATT_HEADS // KV_HEADS
KV_WIDTH = KV_HEADS * HEAD_DIM
RW = 512
LORA_DECAY = 64
LORA_ICLR = 64
LORA_VRES = 32
LORA_GATE = 128
RWKV_COLS = 3 * RW + 2 * (LORA_DECAY + LORA_ICLR) + LORA_GATE
ATT_COLS = ATT_WIDTH + 2 * KV_WIDTH
IN_COLS = ATT_COLS + RWKV_COLS
N_EXPERTS = 16
N_GROUPS = 4
EXPERTS_PER_GROUP = 4
TOP_K = 2
MOE_BLK = 512
NORM_EPS = 1e-6
GN_EPS = 64e-5
NEG_INF = -1e30
ATT_SCALE = HEAD_DIM ** -0.5
ROPE_BASE = 10000.0
ROPE_FREQS = HEAD_DIM // 4

LANES = 128
SUBLANES = 8
TM = 256
QB = 128
CH = 64
HG = 4
GW = HG * HEAD_DIM
SCAN_BATCH = 4
MIX_TILES = 2
SEG_ALIGN = SUBLANES
LROWS = -(-(TOP_K * TM + N_EXPERTS * SEG_ALIGN) // LANES) * LANES
VMEM_LIMIT = 48 * 1024 * 1024


def _cparams(sem):
    return pltpu.CompilerParams(dimension_semantics=sem, vmem_limit_bytes=VMEM_LIMIT)


def _sigmoid(x):
    return 0.5 * jnp.tanh(0.5 * x) + 0.5


def _div_pow2(x, n):
    assert n & (n - 1) == 0
    return lax.shift_right_logical(x, n.bit_length() - 1)


def _mod_pow2(x, n):
    assert n & (n - 1) == 0
    return lax.bitwise_and(x, n - 1)


def _round_up_pow2(x, n):
    assert n & (n - 1) == 0
    return lax.bitwise_and(x + (n - 1), ~(n - 1))


def _dot(a, b):
    return jnp.dot(a, b, preferred_element_type=f32)


def _dot_nt(a, b):
    return lax.dot_general(a, b, (((1,), (1,)), ((), ())), preferred_element_type=f32)


def _dot_tn(a, b):
    return lax.dot_general(a, b, (((0,), (0,)), ((), ())), preferred_element_type=f32)


def _split2(x):
    hi = x.astype(bf16)
    lo = (x - hi.astype(f32)).astype(bf16)
    return hi, lo


def _rmsnorm(x, g):
    ms = jnp.mean(x * x, axis=-1, keepdims=True)
    return x * lax.rsqrt(ms + NORM_EPS) * g


def _head_ones():
    r = _div_pow2(lax.broadcasted_iota(i32, (RW, RW), 0), HEAD_DIM)
    c = _div_pow2(lax.broadcasted_iota(i32, (RW, RW), 1), HEAD_DIM)
    return (r == c).astype(bf16)


def _mod_kernel(c_ref, w_ref, b_ref, o_ref):
    c = c_ref[...]
    s = (c * _sigmoid(c)).astype(bf16)
    o_ref[0] = _dot(s, w_ref[0].astype(bf16)) + b_ref[0]


def _mod_call(cond, w_mod, b_mod):
    nb = cond.shape[0]
    depth = w_mod.shape[0]
    tn = 1024
    return pl.pallas_call(
        _mod_kernel,
        grid=(depth, 6 * D // tn),
        in_specs=[
            pl.BlockSpec((nb, D), lambda l, j: (0, 0)),
            pl.BlockSpec((1, D, tn), lambda l, j: (l, 0, j)),
            pl.BlockSpec((1, 1, tn), lambda l, j: (l, 0, j)),
        ],
        out_specs=pl.BlockSpec((1, nb, tn), lambda l, j: (l, 0, j)),
        out_shape=jax.ShapeDtypeStruct((depth, nb, 6 * D), f32),
        compiler_params=_cparams(("arbitrary", "arbitrary")),
        name="mod",
    )(cond, w_mod, b_mod.reshape(depth, 1, 6 * D))


def _mod_index(nbatch, ctx_tiles):
    return lambda b, i: (jnp.where(i < ctx_tiles, nbatch, b), 0, 0)


def _in_proj_kernel(x_ref, mod_ref, g_ref, w_ref, cos_ref, sin_ref, q_ref, k_ref, v_ref, rw_ref):
    x = x_ref[0]
    tm = x.shape[0]
    h = _rmsnorm(x, g_ref[...])
    sh = mod_ref[0, :, 0:D]
    sc = mod_ref[0, :, D:2 * D]
    h = (h * (1.0 + sc) + sh).astype(bf16)
    p = _dot(h, w_ref[...])
    cos = cos_ref[...]
    sin = sin_ref[...]
    lane = lax.broadcasted_iota(i32, (tm, LANES), 1)
    first_half = _mod_pow2(lane, 2 * ROPE_FREQS) < ROPE_FREQS

    def rope(t):
        rot = jnp.where(first_half, -pltpu.roll(t, LANES - ROPE_FREQS, 1), pltpu.roll(t, ROPE_FREQS, 1))
        return t * cos + rot * sin

    for j in range(ATT_WIDTH // LANES):
        t = (rope(p[:, j * LANES:(j + 1) * LANES]) * ATT_SCALE).astype(bf16)
        q_ref[0, 2 * j] = t[:, :HEAD_DIM]
        q_ref[0, 2 * j + 1] = t[:, HEAD_DIM:]
    kt = rope(p[:, ATT_WIDTH:ATT_WIDTH + KV_WIDTH]).astype(bf16)
    vt = p[:, ATT_WIDTH + KV_WIDTH:ATT_COLS].astype(bf16)
    for hh in range(KV_HEADS):
        k_ref[0, hh] = kt[:, hh * HEAD_DIM:(hh + 1) * HEAD_DIM]
        v_ref[0, hh] = vt[:, hh * HEAD_DIM:(hh + 1) * HEAD_DIM]
    rw_ref[0] = p[:, ATT_COLS:]


def _in_proj_call(x, mod, g, w_in, cos, sin):
    nbatch, tb, _ = x.shape
    nt = tb // TM
    return pl.pallas_call(
        _in_proj_kernel,
        grid=(nbatch, nt),
        in_specs=[
            pl.BlockSpec((1, TM, D), lambda b, i: (b, i, 0)),
            pl.BlockSpec((1, 1, 6 * D), _mod_index(nbatch, CTX // TM)),
            pl.BlockSpec((1, D), lambda b, i: (0, 0)),
            pl.BlockSpec((D, IN_COLS), lambda b, i: (0, 0)),
            pl.BlockSpec((TM, LANES), lambda b, i: (i, 0)),
            pl.BlockSpec((TM, LANES), lambda b, i: (i, 0)),
        ],
        out_specs=[
            pl.BlockSpec((1, ATT_HEADS, TM, HEAD_DIM), lambda b, i: (b, 0, i, 0)),
            pl.BlockSpec((1, KV_HEADS, TM, HEAD_DIM), lambda b, i: (b, 0, i, 0)),
            pl.BlockSpec((1, KV_HEADS, TM, HEAD_DIM), lambda b, i: (b, 0, i, 0)),
            pl.BlockSpec((1, TM, RWKV_COLS), lambda b, i: (b, i, 0)),
        ],
        out_shape=[
            jax.ShapeDtypeStruct((nbatch, ATT_HEADS, tb, HEAD_DIM), bf16),
            jax.ShapeDtypeStruct((nbatch, KV_HEADS, tb, HEAD_DIM), bf16),
            jax.ShapeDtypeStruct((nbatch, KV_HEADS, tb, HEAD_DIM), bf16),
            jax.ShapeDtypeStruct((nbatch, tb, RWKV_COLS), f32),
        ],
        compiler_params=_cparams(("parallel", "arbitrary")),
        name="in_proj",
    )(x, mod, g, w_in, cos, sin)


def _attn_kernel(sink_ref, q_ref, kp_ref, km_ref, kn_ref, vp_ref, vm_ref, vn_ref, kx_ref, vx_ref, o_ref,
                 *, npairs, ctx_pairs):
    j = pl.program_id(1)
    is_lat = j >= ctx_pairs
    before_ok = jnp.logical_and(is_lat, j - 1 >= ctx_pairs)
    after_ok = jnp.logical_and(is_lat, j + 1 <= npairs - 1)
    rows = ATT_GROUP * QB
    qi = _mod_pow2(lax.broadcasted_iota(i32, (rows, QB), 0), QB)
    kj = lax.broadcasted_iota(i32, (rows, QB), 1)
    band_p = kj >= qi
    band_n = kj <= qi
    row_head = _div_pow2(lax.broadcasted_iota(i32, (rows, 1), 0), QB)
    half = (slice(0, QB), slice(QB, 2 * QB))
    chains = [(c, h) for c in range(2) for h in range(KV_HEADS)]
    qh, kprev, kcur, knext, vprev, vcur, vnext, ok_p, ok_n = [], [], [], [], [], [], [], [], []
    for c, h in chains:
        qh.append(q_ref[0, ATT_GROUP * h:ATT_GROUP * (h + 1), half[c], :].reshape(rows, HEAD_DIM))
        kcur.append(km_ref[0, h, half[c], :])
        vcur.append(vm_ref[0, h, half[c], :])
        if c == 0:
            kprev.append(kp_ref[0, h]); vprev.append(vp_ref[0, h]); ok_p.append(before_ok)
            knext.append(km_ref[0, h, half[1], :]); vnext.append(vm_ref[0, h, half[1], :]); ok_n.append(is_lat)
        else:
            kprev.append(km_ref[0, h, half[0], :]); vprev.append(vm_ref[0, h, half[0], :]); ok_p.append(is_lat)
            knext.append(kn_ref[0, h]); vnext.append(vn_ref[0, h]); ok_n.append(after_ok)
    n = range(len(chains))
    s_p = [jnp.where(jnp.logical_and(band_p, ok_p[i]), _dot_nt(qh[i], kprev[i]), NEG_INF) for i in n]
    s_c = [jnp.where(is_lat, _dot_nt(qh[i], kcur[i]), NEG_INF) for i in n]
    s_n = [jnp.where(jnp.logical_and(band_n, ok_n[i]), _dot_nt(qh[i], knext[i]), NEG_INF) for i in n]
    s_x = [_dot_nt(qh[i], kx_ref[0, chains[i][1]]) for i in n]
    sink = []
    for h in range(KV_HEADS):
        sk = jnp.zeros((rows, 1), f32)
        for g in range(ATT_GROUP):
            sk = jnp.where(row_head == g, sink_ref[ATT_GROUP * h + g], sk)
        sink.append(sk)
    sink = [sink[h] for _, h in chains]
    m = [jnp.maximum(jnp.max(jnp.maximum(jnp.maximum(s_p[i], s_c[i]),
                                         jnp.maximum(jnp.maximum(s_n[i], s_x[i][:, :QB]), s_x[i][:, QB:])),
                             axis=-1, keepdims=True), sink[i]) for i in n]
    e_p = [jnp.exp(s_p[i] - m[i]) for i in n]
    e_c = [jnp.exp(s_c[i] - m[i]) for i in n]
    e_n = [jnp.exp(s_n[i] - m[i]) for i in n]
    e_x = [jnp.exp(s_x[i] - m[i]) for i in n]
    den = [jnp.sum((e_p[i] + e_c[i]) + (e_n[i] + e_x[i][:, :QB]) + e_x[i][:, QB:], axis=-1, keepdims=True)
           + jnp.exp(sink[i] - m[i]) for i in n]
    o = [(_dot(e_p[i].astype(bf16), vprev[i]) + _dot(e_c[i].astype(bf16), vcur[i])
          + _dot(e_n[i].astype(bf16), vnext[i]) + _dot(e_x[i].astype(bf16), vx_ref[0, chains[i][1]])) / den[i]
         for i in n]
    for c in range(2):
        o_ref[0, half[c], :] = jnp.concatenate(
            [o[c * KV_HEADS + h][g * QB:(g + 1) * QB] for h in range(KV_HEADS) for g in range(ATT_GROUP)],
            axis=1).astype(bf16)


def _attn_call(sink, q, k, v):
    nbatch, _, tb, _ = q.shape
    nblk = tb // QB
    npairs = nblk // 2
    assert CTX == 2 * QB and nblk % 2 == 0
    kv_blk = (1, KV_HEADS, QB, HEAD_DIM)
    pair_blk = (1, KV_HEADS, 2 * QB, HEAD_DIM)
    before_map = lambda b, j: (b, 0, jnp.maximum(2 * j - 1, 0), 0)
    pair_map = lambda b, j: (b, 0, j, 0)
    after_map = lambda b, j: (b, 0, jnp.minimum(2 * j + 2, nblk - 1), 0)
    ctx_spec = pl.BlockSpec((1, KV_HEADS, CTX, HEAD_DIM), lambda b, j: (b, 0, 0, 0))
    return pl.pallas_call(
        functools.partial(_attn_kernel, npairs=npairs, ctx_pairs=CTX // (2 * QB)),
        grid=(nbatch, npairs),
        in_specs=[
            pl.BlockSpec(memory_space=pltpu.SMEM),
            pl.BlockSpec((1, ATT_HEADS, 2 * QB, HEAD_DIM), pair_map),
            pl.BlockSpec(kv_blk, before_map), pl.BlockSpec(pair_blk, pair_map), pl.BlockSpec(kv_blk, after_map),
            pl.BlockSpec(kv_blk, before_map), pl.BlockSpec(pair_blk, pair_map), pl.BlockSpec(kv_blk, after_map),
            ctx_spec, ctx_spec,
        ],
        out_specs=pl.BlockSpec((1, 2 * QB, ATT_WIDTH), lambda b, j: (b, j, 0)),
        out_shape=jax.ShapeDtypeStruct((nbatch, tb, ATT_WIDTH), bf16),
        compiler_params=_cparams(("parallel", "arbitrary")),
        name="attention",
    )(sink, q, k, k, k, v, v, v, k, v)


def _feat_kernel(*refs, nt, ctx_tiles, has_vres):
    if has_vres:
        (rw_ref, hp_ref, hn_ref, vf_ref, mu_ref, vec_ref, w2_ref, a2_ref, g2_ref, v1_ref, v2_ref,
         r_ref, v_ref, kk_ref, bv_ref, g_ref, kd_ref, lw_ref, bd_ref) = refs
    else:
        (rw_ref, hp_ref, hn_ref, mu_ref, vec_ref, w2_ref, a2_ref, g2_ref,
         r_ref, v_ref, kk_ref, bv_ref, g_ref, kd_ref, lw_ref, bd_ref) = refs
    i = pl.program_id(1)
    u0 = rw_ref[0]
    tm = u0.shape[0]
    prev_zero = jnp.logical_or(i == 0, i == ctx_tiles)
    next_zero = jnp.logical_or(i == ctx_tiles - 1, i == nt - 1)
    halo_p = jnp.where(prev_zero, 0.0, hp_ref[0, SUBLANES - 1:SUBLANES, :])
    halo_n = jnp.where(next_zero, 0.0, hn_ref[0, 0:1, :])
    row = lax.broadcasted_iota(i32, (tm, 1), 0)
    prev = jnp.where(row == 0, halo_p, pltpu.roll(u0, 1, 0))
    nxt = jnp.where(row == tm - 1, halo_n, pltpu.roll(u0, tm - 1, 0))
    mu_p = mu_ref[0:1, :]
    mu_n = mu_ref[1:2, :]
    u = u0 + mu_p * (prev - u0) + mu_n * (nxt - u0)

    r = u[:, 0:RW]
    k = u[:, RW:2 * RW]
    v = u[:, 2 * RW:3 * RW]
    wd = u[:, 3 * RW:3 * RW + 2 * LORA_DECAY]
    ad = u[:, 3 * RW + 2 * LORA_DECAY:3 * RW + 2 * (LORA_DECAY + LORA_ICLR)]
    gd = u[:, 3 * RW + 2 * (LORA_DECAY + LORA_ICLR):]
    k_k = vec_ref[0:1, :]
    k_a = vec_ref[1:2, :]
    r_k = vec_ref[2:3, :]
    ones = _head_ones()

    if has_vres:
        lo = _dot(v.astype(bf16), v1_ref[...])
        gate = _sigmoid(vec_ref[3:4, :] + _dot(lo.astype(bf16), v2_ref[...]))
        v = v + (vf_ref[0] - v) * gate
    decay_in = _dot(jnp.tanh(wd).astype(bf16), w2_ref[...])
    a_in = _dot(ad.astype(bf16), a2_ref[...])
    kk = k * k_k
    n2 = _dot((kk * kk).astype(bf16), ones)
    kk = kk * lax.rsqrt(jnp.maximum(n2, 1e-24))
    g = _dot(_sigmoid(gd).astype(bf16), g2_ref[...])
    ksum = jnp.zeros_like(k)
    for d in range(2):
        w0 = vec_ref[4 + d:5 + d, :]
        a0 = vec_ref[6 + d:7 + d, :]
        lw = -_sigmoid(w0 + decay_in[:, d * RW:(d + 1) * RW]) * math.exp(-0.5)
        a = _sigmoid(a0 + a_in[:, d * RW:(d + 1) * RW])
        kd = k * (1.0 + (a - 1.0) * k_a)
        ksum = ksum + kd
        kd_ref[d, 0] = kd.astype(bf16)
        lw_ref[d, 0] = lw
        bd_ref[d, 0] = (kk * a).astype(bf16)
    bonus = _dot((r * ksum * r_k).astype(bf16), ones)
    r_ref[0] = r.astype(bf16)
    v_ref[0] = v
    kk_ref[0] = kk.astype(bf16)
    bv_ref[0] = bonus * v
    g_ref[0] = g.astype(bf16)


def _feat_call(rw, v_first, mu, vec, w2bd, a2bd, g2, v1, v2):
    nbatch, tb, _ = rw.shape
    nt = tb // TM
    has_vres = v_first is not None
    sub = TM // SUBLANES
    tile = lambda w: pl.BlockSpec((1, TM, w), lambda b, i: (b, i, 0))
    full = lambda a: pl.BlockSpec(a.shape, lambda b, i: (0,) * a.ndim)
    halo = (1, SUBLANES, RWKV_COLS)
    in_specs = [
        tile(RWKV_COLS),
        pl.BlockSpec(halo, lambda b, i: (b, jnp.maximum(i * sub - 1, 0), 0)),
        pl.BlockSpec(halo, lambda b, i: (b, jnp.minimum((i + 1) * sub, tb // SUBLANES - 1), 0)),
    ]
    args = [rw, rw, rw]
    if has_vres:
        in_specs.append(tile(RW))
        args.append(v_first)
    consts = [mu, vec, w2bd, a2bd, g2] + ([v1, v2] if has_vres else [])
    in_specs += [full(a) for a in consts]
    args += consts
    dir_spec = pl.BlockSpec((2, 1, TM, RW), lambda b, i: (0, b, i, 0))
    tok = lambda dt: jax.ShapeDtypeStruct((nbatch, tb, RW), dt)
    dtok = lambda dt: jax.ShapeDtypeStruct((2, nbatch, tb, RW), dt)
    return pl.pallas_call(
        functools.partial(_feat_kernel, nt=nt, ctx_tiles=CTX // TM, has_vres=has_vres),
        grid=(nbatch, nt),
        in_specs=in_specs,
        out_specs=[tile(RW)] * 5 + [dir_spec] * 3,
        out_shape=[tok(bf16), tok(f32), tok(bf16), tok(f32), tok(bf16), dtok(bf16), dtok(f32), dtok(bf16)],
        compiler_params=_cparams(("parallel", "arbitrary")),
        name="rwkv_features",
    )(*args)


def _block_diag_rows(x, width):
    cb = _div_pow2(lax.broadcasted_iota(i32, x.shape, 1), width)
    return jnp.concatenate([jnp.where(cb == h, x, jnp.zeros_like(x)) for h in range(HG)], axis=0)


def _scan_chunks(probs):
    wide = HG * CH
    row_t = lax.broadcasted_iota(i32, (CH, wide), 0)
    col_t = _mod_pow2(lax.broadcasted_iota(i32, (CH, wide), 1), CH)
    row_g = lax.broadcasted_iota(i32, (CH, GW), 0)
    incl_t = {False: col_t <= row_t, True: col_t >= row_t}
    strict_t = {False: col_t < row_t, True: col_t > row_t}
    eye_t = (row_t == col_t).astype(f32)
    rb = _div_pow2(lax.broadcasted_iota(i32, (GW, GW), 0), HEAD_DIM)
    cb = _div_pow2(lax.broadcasted_iota(i32, (GW, GW), 1), HEAD_DIM)
    n = len(probs)
    rev = [p[7] for p in probs]
    def cumsum_rows(x, reverse):
        s = 1
        while s < CH:
            if reverse:
                x = x + jnp.where(row_g < CH - s, pltpu.roll(x, CH - s, 0), 0.0)
            else:
                x = x + jnp.where(row_g >= s, pltpu.roll(x, s, 0), 0.0)
            s *= 2
        return x

    gam = [cumsum_rows(probs[i][5], rev[i]) for i in range(n)]
    ar, bk, k_t, b_t = [], [], [], []
    for i, (s_prev, r, v, kk, k, lw, b, _) in enumerate(probs):
        e_neg = jnp.exp(-gam[i])
        a_s = (-kk * jnp.exp(gam[i] - lw)).astype(bf16)
        r_s = (r * jnp.exp(gam[i])).astype(bf16)
        b_t.append(b * e_neg)
        k_t.append(k * e_neg)
        ar.append(jnp.concatenate([a_s, r_s], axis=0))
        bk.append(jnp.concatenate([_block_diag_rows(b_t[i].astype(bf16), HEAD_DIM),
                                   _block_diag_rows(k_t[i].astype(bf16), HEAD_DIM)], axis=0))
    gram = [_dot_nt(ar[i], bk[i]) for i in range(n)]
    ars = [_dot_nt(ar[i], probs[i][0].astype(bf16)) for i in range(n)]
    v_bd = [_block_diag_rows(probs[i][2].astype(bf16), HEAD_DIM) for i in range(n)]
    p0 = [jnp.where(strict_t[rev[i]], gram[i][:CH, :wide], 0.0).astype(bf16) for i in range(n)]
    lq = [jnp.concatenate([jnp.where(strict_t[rev[i]], gram[i][:CH, wide:], 0.0),
                           jnp.where(incl_t[rev[i]], gram[i][CH:, wide:], 0.0)], axis=0).astype(bf16)
          for i in range(n)]
    lqv = [_dot(lq[i], v_bd[i]) for i in range(n)]
    rhs = [ars[i][:CH] + lqv[i][:CH] for i in range(n)]
    t = [eye_t + p0[i].astype(f32) for i in range(n)]
    p = [_dot(p0[i], _block_diag_rows(p0[i], CH)).astype(bf16) for i in range(n)]
    m = 4
    while m < CH:
        tp = [_dot(jnp.concatenate([t[i].astype(bf16), p[i]], axis=0), _block_diag_rows(p[i], CH))
              for i in range(n)]
        t = [t[i] + tp[i][:CH] for i in range(n)]
        p = [tp[i][CH:].astype(bf16) for i in range(n)]
        m *= 2
    t = [t[i] + _dot(t[i].astype(bf16), _block_diag_rows(p[i], CH)) for i in range(n)]
    u = [_dot(t[i].astype(bf16), _block_diag_rows(rhs[i].astype(bf16), HEAD_DIM)) for i in range(n)]
    out = []
    for i in range(n):
        s_prev, v, lw = probs[i][0], probs[i][2], probs[i][5]
        q_b = jnp.where(incl_t[rev[i]], gram[i][CH:, :wide], 0.0).astype(bf16)
        u_bd = _block_diag_rows(u[i].astype(bf16), HEAD_DIM)
        y = ars[i][CH:] + lqv[i][CH:] + _dot(q_b, u_bd)
        eg = jnp.exp(jnp.sum(lw, axis=0, keepdims=True))
        vu = jnp.concatenate([v, u[i]], axis=0).astype(bf16)
        kb = jnp.concatenate([k_t[i] * eg, b_t[i] * eg], axis=0).astype(bf16)
        s_add = _dot_tn(vu, kb)
        out.append((s_prev * eg + jnp.where(rb == cb, s_add, 0.0), y))
    return out


def _scan_kernel(rf_ref, vf_ref, kkf_ref, kf_ref, lwf_ref, bf_ref,
                 rb_ref, vb_ref, kkb_ref, kb_ref, lwb_ref, bb_ref,
                 yf_ref, yb_ref, s_ref):
    @pl.when(pl.program_id(1) == 0)
    def _():
        s_ref[...] = jnp.zeros_like(s_ref)

    dirs = ((rf_ref, vf_ref, kkf_ref, kf_ref, lwf_ref, bf_ref, yf_ref),
            (rb_ref, vb_ref, kkb_ref, kb_ref, lwb_ref, bb_ref, yb_ref))
    probs, dest = [], []
    for bi in range(SCAN_BATCH):
        for d, (r_ref, v_ref, kk_ref, k_ref, lw_ref, b_ref, y_ref) in enumerate(dirs):
            for g in range(RW // GW):
                sl = slice(g * GW, (g + 1) * GW)
                probs.append((s_ref[bi, d, g], r_ref[bi, :, sl], v_ref[bi, :, sl], kk_ref[bi, :, sl],
                              k_ref[0, bi, :, sl], lw_ref[0, bi, :, sl], b_ref[0, bi, :, sl], d == 1))
                dest.append((bi, d, g, y_ref, sl))
    for (bi, d, g, y_ref, sl), (s_new, y) in zip(dest, _scan_chunks(probs)):
        s_ref[bi, d, g] = s_new
        y_ref[bi, :, sl] = y


def _scan_call(r, v, kk, kd, lw, bd):
    nbatch, tb, _ = r.shape
    nc = tb // CH
    cc = CTX // CH
    sb = SCAN_BATCH
    rev = lambda j: jnp.where(j < cc, cc - 1 - j, nc - 1 + cc - j)
    tok_f = pl.BlockSpec((sb, CH, RW), lambda b, j: (b, j, 0))
    tok_b = pl.BlockSpec((sb, CH, RW), lambda b, j: (b, rev(j), 0))
    dir_f = pl.BlockSpec((1, sb, CH, RW), lambda b, j: (0, b, j, 0))
    dir_b = pl.BlockSpec((1, sb, CH, RW), lambda b, j: (1, b, rev(j), 0))
    out = jax.ShapeDtypeStruct((nbatch, tb, RW), f32)
    return pl.pallas_call(
        _scan_kernel,
        grid=(nbatch // sb, nc),
        in_specs=[tok_f, tok_f, tok_f, dir_f, dir_f, dir_f, tok_b, tok_b, tok_b, dir_b, dir_b, dir_b],
        out_specs=[tok_f, tok_b],
        out_shape=[out, out],
        scratch_shapes=[pltpu.VMEM((sb, 2, RW // GW, GW, GW), f32)],
        compiler_params=_cparams(("parallel", "arbitrary")),
        name="rwkv_scan",
    )(r, v, kk, kd, lw, bd, r, v, kk, kd, lw, bd)


def _pack_bf16_pairs(x):
    bits = pltpu.bitcast(x, u32)
    half = x.shape[1] // 2
    return bits[:, :half] | lax.shift_right_logical(bits[:, half:], jnp.uint32(16))


def _unpack_bf16_pairs(p):
    hi = pltpu.bitcast(p & jnp.uint32(0xFFFF0000), f32)
    lo = pltpu.bitcast(lax.shift_left(p, jnp.uint32(16)), f32)
    return jnp.concatenate([hi, lo], axis=1).astype(bf16)


def _mix_kernel(x_ref, att_ref, yf_ref, yb_ref, bv_ref, g_ref, mod0_ref, mod1_ref, ln_ref, wo_ref, gf_ref,
                wr_ref, rb_ref, xo_ref, xl_ref, lp_ref, gt_ref, cnt_ref):
    ones = _head_ones()
    inv = 1.0 / HEAD_DIM
    tiles = range(MIX_TILES)
    rows = [slice(s * TM, (s + 1) * TM) for s in tiles]
    mods = (mod0_ref, mod1_ref)
    y = [yf_ref[rows[s], :] + yb_ref[rows[s], :] for s in tiles]
    mu = [_dot(y[s].astype(bf16), ones) * inv for s in tiles]
    dlt = [y[s] - mu[s] for s in tiles]
    var = [_dot((dlt[s] * dlt[s]).astype(bf16), ones) * inv for s in tiles]
    rwk = [((dlt[s] * lax.rsqrt(var[s] + GN_EPS) * ln_ref[0:1, :] + ln_ref[1:2, :] + bv_ref[rows[s], :])
            * g_ref[rows[s], :]).astype(bf16) for s in tiles]
    mix = [_dot(att_ref[rows[s], :], wo_ref[0:ATT_WIDTH, :]) + _dot(rwk[s], wo_ref[ATT_WIDTH:, :]) for s in tiles]
    xn = [x_ref[rows[s], :] + mods[s][0, :, 2 * D:3 * D] * mix[s] for s in tiles]
    for s in tiles:
        xo_ref[rows[s], :] = xn[s]
    hf = [_rmsnorm(xn[s], gf_ref[...]) * (1.0 + mods[s][0, :, 4 * D:5 * D]) + mods[s][0, :, 3 * D:4 * D]
          for s in tiles]
    split = [_split2(hf[s]) for s in tiles]
    by_hi = [_dot_nt(wr_ref[...], split[s][0]) for s in tiles]
    logits = [by_hi[s][:N_EXPERTS] + by_hi[s][N_EXPERTS:] + _dot_nt(wr_ref[0:N_EXPERTS, :], split[s][1])
              for s in tiles]
    sorted_rows, lpos, gates, cnt = _route_and_sort(hf, [_sigmoid(lg) for lg in logits], rb_ref[...])
    for s in tiles:
        xl_ref[s * LROWS:(s + 1) * LROWS, :] = sorted_rows[s]
        lp_ref[:, rows[s]] = lpos[s]
        gt_ref[:, rows[s]] = gates[s]
        cnt_ref[s * N_EXPERTS:(s + 1) * N_EXPERTS, :] = cnt[s]


def _route_and_sort(hf, scores, bias):
    tiles = range(len(hf))
    tm = hf[0].shape[0]
    routed = [_route_rows(scores[s], bias) for s in tiles]
    eio = lax.broadcasted_iota(i32, (N_EXPERTS, tm), 0)
    upper = (lax.broadcasted_iota(i32, (tm, tm), 0) < lax.broadcasted_iota(i32, (tm, tm), 1)).astype(bf16)
    lower = (lax.broadcasted_iota(i32, (N_EXPERTS, N_EXPERTS), 0)
             > lax.broadcasted_iota(i32, (N_EXPERTS, N_EXPERTS), 1)).astype(bf16)
    jj = lax.broadcasted_iota(i32, (LROWS, tm), 0)
    oh0 = [(eio == routed[s][0]).astype(f32) for s in tiles]
    oh1 = [(eio == routed[s][1]).astype(f32) for s in tiles]
    both = [oh0[s] + oh1[s] for s in tiles]
    before = [_dot(both[s].astype(bf16), upper) for s in tiles]
    cnt = [jnp.sum(both[s], axis=1, keepdims=True) for s in tiles]
    seg = [_round_up_pow2(cnt[s].astype(i32), SEG_ALIGN).astype(f32) for s in tiles]
    seg_start = [_dot(lower, jnp.broadcast_to(seg[s], (N_EXPERTS, tm)).astype(bf16)) for s in tiles]
    pos = [seg_start[s] + before[s] for s in tiles]
    lp0 = [jnp.sum(oh0[s] * pos[s], axis=0, keepdims=True).astype(i32) for s in tiles]
    lp1 = [jnp.sum(oh1[s] * pos[s], axis=0, keepdims=True).astype(i32) for s in tiles]
    sel_t = [jnp.logical_or(jj == lp0[s], jj == lp1[s]).astype(bf16) for s in tiles]
    sorted_rows = [_pack_bf16_pairs(_dot(sel_t[s], hf[s].astype(bf16))) for s in tiles]
    lpos = [jnp.concatenate([lp0[s], lp1[s]], axis=0) for s in tiles]
    gates = [jnp.concatenate([routed[s][2], routed[s][3]], axis=0) for s in tiles]
    cnt_out = [jnp.broadcast_to(cnt[s], (N_EXPERTS, LANES)) for s in tiles]
    return sorted_rows, lpos, gates, cnt_out


def _route_rows(scores, bias):
    biased = scores + bias
    row = lambda a, e: a[e:e + 1, :]
    best = None
    for gi in range(N_GROUPS):
        m = [row(biased, gi * EXPERTS_PER_GROUP + j) for j in range(EXPERTS_PER_GROUP)]
        gs = None
        for a in range(EXPERTS_PER_GROUP):
            for b in range(a + 1, EXPERTS_PER_GROUP):
                pair = m[a] + m[b]
                gs = pair if gs is None else jnp.maximum(gs, pair)
        if best is None:
            best, g_idx = gs, jnp.zeros(gs.shape, i32)
        else:
            better = gs > best
            g_idx = jnp.where(better, gi, g_idx)
            best = jnp.where(better, gs, best)

    def pick(a, j):
        out = row(a, j)
        for gi in range(1, N_GROUPS):
            out = jnp.where(g_idx == gi, row(a, gi * EXPERTS_PER_GROUP + j), out)
        return out

    vb = [pick(biased, j) for j in range(EXPERTS_PER_GROUP)]
    vs = [pick(scores, j) for j in range(EXPERTS_PER_GROUP)]

    def argmax_first(vals):
        bv, bi = vals[0], jnp.zeros(vals[0].shape, i32)
        for j in range(1, len(vals)):
            better = vals[j] > bv
            bi = jnp.where(better, j, bi)
            bv = jnp.where(better, vals[j], bv)
        return bi

    i1 = argmax_first(vb)
    i2 = argmax_first([jnp.where(i1 == j, -jnp.inf, vb[j]) for j in range(EXPERTS_PER_GROUP)])
    sel = lambda idx: sum(jnp.where(idx == j, vs[j], 0.0) for j in range(EXPERTS_PER_GROUP))
    s1, s2 = sel(i1), sel(i2)
    tot = s1 + s2
    base = g_idx * EXPERTS_PER_GROUP
    return base + i1, base + i2, s1 / tot, s2 / tot


def _mix_call(x, att, yf, yb, bv, g, mod, ln, w_out, g_ffn, w_router, b_router):
    nbatch, tb, _ = x.shape
    tiles_per_batch = tb // TM
    ntiles = nbatch * tiles_per_batch
    assert ntiles % MIX_TILES == 0 and MIX_TILES == 2
    ntok = nbatch * tb
    flat = lambda a: a.reshape(ntok, a.shape[-1])
    rows = lambda w: pl.BlockSpec((MIX_TILES * TM, w), lambda p: (p, 0))
    full = lambda a: pl.BlockSpec(a.shape, lambda p: (0,) * a.ndim)
    ctx_tiles = CTX // TM

    def mod_spec(s):
        def index(p):
            tile = MIX_TILES * p + s
            return (jnp.where(tile % tiles_per_batch < ctx_tiles, nbatch, tile // tiles_per_batch), 0, 0)
        return pl.BlockSpec((1, 1, 6 * D), index)

    route = pl.BlockSpec((TOP_K, MIX_TILES * TM), lambda p: (0, p))
    return pl.pallas_call(
        _mix_kernel,
        grid=(ntiles // MIX_TILES,),
        in_specs=[rows(D), rows(ATT_WIDTH), rows(RW), rows(RW), rows(RW), rows(RW), mod_spec(0), mod_spec(1),
                  full(ln), full(w_out), full(g_ffn), full(w_router), full(b_router)],
        out_specs=[rows(D), pl.BlockSpec((MIX_TILES * LROWS, D // 2), lambda p: (p, 0)), route, route,
                   pl.BlockSpec((MIX_TILES * N_EXPERTS, LANES), lambda p: (p, 0))],
        out_shape=[jax.ShapeDtypeStruct((ntok, D), f32),
                   jax.ShapeDtypeStruct((ntiles * LROWS, D // 2), u32),
                   jax.ShapeDtypeStruct((TOP_K, ntok), i32),
                   jax.ShapeDtypeStruct((TOP_K, ntok), f32),
                   jax.ShapeDtypeStruct((ntiles * N_EXPERTS, LANES), f32)],
        compiler_params=_cparams(("arbitrary",)),
        name="mix_out",
    )(flat(x), flat(att), flat(yf), flat(yb), flat(bv), flat(g), mod, mod, ln, w_out, g_ffn, w_router, b_router)


def _segment_copies(fn, tile, base_ref, seg_ref, ls_ref, src, dst, sem, src_is_global):
    for e in range(N_EXPERTS):
        idx = tile * N_EXPERTS + e
        seg = seg_ref[idx]
        g0 = base_ref[idx]
        l0 = ls_ref[idx]
        size = TM
        while size >= SEG_ALIGN:
            done = lax.bitwise_and(seg, ~(2 * size - 1))

            @pl.when(lax.bitwise_and(seg, size) != 0)
            def _():
                g_rows = pl.ds(pl.multiple_of(g0 + done, SEG_ALIGN), size)
                l_rows = pl.ds(pl.multiple_of(l0 + done, SEG_ALIGN), size)
                s_rows, d_rows = (g_rows, l_rows) if src_is_global else (l_rows, g_rows)
                fn(pltpu.make_async_copy(src.at[s_rows, :], dst.at[d_rows, :], sem))

            size //= 2


def _segment_waits(tile, seg_ref, src, dst, sem):
    total = seg_ref[tile * N_EXPERTS]
    for e in range(1, N_EXPERTS):
        total = total + seg_ref[tile * N_EXPERTS + e]
    size = 2 * TM
    assert LROWS < 2 * size
    while size >= SEG_ALIGN:
        @pl.when(lax.bitwise_and(total, size) != 0)
        def _():
            pltpu.make_async_copy(src.at[pl.ds(0, size), :], dst.at[pl.ds(0, size), :], sem).wait()

        size //= 2


def _dispatch_kernel(base_ref, seg_ref, ls_ref, ends_ref, xl_ref, xs_ref, zeros, stage, sem, in_sem, out_sem):
    tile = pl.program_id(0)
    ntiles = pl.num_programs(0)

    def load(t):
        return pltpu.make_async_copy(xl_ref.at[pl.ds(pl.multiple_of(t * LROWS, SEG_ALIGN), LROWS), :],
                                     stage.at[lax.rem(t, 3)], in_sem.at[lax.rem(t, 3)])

    def zero_tails(fn):
        for e in range(N_EXPERTS):
            end = ends_ref[e]
            start = ends_ref[e - 1] if e else 0

            @pl.when(end > start)
            def _():
                rows = pl.ds(pl.multiple_of(end - MOE_BLK, MOE_BLK), MOE_BLK)
                fn(pltpu.make_async_copy(zeros, xs_ref.at[rows, :], sem))

    def zero_unused(fn):
        def body(blk, carry):
            fn(pltpu.make_async_copy(zeros, xs_ref.at[pl.ds(pl.multiple_of(blk * MOE_BLK, MOE_BLK), MOE_BLK), :],
                                     sem))
            return carry
        lax.fori_loop(ends_ref[N_EXPERTS - 1] // MOE_BLK, xs_ref.shape[0] // MOE_BLK, body, 0)

    @pl.when(tile == 0)
    def _():
        load(tile).start()
        zeros[...] = jnp.zeros_like(zeros)
        zero_tails(lambda cp: cp.start())
        zero_unused(lambda cp: cp.start())
        zero_tails(lambda cp: cp.wait())
        zero_unused(lambda cp: cp.wait())

    @pl.when(tile + 1 < ntiles)
    def _():
        load(tile + 1).start()

    load(tile).wait()
    cur = lax.rem(tile, 2)
    _segment_copies(lambda cp: cp.start(), tile, base_ref, seg_ref, ls_ref, stage.at[lax.rem(tile, 3)], xs_ref,
                    out_sem.at[cur], False)

    @pl.when(tile > 0)
    def _():
        _segment_waits(tile - 1, seg_ref, stage.at[0], xs_ref, out_sem.at[1 - cur])

    @pl.when(tile == ntiles - 1)
    def _():
        _segment_waits(tile, seg_ref, stage.at[0], xs_ref, out_sem.at[cur])


def _dispatch_call(base, seg, lstart, pad_ends, x_local, nrows):
    ntiles = x_local.shape[0] // LROWS
    return pl.pallas_call(
        _dispatch_kernel,
        grid_spec=pltpu.PrefetchScalarGridSpec(
            num_scalar_prefetch=4,
            grid=(ntiles,),
            in_specs=[pl.BlockSpec(memory_space=pl.ANY)],
            out_specs=pl.BlockSpec(memory_space=pl.ANY),
            scratch_shapes=[pltpu.VMEM((MOE_BLK, D // 2), u32), pltpu.VMEM((3, LROWS, D // 2), u32),
                            pltpu.SemaphoreType.DMA(()), pltpu.SemaphoreType.DMA((3,)),
                            pltpu.SemaphoreType.DMA((2,))],
        ),
        out_shape=jax.ShapeDtypeStruct((nrows, D // 2), u32),
        compiler_params=_cparams(("arbitrary",)),
        name="moe_dispatch",
    )(base, seg, lstart, pad_ends, x_local)


def _ffn_kernel(be_ref, nu_ref, x_ref, wg_ref, wu_ref, wd_ref, y_ref, wg_bf, wu_bf, wd_bf):
    i = pl.program_id(0)
    used = i < nu_ref[0]
    new_expert = jnp.logical_or(i == 0, be_ref[i] != be_ref[jnp.maximum(i - 1, 0)])

    @pl.when(jnp.logical_and(used, new_expert))
    def _():
        wg_bf[...] = wg_ref[0, 0].astype(bf16)
        wu_bf[...] = wu_ref[0, 0].astype(bf16)
        wd_bf[...] = wd_ref[0, 0].astype(bf16)

    @pl.when(used)
    def _():
        x = _unpack_bf16_pairs(x_ref[...])
        gt = _dot(x, wg_bf[...])
        up = _dot(x, wu_bf[...])
        hid = (gt * _sigmoid(gt) * up).astype(bf16)
        y = _dot(hid, wd_bf[...])
        y_ref[...] = _pack_bf16_pairs(y.astype(bf16).astype(f32))

    @pl.when(i >= nu_ref[0])
    def _():
        y_ref[...] = jnp.zeros_like(y_ref)


def _ffn_call(blk_expert, n_used, x_sorted, wg, wu, wd, layer):
    nrows = x_sorted.shape[0]
    nblk = nrows // MOE_BLK
    wspec = pl.BlockSpec((1, 1, D, D), lambda i, be, nu: (layer, be[i], 0, 0))
    return pl.pallas_call(
        _ffn_kernel,
        grid_spec=pltpu.PrefetchScalarGridSpec(
            num_scalar_prefetch=2,
            grid=(nblk,),
            in_specs=[pl.BlockSpec((MOE_BLK, D // 2), lambda i, be, nu: (jnp.minimum(i, nu[0] - 1), 0)),
                      wspec, wspec, wspec],
            out_specs=pl.BlockSpec((MOE_BLK, D // 2), lambda i, be, nu: (i, 0)),
            scratch_shapes=[pltpu.VMEM((D, D), bf16)] * 3,
        ),
        out_shape=jax.ShapeDtypeStruct((nrows, D // 2), u32),
        compiler_params=_cparams(("arbitrary",)),
        name="moe_ffn",
    )(blk_expert, n_used, x_sorted, wg, wu, wd)


def _combine_kernel(base_ref, seg_ref, ls_ref, x_ref, lp_ref, gt_ref, mod_ref, *rest,
                    tiles_per_batch, first_tile, final):
    g_ref = rest[0] if final else None
    y_ref, o_ref, ybuf, sem = rest[-4:]
    b = pl.program_id(0)
    i = pl.program_id(1)
    nt = pl.num_programs(1)
    step = b * nt + i
    slot = lax.rem(step, 2)
    tile = b * tiles_per_batch + i + first_tile
    next_tile = jnp.where(i + 1 < nt, tile + 1, (b + 1) * tiles_per_batch + first_tile)

    def copies(fn, which_tile, which_slot):
        _segment_copies(fn, which_tile, base_ref, seg_ref, ls_ref, y_ref, ybuf.at[which_slot],
                        sem.at[which_slot], True)

    @pl.when(step == 0)
    def _():
        ybuf[...] = jnp.zeros_like(ybuf)
        copies(lambda cp: cp.start(), tile, slot)

    @pl.when(step + 1 < pl.num_programs(0) * nt)
    def _():
        copies(lambda cp: cp.start(), next_tile, 1 - slot)

    _segment_waits(tile, seg_ref, y_ref, ybuf.at[slot], sem.at[slot])
    y_loc = _unpack_bf16_pairs(ybuf[slot])
    tm = x_ref.shape[1]
    jj = lax.broadcasted_iota(i32, (LROWS, tm), 0)
    gmat = (jnp.where(jj == lp_ref[0:1, :], gt_ref[0:1, :], 0.0)
            + jnp.where(jj == lp_ref[1:2, :], gt_ref[1:2, :], 0.0))
    g_hi, g_lo = _split2(gmat)
    moe = _dot_tn(g_hi, y_loc) + _dot_tn(g_lo, y_loc)
    xn = x_ref[0] + mod_ref[0, :, 5 * D:6 * D] * moe
    o_ref[0] = _rmsnorm(xn, g_ref[...]) if final else xn


def _combine_call(base, seg, lstart, x, lpos, gates, mod, y_sorted, final_g):
    nbatch, tb, _ = x.shape
    ctx_tiles = CTX // TM
    tiles_per_batch = tb // TM
    final = final_g is not None
    first_tile = ctx_tiles if final else 0
    nt = tiles_per_batch - first_tile
    tile = pl.BlockSpec((1, TM, D), lambda b, i, *_: (b, i + first_tile, 0))
    route = pl.BlockSpec((TOP_K, TM), lambda b, i, *_: (0, b * tiles_per_batch + i + first_tile))
    mod_map = _mod_index(nbatch, ctx_tiles - first_tile)
    in_specs = [tile, route, route, pl.BlockSpec((1, 1, 6 * D), lambda b, i, *_: mod_map(b, i))]
    args = [x, lpos, gates, mod]
    if final:
        in_specs.append(pl.BlockSpec((1, D), lambda b, i, *_: (0, 0)))
        args.append(final_g)
    return pl.pallas_call(
        functools.partial(_combine_kernel, tiles_per_batch=tiles_per_batch, first_tile=first_tile, final=final),
        grid_spec=pltpu.PrefetchScalarGridSpec(
            num_scalar_prefetch=3,
            grid=(nbatch, nt),
            in_specs=in_specs + [pl.BlockSpec(memory_space=pl.ANY)],
            out_specs=pl.BlockSpec((1, TM, D), lambda b, i, *_: (b, i, 0)),
            scratch_shapes=[pltpu.VMEM((2, LROWS, D // 2), u32), pltpu.SemaphoreType.DMA((2,))],
        ),
        out_shape=jax.ShapeDtypeStruct((nbatch, nt * TM, D), f32),
        compiler_params=_cparams(("arbitrary", "arbitrary")),
        name="ffn_residual_final" if final else "ffn_residual",
    )(base, seg, lstart, *args, y_sorted)


def _moe_rows(ntok):
    ntiles = ntok // TM
    worst = TOP_K * ntok + ntiles * N_EXPERTS * (SEG_ALIGN - 1)
    return (-(-worst // MOE_BLK) + N_EXPERTS) * MOE_BLK


def _segment_plan(cnt, nblk):
    seg = (cnt + SEG_ALIGN - 1) // SEG_ALIGN * SEG_ALIGN
    lstart = jnp.cumsum(seg, axis=1) - seg
    rows = jnp.sum(seg, axis=0)
    padded = (rows + MOE_BLK - 1) // MOE_BLK * MOE_BLK
    pad_ends = jnp.cumsum(padded)
    base = (pad_ends - padded)[None, :] + jnp.cumsum(seg, axis=0) - seg
    blk_start = jnp.arange(nblk, dtype=i32) * MOE_BLK
    blk_expert = jnp.minimum(jnp.sum(pad_ends[None, :] <= blk_start[:, None], axis=1), N_EXPERTS - 1)
    n_used = (pad_ends[-1] // MOE_BLK).reshape(1)
    flat = lambda a: a.reshape(-1).astype(i32)
    return flat(base), flat(seg), flat(lstart), flat(pad_ends), blk_expert.astype(i32), n_used.astype(i32)


def _rope_tables(tb):
    rows = SEQ // GRID_W
    row = jnp.repeat(jnp.arange(rows, dtype=f32), GRID_W)
    col = jnp.tile(jnp.arange(GRID_W, dtype=f32), rows)
    inv_freq = ROPE_BASE ** (-jnp.arange(ROPE_FREQS, dtype=f32) / ROPE_FREQS)
    ang_r = row[:, None] * inv_freq[None, :]
    ang_c = col[:, None] * inv_freq[None, :]
    ang = jnp.concatenate([ang_r, ang_r, ang_c, ang_c], axis=-1)
    cos = jnp.concatenate([jnp.ones((CTX, HEAD_DIM), f32), jnp.cos(ang)], axis=0)
    sin = jnp.concatenate([jnp.zeros((CTX, HEAD_DIM), f32), jnp.sin(ang)], axis=0)
    return jnp.tile(cos, (1, LANES // HEAD_DIM)), jnp.tile(sin, (1, LANES // HEAD_DIM))


def _block_diag2(w):
    z = jnp.zeros_like(w[0])
    return jnp.concatenate([jnp.concatenate([w[0], z], axis=1), jnp.concatenate([z, w[1]], axis=1)], axis=0)


def kernel(x, c, ctx, c_ctx, w_mod, b_mod, norm_mix_g, norm_ffn_g, w_in, w_out, att_sink, shift_mu_prev, shift_mu_next, decay_w0, decay_w2, iclr_a0, iclr_a2, vres_v0, vres_v1, vres_v2, gate_g2, k_k, k_a, r_k, ln_x_w, ln_x_b, router_w, router_b, expert_w_gate, expert_w_up, expert_w_down, final_norm_g):
    nbatch = x.shape[0]
    depth = w_mod.shape[0]
    tb = ctx.shape[1] + x.shape[1]
    xa = jnp.concatenate([ctx, x], axis=1)
    nb_pad = -(-(nbatch + 1) // SUBLANES) * SUBLANES
    cond = jnp.zeros((nb_pad, D), f32).at[:nbatch].set(c).at[nbatch].set(c_ctx)
    mod_all = _mod_call(cond, w_mod, b_mod).reshape(depth, nb_pad, 1, 6 * D)
    cos, sin = _rope_tables(tb)
    wr_hi = router_w.T.astype(bf16)
    wr_lo = (router_w.T - wr_hi.astype(f32)).astype(bf16)
    w_router = jnp.concatenate([wr_hi, wr_lo], axis=0)
    b_router = router_b.reshape(N_EXPERTS, 1)
    v_first = None
    for l in range(depth):
        mod = mod_all[l]
        q, k, v, rw = _in_proj_call(xa, mod, norm_mix_g[l].reshape(1, D), w_in[l].astype(bf16), cos, sin)
        att = _attn_call(att_sink[l], q, k, v)
        mu = jnp.stack([shift_mu_prev[l], shift_mu_next[l]])
        v0 = vres_v0[l - 1] if l > 0 else jnp.zeros((RW,), f32)
        vec = jnp.stack([k_k[l], k_a[l], r_k[l].reshape(RW), v0,
                         decay_w0[l, 0], decay_w0[l, 1], iclr_a0[l, 0], iclr_a0[l, 1]])
        if l > 0:
            v1 = jnp.zeros((RW, LANES), f32).at[:, :LORA_VRES].set(vres_v1[l - 1]).astype(bf16)
            v2 = jnp.zeros((LANES, RW), f32).at[:LORA_VRES].set(vres_v2[l - 1]).astype(bf16)
        else:
            v1 = v2 = None
        r_, v_, kk, bv, g, kd, lw, bd = _feat_call(
            rw, v_first, mu, vec, _block_diag2(decay_w2[l]).astype(bf16), _block_diag2(iclr_a2[l]).astype(bf16),
            gate_g2[l].astype(bf16), v1, v2)
        if l == 0:
            v_first = v_
        yf, yb = _scan_call(r_, v_, kk, kd, lw, bd)
        ln = jnp.stack([ln_x_w[l], ln_x_b[l]])
        xa, x_local, lpos, gates, cnt = _mix_call(xa, att, yf, yb, bv, g, mod, ln, w_out[l].astype(bf16),
                                                  norm_ffn_g[l].reshape(1, D), w_router, b_router)
        ntok = nbatch * tb
        nrows = _moe_rows(ntok)
        xa = xa.reshape(nbatch, tb, D)
        cnt = cnt[:, 0].astype(i32).reshape(ntok // TM, N_EXPERTS)
        base, seg, lstart, pad_ends, blk_expert, n_used = _segment_plan(cnt, nrows // MOE_BLK)
        x_sorted = _dispatch_call(base, seg, lstart, pad_ends, x_local, nrows)
        y_sorted = _ffn_call(blk_expert, n_used, x_sorted, expert_w_gate, expert_w_up, expert_w_down, l)
        xa = _combine_call(base, seg, lstart, xa, lpos, gates, mod, y_sorted,
                           final_norm_g.reshape(1, D) if l == depth - 1 else None)
    return xa
```

```python
import functools
import math

import jax
import jax.numpy as jnp
from jax import lax
from jax.experimental import pallas as pl
from jax.experimental.pallas import tpu as pltpu

f32 = jnp.float32
bf16 = jnp.bfloat16
i32 = jnp.int32
u32 = jnp.uint32

D = 1024
SEQ = 4096
CTX = 256
TB = CTX + SEQ
GRID_W = 64
HEAD_DIM = 64
ATT_WIDTH = 512
ATT_HEADS = 8
KV_HEADS = 2
ATT_GROUP = ATT_HEADS // KV_HEADS
KV_WIDTH = KV_HEADS * HEAD_DIM
RW = 512
LORA_DECAY = 64
LORA_ICLR = 64
LORA_VRES = 32
LORA_GATE = 128
RWKV_COLS = 3 * RW + 2 * (LORA_DECAY + LORA_ICLR) + LORA_GATE
ATT_COLS = ATT_WIDTH + 2 * KV_WIDTH
IN_COLS = ATT_COLS + RWKV_COLS
N_EXPERTS = 16
N_GROUPS = 4
EXPERTS_PER_GROUP = 4
TOP_K = 2
MOE_BLK = 512
NORM_EPS = 1e-6
GN_EPS = 64e-5
NEG_INF = -1e30
ATT_SCALE = HEAD_DIM ** -0.5
ROPE_BASE = 10000.0
ROPE_FREQS = HEAD_DIM // 4

LANES = 128
SUBLANES = 8
TM = 256
QB = 128
CH = 64
HG = 4
GW = HG * HEAD_DIM
SCAN_BATCH = 4
MIX_TILES = 2
SEG_ALIGN = SUBLANES
LROWS = -(-(TOP_K * TM + N_EXPERTS * SEG_ALIGN) // LANES) * LANES
VMEM_LIMIT = 48 * 1024 * 1024


def _cparams(sem):
    return pltpu.CompilerParams(dimension_semantics=sem, vmem_limit_bytes=VMEM_LIMIT)


def _sigmoid(x):
    return 0.5 * jnp.tanh(0.5 * x) + 0.5


def _div_pow2(x, n):
    assert n & (n - 1) == 0
    return lax.shift_right_logical(x, n.bit_length() - 1)


def _mod_pow2(x, n):
    assert n & (n - 1) == 0
    return lax.bitwise_and(x, n - 1)


def _round_up_pow2(x, n):
    assert n & (n - 1) == 0
    return lax.bitwise_and(x + (n - 1), ~(n - 1))


def _dot(a, b):
    return jnp.dot(a, b, preferred_element_type=f32)


def _dot_nt(a, b):
    return lax.dot_general(a, b, (((1,), (1,)), ((), ())), preferred_element_type=f32)


def _dot_tn(a, b):
    return lax.dot_general(a, b, (((0,), (0,)), ((), ())), preferred_element_type=f32)


def _split2(x):
    hi = x.astype(bf16)
    lo = (x - hi.astype(f32)).astype(bf16)
    return hi, lo


def _rmsnorm(x, g):
    ms = jnp.mean(x * x, axis=-1, keepdims=True)
    return x * lax.rsqrt(ms + NORM_EPS) * g


def _head_ones():
    r = _div_pow2(lax.broadcasted_iota(i32, (RW, RW), 0), HEAD_DIM)
    c = _div_pow2(lax.broadcasted_iota(i32, (RW, RW), 1), HEAD_DIM)
    return (r == c).astype(bf16)


def _mod_kernel(c_ref, w_ref, b_ref, o_ref):
    c = c_ref[...]
    s = (c * _sigmoid(c)).astype(bf16)
    o_ref[0] = _dot(s, w_ref[0].astype(bf16)) + b_ref[0]


def _mod_call(cond, w_mod, b_mod):
    nb = cond.shape[0]
    depth = w_mod.shape[0]
    tn = 1024
    return pl.pallas_call(
        _mod_kernel,
        grid=(depth, 6 * D // tn),
        in_specs=[
            pl.BlockSpec((nb, D), lambda l, j: (0, 0)),
            pl.BlockSpec((1, D, tn), lambda l, j: (l, 0, j)),
            pl.BlockSpec((1, 1, tn), lambda l, j: (l, 0, j)),
        ],
        out_specs=pl.BlockSpec((1, nb, tn), lambda l, j: (l, 0, j)),
        out_shape=jax.ShapeDtypeStruct((depth, nb, 6 * D), f32),
        compiler_params=_cparams(("arbitrary", "arbitrary")),
        name="mod",
    )(cond, w_mod, b_mod.reshape(depth, 1, 6 * D))


def _mod_index(nbatch, ctx_tiles):
    return lambda b, i: (jnp.where(i < ctx_tiles, nbatch, b), 0, 0)


def _in_proj_kernel(x_ref, mod_ref, g_ref, w_ref, cos_ref, sin_ref, q_ref, k_ref, v_ref, rw_ref):
    x = x_ref[0]
    tm = x.shape[0]
    h = _rmsnorm(x, g_ref[...])
    sh = mod_ref[0, :, 0:D]
    sc = mod_ref[0, :, D:2 * D]
    h = (h * (1.0 + sc) + sh).astype(bf16)
    p = _dot(h, w_ref[...])
    cos = cos_ref[...]
    sin = sin_ref[...]
    lane = lax.broadcasted_iota(i32, (tm, LANES), 1)
    first_half = _mod_pow2(lane, 2 * ROPE_FREQS) < ROPE_FREQS

    def rope(t):
        rot = jnp.where(first_half, -pltpu.roll(t, LANES - ROPE_FREQS, 1), pltpu.roll(t, ROPE_FREQS, 1))
        return t * cos + rot * sin

    for j in range(ATT_WIDTH // LANES):
        t = (rope(p[:, j * LANES:(j + 1) * LANES]) * ATT_SCALE).astype(bf16)
        q_ref[0, 2 * j] = t[:, :HEAD_DIM]
        q_ref[0, 2 * j + 1] = t[:, HEAD_DIM:]
    kt = rope(p[:, ATT_WIDTH:ATT_WIDTH + KV_WIDTH]).astype(bf16)
    vt = p[:, ATT_WIDTH + KV_WIDTH:ATT_COLS].astype(bf16)
    for hh in range(KV_HEADS):
        k_ref[0, hh] = kt[:, hh * HEAD_DIM:(hh + 1) * HEAD_DIM]
        v_ref[0, hh] = vt[:, hh * HEAD_DIM:(hh + 1) * HEAD_DIM]
    rw_ref[0] = p[:, ATT_COLS:]


def _in_proj_call(x, mod, g, w_in, cos, sin):
    nbatch, tb, _ = x.shape
    nt = tb // TM
    return pl.pallas_call(
        _in_proj_kernel,
        grid=(nbatch, nt),
        in_specs=[
            pl.BlockSpec((1, TM, D), lambda b, i: (b, i, 0)),
            pl.BlockSpec((1, 1, 6 * D), _mod_index(nbatch, CTX // TM)),
            pl.BlockSpec((1, D), lambda b, i: (0, 0)),
            pl.BlockSpec((D, IN_COLS), lambda b, i: (0, 0)),
            pl.BlockSpec((TM, LANES), lambda b, i: (i, 0)),
            pl.BlockSpec((TM, LANES), lambda b, i: (i, 0)),
        ],
        out_specs=[
            pl.BlockSpec((1, ATT_HEADS, TM, HEAD_DIM), lambda b, i: (b, 0, i, 0)),
            pl.BlockSpec((1, KV_HEADS, TM, HEAD_DIM), lambda b, i: (b, 0, i, 0)),
            pl.BlockSpec((1, KV_HEADS, TM, HEAD_DIM), lambda b, i: (b, 0, i, 0)),
            pl.BlockSpec((1, TM, RWKV_COLS), lambda b, i: (b, i, 0)),
        ],
        out_shape=[
            jax.ShapeDtypeStruct((nbatch, ATT_HEADS, tb, HEAD_DIM), bf16),
            jax.ShapeDtypeStruct((nbatch, KV_HEADS, tb, HEAD_DIM), bf16),
            jax.ShapeDtypeStruct((nbatch, KV_HEADS, tb, HEAD_DIM), bf16),
            jax.ShapeDtypeStruct((nbatch, tb, RWKV_COLS), f32),
        ],
        compiler_params=_cparams(("parallel", "arbitrary")),
        name="in_proj",
    )(x, mod, g, w_in, cos, sin)


def _attn_kernel(sink_ref, q_ref, kp_ref, km_ref, kn_ref, vp_ref, vm_ref, vn_ref, kx_ref, vx_ref, o_ref,
                 *, npairs, ctx_pairs):
    j = pl.program_id(1)
    is_lat = j >= ctx_pairs
    before_ok = jnp.logical_and(is_lat, j - 1 >= ctx_pairs)
    after_ok = jnp.logical_and(is_lat, j + 1 <= npairs - 1)
    rows = ATT_GROUP * QB
    qi = _mod_pow2(lax.broadcasted_iota(i32, (rows, QB), 0), QB)
    kj = lax.broadcasted_iota(i32, (rows, QB), 1)
    band_p = kj >= qi
    band_n = kj <= qi
    row_head = _div_pow2(lax.broadcasted_iota(i32, (rows, 1), 0), QB)
    half = (slice(0, QB), slice(QB, 2 * QB))
    chains = [(c, h) for c in range(2) for h in range(KV_HEADS)]
    qh, kprev, kcur, knext, vprev, vcur, vnext, ok_p, ok_n = [], [], [], [], [], [], [], [], []
    for c, h in chains:
        qh.append(q_ref[0, ATT_GROUP * h:ATT_GROUP * (h + 1), half[c], :].reshape(rows, HEAD_DIM))
        kcur.append(km_ref[0, h, half[c], :])
        vcur.append(vm_ref[0, h, half[c], :])
        if c == 0:
            kprev.append(kp_ref[0, h]); vprev.append(vp_ref[0, h]); ok_p.append(before_ok)
            knext.append(km_ref[0, h, half[1], :]); vnext.append(vm_ref[0, h, half[1], :]); ok_n.append(is_lat)
        else:
            kprev.append(km_ref[0, h, half[0], :]); vprev.append(vm_ref[0, h, half[0], :]); ok_p.append(is_lat)
            knext.append(kn_ref[0, h]); vnext.append(vn_ref[0, h]); ok_n.append(after_ok)
    n = range(len(chains))
    s_p = [jnp.where(jnp.logical_and(band_p, ok_p[i]), _dot_nt(qh[i], kprev[i]), NEG_INF) for i in n]
    s_c = [jnp.where(is_lat, _dot_nt(qh[i], kcur[i]), NEG_INF) for i in n]
    s_n = [jnp.where(jnp.logical_and(band_n, ok_n[i]), _dot_nt(qh[i], knext[i]), NEG_INF) for i in n]
    s_x = [_dot_nt(qh[i], kx_ref[0, chains[i][1]]) for i in n]
    sink = []
    for h in range(KV_HEADS):
        sk = jnp.zeros((rows, 1), f32)
        for g in range(ATT_GROUP):
            sk = jnp.where(row_head == g, sink_ref[ATT_GROUP * h + g], sk)
        sink.append(sk)
    sink = [sink[h] for _, h in chains]
    m = [jnp.maximum(jnp.max(jnp.maximum(jnp.maximum(s_p[i], s_c[i]),
                                         jnp.maximum(jnp.maximum(s_n[i], s_x[i][:, :QB]), s_x[i][:, QB:])),
                             axis=-1, keepdims=True), sink[i]) for i in n]
    e_p = [jnp.exp(s_p[i] - m[i]) for i in n]
    e_c = [jnp.exp(s_c[i] - m[i]) for i in n]
    e_n = [jnp.exp(s_n[i] - m[i]) for i in n]
    e_x = [jnp.exp(s_x[i] - m[i]) for i in n]
    den = [jnp.sum((e_p[i] + e_c[i]) + (e_n[i] + e_x[i][:, :QB]) + e_x[i][:, QB:], axis=-1, keepdims=True)
           + jnp.exp(sink[i] - m[i]) for i in n]
    o = [(_dot(e_p[i].astype(bf16), vprev[i]) + _dot(e_c[i].astype(bf16), vcur[i])
          + _dot(e_n[i].astype(bf16), vnext[i]) + _dot(e_x[i].astype(bf16), vx_ref[0, chains[i][1]])) / den[i]
         for i in n]
    for c in range(2):
        o_ref[0, half[c], :] = jnp.concatenate(
            [o[c * KV_HEADS + h][g * QB:(g + 1) * QB] for h in range(KV_HEADS) for g in range(ATT_GROUP)],
            axis=1).astype(bf16)


def _attn_call(sink, q, k, v):
    nbatch, _, tb, _ = q.shape
    nblk = tb // QB
    npairs = nblk // 2
    assert CTX == 2 * QB and nblk % 2 == 0
    kv_blk = (1, KV_HEADS, QB, HEAD_DIM)
    pair_blk = (1, KV_HEADS, 2 * QB, HEAD_DIM)
    before_map = lambda b, j: (b, 0, jnp.maximum(2 * j - 1, 0), 0)
    pair_map = lambda b, j: (b, 0, j, 0)
    after_map = lambda b, j: (b, 0, jnp.minimum(2 * j + 2, nblk - 1), 0)
    ctx_spec = pl.BlockSpec((1, KV_HEADS, CTX, HEAD_DIM), lambda b, j: (b, 0, 0, 0))
    return pl.pallas_call(
        functools.partial(_attn_kernel, npairs=npairs, ctx_pairs=CTX // (2 * QB)),
        grid=(nbatch, npairs),
        in_specs=[
            pl.BlockSpec(memory_space=pltpu.SMEM),
            pl.BlockSpec((1, ATT_HEADS, 2 * QB, HEAD_DIM), pair_map),
            pl.BlockSpec(kv_blk, before_map), pl.BlockSpec(pair_blk, pair_map), pl.BlockSpec(kv_blk, after_map),
            pl.BlockSpec(kv_blk, before_map), pl.BlockSpec(pair_blk, pair_map), pl.BlockSpec(kv_blk, after_map),
            ctx_spec, ctx_spec,
        ],
        out_specs=pl.BlockSpec((1, 2 * QB, ATT_WIDTH), lambda b, j: (b, j, 0)),
        out_shape=jax.ShapeDtypeStruct((nbatch, tb, ATT_WIDTH), bf16),
        compiler_params=_cparams(("parallel", "arbitrary")),
        name="attention",
    )(sink, q, k, k, k, v, v, v, k, v)


def _feat_kernel(*refs, nt, ctx_tiles, has_vres):
    if has_vres:
        (rw_ref, hp_ref, hn_ref, vf_ref, mu_ref, vec_ref, w2_ref, a2_ref, g2_ref, v1_ref, v2_ref,
         r_ref, v_ref, kk_ref, bv_ref, g_ref, kd_ref, lw_ref, bd_ref) = refs
    else:
        (rw_ref, hp_ref, hn_ref, mu_ref, vec_ref, w2_ref, a2_ref, g2_ref,
         r_ref, v_ref, kk_ref, bv_ref, g_ref, kd_ref, lw_ref, bd_ref) = refs
    i = pl.program_id(1)
    u0 = rw_ref[0]
    tm = u0.shape[0]
    prev_zero = jnp.logical_or(i == 0, i == ctx_tiles)
    next_zero = jnp.logical_or(i == ctx_tiles - 1, i == nt - 1)
    halo_p = jnp.where(prev_zero, 0.0, hp_ref[0, SUBLANES - 1:SUBLANES, :])
    halo_n = jnp.where(next_zero, 0.0, hn_ref[0, 0:1, :])
    row = lax.broadcasted_iota(i32, (tm, 1), 0)
    prev = jnp.where(row == 0, halo_p, pltpu.roll(u0, 1, 0))
    nxt = jnp.where(row == tm - 1, halo_n, pltpu.roll(u0, tm - 1, 0))
    mu_p = mu_ref[0:1, :]
    mu_n = mu_ref[1:2, :]
    u = u0 + mu_p * (prev - u0) + mu_n * (nxt - u0)

    r = u[:, 0:RW]
    k = u[:, RW:2 * RW]
    v = u[:, 2 * RW:3 * RW]
    wd = u[:, 3 * RW:3 * RW + 2 * LORA_DECAY]
    ad = u[:, 3 * RW + 2 * LORA_DECAY:3 * RW + 2 * (LORA_DECAY + LORA_ICLR)]
    gd = u[:, 3 * RW + 2 * (LORA_DECAY + LORA_ICLR):]
    k_k = vec_ref[0:1, :]
    k_a = vec_ref[1:2, :]
    r_k = vec_ref[2:3, :]
    ones = _head_ones()

    if has_vres:
        lo = _dot(v.astype(bf16), v1_ref[...])
        gate = _sigmoid(vec_ref[3:4, :] + _dot(lo.astype(bf16), v2_ref[...]))
        v = v + (vf_ref[0] - v) * gate
    decay_in = _dot(jnp.tanh(wd).astype(bf16), w2_ref[...])
    a_in = _dot(ad.astype(bf16), a2_ref[...])
    kk = k * k_k
    n2 = _dot((kk * kk).astype(bf16), ones)
    kk = kk * lax.rsqrt(jnp.maximum(n2, 1e-24))
    g = _dot(_sigmoid(gd).astype(bf16), g2_ref[...])
    ksum = jnp.zeros_like(k)
    for d in range(2):
        w0 = vec_ref[4 + d:5 + d, :]
        a0 = vec_ref[6 + d:7 + d, :]
        lw = -_sigmoid(w0 + decay_in[:, d * RW:(d + 1) * RW]) * math.exp(-0.5)
        a = _sigmoid(a0 + a_in[:, d * RW:(d + 1) * RW])
        kd = k * (1.0 + (a - 1.0) * k_a)
        ksum = ksum + kd
        kd_ref[d, 0] = kd.astype(bf16)
        lw_ref[d, 0] = lw
        bd_ref[d, 0] = (kk * a).astype(bf16)
    bonus = _dot((r * ksum * r_k).astype(bf16), ones)
    r_ref[0] = r.astype(bf16)
    v_ref[0] = v
    kk_ref[0] = kk.astype(bf16)
    bv_ref[0] = bonus * v
    g_ref[0] = g.astype(bf16)


def _feat_call(rw, v_first, mu, vec, w2bd, a2bd, g2, v1, v2):
    nbatch, tb, _ = rw.shape
    nt = tb // TM
    has_vres = v_first is not None
    sub = TM // SUBLANES
    tile = lambda w: pl.BlockSpec((1, TM, w), lambda b, i: (b, i, 0))
    full = lambda a: pl.BlockSpec(a.shape, lambda b, i: (0,) * a.ndim)
    halo = (1, SUBLANES, RWKV_COLS)
    in_specs = [
        tile(RWKV_COLS),
        pl.BlockSpec(halo, lambda b, i: (b, jnp.maximum(i * sub - 1, 0), 0)),
        pl.BlockSpec(halo, lambda b, i: (b, jnp.minimum((i + 1) * sub, tb // SUBLANES - 1), 0)),
    ]
    args = [rw, rw, rw]
    if has_vres:
        in_specs.append(tile(RW))
        args.append(v_first)
    consts = [mu, vec, w2bd, a2bd, g2] + ([v1, v2] if has_vres else [])
    in_specs += [full(a) for a in consts]
    args += consts
    dir_spec = pl.BlockSpec((2, 1, TM, RW), lambda b, i: (0, b, i, 0))
    tok = lambda dt: jax.ShapeDtypeStruct((nbatch, tb, RW), dt)
    dtok = lambda dt: jax.ShapeDtypeStruct((2, nbatch, tb, RW), dt)
    return pl.pallas_call(
        functools.partial(_feat_kernel, nt=nt, ctx_tiles=CTX // TM, has_vres=has_vres),
        grid=(nbatch, nt),
        in_specs=in_specs,
        out_specs=[tile(RW)] * 5 + [dir_spec] * 3,
        out_shape=[tok(bf16), tok(f32), tok(bf16), tok(f32), tok(bf16), dtok(bf16), dtok(f32), dtok(bf16)],
        compiler_params=_cparams(("parallel", "arbitrary")),
        name="rwkv_features",
    )(*args)


def _block_diag_rows(x, width):
    cb = _div_pow2(lax.broadcasted_iota(i32, x.shape, 1), width)
    return jnp.concatenate([jnp.where(cb == h, x, jnp.zeros_like(x)) for h in range(HG)], axis=0)


def _scan_chunks(probs):
    wide = HG * CH
    row_t = lax.broadcasted_iota(i32, (CH, wide), 0)
    col_t = _mod_pow2(lax.broadcasted_iota(i32, (CH, wide), 1), CH)
    row_g = lax.broadcasted_iota(i32, (CH, GW), 0)
    incl_t = {False: col_t <= row_t, True: col_t >= row_t}
    strict_t = {False: col_t < row_t, True: col_t > row_t}
    eye_t = (row_t == col_t).astype(f32)
    rb = _div_pow2(lax.broadcasted_iota(i32, (GW, GW), 0), HEAD_DIM)
    cb = _div_pow2(lax.broadcasted_iota(i32, (GW, GW), 1), HEAD_DIM)
    n = len(probs)
    rev = [p[7] for p in probs]
    def cumsum_rows(x, reverse):
        s = 1
        while s < CH:
            if reverse:
                x = x + jnp.where(row_g < CH - s, pltpu.roll(x, CH - s, 0), 0.0)
            else:
                x = x + jnp.where(row_g >= s, pltpu.roll(x, s, 0), 0.0)
            s *= 2
        return x

    gam = [cumsum_rows(probs[i][5], rev[i]) for i in range(n)]
    ar, bk, k_t, b_t = [], [], [], []
    for i, (s_prev, r, v, kk, k, lw, b, _) in enumerate(probs):
        e_neg = jnp.exp(-gam[i])
        a_s = (-kk * jnp.exp(gam[i] - lw)).astype(bf16)
        r_s = (r * jnp.exp(gam[i])).astype(bf16)
        b_t.append(b * e_neg)
        k_t.append(k * e_neg)
        ar.append(jnp.concatenate([a_s, r_s], axis=0))
        bk.append(jnp.concatenate([_block_diag_rows(b_t[i].astype(bf16), HEAD_DIM),
                                   _block_diag_rows(k_t[i].astype(bf16), HEAD_DIM)], axis=0))
    gram = [_dot_nt(ar[i], bk[i]) for i in range(n)]
    ars = [_dot_nt(ar[i], probs[i][0].astype(bf16)) for i in range(n)]
    v_bd = [_block_diag_rows(probs[i][2].astype(bf16), HEAD_DIM) for i in range(n)]
    p0 = [jnp.where(strict_t[rev[i]], gram[i][:CH, :wide], 0.0).astype(bf16) for i in range(n)]
    lq = [jnp.concatenate([jnp.where(strict_t[rev[i]], gram[i][:CH, wide:], 0.0),
                           jnp.where(incl_t[rev[i]], gram[i][CH:, wide:], 0.0)], axis=0).astype(bf16)
          for i in range(n)]
    lqv = [_dot(lq[i], v_bd[i]) for i in range(n)]
    rhs = [ars[i][:CH] + lqv[i][:CH] for i in range(n)]
    t = [eye_t + p0[i].astype(f32) for i in range(n)]
    p = [_dot(p0[i], _block_diag_rows(p0[i], CH)).astype(bf16) for i in range(n)]
    m = 4
    while m < CH:
        tp = [_dot(jnp.concatenate([t[i].astype(bf16), p[i]], axis=0), _block_diag_rows(p[i], CH))
              for i in range(n)]
        t = [t[i] + tp[i][:CH] for i in range(n)]
        p = [tp[i][CH:].astype(bf16) for i in range(n)]
        m *= 2
    t = [t[i] + _dot(t[i].astype(bf16), _block_diag_rows(p[i], CH)) for i in range(n)]
    u = [_dot(t[i].astype(bf16), _block_diag_rows(rhs[i].astype(bf16), HEAD_DIM)) for i in range(n)]
    out = []
    for i in range(n):
        s_prev, v, lw = probs[i][0], probs[i][2], probs[i][5]
        q_b = jnp.where(incl_t[rev[i]], gram[i][CH:, :wide], 0.0).astype(bf16)
        u_bd = _block_diag_rows(u[i].astype(bf16), HEAD_DIM)
        y = ars[i][CH:] + lqv[i][CH:] + _dot(q_b, u_bd)
        eg = jnp.exp(jnp.sum(lw, axis=0, keepdims=True))
        vu = jnp.concatenate([v, u[i]], axis=0).astype(bf16)
        kb = jnp.concatenate([k_t[i] * eg, b_t[i] * eg], axis=0).astype(bf16)
        s_add = _dot_tn(vu, kb)
        out.append((s_prev * eg + jnp.where(rb == cb, s_add, 0.0), y))
    return out


def _scan_kernel(rf_ref, vf_ref, kkf_ref, kf_ref, lwf_ref, bf_ref,
                 rb_ref, vb_ref, kkb_ref, kb_ref, lwb_ref, bb_ref,
                 yf_ref, yb_ref, s_ref):
    @pl.when(pl.program_id(1) == 0)
    def _():
        s_ref[...] = jnp.zeros_like(s_ref)

    dirs = ((rf_ref, vf_ref, kkf_ref, kf_ref, lwf_ref, bf_ref, yf_ref),
            (rb_ref, vb_ref, kkb_ref, kb_ref, lwb_ref, bb_ref, yb_ref))
    probs, dest = [], []
    for bi in range(SCAN_BATCH):
        for d, (r_ref, v_ref, kk_ref, k_ref, lw_ref, b_ref, y_ref) in enumerate(dirs):
            for g in range(RW // GW):
                sl = slice(g * GW, (g + 1) * GW)
                probs.append((s_ref[bi, d, g], r_ref[bi, :, sl], v_ref[bi, :, sl], kk_ref[bi, :, sl],
                              k_ref[0, bi, :, sl], lw_ref[0, bi, :, sl], b_ref[0, bi, :, sl], d == 1))
                dest.append((bi, d, g, y_ref, sl))
    for (bi, d, g, y_ref, sl), (s_new, y) in zip(dest, _scan_chunks(probs)):
        s_ref[bi, d, g] = s_new
        y_ref[bi, :, sl] = y


def _scan_call(r, v, kk, kd, lw, bd):
    nbatch, tb, _ = r.shape
    nc = tb // CH
    cc = CTX // CH
    sb = SCAN_BATCH
    rev = lambda j: jnp.where(j < cc, cc - 1 - j, nc - 1 + cc - j)
    tok_f = pl.BlockSpec((sb, CH, RW), lambda b, j: (b, j, 0))
    tok_b = pl.BlockSpec((sb, CH, RW), lambda b, j: (b, rev(j), 0))
    dir_f = pl.BlockSpec((1, sb, CH, RW), lambda b, j: (0, b, j, 0))
    dir_b = pl.BlockSpec((1, sb, CH, RW), lambda b, j: (1, b, rev(j), 0))
    out = jax.ShapeDtypeStruct((nbatch, tb, RW), f32)
    return pl.pallas_call(
        _scan_kernel,
        grid=(nbatch // sb, nc),
        in_specs=[tok_f, tok_f, tok_f, dir_f, dir_f, dir_f, tok_b, tok_b, tok_b, dir_b, dir_b, dir_b],
        out_specs=[tok_f, tok_b],
        out_shape=[out, out],
        scratch_shapes=[pltpu.VMEM((sb, 2, RW // GW, GW, GW), f32)],
        compiler_params=_cparams(("parallel", "arbitrary")),
        name="rwkv_scan",
    )(r, v, kk, kd, lw, bd, r, v, kk, kd, lw, bd)


def _pack_bf16_pairs(x):
    bits = pltpu.bitcast(x, u32)
    half = x.shape[1] // 2
    return bits[:, :half] | lax.shift_right_logical(bits[:, half:], jnp.uint32(16))


def _unpack_bf16_pairs(p):
    hi = pltpu.bitcast(p & jnp.uint32(0xFFFF0000), f32)
    lo = pltpu.bitcast(lax.shift_left(p, jnp.uint32(16)), f32)
    return jnp.concatenate([hi, lo], axis=1).astype(bf16)


def _mix_kernel(x_ref, att_ref, yf_ref, yb_ref, bv_ref, g_ref, mod0_ref, mod1_ref, ln_ref, wo_ref, gf_ref,
                wr_ref, rb_ref, xo_ref, xl_ref, lp_ref, gt_ref, cnt_ref, *, tiles_per_batch, dead_tiles):
    ones = _head_ones()
    inv = 1.0 / HEAD_DIM
    tiles = range(MIX_TILES)
    rows = [slice(s * TM, (s + 1) * TM) for s in tiles]
    mods = (mod0_ref, mod1_ref)
    y = [yf_ref[rows[s], :] + yb_ref[rows[s], :] for s in tiles]
    mu = [_dot(y[s].astype(bf16), ones) * inv for s in tiles]
    dlt = [y[s] - mu[s] for s in tiles]
    var = [_dot((dlt[s] * dlt[s]).astype(bf16), ones) * inv for s in tiles]
    rwk = [((dlt[s] * lax.rsqrt(var[s] + GN_EPS) * ln_ref[0:1, :] + ln_ref[1:2, :] + bv_ref[rows[s], :])
            * g_ref[rows[s], :]).astype(bf16) for s in tiles]
    mix = [_dot(att_ref[rows[s], :], wo_ref[0:ATT_WIDTH, :]) + _dot(rwk[s], wo_ref[ATT_WIDTH:, :]) for s in tiles]
    xn = [x_ref[rows[s], :] + mods[s][0, :, 2 * D:3 * D] * mix[s] for s in tiles]
    for s in tiles:
        xo_ref[rows[s], :] = xn[s]
    hf = [_rmsnorm(xn[s], gf_ref[...]) * (1.0 + mods[s][0, :, 4 * D:5 * D]) + mods[s][0, :, 3 * D:4 * D]
          for s in tiles]
    split = [_split2(hf[s]) for s in tiles]
    by_hi = [_dot_nt(wr_ref[...], split[s][0]) for s in tiles]
    logits = [by_hi[s][:N_EXPERTS] + by_hi[s][N_EXPERTS:] + _dot_nt(wr_ref[0:N_EXPERTS, :], split[s][1])
              for s in tiles]
    first = MIX_TILES * pl.program_id(0)
    keep = [lax.rem(first + s, tiles_per_batch) >= dead_tiles for s in tiles]
    sorted_rows, lpos, gates, cnt = _route_and_sort(hf, [_sigmoid(lg) for lg in logits], rb_ref[...], keep)
    for s in tiles:
        xl_ref[s * LROWS:(s + 1) * LROWS, :] = sorted_rows[s]
        lp_ref[:, rows[s]] = lpos[s]
        gt_ref[:, rows[s]] = gates[s]
        cnt_ref[s * N_EXPERTS:(s + 1) * N_EXPERTS, :] = cnt[s]


def _route_and_sort(hf, scores, bias, keep):
    tiles = range(len(hf))
    tm = hf[0].shape[0]
    routed = [_route_rows(scores[s], bias) for s in tiles]
    eio = lax.broadcasted_iota(i32, (N_EXPERTS, tm), 0)
    upper = (lax.broadcasted_iota(i32, (tm, tm), 0) < lax.broadcasted_iota(i32, (tm, tm), 1)).astype(bf16)
    lower = (lax.broadcasted_iota(i32, (N_EXPERTS, N_EXPERTS), 0)
             > lax.broadcasted_iota(i32, (N_EXPERTS, N_EXPERTS), 1)).astype(bf16)
    jj = lax.broadcasted_iota(i32, (LROWS, tm), 0)
    oh0 = [jnp.logical_and(eio == routed[s][0], keep[s]).astype(f32) for s in tiles]
    oh1 = [jnp.logical_and(eio == routed[s][1], keep[s]).astype(f32) for s in tiles]
    both = [oh0[s] + oh1[s] for s in tiles]
    before = [_dot(both[s].astype(bf16), upper) for s in tiles]
    cnt = [jnp.sum(both[s], axis=1, keepdims=True) for s in tiles]
    seg = [_round_up_pow2(cnt[s].astype(i32), SEG_ALIGN).astype(f32) for s in tiles]
    seg_start = [_dot(lower, jnp.broadcast_to(seg[s], (N_EXPERTS, tm)).astype(bf16)) for s in tiles]
    pos = [seg_start[s] + before[s] for s in tiles]
    lp0 = [jnp.sum(oh0[s] * pos[s], axis=0, keepdims=True).astype(i32) for s in tiles]
    lp1 = [jnp.sum(oh1[s] * pos[s], axis=0, keepdims=True).astype(i32) for s in tiles]
    sel_t = [jnp.logical_and(jnp.logical_or(jj == lp0[s], jj == lp1[s]), keep[s]).astype(bf16)
             for s in tiles]
    sorted_rows = [_pack_bf16_pairs(_dot(sel_t[s], hf[s].astype(bf16))) for s in tiles]
    lpos = [jnp.concatenate([lp0[s], lp1[s]], axis=0) for s in tiles]
    gates = [jnp.concatenate([routed[s][2], routed[s][3]], axis=0) for s in tiles]
    cnt_out = [jnp.broadcast_to(cnt[s], (N_EXPERTS, LANES)) for s in tiles]
    return sorted_rows, lpos, gates, cnt_out


def _route_rows(scores, bias):
    biased = scores + bias
    row = lambda a, e: a[e:e + 1, :]
    best = None
    for gi in range(N_GROUPS):
        m = [row(biased, gi * EXPERTS_PER_GROUP + j) for j in range(EXPERTS_PER_GROUP)]
        gs = None
        for a in range(EXPERTS_PER_GROUP):
            for b in range(a + 1, EXPERTS_PER_GROUP):
                pair = m[a] + m[b]
                gs = pair if gs is None else jnp.maximum(gs, pair)
        if best is None:
            best, g_idx = gs, jnp.zeros(gs.shape, i32)
        else:
            better = gs > best
            g_idx = jnp.where(better, gi, g_idx)
            best = jnp.where(better, gs, best)

    def pick(a, j):
        out = row(a, j)
        for gi in range(1, N_GROUPS):
            out = jnp.where(g_idx == gi, row(a, gi * EXPERTS_PER_GROUP + j), out)
        return out

    vb = [pick(biased, j) for j in range(EXPERTS_PER_GROUP)]
    vs = [pick(scores, j) for j in range(EXPERTS_PER_GROUP)]

    def argmax_first(vals):
        bv, bi = vals[0], jnp.zeros(vals[0].shape, i32)
        for j in range(1, len(vals)):
            better = vals[j] > bv
            bi = jnp.where(better, j, bi)
            bv = jnp.where(better, vals[j], bv)
        return bi

    i1 = argmax_first(vb)
    i2 = argmax_first([jnp.where(i1 == j, -jnp.inf, vb[j]) for j in range(EXPERTS_PER_GROUP)])
    sel = lambda idx: sum(jnp.where(idx == j, vs[j], 0.0) for j in range(EXPERTS_PER_GROUP))
    s1, s2 = sel(i1), sel(i2)
    tot = s1 + s2
    base = g_idx * EXPERTS_PER_GROUP
    return base + i1, base + i2, s1 / tot, s2 / tot


def _mix_call(x, att, yf, yb, bv, g, mod, ln, w_out, g_ffn, w_router, b_router, last_layer):
    nbatch, tb, _ = x.shape
    tiles_per_batch = tb // TM
    ntiles = nbatch * tiles_per_batch
    assert ntiles % MIX_TILES == 0 and MIX_TILES == 2
    ntok = nbatch * tb
    flat = lambda a: a.reshape(ntok, a.shape[-1])
    rows = lambda w: pl.BlockSpec((MIX_TILES * TM, w), lambda p: (p, 0))
    full = lambda a: pl.BlockSpec(a.shape, lambda p: (0,) * a.ndim)
    ctx_tiles = CTX // TM

    def mod_spec(s):
        def index(p):
            tile = MIX_TILES * p + s
            return (jnp.where(tile % tiles_per_batch < ctx_tiles, nbatch, tile // tiles_per_batch), 0, 0)
        return pl.BlockSpec((1, 1, 6 * D), index)

    route = pl.BlockSpec((TOP_K, MIX_TILES * TM), lambda p: (0, p))
    return pl.pallas_call(
        functools.partial(_mix_kernel, tiles_per_batch=tiles_per_batch, dead_tiles=ctx_tiles if last_layer else 0),
        grid=(ntiles // MIX_TILES,),
        in_specs=[rows(D), rows(ATT_WIDTH), rows(RW), rows(RW), rows(RW), rows(RW), mod_spec(0), mod_spec(1),
                  full(ln), full(w_out), full(g_ffn), full(w_router), full(b_router)],
        out_specs=[rows(D), pl.BlockSpec((MIX_TILES * LROWS, D // 2), lambda p: (p, 0)), route, route,
                   pl.BlockSpec((MIX_TILES * N_EXPERTS, LANES), lambda p: (p, 0))],
        out_shape=[jax.ShapeDtypeStruct((ntok, D), f32),
                   jax.ShapeDtypeStruct((ntiles * LROWS, D // 2), u32),
                   jax.ShapeDtypeStruct((TOP_K, ntok), i32),
                   jax.ShapeDtypeStruct((TOP_K, ntok), f32),
                   jax.ShapeDtypeStruct((ntiles * N_EXPERTS, LANES), f32)],
        compiler_params=_cparams(("arbitrary",)),
        name="mix_out",
    )(flat(x), flat(att), flat(yf), flat(yb), flat(bv), flat(g), mod, mod, ln, w_out, g_ffn, w_router, b_router)


def _segment_copies(fn, tile, base_ref, seg_ref, ls_ref, src, dst, sem, src_is_global):
    for e in range(N_EXPERTS):
        idx = tile * N_EXPERTS + e
        seg = seg_ref[idx]
        g0 = base_ref[idx]
        l0 = ls_ref[idx]
        size = TM
        while size >= SEG_ALIGN:
            done = lax.bitwise_and(seg, ~(2 * size - 1))

            @pl.when(lax.bitwise_and(seg, size) != 0)
            def _():
                g_rows = pl.ds(pl.multiple_of(g0 + done, SEG_ALIGN), size)
                l_rows = pl.ds(pl.multiple_of(l0 + done, SEG_ALIGN), size)
                s_rows, d_rows = (g_rows, l_rows) if src_is_global else (l_rows, g_rows)
                fn(pltpu.make_async_copy(src.at[s_rows, :], dst.at[d_rows, :], sem))

            size //= 2


def _segment_waits(tile, seg_ref, src, dst, sem):
    total = seg_ref[tile * N_EXPERTS]
    for e in range(1, N_EXPERTS):
        total = total + seg_ref[tile * N_EXPERTS + e]
    size = 2 * TM
    assert LROWS < 2 * size
    while size >= SEG_ALIGN:
        @pl.when(lax.bitwise_and(total, size) != 0)
        def _():
            pltpu.make_async_copy(src.at[pl.ds(0, size), :], dst.at[pl.ds(0, size), :], sem).wait()

        size //= 2


def _dispatch_kernel(base_ref, seg_ref, ls_ref, ends_ref, xl_ref, xs_ref, zeros, stage, sem, in_sem, out_sem):
    tile = pl.program_id(0)
    ntiles = pl.num_programs(0)

    def load(t):
        return pltpu.make_async_copy(xl_ref.at[pl.ds(pl.multiple_of(t * LROWS, SEG_ALIGN), LROWS), :],
                                     stage.at[lax.rem(t, 3)], in_sem.at[lax.rem(t, 3)])

    def zero_tails(fn):
        for e in range(N_EXPERTS):
            end = ends_ref[e]
            start = ends_ref[e - 1] if e else 0

            @pl.when(end > start)
            def _():
                rows = pl.ds(pl.multiple_of(end - MOE_BLK, MOE_BLK), MOE_BLK)
                fn(pltpu.make_async_copy(zeros, xs_ref.at[rows, :], sem))

    def zero_unused(fn):
        def body(blk, carry):
            fn(pltpu.make_async_copy(zeros, xs_ref.at[pl.ds(pl.multiple_of(blk * MOE_BLK, MOE_BLK), MOE_BLK), :],
                                     sem))
            return carry
        lax.fori_loop(ends_ref[N_EXPERTS - 1] // MOE_BLK, xs_ref.shape[0] // MOE_BLK, body, 0)

    @pl.when(tile == 0)
    def _():
        load(tile).start()
        zeros[...] = jnp.zeros_like(zeros)
        zero_tails(lambda cp: cp.start())
        zero_unused(lambda cp: cp.start())
        zero_tails(lambda cp: cp.wait())
        zero_unused(lambda cp: cp.wait())

    @pl.when(tile + 1 < ntiles)
    def _():
        load(tile + 1).start()

    load(tile).wait()
    cur = lax.rem(tile, 2)
    _segment_copies(lambda cp: cp.start(), tile, base_ref, seg_ref, ls_ref, stage.at[lax.rem(tile, 3)], xs_ref,
                    out_sem.at[cur], False)

    @pl.when(tile > 0)
    def _():
        _segment_waits(tile - 1, seg_ref, stage.at[0], xs_ref, out_sem.at[1 - cur])

    @pl.when(tile == ntiles - 1)
    def _():
        _segment_waits(tile, seg_ref, stage.at[0], xs_ref, out_sem.at[cur])


def _dispatch_call(base, seg, lstart, pad_ends, x_local, nrows):
    ntiles = x_local.shape[0] // LROWS
    return pl.pallas_call(
        _dispatch_kernel,
        grid_spec=pltpu.PrefetchScalarGridSpec(
            num_scalar_prefetch=4,
            grid=(ntiles,),
            in_specs=[pl.BlockSpec(memory_space=pl.ANY)],
            out_specs=pl.BlockSpec(memory_space=pl.ANY),
            scratch_shapes=[pltpu.VMEM((MOE_BLK, D // 2), u32), pltpu.VMEM((3, LROWS, D // 2), u32),
                            pltpu.SemaphoreType.DMA(()), pltpu.SemaphoreType.DMA((3,)),
                            pltpu.SemaphoreType.DMA((2,))],
        ),
        out_shape=jax.ShapeDtypeStruct((nrows, D // 2), u32),
        compiler_params=_cparams(("arbitrary",)),
        name="moe_dispatch",
    )(base, seg, lstart, pad_ends, x_local)


def _ffn_kernel(be_ref, nu_ref, x_ref, wg_ref, wu_ref, wd_ref, y_ref, wg_bf, wu_bf, wd_bf):
    i = pl.program_id(0)
    used = i < nu_ref[0]
    new_expert = jnp.logical_or(i == 0, be_ref[i] != be_ref[jnp.maximum(i - 1, 0)])

    @pl.when(jnp.logical_and(used, new_expert))
    def _():
        wg_bf[...] = wg_ref[0, 0].astype(bf16)
        wu_bf[...] = wu_ref[0, 0].astype(bf16)
        wd_bf[...] = wd_ref[0, 0].astype(bf16)

    @pl.when(used)
    def _():
        x = _unpack_bf16_pairs(x_ref[...])
        gt = _dot(x, wg_bf[...])
        up = _dot(x, wu_bf[...])
        hid = (gt * _sigmoid(gt) * up).astype(bf16)
        y = _dot(hid, wd_bf[...])
        y_ref[...] = _pack_bf16_pairs(y.astype(bf16).astype(f32))

    @pl.when(i >= nu_ref[0])
    def _():
        y_ref[...] = jnp.zeros_like(y_ref)


def _ffn_call(blk_expert, n_used, x_sorted, wg, wu, wd, layer):
    nrows = x_sorted.shape[0]
    nblk = nrows // MOE_BLK
    wspec = pl.BlockSpec((1, 1, D, D), lambda i, be, nu: (layer, be[i], 0, 0))
    return pl.pallas_call(
        _ffn_kernel,
        grid_spec=pltpu.PrefetchScalarGridSpec(
            num_scalar_prefetch=2,
            grid=(nblk,),
            in_specs=[pl.BlockSpec((MOE_BLK, D // 2), lambda i, be, nu: (jnp.minimum(i, nu[0] - 1), 0)),
                      wspec, wspec, wspec],
            out_specs=pl.BlockSpec((MOE_BLK, D // 2), lambda i, be, nu: (i, 0)),
            scratch_shapes=[pltpu.VMEM((D, D), bf16)] * 3,
        ),
        out_shape=jax.ShapeDtypeStruct((nrows, D // 2), u32),
        compiler_params=_cparams(("arbitrary",)),
        name="moe_ffn",
    )(blk_expert, n_used, x_sorted, wg, wu, wd)


def _combine_kernel(base_ref, seg_ref, ls_ref, x_ref, lp_ref, gt_ref, mod_ref, *rest,
                    tiles_per_batch, first_tile, final):
    g_ref = rest[0] if final else None
    y_ref, o_ref, ybuf, sem = rest[-4:]
    b = pl.program_id(0)
    i = pl.program_id(1)
    nt = pl.num_programs(1)
    step = b * nt + i
    slot = lax.rem(step, 2)
    tile = b * tiles_per_batch + i + first_tile
    next_tile = jnp.where(i + 1 < nt, tile + 1, (b + 1) * tiles_per_batch + first_tile)

    def copies(fn, which_tile, which_slot):
        _segment_copies(fn, which_tile, base_ref, seg_ref, ls_ref, y_ref, ybuf.at[which_slot],
                        sem.at[which_slot], True)

    @pl.when(step == 0)
    def _():
        ybuf[...] = jnp.zeros_like(ybuf)
        copies(lambda cp: cp.start(), tile, slot)

    @pl.when(step + 1 < pl.num_programs(0) * nt)
    def _():
        copies(lambda cp: cp.start(), next_tile, 1 - slot)

    _segment_waits(tile, seg_ref, y_ref, ybuf.at[slot], sem.at[slot])
    y_loc = _unpack_bf16_pairs(ybuf[slot])
    tm = x_ref.shape[1]
    jj = lax.broadcasted_iota(i32, (LROWS, tm), 0)
    gmat = (jnp.where(jj == lp_ref[0:1, :], gt_ref[0:1, :], 0.0)
            + jnp.where(jj == lp_ref[1:2, :], gt_ref[1:2, :], 0.0))
    g_hi, g_lo = _split2(gmat)
    moe = _dot_tn(g_hi, y_loc) + _dot_tn(g_lo, y_loc)
    xn = x_ref[0] + mod_ref[0, :, 5 * D:6 * D] * moe
    o_ref[0] = _rmsnorm(xn, g_ref[...]) if final else xn


def _combine_call(base, seg, lstart, x, lpos, gates, mod, y_sorted, final_g):
    nbatch, tb, _ = x.shape
    ctx_tiles = CTX // TM
    tiles_per_batch = tb // TM
    final = final_g is not None
    first_tile = ctx_tiles if final else 0
    nt = tiles_per_batch - first_tile
    tile = pl.BlockSpec((1, TM, D), lambda b, i, *_: (b, i + first_tile, 0))
    route = pl.BlockSpec((TOP_K, TM), lambda b, i, *_: (0, b * tiles_per_batch + i + first_tile))
    mod_map = _mod_index(nbatch, ctx_tiles - first_tile)
    in_specs = [tile, route, route, pl.BlockSpec((1, 1, 6 * D), lambda b, i, *_: mod_map(b, i))]
    args = [x, lpos, gates, mod]
    if final:
        in_specs.append(pl.BlockSpec((1, D), lambda b, i, *_: (0, 0)))
        args.append(final_g)
    return pl.pallas_call(
        functools.partial(_combine_kernel, tiles_per_batch=tiles_per_batch, first_tile=first_tile, final=final),
        grid_spec=pltpu.PrefetchScalarGridSpec(
            num_scalar_prefetch=3,
            grid=(nbatch, nt),
            in_specs=in_specs + [pl.BlockSpec(memory_space=pl.ANY)],
            out_specs=pl.BlockSpec((1, TM, D), lambda b, i, *_: (b, i, 0)),
            scratch_shapes=[pltpu.VMEM((2, LROWS, D // 2), u32), pltpu.SemaphoreType.DMA((2,))],
        ),
        out_shape=jax.ShapeDtypeStruct((nbatch, nt * TM, D), f32),
        compiler_params=_cparams(("arbitrary", "arbitrary")),
        name="ffn_residual_final" if final else "ffn_residual",
    )(base, seg, lstart, *args, y_sorted)


def _moe_rows(ntok):
    ntiles = ntok // TM
    worst = TOP_K * ntok + ntiles * N_EXPERTS * (SEG_ALIGN - 1)
    return (-(-worst // MOE_BLK) + N_EXPERTS) * MOE_BLK


def _segment_plan(cnt, nblk):
    seg = (cnt + SEG_ALIGN - 1) // SEG_ALIGN * SEG_ALIGN
    lstart = jnp.cumsum(seg, axis=1) - seg
    rows = jnp.sum(seg, axis=0)
    padded = (rows + MOE_BLK - 1) // MOE_BLK * MOE_BLK
    pad_ends = jnp.cumsum(padded)
    base = (pad_ends - padded)[None, :] + jnp.cumsum(seg, axis=0) - seg
    blk_start = jnp.arange(nblk, dtype=i32) * MOE_BLK
    blk_expert = jnp.minimum(jnp.sum(pad_ends[None, :] <= blk_start[:, None], axis=1), N_EXPERTS - 1)
    n_used = (pad_ends[-1] // MOE_BLK).reshape(1)
    flat = lambda a: a.reshape(-1).astype(i32)
    return flat(base), flat(seg), flat(lstart), flat(pad_ends), blk_expert.astype(i32), n_used.astype(i32)


def _rope_tables(tb):
    rows = SEQ // GRID_W
    row = jnp.repeat(jnp.arange(rows, dtype=f32), GRID_W)
    col = jnp.tile(jnp.arange(GRID_W, dtype=f32), rows)
    inv_freq = ROPE_BASE ** (-jnp.arange(ROPE_FREQS, dtype=f32) / ROPE_FREQS)
    ang_r = row[:, None] * inv_freq[None, :]
    ang_c = col[:, None] * inv_freq[None, :]
    ang = jnp.concatenate([ang_r, ang_r, ang_c, ang_c], axis=-1)
    cos = jnp.concatenate([jnp.ones((CTX, HEAD_DIM), f32), jnp.cos(ang)], axis=0)
    sin = jnp.concatenate([jnp.zeros((CTX, HEAD_DIM), f32), jnp.sin(ang)], axis=0)
    return jnp.tile(cos, (1, LANES // HEAD_DIM)), jnp.tile(sin, (1, LANES // HEAD_DIM))


def _block_diag2(w):
    z = jnp.zeros_like(w[0])
    return jnp.concatenate([jnp.concatenate([w[0], z], axis=1), jnp.concatenate([z, w[1]], axis=1)], axis=0)


def kernel(x, c, ctx, c_ctx, w_mod, b_mod, norm_mix_g, norm_ffn_g, w_in, w_out, att_sink, shift_mu_prev, shift_mu_next, decay_w0, decay_w2, iclr_a0, iclr_a2, vres_v0, vres_v1, vres_v2, gate_g2, k_k, k_a, r_k, ln_x_w, ln_x_b, router_w, router_b, expert_w_gate, expert_w_up, expert_w_down, final_norm_g):
    nbatch = x.shape[0]
    depth = w_mod.shape[0]
    tb = ctx.shape[1] + x.shape[1]
    xa = jnp.concatenate([ctx, x], axis=1)
    nb_pad = -(-(nbatch + 1) // SUBLANES) * SUBLANES
    cond = jnp.zeros((nb_pad, D), f32).at[:nbatch].set(c).at[nbatch].set(c_ctx)
    mod_all = _mod_call(cond, w_mod, b_mod).reshape(depth, nb_pad, 1, 6 * D)
    cos, sin = _rope_tables(tb)
    wr_hi = router_w.T.astype(bf16)
    wr_lo = (router_w.T - wr_hi.astype(f32)).astype(bf16)
    w_router = jnp.concatenate([wr_hi, wr_lo], axis=0)
    b_router = router_b.reshape(N_EXPERTS, 1)
    v_first = None
    for l in range(depth):
        mod = mod_all[l]
        q, k, v, rw = _in_proj_call(xa, mod, norm_mix_g[l].reshape(1, D), w_in[l].astype(bf16), cos, sin)
        att = _attn_call(att_sink[l], q, k, v)
        mu = jnp.stack([shift_mu_prev[l], shift_mu_next[l]])
        v0 = vres_v0[l - 1] if l > 0 else jnp.zeros((RW,), f32)
        vec = jnp.stack([k_k[l], k_a[l], r_k[l].reshape(RW), v0,
                         decay_w0[l, 0], decay_w0[l, 1], iclr_a0[l, 0], iclr_a0[l, 1]])
        if l > 0:
            v1 = jnp.zeros((RW, LANES), f32).at[:, :LORA_VRES].set(vres_v1[l - 1]).astype(bf16)
            v2 = jnp.zeros((LANES, RW), f32).at[:LORA_VRES].set(vres_v2[l - 1]).astype(bf16)
        else:
            v1 = v2 = None
        r_, v_, kk, bv, g, kd, lw, bd = _feat_call(
            rw, v_first, mu, vec, _block_diag2(decay_w2[l]).astype(bf16), _block_diag2(iclr_a2[l]).astype(bf16),
            gate_g2[l].astype(bf16), v1, v2)
        if l == 0:
            v_first = v_
        yf, yb = _scan_call(r_, v_, kk, kd, lw, bd)
        ln = jnp.stack([ln_x_w[l], ln_x_b[l]])
        xa, x_local, lpos, gates, cnt = _mix_call(xa, att, yf, yb, bv, g, mod, ln, w_out[l].astype(bf16),
                                                  norm_ffn_g[l].reshape(1, D), w_router, b_router, l == depth - 1)
        ntok = nbatch * tb
        nrows = _moe_rows(ntok)
        xa = xa.reshape(nbatch, tb, D)
        cnt = cnt[:, 0].astype(i32).reshape(ntok // TM, N_EXPERTS)
        base, seg, lstart, pad_ends, blk_expert, n_used = _segment_plan(cnt, nrows // MOE_BLK)
        x_sorted = _dispatch_call(base, seg, lstart, pad_ends, x_local, nrows)
        y_sorted = _ffn_call(blk_expert, n_used, x_sorted, expert_w_gate, expert_w_up, expert_w_down, l)
        xa = _combine_call(base, seg, lstart, xa, lpos, gates, mod, y_sorted,
                           final_norm_g.reshape(1, D) if l == depth - 1 else None)
    return xa
```
